```python
import math
import jax, jax.numpy as jnp
from jax import lax
import numpy as np

D_MODEL = 1024
BATCH = 2
SEQ = 8192
DEPTH = 2

MLA_HEADS = 6
MLA_Q_RANK = 192
MLA_KV_RANK = 128
MLA_NOPE = 64
MLA_ROPE = 32
MLA_V = 64
RET_HEADS = 4
RET_DK = 64
RET_DV = 64
RET_CHUNK = 128
NSA_HEADS = 6
NSA_KV_HEADS = 2
NSA_GROUP = NSA_HEADS // NSA_KV_HEADS
NSA_DH = 64
CMP_BLOCK = 32
CMP_STRIDE = 16
CMP_HIDDEN = 256
SEL_BLOCK = 64
SEL_TOPK = 16
SEL_LOCAL = 2
WINDOW = 512
Q_BLOCK = 128
REL_BUCKETS = 32
REL_MAX_DIST = 128
D_FF = 2816
CONV_WIDTH = 3

ROPE_BASE = 10000.0
EPS = 1e-6
NEG_INF = -1e30
FORCE = 1e9

D_MIX = MLA_HEADS * MLA_V + RET_HEADS * RET_DV + NSA_HEADS * NSA_DH
IN_SPLITS = (MLA_Q_RANK, MLA_KV_RANK, MLA_ROPE,
             RET_HEADS * RET_DK, RET_HEADS * RET_DK, RET_HEADS * RET_DV, RET_HEADS * RET_DV,
             NSA_HEADS * NSA_DH,
             NSA_KV_HEADS * NSA_DH, NSA_KV_HEADS * NSA_DH,
             NSA_KV_HEADS * NSA_DH, NSA_KV_HEADS * NSA_DH,
             NSA_KV_HEADS * NSA_DH, NSA_KV_HEADS * NSA_DH,
             3 * NSA_HEADS)
D_IN = sum(IN_SPLITS)

kernel_name = "hymba_mla_retnet_nsa_convffn"


def rmsnorm(x, g):
    xf = x.astype(jnp.float32)
    y = xf * lax.rsqrt(jnp.mean(xf * xf, axis=-1, keepdims=True) + EPS)
    return (y * g.astype(jnp.float32)).astype(x.dtype)


def rope(x, pos):
    d = x.shape[-1]
    inv = ROPE_BASE ** (-jnp.arange(0, d, 2, dtype=jnp.float32) / d)
    ang = pos.astype(jnp.float32)[:, None] * inv[None, :]
    cos, sin = jnp.cos(ang).astype(x.dtype), jnp.sin(ang).astype(x.dtype)
    x1, x2 = x[..., : d // 2], x[..., d // 2:]
    return jnp.concatenate([x1 * cos - x2 * sin, x1 * sin + x2 * cos], axis=-1)


def t5_bucket(dist):
    max_exact = REL_BUCKETS // 2
    d = jnp.maximum(dist, 1).astype(jnp.float32)
    log_b = max_exact + (jnp.log(d / max_exact) / math.log(REL_MAX_DIST / max_exact)
                         * (REL_BUCKETS - max_exact)).astype(jnp.int32)
    return jnp.where(dist < max_exact, dist, jnp.minimum(log_b, REL_BUCKETS - 1))


def dense_causal_attention(q, k, v, scale):
    B, H, T, D = q.shape
    nb = T // Q_BLOCK
    qb = q.reshape(B, H, nb, Q_BLOCK, D).transpose(2, 0, 1, 3, 4)
    key_pos = jnp.arange(T)

    def one(args):
        i, qi = args
        q_pos = i * Q_BLOCK + jnp.arange(Q_BLOCK)
        s = jnp.einsum('bhqd,bhkd->bhqk', qi, k).astype(jnp.float32) * scale
        s = jnp.where(key_pos[None, :] <= q_pos[:, None], s, NEG_INF)
        p = jax.nn.softmax(s, axis=-1).astype(v.dtype)
        return jnp.einsum('bhqk,bhkd->bhqd', p, v)

    o = lax.map(one, (jnp.arange(nb), qb))
    return o.transpose(1, 2, 0, 3, 4).reshape(B, H, T, v.shape[-1])


def mla(c_q, c_kv, k_pe, q_a_norm, w_uq, kv_a_norm, w_ukv, q_norm, k_norm):
    B, T, _ = c_q.shape
    H = MLA_HEADS
    pos = jnp.arange(T)
    q = (rmsnorm(c_q, q_a_norm) @ w_uq).reshape(B, T, H, MLA_NOPE + MLA_ROPE)
    kv = (rmsnorm(c_kv, kv_a_norm) @ w_ukv).reshape(B, T, H, MLA_NOPE + MLA_V)
    k_nope, v = kv[..., :MLA_NOPE], kv[..., MLA_NOPE:]
    k = jnp.concatenate([k_nope, jnp.broadcast_to(k_pe[:, :, None, :], (B, T, H, MLA_ROPE))], axis=-1)
    q = rmsnorm(q, q_norm).transpose(0, 2, 1, 3)
    k = rmsnorm(k, k_norm).transpose(0, 2, 1, 3)
    q = jnp.concatenate([q[..., :MLA_NOPE], rope(q[..., MLA_NOPE:], pos)], axis=-1)
    k = jnp.concatenate([k[..., :MLA_NOPE], rope(k[..., MLA_NOPE:], pos)], axis=-1)
    o = dense_causal_attention(q, k, v.transpose(0, 2, 1, 3), (MLA_NOPE + MLA_ROPE) ** -0.5)
    return o.transpose(0, 2, 1, 3).reshape(B, T, H * MLA_V)


def retention(q, k, v, g, norm_g):
    B, T, _ = q.shape
    H, C = RET_HEADS, RET_CHUNK
    nc = T // C
    pos = jnp.arange(T)

    def heads(a, d):
        return a.reshape(B, T, H, d).transpose(0, 2, 1, 3)

    qh = rope(heads(q, RET_DK), pos)
    kh = rope(heads(k, RET_DK), pos) * (RET_DK ** -0.5)
    vh = heads(v, RET_DV)
    log_gamma = jnp.log(1.0 - 2.0 ** (-5.0 - jnp.arange(H, dtype=jnp.float32)))
    idx = jnp.arange(C, dtype=jnp.float32)
    diff = idx[:, None] - idx[None, :]
    decay_in = jnp.exp(jnp.maximum(diff, 0.0) * log_gamma[:, None, None]) * (diff >= 0)
    q_decay = jnp.exp((idx + 1.0) * log_gamma[:, None])[..., None]
    k_decay = jnp.exp((C - 1.0 - idx) * log_gamma[:, None])[..., None]
    chunk_decay = jnp.exp(C * log_gamma)[:, None, None]

    def chunks(a):
        return a.reshape(B, H, nc, C, a.shape[-1]).transpose(2, 0, 1, 3, 4)

    def step(state, inp):
        qc, kc, vc = inp
        inner = jnp.einsum('bhqd,bhkd->bhqk', qc, kc) * decay_in
        out = (jnp.einsum('bhqk,bhkd->bhqd', inner, vc)
               + jnp.einsum('bhqd,bhde->bhqe', qc * q_decay, state))
        state = state * chunk_decay + jnp.einsum('bhkd,bhke->bhde', kc * k_decay, vc)
        return state, out

    state0 = jnp.zeros((B, H, RET_DK, RET_DV), jnp.float32)
    _, o = lax.scan(step, state0, (chunks(qh), chunks(kh), chunks(vh)))
    o = o.transpose(1, 0, 3, 2, 4).reshape(B, T, H, RET_DV).astype(v.dtype)
    o = rmsnorm(o, norm_g).reshape(B, T, H * RET_DV)
    return jax.nn.silu(g) * o


def nsa(q, k_cmp, v_cmp, k_sel, v_sel, k_win, v_win, gate_logits, rel_bias,
        pos_k, w1_k, w2_k, pos_v, w1_v, w2_v, q_norm, kn_cmp, kn_sel, kn_win):
    B, T, _ = q.shape
    G, R, D = NSA_KV_HEADS, NSA_GROUP, NSA_DH
    scale = D ** -0.5
    n_cmp = (T - CMP_BLOCK) // CMP_STRIDE + 1
    n_sel = T // SEL_BLOCK
    top_k = min(SEL_TOPK, n_sel)
    nb = T // Q_BLOCK
    q = rmsnorm(q.reshape(B, T, G, R, D), q_norm)
    q_blocks = q.reshape(B, nb, Q_BLOCK, G, R, D).transpose(1, 0, 3, 4, 2, 5)
    gates = jax.nn.sigmoid(gate_logits.astype(jnp.float32)).astype(q.dtype)
    gates = gates.reshape(B, nb, Q_BLOCK, G, R, 3).transpose(1, 0, 3, 4, 2, 5)

    def kv_heads(a):
        return a.reshape(B, T, G, D).transpose(0, 2, 1, 3)

    cmp_idx = jnp.arange(n_cmp)[:, None] * CMP_STRIDE + jnp.arange(CMP_BLOCK)[None, :]

    def compress(a, pos, w1, w2):
        blocks = kv_heads(a)[:, :, cmp_idx] + pos
        hid = jax.nn.gelu(blocks.reshape(B, G, n_cmp, CMP_BLOCK * D) @ w1)
        return hid @ w2

    k_c = rmsnorm(compress(k_cmp, pos_k, w1_k, w2_k), kn_cmp)
    v_c = compress(v_cmp, pos_v, w1_v, w2_v)
    k_s = rmsnorm(kv_heads(k_sel), kn_sel)
    v_s = kv_heads(v_sel)
    pad = ((0, 0), (0, 0), (WINDOW, 0), (0, 0))
    k_w = jnp.pad(rmsnorm(kv_heads(k_win), kn_win), pad)
    v_w = jnp.pad(kv_heads(v_win), pad)

    cmp_end = jnp.arange(n_cmp) * CMP_STRIDE + (CMP_BLOCK - 1)
    c_start = np.arange(n_cmp) * CMP_STRIDE
    s_start = np.arange(n_sel) * SEL_BLOCK
    overlap = np.clip(np.minimum(c_start[:, None] + CMP_BLOCK, s_start[None, :] + SEL_BLOCK)
                      - np.maximum(c_start[:, None], s_start[None, :]), 0, None)
    cmp_to_sel = jnp.asarray(overlap / CMP_BLOCK, dtype=jnp.float32)
    rb = rel_bias.reshape(REL_BUCKETS, G, R)
    blk = jnp.arange(n_sel)
    bi = jnp.arange(B)[:, None, None, None]
    gi = jnp.arange(G)[None, :, None, None]

    def block(args):
        i, qb, gb = args
        q_pos = i * Q_BLOCK + jnp.arange(Q_BLOCK)
        d_c = q_pos[:, None] - cmp_end[None, :]
        m_c = d_c >= 0
        bias_c = rb[t5_bucket(jnp.maximum(d_c, 0))].transpose(2, 3, 0, 1)
        s_c = jnp.einsum('bgrqd,bgnd->bgrqn', qb, k_c).astype(jnp.float32) * scale + bias_c
        p_c = jax.nn.softmax(jnp.where(m_c, s_c, NEG_INF), axis=-1) * m_c
        o_c = jnp.einsum('bgrqn,bgnd->bgrqd', p_c.astype(v_c.dtype), v_c)
        cur = q_pos // SEL_BLOCK
        imp = jnp.einsum('bgrqn,nj->bgqj', p_c, cmp_to_sel)
        back = cur[:, None] - blk[None, :]
        forced = (blk[None, :] == 0) | ((back >= 0) & (back < SEL_LOCAL))
        imp = jnp.where(forced, FORCE, imp)
        imp = jnp.where(back >= 0, imp, NEG_INF)
        _, sel = lax.top_k(imp, top_k)
        tok = (sel[..., None] * SEL_BLOCK + jnp.arange(SEL_BLOCK)).reshape(B, G, Q_BLOCK, top_k * SEL_BLOCK)
        k_g = k_s[bi, gi, tok]
        v_g = v_s[bi, gi, tok]
        d_s = q_pos[None, None, :, None] - tok
        m_s = (d_s >= 0)[:, :, None]
        bias_s = jnp.moveaxis(rb[t5_bucket(jnp.maximum(d_s, 0)), gi], -1, 2)
        s_s = jnp.einsum('bgrqd,bgqkd->bgrqk', qb, k_g).astype(jnp.float32) * scale + bias_s
        p_s = jax.nn.softmax(jnp.where(m_s, s_s, NEG_INF), axis=-1)
        o_s = jnp.einsum('bgrqk,bgqkd->bgrqd', p_s.astype(v_g.dtype), v_g)
        k_b = lax.dynamic_slice_in_dim(k_w, i * Q_BLOCK, Q_BLOCK + WINDOW, axis=2)
        v_b = lax.dynamic_slice_in_dim(v_w, i * Q_BLOCK, Q_BLOCK + WINDOW, axis=2)
        kp = i * Q_BLOCK - WINDOW + jnp.arange(Q_BLOCK + WINDOW)
        d_w = q_pos[:, None] - kp[None, :]
        m_w = (d_w >= 0) & (d_w < WINDOW) & (kp[None, :] >= 0)
        bias_w = rb[t5_bucket(jnp.maximum(d_w, 0))].transpose(2, 3, 0, 1)
        s_w = jnp.einsum('bgrqd,bgkd->bgrqk', qb, k_b).astype(jnp.float32) * scale + bias_w
        p_w = jax.nn.softmax(jnp.where(m_w, s_w, NEG_INF), axis=-1)
        o_w = jnp.einsum('bgrqk,bgkd->bgrqd', p_w.astype(v_b.dtype), v_b)
        return gb[..., 0:1] * o_c + gb[..., 1:2] * o_s + gb[..., 2:3] * o_w

    o = lax.map(block, (jnp.arange(nb), q_blocks, gates))
    return o.transpose(1, 0, 4, 2, 3, 5).reshape(B, T, G * R * D)


def conv_ffn(x, w_up, conv_w, conv_b, w_down):
    h = x @ w_up
    h = lax.conv_general_dilated(h, conv_w[:, None, :], window_strides=(1,),
                                 padding=((CONV_WIDTH - 1, 0),),
                                 dimension_numbers=('NWC', 'WIO', 'NWC'),
                                 feature_group_count=2 * D_FF) + conv_b
    gate, up = h[..., :D_FF], h[..., D_FF:]
    return (jax.nn.silu(gate) * up) @ w_down


def setup_inputs(seed: int = 0) -> dict:
    key = jax.random.key(seed)
    it = iter(jax.random.split(key, 32))

    def nrm(shape, scale):
        return jax.random.normal(next(it), shape, jnp.float32) * scale

    def gain(shape):
        return 1.0 + nrm(shape, 0.05)

    L = DEPTH
    return {
        "x": nrm((BATCH, SEQ, D_MODEL), 1.0),
        "w_in": nrm((L, D_MODEL, D_IN), D_MODEL ** -0.5),
        "w_out": nrm((L, D_MIX, D_MODEL), D_MIX ** -0.5),
        "attn_norm": gain((L, D_MODEL)),
        "ffn_norm": gain((L, D_MODEL)),
        "mla_q_a_norm": gain((L, MLA_Q_RANK)),
        "mla_w_uq": nrm((L, MLA_Q_RANK, MLA_HEADS * (MLA_NOPE + MLA_ROPE)), MLA_Q_RANK ** -0.5),
        "mla_kv_a_norm": gain((L, MLA_KV_RANK)),
        "mla_w_ukv": nrm((L, MLA_KV_RANK, MLA_HEADS * (MLA_NOPE + MLA_V)), MLA_KV_RANK ** -0.5),
        "mla_q_norm": gain((L, MLA_NOPE + MLA_ROPE)),
        "mla_k_norm": gain((L, MLA_NOPE + MLA_ROPE)),
        "ret_norm": gain((L, RET_DV)),
        "nsa_cmp_pos_k": nrm((L, CMP_BLOCK, NSA_DH), 0.1),
        "nsa_cmp_w1_k": nrm((L, CMP_BLOCK * NSA_DH, CMP_HIDDEN), (CMP_BLOCK * NSA_DH) ** -0.5),
        "nsa_cmp_w2_k": nrm((L, CMP_HIDDEN, NSA_DH), CMP_HIDDEN ** -0.5),
        "nsa_cmp_pos_v": nrm((L, CMP_BLOCK, NSA_DH), 0.1),
        "nsa_cmp_w1_v": nrm((L, CMP_BLOCK * NSA_DH, CMP_HIDDEN), (CMP_BLOCK * NSA_DH) ** -0.5),
        "nsa_cmp_w2_v": nrm((L, CMP_HIDDEN, NSA_DH), CMP_HIDDEN ** -0.5),
        "nsa_q_norm": gain((L, NSA_DH)),
        "nsa_k_norm_cmp": gain((L, NSA_DH)),
        "nsa_k_norm_sel": gain((L, NSA_DH)),
        "nsa_k_norm_win": gain((L, NSA_DH)),
        "rel_bias": nrm((REL_BUCKETS, NSA_HEADS), 0.5),
        "ffn_w_up": nrm((L, D_MODEL, 2 * D_FF), D_MODEL ** -0.5),
        "ffn_conv_w": nrm((L, CONV_WIDTH, 2 * D_FF), 0.5),
        "ffn_conv_b": nrm((L, 2 * D_FF), 0.01),
        "ffn_w_down": nrm((L, D_FF, D_MODEL), D_FF ** -0.5),
    }


def reference(x, w_in, w_out, attn_norm, ffn_norm, mla_q_a_norm, mla_w_uq, mla_kv_a_norm,
              mla_w_ukv, mla_q_norm, mla_k_norm, ret_norm, nsa_cmp_pos_k, nsa_cmp_w1_k,
              nsa_cmp_w2_k, nsa_cmp_pos_v, nsa_cmp_w1_v, nsa_cmp_w2_v, nsa_q_norm,
              nsa_k_norm_cmp, nsa_k_norm_sel, nsa_k_norm_win, rel_bias, ffn_w_up,
              ffn_conv_w, ffn_conv_b, ffn_w_down):
    split_points = [int(s) for s in np.cumsum(IN_SPLITS)[:-1]]
    for l in range(DEPTH):
        h = rmsnorm(x, attn_norm[l])
        (c_q, c_kv, k_pe, r_q, r_k, r_v, r_g, n_q, n_kc, n_vc, n_ks, n_vs,
         n_kw, n_vw, n_gate) = jnp.split(h @ w_in[l], split_points, axis=-1)
        o_mla = mla(c_q, c_kv, k_pe, mla_q_a_norm[l], mla_w_uq[l], mla_kv_a_norm[l],
                    mla_w_ukv[l], mla_q_norm[l], mla_k_norm[l])
        o_ret = retention(r_q, r_k, r_v, r_g, ret_norm[l])
        o_nsa = nsa(n_q, n_kc, n_vc, n_ks, n_vs, n_kw, n_vw, n_gate, rel_bias,
                    nsa_cmp_pos_k[l], nsa_cmp_w1_k[l], nsa_cmp_w2_k[l],
                    nsa_cmp_pos_v[l], nsa_cmp_w1_v[l], nsa_cmp_w2_v[l],
                    nsa_q_norm[l], nsa_k_norm_cmp[l], nsa_k_norm_sel[l], nsa_k_norm_win[l])
        x = x + jnp.concatenate([o_mla, o_ret, o_nsa], axis=-1) @ w_out[l]
        x = x + conv_ffn(rmsnorm(x, ffn_norm[l]), ffn_w_up[l], ffn_conv_w[l],
                         ffn_conv_b[l], ffn_w_down[l])
    return x
```

```python
import functools
import math

import numpy as np
import jax
import jax.numpy as jnp
from jax import lax
from jax.experimental import pallas as pl
from jax.experimental.pallas import tpu as pltpu

D_MODEL = 1024
DEPTH = 2
MLA_HEADS = 6
MLA_Q_RANK = 192
MLA_KV_RANK = 128
MLA_NOPE = 64
MLA_ROPE = 32
MLA_V = 64
MLA_QK = MLA_NOPE + MLA_ROPE
RET_HEADS = 4
RET_DK = 64
RET_DV = 64
RET_CHUNK = 128
NSA_HEADS = 6
NSA_KV_HEADS = 2
NSA_GROUP = NSA_HEADS // NSA_KV_HEADS
NSA_DH = 64
CMP_BLOCK = 32
CMP_STRIDE = 16
CMP_HIDDEN = 256
SEL_BLOCK = 64
SEL_TOPK = 16
SEL_LOCAL = 2
WINDOW = 512
REL_BUCKETS = 32
REL_MAX_DIST = 128
D_FF = 2816
ROPE_BASE = 10000.0
EPS = 1e-6
NEG_INF = -1e30
FORCE = 1e9

_IN_SPLITS = (MLA_Q_RANK, MLA_KV_RANK, MLA_ROPE,
              RET_HEADS * RET_DK, RET_HEADS * RET_DK, RET_HEADS * RET_DV, RET_HEADS * RET_DV,
              NSA_HEADS * NSA_DH) + (NSA_KV_HEADS * NSA_DH,) * 6 + (3 * NSA_HEADS,)
_IN_OFF = [0] + [int(v) for v in np.cumsum(_IN_SPLITS)]
D_IN = _IN_OFF[-1]

LANES = 128
ATT_TILE = 512
CMP_TQ = 256
VMEM_LIMIT = 56 * 1024 * 1024

_MXU_DTYPE = jnp.bfloat16
F32 = jnp.float32


def _cparams(*sem):
    return pltpu.CompilerParams(dimension_semantics=sem, vmem_limit_bytes=VMEM_LIMIT)


def _dot(a, b):
    return jnp.dot(a, b, preferred_element_type=F32)


def _sigmoid(x):
    return 1.0 / (1.0 + jnp.exp(-x))


def _resident(shape):
    nd = len(shape)
    return pl.BlockSpec(shape, lambda *_: (0,) * nd, pipeline_mode=pl.Buffered(1))


def _in_proj_kernel(x_ref, g_ref, w_ref, *o_refs, widths):
    x = x_ref[...]
    y = x * lax.rsqrt(jnp.mean(x * x, axis=-1, keepdims=True) + EPS)
    xn = (y * g_ref[...]).astype(_MXU_DTYPE)
    off = 0
    for o_ref, wd in zip(o_refs, widths):
        o_ref[...] = _dot(xn, w_ref[:, off:off + wd])
        off += wd


def _in_proj(x, g, w, widths, tm=256):
    m, d = x.shape
    n = w.shape[1]
    return pl.pallas_call(
        functools.partial(_in_proj_kernel, widths=widths),
        grid=(m // tm,),
        in_specs=[pl.BlockSpec((tm, d), lambda i: (i, 0)),
                  _resident((1, d)),
                  _resident((d, n))],
        out_specs=[pl.BlockSpec((tm, wd), lambda i: (i, 0)) for wd in widths],
        out_shape=[jax.ShapeDtypeStruct((m, wd), F32) for wd in widths],
        compiler_params=_cparams("parallel"),
        name="in_proj",
    )(x, g, w)


def _mla_prep_kernel(pm_ref, cos_ref, sin_ref, gqa_ref, gkva_ref, wq_ref, wkv_ref,
                     gq_ref, gqr_ref, gk_ref, gkr_ref, q_ref, k_ref, v_ref):
    pm = pm_ref[...]
    hs = MLA_HEADS * LANES
    cq = pm[:, 0:256]
    r = lax.rsqrt(jnp.sum(cq * cq, axis=-1, keepdims=True) * (1.0 / MLA_Q_RANK) + EPS)
    qq = _dot((cq * r * gqa_ref[...]).astype(_MXU_DTYPE), wq_ref[...])
    ckv = pm[:, 256:384]
    r = lax.rsqrt(jnp.mean(ckv * ckv, axis=-1, keepdims=True) + EPS)
    kv = _dot((ckv * r * gkva_ref[...]).astype(_MXU_DTYPE), wkv_ref[...])
    kpe = pm[:, 384:512]
    kpe_rot = pm[:, 512:640]
    cos = cos_ref[...]
    sin = sin_ref[...]
    scale = MLA_QK ** -0.5
    aq = cos * gq_ref[...] * scale
    bq = sin * gqr_ref[...] * scale
    ak = cos * gk_ref[...]
    bk = sin * gkr_ref[...]
    for h in range(MLA_HEADS):
        sl = slice(h * LANES, (h + 1) * LANES)
        sr = slice(hs + h * LANES, hs + (h + 1) * LANES)
        qh = qq[:, sl]
        rq = lax.rsqrt(jnp.sum(qh * qh, axis=-1, keepdims=True) * (1.0 / MLA_QK) + EPS)
        q_ref[0, h] = ((qh * aq + qq[:, sr] * bq) * rq).astype(q_ref.dtype)
        kh = kv[:, sl] + kpe
        rk = lax.rsqrt(jnp.sum(kh * kh, axis=-1, keepdims=True) * (1.0 / MLA_QK) + EPS)
        k_ref[0, h] = ((kh * ak + kpe_rot * bk) * rk).astype(k_ref.dtype)
        v_ref[0, h] = kv[:, sr][:, 0:MLA_V].astype(v_ref.dtype)


def _mla_prep(pm, cos, sin, gqa, gkva, wq, wkv, gq, gqr, gk, gkr, b, t, tm=512):
    nt = t // tm
    hs = MLA_HEADS * LANES
    vec = lambda n: _resident((1, n))
    return pl.pallas_call(
        _mla_prep_kernel,
        grid=(b, nt),
        in_specs=[pl.BlockSpec((tm, 640), lambda bi, i: (bi * nt + i, 0)),
                  pl.BlockSpec((tm, LANES), lambda bi, i: (i, 0)),
                  pl.BlockSpec((tm, LANES), lambda bi, i: (i, 0)),
                  vec(256), vec(LANES), _resident((256, 2 * hs)), _resident((LANES, 2 * hs)),
                  vec(LANES), vec(LANES), vec(LANES), vec(LANES)],
        out_specs=[pl.BlockSpec((1, MLA_HEADS, tm, LANES), lambda bi, i: (bi, 0, i, 0)),
                   pl.BlockSpec((1, MLA_HEADS, tm, LANES), lambda bi, i: (bi, 0, i, 0)),
                   pl.BlockSpec((1, MLA_HEADS, tm, MLA_V), lambda bi, i: (bi, 0, i, 0))],
        out_shape=[jax.ShapeDtypeStruct((b, MLA_HEADS, t, LANES), _MXU_DTYPE),
                   jax.ShapeDtypeStruct((b, MLA_HEADS, t, LANES), _MXU_DTYPE),
                   jax.ShapeDtypeStruct((b, MLA_HEADS, t, MLA_V), _MXU_DTYPE)],
        compiler_params=_cparams("parallel", "parallel"),
        name="mla_prep",
    )(pm, cos, sin, gqa, gkva, wq, wkv, gq, gqr, gk, gkr)


def _flash_kernel(q_ref, k_ref, v_ref, tab_ref, o_ref, m_ref, l_ref, acc_ref, *, n_near, use_far):
    i = pl.program_id(2)
    q = q_ref[0, 0]
    tc = ATT_TILE
    m_ref[...] = jnp.full(m_ref.shape, -3e38, F32)
    l_ref[...] = jnp.zeros(l_ref.shape, F32)
    acc_ref[...] = jnp.zeros(acc_ref.shape, F32)

    def update(c, table):
        start = pl.multiple_of(c * tc, tc)
        kc = k_ref[0, 0, pl.ds(start, tc), :]
        vc = v_ref[0, 0, :, pl.ds(start, tc)]
        s = _dot(kc, q)
        if table is not None:
            s = s + table
        m_old = m_ref[...]
        m_new = jnp.maximum(m_old, jnp.max(s, axis=0, keepdims=True))
        alpha = jnp.exp(m_old - m_new)
        p = jnp.exp(s - m_new)
        l_ref[...] = alpha * l_ref[...] + jnp.sum(p, axis=0, keepdims=True)
        acc_ref[...] = alpha * acc_ref[...] + _dot(vc, p.astype(_MXU_DTYPE))
        m_ref[...] = m_new

    if use_far:
        def body(c, carry):
            update(c, None)
            return carry
        lax.fori_loop(0, jnp.maximum(i - (n_near - 1), 0), body, 0)
    for nn in range(n_near):
        c = i - (n_near - 1) + nn
        if nn == n_near - 1:
            update(c, tab_ref[0, nn])
        else:
            @pl.when(c >= 0)
            def _():
                update(c, tab_ref[0, nn])
    o_ref[0, 0] = acc_ref[...] / l_ref[...]


def _flash(qt, k, vt, tab, *, use_far, name):
    b, h, dk, t = qt.shape
    hk, dv = vt.shape[1], vt.shape[2]
    rep = h // hk
    ht, n_near = tab.shape[0], tab.shape[1]
    tq = ATT_TILE
    return pl.pallas_call(
        functools.partial(_flash_kernel, n_near=n_near, use_far=use_far),
        grid=(b, h, t // tq),
        in_specs=[pl.BlockSpec((1, 1, dk, tq), lambda bi, hi, i: (bi, hi, 0, i)),
                  pl.BlockSpec((1, 1, t, dk), lambda bi, hi, i: (bi, hi // rep, 0, 0)),
                  pl.BlockSpec((1, 1, dv, t), lambda bi, hi, i: (bi, hi // rep, 0, 0)),
                  pl.BlockSpec((1, n_near, tq, tq), lambda bi, hi, i: (hi % ht, 0, 0, 0))],
        out_specs=pl.BlockSpec((1, 1, dv, tq), lambda bi, hi, i: (bi, hi, 0, i)),
        out_shape=jax.ShapeDtypeStruct((b, h, dv, t), F32),
        scratch_shapes=[pltpu.VMEM((1, tq), F32), pltpu.VMEM((1, tq), F32),
                        pltpu.VMEM((dv, tq), F32)],
        compiler_params=_cparams("parallel", "parallel", "parallel"),
        name=name,
    )(qt, k, vt, tab)


def _ret_kernel(q_ref, k_ref, v_ref, g_ref, qr_ref, kr_ref, cos_ref, sin_ref, din_ref, qd_ref,
                kd_ref, cd_ref, gn_ref, o_ref, state_ref, *, nchunk):
    @pl.when(pl.program_id(2) == 0)
    def _():
        state_ref[...] = jnp.zeros(state_ref.shape, F32)

    c_ = RET_CHUNK
    for c in range(nchunk):
        sl = slice(c * c_, (c + 1) * c_)
        cos = cos_ref[sl, :]
        sin = sin_ref[sl, :]
        qh = q_ref[0, 0, sl, :] * cos + qr_ref[0, 0, sl, :] * sin
        kh = (k_ref[0, 0, sl, :] * cos + kr_ref[0, 0, sl, :] * sin) * (RET_DK ** -0.5)
        vb = v_ref[0, 0, sl, :].astype(_MXU_DTYPE)
        st = state_ref[...]
        inner = lax.dot_general(qh.astype(_MXU_DTYPE), kh.astype(_MXU_DTYPE),
                                (((1,), (1,)), ((), ())), preferred_element_type=F32) * din_ref[0]
        out = (_dot(inner.astype(_MXU_DTYPE), vb)
               + _dot((qh * qd_ref[0]).astype(_MXU_DTYPE), st.astype(_MXU_DTYPE)))
        kdt = (kh * kd_ref[0]).T
        state_ref[...] = st * cd_ref[0] + _dot(kdt.astype(_MXU_DTYPE), vb)
        y = out * lax.rsqrt(jnp.mean(out * out, axis=-1, keepdims=True) + EPS) * gn_ref[...]
        gg = g_ref[0, 0, sl, :]
        o_ref[0, 0, sl, :] = (gg * _sigmoid(gg) * y).astype(o_ref.dtype)


def _retention(q, k, v, g, qr, kr, cos, sin, din, qd, kd, cd, gn, tt=512):
    b, h, t, d = q.shape
    tok = pl.BlockSpec((1, 1, tt, d), lambda bi, hi, i: (bi, hi, i, 0))
    pos = pl.BlockSpec((tt, d), lambda bi, hi, i: (i, 0))
    per_head = lambda r, c: pl.BlockSpec((1, r, c), lambda bi, hi, i: (hi, 0, 0))
    return pl.pallas_call(
        functools.partial(_ret_kernel, nchunk=tt // RET_CHUNK),
        grid=(b, h, t // tt),
        in_specs=[tok, tok, tok, tok, tok, tok, pos, pos,
                  per_head(RET_CHUNK, RET_CHUNK), per_head(RET_CHUNK, d), per_head(RET_CHUNK, d),
                  per_head(1, d), _resident((1, d))],
        out_specs=tok,
        out_shape=jax.ShapeDtypeStruct((b, h, t, d), _MXU_DTYPE),
        scratch_shapes=[pltpu.VMEM((RET_DK, RET_DV), F32)],
        compiler_params=_cparams("parallel", "parallel", "arbitrary"),
        name="retention",
    )(q, k, v, g, qr, kr, cos, sin, din, qd, kd, cd, gn)


def _head_norm_kernel(x_ref, g_ref, o_ref):
    x = x_ref[0, 0]
    y = x * lax.rsqrt(jnp.mean(x * x, axis=-1, keepdims=True) + EPS)
    o_ref[0, 0] = (y * g_ref[0]).astype(o_ref.dtype)


def _head_norm(x, g, tt=2048):
    b, n, t, d = x.shape
    tt = min(tt, t)
    return pl.pallas_call(
        _head_norm_kernel,
        grid=(b, n, t // tt),
        in_specs=[pl.BlockSpec((1, 1, tt, d), lambda bi, ni, i: (bi, ni, i, 0)),
                  pl.BlockSpec((1, 1, d), lambda bi, ni, i: (ni, 0, 0))],
        out_specs=pl.BlockSpec((1, 1, tt, d), lambda bi, ni, i: (bi, ni, i, 0)),
        out_shape=jax.ShapeDtypeStruct(x.shape, _MXU_DTYPE),
        compiler_params=_cparams("parallel", "parallel", "parallel"),
        name="nsa_head_norm",
    )(x, g)


def _gelu_tanh(x):
    return 0.5 * x * (1.0 + jnp.tanh(math.sqrt(2.0 / math.pi) * (x + 0.044715 * (x * x * x))))


def _compress_kernel(ak_ref, bk_ref, av_ref, bv_ref, w1k_ref, w2k_ref, pk_ref, w1v_ref, w2v_ref,
                     pv_ref, gk_ref, kc_ref, vc_ref):
    half = CMP_STRIDE * NSA_DH

    def comp(a_ref, b_ref, w1_ref, w2_ref, p_ref):
        pb = _dot(p_ref[...], w1_ref[...])[0:1]
        hid = _dot(a_ref[0], w1_ref[0:half, :]) + _dot(b_ref[0], w1_ref[half:2 * half, :]) + pb
        return _dot(_gelu_tanh(hid).astype(_MXU_DTYPE), w2_ref[...])

    kc = comp(ak_ref, bk_ref, w1k_ref, w2k_ref, pk_ref)
    y = kc * lax.rsqrt(jnp.mean(kc * kc, axis=-1, keepdims=True) + EPS)
    kc_ref[0] = (y * gk_ref[...]).astype(kc_ref.dtype)
    vc_ref[0] = comp(av_ref, bv_ref, w1v_ref, w2v_ref, pv_ref).astype(vc_ref.dtype)


def _compress(ak, bk, av, bv, w1k, w2k, pk, w1v, w2v, pv, gk):
    n, nc, kk = ak.shape
    blk = pl.BlockSpec((1, nc, kk), lambda i: (i, 0, 0))
    out = pl.BlockSpec((1, nc, NSA_DH), lambda i: (i, 0, 0))
    w1 = _resident((2 * kk, CMP_HIDDEN))
    w2 = _resident((CMP_HIDDEN, NSA_DH))
    pp = _resident((8, 2 * kk))
    return pl.pallas_call(
        _compress_kernel,
        grid=(n,),
        in_specs=[blk, blk, blk, blk, w1, w2, pp, w1, w2, pp, _resident((1, NSA_DH))],
        out_specs=[out, out],
        out_shape=[jax.ShapeDtypeStruct((n, nc, NSA_DH), _MXU_DTYPE)] * 2,
        compiler_params=_cparams("parallel"),
        name="nsa_compress",
    )(ak, bk, av, bv, w1k, w2k, pk, w1v, w2v, pv, gk)


def _cmp_sel_kernel(q_ref, kc_ref, vct_ref, dbc_ref, c2s_ref, dm_ref, bk_ref, oc_ref, sel_ref,
                    s_ref, p_ref, *, tq, nc, ns, nrows):
    i = pl.program_id(2)
    valid = dm_ref[...] >= (CMP_BLOCK - 1) - tq * i
    kc = kc_ref[0, 0]
    vct = vct_ref[0, 0]
    row0 = pl.multiple_of(i * (tq // CMP_STRIDE), 8)
    for r in range(NSA_GROUP):
        s_ref[8:nc + 8, :] = _dot(kc, q_ref[0, r])
        s_ref[pl.ds(row0, nrows), :] = s_ref[pl.ds(row0, nrows), :] + dbc_ref[r]
        s = jnp.where(valid, s_ref[8:nc + 8, :], NEG_INF)
        e = jnp.exp(s - jnp.max(s, axis=0, keepdims=True))
        p = jnp.where(valid, e / jnp.sum(e, axis=0, keepdims=True), 0.0)
        pb = p.astype(_MXU_DTYPE)
        p_ref[r * nc:(r + 1) * nc, :] = pb
        oc_ref[0, r] = _dot(vct, pb)
    imp = _dot(c2s_ref[...], p_ref[...])
    back = bk_ref[...] + i * (tq // SEL_BLOCK)
    jidx = lax.broadcasted_iota(jnp.int32, (ns, tq), 0)
    forced = (jidx == 0) | ((back >= 0) & (back < SEL_LOCAL))
    imp = jnp.where(forced, FORCE, imp)
    imp = jnp.where(back >= 0, imp, NEG_INF)
    jf = jidx.astype(F32)
    chosen = jnp.zeros((ns, tq), F32)
    for _ in range(min(SEL_TOPK, ns)):
        mx = jnp.max(imp, axis=0, keepdims=True)
        first = jnp.min(jnp.where(imp == mx, jf, 1e9), axis=0, keepdims=True)
        hit = jf == first
        chosen = jnp.where(hit, 1.0, chosen)
        imp = jnp.where(hit, -3e38, imp)
    sel_ref[0, 0] = jnp.where(chosen > 0.5, 0.0, NEG_INF).astype(sel_ref.dtype)


def _cmp_sel(qt, kc, vct, dbc, c2s, dm, bk):
    b, h, d, t = qt.shape
    g = kc.shape[1]
    nc = kc.shape[2]
    ns = c2s.shape[0]
    tq = dm.shape[1]
    nrows = dbc.shape[1]
    return pl.pallas_call(
        functools.partial(_cmp_sel_kernel, tq=tq, nc=nc, ns=ns, nrows=nrows),
        grid=(b, g, t // tq),
        in_specs=[pl.BlockSpec((1, NSA_GROUP, d, tq), lambda bi, gi, i: (bi, gi, 0, i)),
                  pl.BlockSpec((1, 1, nc, d), lambda bi, gi, i: (bi, gi, 0, 0)),
                  pl.BlockSpec((1, 1, d, nc), lambda bi, gi, i: (bi, gi, 0, 0)),
                  pl.BlockSpec((NSA_GROUP, nrows, tq), lambda bi, gi, i: (gi, 0, 0)),
                  _resident(c2s.shape), _resident(dm.shape), _resident(bk.shape)],
        out_specs=[pl.BlockSpec((1, NSA_GROUP, d, tq), lambda bi, gi, i: (bi, gi, 0, i)),
                   pl.BlockSpec((1, 1, ns, tq), lambda bi, gi, i: (bi, gi, 0, i))],
        out_shape=[jax.ShapeDtypeStruct((b, h, d, t), F32),
                   jax.ShapeDtypeStruct((b, g, ns, t), _MXU_DTYPE)],
        scratch_shapes=[pltpu.VMEM((nc + 8, tq), F32), pltpu.VMEM((NSA_GROUP * nc, tq), _MXU_DTYPE)],
        compiler_params=_cparams("parallel", "parallel", "parallel"),
        name="nsa_cmp_sel",
    )(qt, kc, vct, dbc, c2s, dm, bk)


def _combine_kernel(oc_ref, os_ref, ow_ref, g_ref, o_ref):
    g = _sigmoid(g_ref[0, 0])
    o_ref[0, 0] = (g[0:1] * oc_ref[0, 0] + g[1:2] * os_ref[0, 0]
                   + g[2:3] * ow_ref[0, 0]).astype(o_ref.dtype)


def _combine(oc, os_, ow, gl, tt=2048):
    b, h, d, t = oc.shape
    tt = min(tt, t)
    blk = pl.BlockSpec((1, 1, d, tt), lambda bi, hi, i: (bi, hi, 0, i))
    return pl.pallas_call(
        _combine_kernel,
        grid=(b, h, t // tt),
        in_specs=[blk, blk, blk, pl.BlockSpec((1, 1, 3, tt), lambda bi, hi, i: (bi, hi, 0, i))],
        out_specs=blk,
        out_shape=jax.ShapeDtypeStruct(oc.shape, _MXU_DTYPE),
        compiler_params=_cparams("parallel", "parallel", "parallel"),
        name="nsa_combine",
    )(oc, os_, ow, gl)


def _out_proj_kernel(a_ref, w_ref, r_ref, o_ref):
    o_ref[...] = r_ref[...] + _dot(a_ref[...], w_ref[...])


def _out_proj(a, w, res, tm=512):
    m, k = a.shape
    n = w.shape[1]
    return pl.pallas_call(
        _out_proj_kernel,
        grid=(m // tm,),
        in_specs=[pl.BlockSpec((tm, k), lambda i: (i, 0)), _resident((k, n)),
                  pl.BlockSpec((tm, n), lambda i: (i, 0))],
        out_specs=pl.BlockSpec((tm, n), lambda i: (i, 0)),
        out_shape=jax.ShapeDtypeStruct((m, n), F32),
        compiler_params=_cparams("parallel"),
        name="out_proj",
    )(a, w, res)


def _ffn_kernel(x_ref, gn_ref, wup_ref, cw_ref, cb_ref, wdn_ref, o_ref, carry_ref, hbuf_ref,
                acc_ref, *, tm, fc):
    @pl.when(pl.program_id(1) == 0)
    def _():
        carry_ref[...] = jnp.zeros(carry_ref.shape, F32)

    x = x_ref[...]
    y = x * lax.rsqrt(jnp.mean(x * x, axis=-1, keepdims=True) + EPS)
    xn = (y * gn_ref[...]).astype(_MXU_DTYPE)

    def conv(col0):
        cols = slice(col0, col0 + fc)
        h = _dot(xn, wup_ref[:, cols])
        hbuf_ref[0:8, :] = carry_ref[:, cols]
        hbuf_ref[8:tm + 8, :] = h
        carry_ref[:, cols] = h[tm - 8:tm, :]
        w = cw_ref[:, cols]
        return (h * w[2:3] + hbuf_ref[7:tm + 7, :] * w[1:2] + hbuf_ref[6:tm + 6, :] * w[0:1]
                + cb_ref[:, cols])

    for f in range(D_FF // fc):
        gate = conv(f * fc)
        up = conv(D_FF + f * fc)
        act = (gate * _sigmoid(gate) * up).astype(_MXU_DTYPE)
        contrib = _dot(act, wdn_ref[f * fc:(f + 1) * fc, :])
        if f == 0:
            acc_ref[...] = contrib
        else:
            acc_ref[...] += contrib
    o_ref[...] = x + acc_ref[...]


def _ffn(x, gn, wup, cw, cb, wdn, b, t, tm=512, fc=256):
    d = x.shape[1]
    nt = t // tm
    return pl.pallas_call(
        functools.partial(_ffn_kernel, tm=tm, fc=fc),
        grid=(b, nt),
        in_specs=[pl.BlockSpec((tm, d), lambda bi, i: (bi * nt + i, 0)),
                  _resident((1, d)), _resident(wup.shape), _resident(cw.shape),
                  _resident(cb.shape), _resident(wdn.shape)],
        out_specs=pl.BlockSpec((tm, d), lambda bi, i: (bi * nt + i, 0)),
        out_shape=jax.ShapeDtypeStruct(x.shape, F32),
        scratch_shapes=[pltpu.VMEM((8, 2 * D_FF), F32), pltpu.VMEM((tm + 8, fc), F32),
                        pltpu.VMEM((tm, d), F32)],
        compiler_params=_cparams("parallel", "arbitrary"),
        name="conv_ffn",
    )(x, gn, wup, cw, cb, wdn)


def _rope_tables(t, d):
    inv = ROPE_BASE ** (-np.arange(0, d, 2, dtype=np.float64) / d)
    ang = np.arange(t, dtype=np.float64)[:, None] * inv[None, :]
    return (np.concatenate([np.cos(ang)] * 2, axis=1), np.concatenate([np.sin(ang)] * 2, axis=1))


def _t5_bucket_np(dist):
    max_exact = REL_BUCKETS // 2
    d = np.maximum(dist, 1).astype(np.float64)
    log_b = max_exact + (np.log(d / max_exact) / math.log(REL_MAX_DIST / max_exact)
                         * (REL_BUCKETS - max_exact)).astype(np.int32)
    return np.where(dist < max_exact, dist, np.minimum(log_b, REL_BUCKETS - 1))


@functools.lru_cache(maxsize=None)
def _constants(t):
    c = {}
    cos, sin = _rope_tables(t, MLA_ROPE)
    pad = LANES - MLA_QK
    c["mla_cos"] = np.concatenate([np.ones((t, MLA_NOPE)), cos, np.ones((t, pad))], 1).astype(np.float32)
    c["mla_sin"] = np.concatenate([np.zeros((t, MLA_NOPE)), sin, np.zeros((t, pad))], 1).astype(np.float32)
    cos, sin = _rope_tables(t, RET_DK)
    c["ret_cos"], c["ret_sin"] = cos.astype(np.float32), sin.astype(np.float32)
    lg = np.log(1.0 - 2.0 ** (-5.0 - np.arange(RET_HEADS, dtype=np.float64)))
    idx = np.arange(RET_CHUNK, dtype=np.float64)
    diff = idx[:, None] - idx[None, :]
    c["ret_din"] = (np.exp(np.maximum(diff, 0.0) * lg[:, None, None]) * (diff >= 0)).astype(np.float32)
    qd = np.exp((idx + 1.0) * lg[:, None])[..., None]
    kd = np.exp((RET_CHUNK - 1.0 - idx) * lg[:, None])[..., None]
    c["ret_qd"] = np.broadcast_to(qd, (RET_HEADS, RET_CHUNK, RET_DK)).astype(np.float32)
    c["ret_kd"] = np.broadcast_to(kd, (RET_HEADS, RET_CHUNK, RET_DK)).astype(np.float32)
    c["ret_cd"] = np.broadcast_to(np.exp(RET_CHUNK * lg)[:, None, None], (RET_HEADS, 1, RET_DV)).astype(np.float32)
    c["bucket"] = _t5_bucket_np(np.arange(LANES)).astype(np.int32)
    kk = np.arange(ATT_TILE)[:, None]
    qq = np.arange(ATT_TILE)[None, :]
    c["causal_tab"] = np.where(qq >= kk, 0.0, NEG_INF).astype(np.float32)[None, None]
    nc, ns = t // CMP_STRIDE, t // SEL_BLOCK
    n_cmp = (t - CMP_BLOCK) // CMP_STRIDE + 1
    c_start = np.arange(nc) * CMP_STRIDE
    s_start = np.arange(ns) * SEL_BLOCK
    overlap = np.clip(np.minimum(c_start[:, None] + CMP_BLOCK, s_start[None, :] + SEL_BLOCK)
                      - np.maximum(c_start[:, None], s_start[None, :]), 0, None).astype(np.float64)
    overlap[n_cmp:] = 0.0
    c["c2s"] = np.tile((overlap / CMP_BLOCK).T, (1, NSA_GROUP)).astype(np.float32)
    n = np.arange(nc)[:, None]
    q = np.arange(CMP_TQ)[None, :]
    dm = q - CMP_STRIDE * n
    dm[n_cmp:] = -(2 ** 30)
    c["dm"] = dm.astype(np.int32)
    c["bk"] = (np.arange(CMP_TQ)[None, :] // SEL_BLOCK - np.arange(ns)[:, None]).astype(np.int32)
    nrows = CMP_TQ // CMP_STRIDE + 8
    dnear = q - CMP_STRIDE * np.arange(nrows)[:, None] + (CMP_STRIDE * 8 - (CMP_BLOCK - 1))
    c["dnear_idx"] = np.clip(dnear, 0, LANES - 1).astype(np.int32)
    c["dnear_ok"] = ((dnear >= 0) & (dnear < LANES))
    c["sel_onehot"] = (np.arange(t)[:, None] // SEL_BLOCK == np.arange(ns)[None, :]).astype(np.float32)
    return c


def _cols(w, pieces):
    out = []
    for p in pieces:
        if p[0] is None:
            out.append(jnp.zeros((w.shape[0], p[1]), w.dtype))
        else:
            blk = w[:, p[0]:p[1]]
            out.append(-blk if p[2] < 0 else blk)
    return jnp.concatenate(out, axis=1)


def _rot_pieces(base, d):
    return [(base + d // 2, base + d, -1), (base, base + d // 2, 1)]


def _toeplitz(w, n):
    hh, ll = w.shape
    flat = jnp.tile(w, (1, n))[:, :n * (ll - 1)]
    return flat.reshape(hh, n, ll - 1)[:, :, :n]


def _pad_to(v, n):
    return jnp.concatenate([v, jnp.zeros((n - v.shape[0],), v.dtype)])


def _layer(xf, b, t, cst, tabs, w_in, w_out, attn_norm, ffn_norm, mla_q_a_norm, mla_w_uq,
           mla_kv_a_norm, mla_w_ukv, mla_q_norm, mla_k_norm, ret_norm, pos_k, w1_k, w2_k, pos_v,
           w1_v, w2_v, nsa_q_norm, kn_cmp, kn_sel, kn_win, ffn_w_up, ffn_conv_w, ffn_conv_b,
           ffn_w_down):
    md = _MXU_DTYPE
    o = _IN_OFF
    pieces = [(o[0], o[1], 1), (None, 64), (o[1], o[2], 1),
              (None, 64), (o[2], o[3], 1), (None, 32),
              (None, 64)] + _rot_pieces(o[2], MLA_ROPE) + [(None, 32)]
    pieces += [(o[3], o[7], 1)]
    for base in (o[3], o[4]):
        for h in range(RET_HEADS):
            pieces += _rot_pieces(base + h * RET_DK, RET_DK)
    pieces += [(o[7], o[15], 1), (None, 1280 - (o[15] - o[7]))]
    w_in_r = _cols(w_in, pieces).astype(md)
    pm, pr, pn = _in_proj(xf, attn_norm[None, :], w_in_r, (640, 1536, 1280))

    wq_pieces, wq_rot = [], []
    wkv_k, wkv_v = [], []
    for h in range(MLA_HEADS):
        qb = h * MLA_QK
        wq_pieces += [(qb, qb + MLA_QK, 1), (None, LANES - MLA_QK)]
        wq_rot += [(None, MLA_NOPE)] + _rot_pieces(qb + MLA_NOPE, MLA_ROPE) + [(None, LANES - MLA_QK)]
        kb = h * (MLA_NOPE + MLA_V)
        wkv_k += [(kb, kb + MLA_NOPE, 1), (None, LANES - MLA_NOPE)]
        wkv_v += [(kb + MLA_NOPE, kb + MLA_NOPE + MLA_V, 1), (None, LANES - MLA_V)]
    wq = _cols(mla_w_uq, wq_pieces + wq_rot)
    wq = jnp.concatenate([wq, jnp.zeros((256 - MLA_Q_RANK, wq.shape[1]), wq.dtype)], axis=0).astype(md)
    wkv = _cols(mla_w_ukv, wkv_k + wkv_v).astype(md)
    half = MLA_ROPE // 2

    def rot_gain(gv):
        return jnp.concatenate([jnp.zeros((MLA_NOPE,), gv.dtype), gv[MLA_NOPE + half:],
                                gv[MLA_NOPE:MLA_NOPE + half], jnp.zeros((LANES - MLA_QK,), gv.dtype)])

    q, k, v = _mla_prep(pm, cst["mla_cos"], cst["mla_sin"], _pad_to(mla_q_a_norm, 256)[None, :],
                        mla_kv_a_norm[None, :], wq, wkv,
                        _pad_to(mla_q_norm, LANES)[None, :], rot_gain(mla_q_norm)[None, :],
                        _pad_to(mla_k_norm, LANES)[None, :], rot_gain(mla_k_norm)[None, :], b, t)
    o_mla = _flash(q.transpose(0, 1, 3, 2), k, v.transpose(0, 1, 3, 2), tabs["causal"],
                   use_far=True, name="mla_attention")
    o_mla = o_mla.transpose(0, 3, 1, 2).reshape(b * t, MLA_HEADS * MLA_V).astype(md)

    def ret_heads(cols):
        return cols.reshape(b, t, RET_HEADS, RET_DK).transpose(0, 2, 1, 3)

    rq, rk, rv, rg, rqr, rkr = [ret_heads(pr[:, j * 256:(j + 1) * 256]) for j in range(6)]
    o_ret = _retention(rq, rk, rv, rg, rqr, rkr, cst["ret_cos"], cst["ret_sin"], cst["ret_din"],
                       cst["ret_qd"], cst["ret_kd"], cst["ret_cd"], ret_norm[None, :])
    o_ret = o_ret.transpose(0, 2, 1, 3).reshape(b * t, RET_HEADS * RET_DV)

    g_, r_, d_ = NSA_KV_HEADS, NSA_GROUP, NSA_DH
    nc, ns = t // CMP_STRIDE, t // SEL_BLOCK

    def kv_heads(c0):
        return pn[:, c0:c0 + g_ * d_].reshape(b, t, g_, d_).transpose(0, 2, 1, 3)

    nq = pn[:, 0:NSA_HEADS * d_].reshape(b, t, NSA_HEADS, d_).transpose(0, 2, 1, 3)
    k_cmp, v_cmp, k_sel, v_sel, k_win, v_win = [kv_heads(384 + 128 * j) for j in range(6)]
    gate_t = pn[:, 1152:1152 + 3 * NSA_HEADS].reshape(b, t, NSA_HEADS, 3).transpose(0, 2, 3, 1)
    gains = jnp.concatenate([jnp.tile(nsa_q_norm[None, :] * (d_ ** -0.5), (NSA_HEADS, 1)),
                             jnp.tile(kn_sel[None, :], (g_, 1)),
                             jnp.tile(kn_win[None, :], (g_, 1))], axis=0)[:, None, :]
    normed = _head_norm(jnp.concatenate([nq, k_sel, k_win], axis=1), gains)
    qt = normed[:, 0:NSA_HEADS].transpose(0, 1, 3, 2)
    ks_n = normed[:, NSA_HEADS:NSA_HEADS + g_]
    kw_n = normed[:, NSA_HEADS + g_:]

    def chunks(a):
        a2 = a.reshape(b * g_, nc, CMP_STRIDE * d_).astype(md)
        return a2, jnp.concatenate([a2[:, 1:], jnp.zeros_like(a2[:, :1])], axis=1)

    ak, bk_ = chunks(k_cmp)
    av, bv_ = chunks(v_cmp)
    pos8 = lambda p: jnp.broadcast_to(p.reshape(1, -1), (8, CMP_BLOCK * d_)).astype(md)
    k_c, v_c = _compress(ak, bk_, av, bv_, w1_k.astype(md), w2_k.astype(md), pos8(pos_k),
                         w1_v.astype(md), w2_v.astype(md), pos8(pos_v), kn_cmp[None, :])
    k_c = k_c.reshape(b, g_, nc, d_)
    v_ct = v_c.reshape(b, g_, nc, d_).transpose(0, 1, 3, 2)
    oc_t, selneg = _cmp_sel(qt, k_c, v_ct, tabs["cmp"], cst["c2s"].astype(md), cst["dm"], cst["bk"])

    q_aug = jnp.concatenate([qt, jnp.repeat(selneg, r_, axis=1)], axis=2)
    k_aug = jnp.concatenate([ks_n, jnp.broadcast_to(cst["sel_onehot"].astype(md), (b, g_, t, ns))],
                            axis=-1)
    bt = lambda a: a.astype(md).transpose(0, 1, 3, 2)
    os_t = _flash(q_aug, k_aug, bt(v_sel), tabs["sel"], use_far=True, name="nsa_selected")
    ow_t = _flash(qt, kw_n, bt(v_win), tabs["win"], use_far=False, name="nsa_window")
    o_nsa = _combine(oc_t, os_t, ow_t, gate_t)
    o_nsa = o_nsa.transpose(0, 3, 1, 2).reshape(b * t, NSA_HEADS * d_)

    mixed = jnp.concatenate([o_mla, o_ret, o_nsa], axis=-1)
    xf = _out_proj(mixed, w_out.astype(md), xf)
    return _ffn(xf, ffn_norm[None, :], ffn_w_up.astype(md), ffn_conv_w, ffn_conv_b[None, :],
                ffn_w_down.astype(md), b, t)


def _bias_tables(rel_bias, cst):
    n = ATT_TILE
    lut = rel_bias[cst["bucket"]].T
    delta = lut - rel_bias[REL_BUCKETS - 1][:, None]
    hh = delta.shape[0]
    dn = jnp.concatenate([delta, jnp.zeros((hh, n - LANES), F32)], axis=1)
    neg = jnp.full((hh, n), NEG_INF, F32)
    zero = jnp.zeros((hh, n), F32)
    diag = _toeplitz(jnp.concatenate([dn, neg], axis=1), n)
    prev_sel = _toeplitz(jnp.concatenate([zero, dn], axis=1), n)
    prev_win = _toeplitz(jnp.concatenate([neg, dn], axis=1), n)
    cmp_tab = jnp.where(cst["dnear_ok"][None], delta[:, cst["dnear_idx"]], 0.0)
    return {"sel": jnp.stack([prev_sel, diag], axis=1), "win": jnp.stack([prev_win, diag], axis=1),
            "cmp": cmp_tab, "causal": jnp.asarray(cst["causal_tab"])}


def kernel(x, w_in, w_out, attn_norm, ffn_norm, mla_q_a_norm, mla_w_uq, mla_kv_a_norm, mla_w_ukv, mla_q_norm, mla_k_norm, ret_norm, nsa_cmp_pos_k, nsa_cmp_w1_k, nsa_cmp_w2_k, nsa_cmp_pos_v, nsa_cmp_w1_v, nsa_cmp_w2_v, nsa_q_norm, nsa_k_norm_cmp, nsa_k_norm_sel, nsa_k_norm_win, rel_bias, ffn_w_up, ffn_conv_w, ffn_conv_b, ffn_w_down):
    b, t, d = x.shape
    assert d == D_MODEL and t % ATT_TILE == 0 and WINDOW == ATT_TILE
    cst = _constants(t)
    tabs = _bias_tables(rel_bias, cst)
    per_layer = (w_in, w_out, attn_norm, ffn_norm, mla_q_a_norm, mla_w_uq, mla_kv_a_norm, mla_w_ukv,
                 mla_q_norm, mla_k_norm, ret_norm, nsa_cmp_pos_k, nsa_cmp_w1_k, nsa_cmp_w2_k,
                 nsa_cmp_pos_v, nsa_cmp_w1_v, nsa_cmp_w2_v, nsa_q_norm, nsa_k_norm_cmp,
                 nsa_k_norm_sel, nsa_k_norm_win, ffn_w_up, ffn_conv_w, ffn_conv_b, ffn_w_down)
    xf = x.reshape(b * t, d)
    for l in range(w_in.shape[0]):
        xf = _layer(xf, b, t, cst, tabs, *[p[l] for p in per_layer])
    return xf.reshape(b, t, d)
```

```python
import functools
import math

import numpy as np
import jax
import jax.numpy as jnp
from jax import lax
from jax.experimental import pallas as pl
from jax.experimental.pallas import tpu as pltpu

D_MODEL = 1024
DEPTH = 2
MLA_HEADS = 6
MLA_Q_RANK = 192
MLA_KV_RANK = 128
MLA_NOPE = 64
MLA_ROPE = 32
MLA_V = 64
MLA_QK = MLA_NOPE + MLA_ROPE
RET_HEADS = 4
RET_DK = 64
RET_DV = 64
RET_CHUNK = 128
NSA_HEADS = 6
NSA_KV_HEADS = 2
NSA_GROUP = NSA_HEADS // NSA_KV_HEADS
NSA_DH = 64
CMP_BLOCK = 32
CMP_STRIDE = 16
CMP_HIDDEN = 256
SEL_BLOCK = 64
SEL_TOPK = 16
SEL_LOCAL = 2
WINDOW = 512
REL_BUCKETS = 32
REL_MAX_DIST = 128
D_FF = 2816
ROPE_BASE = 10000.0
EPS = 1e-6
NEG_INF = -1e30
FORCE = 1e9

_IN_SPLITS = (MLA_Q_RANK, MLA_KV_RANK, MLA_ROPE,
              RET_HEADS * RET_DK, RET_HEADS * RET_DK, RET_HEADS * RET_DV, RET_HEADS * RET_DV,
              NSA_HEADS * NSA_DH) + (NSA_KV_HEADS * NSA_DH,) * 6 + (3 * NSA_HEADS,)
_IN_OFF = [0] + [int(v) for v in np.cumsum(_IN_SPLITS)]
D_IN = _IN_OFF[-1]

LANES = 128
ATT_TILE = 512
CMP_TQ = 256
VMEM_LIMIT = 56 * 1024 * 1024

_MXU_DTYPE = jnp.bfloat16
F32 = jnp.float32
LOG2E = math.log2(math.e)


def _cparams(*sem):
    return pltpu.CompilerParams(dimension_semantics=sem, vmem_limit_bytes=VMEM_LIMIT)


def _dot(a, b):
    return jnp.dot(a, b, preferred_element_type=F32)


def _sigmoid(x):
    return 1.0 / (1.0 + jnp.exp(-x))


def _resident(shape):
    nd = len(shape)
    return pl.BlockSpec(shape, lambda *_: (0,) * nd, pipeline_mode=pl.Buffered(1))


def _in_proj_kernel(x_ref, g_ref, w_ref, *o_refs, widths):
    x = x_ref[...]
    y = x * lax.rsqrt(jnp.mean(x * x, axis=-1, keepdims=True) + EPS)
    xn = (y * g_ref[...]).astype(_MXU_DTYPE)
    off = 0
    for o_ref, wd in zip(o_refs, widths):
        o_ref[...] = _dot(xn, w_ref[:, off:off + wd])
        off += wd


def _in_proj(x, g, w, widths, tm=256):
    m, d = x.shape
    n = w.shape[1]
    return pl.pallas_call(
        functools.partial(_in_proj_kernel, widths=widths),
        grid=(m // tm,),
        in_specs=[pl.BlockSpec((tm, d), lambda i: (i, 0)),
                  _resident((1, d)),
                  _resident((d, n))],
        out_specs=[pl.BlockSpec((tm, wd), lambda i: (i, 0)) for wd in widths],
        out_shape=[jax.ShapeDtypeStruct((m, wd), F32) for wd in widths],
        compiler_params=_cparams("parallel"),
        name="in_proj",
    )(x, g, w)


def _mla_prep_kernel(pm_ref, cos_ref, sin_ref, gqa_ref, gkva_ref, wq_ref, wkv_ref,
                     gq_ref, gqr_ref, gk_ref, gkr_ref, q_ref, k_ref, v_ref):
    pm = pm_ref[...]
    hs = MLA_HEADS * LANES
    cq = pm[:, 0:256]
    r = lax.rsqrt(jnp.sum(cq * cq, axis=-1, keepdims=True) * (1.0 / MLA_Q_RANK) + EPS)
    qq = _dot((cq * r * gqa_ref[...]).astype(_MXU_DTYPE), wq_ref[...])
    ckv = pm[:, 256:384]
    r = lax.rsqrt(jnp.mean(ckv * ckv, axis=-1, keepdims=True) + EPS)
    kv = _dot((ckv * r * gkva_ref[...]).astype(_MXU_DTYPE), wkv_ref[...])
    kpe = pm[:, 384:512]
    kpe_rot = pm[:, 512:640]
    cos = cos_ref[...]
    sin = sin_ref[...]
    scale = MLA_QK ** -0.5 * LOG2E
    aq = cos * gq_ref[...] * scale
    bq = sin * gqr_ref[...] * scale
    ak = cos * gk_ref[...]
    bk = sin * gkr_ref[...]
    for h in range(MLA_HEADS):
        sl = slice(h * LANES, (h + 1) * LANES)
        sr = slice(hs + h * LANES, hs + (h + 1) * LANES)
        qh = qq[:, sl]
        rq = lax.rsqrt(jnp.sum(qh * qh, axis=-1, keepdims=True) * (1.0 / MLA_QK) + EPS)
        q_ref[0, h] = ((qh * aq + qq[:, sr] * bq) * rq).astype(q_ref.dtype)
        kh = kv[:, sl] + kpe
        rk = lax.rsqrt(jnp.sum(kh * kh, axis=-1, keepdims=True) * (1.0 / MLA_QK) + EPS)
        k_ref[0, h] = ((kh * ak + kpe_rot * bk) * rk).astype(k_ref.dtype)
        v_ref[0, h] = kv[:, sr][:, 0:MLA_V].astype(v_ref.dtype)


def _mla_prep(pm, cos, sin, gqa, gkva, wq, wkv, gq, gqr, gk, gkr, b, t, tm=512):
    nt = t // tm
    hs = MLA_HEADS * LANES
    vec = lambda n: _resident((1, n))
    return pl.pallas_call(
        _mla_prep_kernel,
        grid=(b, nt),
        in_specs=[pl.BlockSpec((tm, 640), lambda bi, i: (bi * nt + i, 0)),
                  pl.BlockSpec((tm, LANES), lambda bi, i: (i, 0)),
                  pl.BlockSpec((tm, LANES), lambda bi, i: (i, 0)),
                  vec(256), vec(LANES), _resident((256, 2 * hs)), _resident((LANES, 2 * hs)),
                  vec(LANES), vec(LANES), vec(LANES), vec(LANES)],
        out_specs=[pl.BlockSpec((1, MLA_HEADS, tm, LANES), lambda bi, i: (bi, 0, i, 0)),
                   pl.BlockSpec((1, MLA_HEADS, tm, LANES), lambda bi, i: (bi, 0, i, 0)),
                   pl.BlockSpec((1, MLA_HEADS, tm, MLA_V), lambda bi, i: (bi, 0, i, 0))],
        out_shape=[jax.ShapeDtypeStruct((b, MLA_HEADS, t, LANES), _MXU_DTYPE),
                   jax.ShapeDtypeStruct((b, MLA_HEADS, t, LANES), _MXU_DTYPE),
                   jax.ShapeDtypeStruct((b, MLA_HEADS, t, MLA_V), _MXU_DTYPE)],
        compiler_params=_cparams("parallel", "parallel"),
        name="mla_prep",
    )(pm, cos, sin, gqa, gkva, wq, wkv, gq, gqr, gk, gkr)


def _flash_kernel(q_ref, k_ref, v_ref, tab_ref, o_ref, *, use_far, sub, dv):
    i = pl.program_id(2)
    q = q_ref[0, 0]
    tc = ATT_TILE
    tq = q.shape[1]

    def block(start, nkeys, with_table, m, acc):
        s = _dot(k_ref[0, 0, pl.ds(start, nkeys), :], q)
        if with_table:
            s = s + tab_ref[0, 0]
        for j in range(nkeys // sub):
            sj = s[j * sub:(j + 1) * sub]
            vc = v_ref[0, 0, :, pl.ds(pl.multiple_of(start + j * sub, sub), sub)]
            m_new = jnp.maximum(m, jnp.max(sj, axis=0, keepdims=True))
            p = jnp.exp2(sj - m_new).astype(_MXU_DTYPE)
            acc = jnp.exp2(m - m_new) * acc + _dot(vc, p)
            m = m_new
        return m, acc

    m = jnp.full((1, tq), -3e38, F32)
    acc = jnp.zeros((dv + ONES_ROWS, tq), F32)
    n_far = jnp.maximum(i - 1, 0)
    if use_far:
        m, acc = lax.fori_loop(
            0, n_far // 2,
            lambda c, carry: block(pl.multiple_of(c * (2 * tc), 2 * tc), 2 * tc, False, *carry),
            (m, acc))
        m, acc = lax.fori_loop(
            0, n_far % 2,
            lambda c, carry: block(pl.multiple_of((n_far - 1) * tc, tc), tc, False, *carry),
            (m, acc))
    m, acc = block(pl.multiple_of(n_far * tc, tc), 2 * tc, True, m, acc)
    o_ref[0, 0] = acc[0:dv] / acc[dv:dv + 1]


ONES_ROWS = 16


def _with_ones_rows(vt):
    b, h, _, t = vt.shape
    extra = jnp.concatenate([jnp.ones((b, h, 1, t), vt.dtype),
                             jnp.zeros((b, h, ONES_ROWS - 1, t), vt.dtype)], axis=2)
    return jnp.concatenate([vt, extra], axis=2)


def _flash(qt, k, vt, tab, *, use_far, name, sub=128):
    b, h, dk, t = qt.shape
    hk, dv = vt.shape[1], vt.shape[2]
    va = _with_ones_rows(vt)
    dva = dv + ONES_ROWS
    rep = h // hk
    ht = tab.shape[0]
    tq = ATT_TILE
    return pl.pallas_call(
        functools.partial(_flash_kernel, use_far=use_far, sub=sub, dv=dv),
        grid=(b, h, t // tq),
        in_specs=[pl.BlockSpec((1, 1, dk, tq), lambda bi, hi, i: (bi, hi, 0, i)),
                  pl.BlockSpec((1, 1, t, dk), lambda bi, hi, i: (bi, hi // rep, 0, 0)),
                  pl.BlockSpec((1, 1, dva, t), lambda bi, hi, i: (bi, hi // rep, 0, 0)),
                  pl.BlockSpec((1, 1, 2 * tq, tq),
                               lambda bi, hi, i: (hi % ht, jnp.minimum(i, 1), 0, 0))],
        out_specs=pl.BlockSpec((1, 1, dv, tq), lambda bi, hi, i: (bi, hi, 0, i)),
        out_shape=jax.ShapeDtypeStruct((b, h, dv, t), F32),
        compiler_params=_cparams("parallel", "parallel", "parallel"),
        name=name,
    )(qt, k, va, tab)


def _ret_kernel(q_ref, k_ref, v_ref, g_ref, qr_ref, kr_ref, cos_ref, sin_ref, din_ref, qd_ref,
                kd_ref, cd_ref, gn_ref, o_ref, state_ref, *, nchunk):
    @pl.when(pl.program_id(2) == 0)
    def _():
        state_ref[...] = jnp.zeros(state_ref.shape, F32)

    c_ = RET_CHUNK
    for c in range(nchunk):
        sl = slice(c * c_, (c + 1) * c_)
        cos = cos_ref[sl, :]
        sin = sin_ref[sl, :]
        qh = q_ref[0, 0, sl, :] * cos + qr_ref[0, 0, sl, :] * sin
        kh = (k_ref[0, 0, sl, :] * cos + kr_ref[0, 0, sl, :] * sin) * (RET_DK ** -0.5)
        vb = v_ref[0, 0, sl, :].astype(_MXU_DTYPE)
        st = state_ref[...]
        inner = lax.dot_general(qh.astype(_MXU_DTYPE), kh.astype(_MXU_DTYPE),
                                (((1,), (1,)), ((), ())), preferred_element_type=F32) * din_ref[0]
        out = (_dot(inner.astype(_MXU_DTYPE), vb)
               + _dot((qh * qd_ref[0]).astype(_MXU_DTYPE), st.astype(_MXU_DTYPE)))
        kdt = (kh * kd_ref[0]).T
        state_ref[...] = st * cd_ref[0] + _dot(kdt.astype(_MXU_DTYPE), vb)
        y = out * lax.rsqrt(jnp.mean(out * out, axis=-1, keepdims=True) + EPS) * gn_ref[...]
        gg = g_ref[0, 0, sl, :]
        o_ref[0, 0, sl, :] = (gg * _sigmoid(gg) * y).astype(o_ref.dtype)


def _retention(q, k, v, g, qr, kr, cos, sin, din, qd, kd, cd, gn, tt=512):
    b, h, t, d = q.shape
    tok = pl.BlockSpec((1, 1, tt, d), lambda bi, hi, i: (bi, hi, i, 0))
    pos = pl.BlockSpec((tt, d), lambda bi, hi, i: (i, 0))
    per_head = lambda r, c: pl.BlockSpec((1, r, c), lambda bi, hi, i: (hi, 0, 0))
    return pl.pallas_call(
        functools.partial(_ret_kernel, nchunk=tt // RET_CHUNK),
        grid=(b, h, t // tt),
        in_specs=[tok, tok, tok, tok, tok, tok, pos, pos,
                  per_head(RET_CHUNK, RET_CHUNK), per_head(RET_CHUNK, d), per_head(RET_CHUNK, d),
                  per_head(1, d), _resident((1, d))],
        out_specs=tok,
        out_shape=jax.ShapeDtypeStruct((b, h, t, d), _MXU_DTYPE),
        scratch_shapes=[pltpu.VMEM((RET_DK, RET_DV), F32)],
        compiler_params=_cparams("parallel", "parallel", "arbitrary"),
        name="retention",
    )(q, k, v, g, qr, kr, cos, sin, din, qd, kd, cd, gn)


def _head_norm_kernel(x_ref, g_ref, o_ref):
    x = x_ref[0, 0]
    y = x * lax.rsqrt(jnp.mean(x * x, axis=-1, keepdims=True) + EPS)
    o_ref[0, 0] = (y * g_ref[0]).astype(o_ref.dtype)


def _head_norm(x, g, tt=2048):
    b, n, t, d = x.shape
    tt = min(tt, t)
    return pl.pallas_call(
        _head_norm_kernel,
        grid=(b, n, t // tt),
        in_specs=[pl.BlockSpec((1, 1, tt, d), lambda bi, ni, i: (bi, ni, i, 0)),
                  pl.BlockSpec((1, 1, d), lambda bi, ni, i: (ni, 0, 0))],
        out_specs=pl.BlockSpec((1, 1, tt, d), lambda bi, ni, i: (bi, ni, i, 0)),
        out_shape=jax.ShapeDtypeStruct(x.shape, _MXU_DTYPE),
        compiler_params=_cparams("parallel", "parallel", "parallel"),
        name="nsa_head_norm",
    )(x, g)


def _gelu_tanh(x):
    return 0.5 * x * (1.0 + jnp.tanh(math.sqrt(2.0 / math.pi) * (x + 0.044715 * (x * x * x))))


def _compress_kernel(ak_ref, bk_ref, av_ref, bv_ref, w1k_ref, w2k_ref, pk_ref, w1v_ref, w2v_ref,
                     pv_ref, gk_ref, kc_ref, vc_ref):
    half = CMP_STRIDE * NSA_DH

    def comp(a_ref, b_ref, w1_ref, w2_ref, p_ref):
        pb = _dot(p_ref[...], w1_ref[...])[0:1]
        hid = _dot(a_ref[0], w1_ref[0:half, :]) + _dot(b_ref[0], w1_ref[half:2 * half, :]) + pb
        return _dot(_gelu_tanh(hid).astype(_MXU_DTYPE), w2_ref[...])

    kc = comp(ak_ref, bk_ref, w1k_ref, w2k_ref, pk_ref)
    y = kc * lax.rsqrt(jnp.mean(kc * kc, axis=-1, keepdims=True) + EPS)
    kc_ref[0] = (y * gk_ref[...]).astype(kc_ref.dtype)
    vc_ref[0] = comp(av_ref, bv_ref, w1v_ref, w2v_ref, pv_ref).astype(vc_ref.dtype)


def _compress(ak, bk, av, bv, w1k, w2k, pk, w1v, w2v, pv, gk):
    n, nc, kk = ak.shape
    blk = pl.BlockSpec((1, nc, kk), lambda i: (i, 0, 0))
    out = pl.BlockSpec((1, nc, NSA_DH), lambda i: (i, 0, 0))
    w1 = _resident((2 * kk, CMP_HIDDEN))
    w2 = _resident((CMP_HIDDEN, NSA_DH))
    pp = _resident((8, 2 * kk))
    return pl.pallas_call(
        _compress_kernel,
        grid=(n,),
        in_specs=[blk, blk, blk, blk, w1, w2, pp, w1, w2, pp, _resident((1, NSA_DH))],
        out_specs=[out, out],
        out_shape=[jax.ShapeDtypeStruct((n, nc, NSA_DH), _MXU_DTYPE)] * 2,
        compiler_params=_cparams("parallel"),
        name="nsa_compress",
    )(ak, bk, av, bv, w1k, w2k, pk, w1v, w2v, pv, gk)


def _cmp_sel_kernel(q_ref, kc_ref, vct_ref, dbc_ref, c2s_ref, dm_ref, bk_ref, oc_ref, sel_ref,
                    s_ref, p_ref, *, tq, nc, ns, nrows):
    i = pl.program_id(2)
    valid = dm_ref[...] >= (CMP_BLOCK - 1) - tq * i
    kc = kc_ref[0, 0]
    vct = vct_ref[0, 0]
    row0 = pl.multiple_of(i * (tq // CMP_STRIDE), 8)
    for r in range(NSA_GROUP):
        s_ref[8:nc + 8, :] = _dot(kc, q_ref[0, r])
        s_ref[pl.ds(row0, nrows), :] = s_ref[pl.ds(row0, nrows), :] + dbc_ref[r]
        s = jnp.where(valid, s_ref[8:nc + 8, :], NEG_INF)
        e = jnp.exp2(s - jnp.max(s, axis=0, keepdims=True))
        p = jnp.where(valid, e / jnp.sum(e, axis=0, keepdims=True), 0.0)
        pb = p.astype(_MXU_DTYPE)
        p_ref[r * nc:(r + 1) * nc, :] = pb
        oc_ref[0, r] = _dot(vct, pb)
    imp = _dot(c2s_ref[...], p_ref[...])
    back = bk_ref[...] + i * (tq // SEL_BLOCK)
    jidx = lax.broadcasted_iota(jnp.int32, (ns, tq), 0)
    forced = (jidx == 0) | ((back >= 0) & (back < SEL_LOCAL))
    imp = jnp.where(forced, FORCE, imp)
    imp = jnp.where(back >= 0, imp, NEG_INF)
    jf = jidx.astype(F32)
    chosen = jnp.zeros((ns, tq), F32)
    for _ in range(min(SEL_TOPK, ns)):
        mx = jnp.max(imp, axis=0, keepdims=True)
        first = jnp.min(jnp.where(imp == mx, jf, 1e9), axis=0, keepdims=True)
        hit = jf == first
        chosen = jnp.where(hit, 1.0, chosen)
        imp = jnp.where(hit, -3e38, imp)
    sel_ref[0, 0] = jnp.where(chosen > 0.5, 0.0, NEG_INF).astype(sel_ref.dtype)


def _cmp_sel(qt, kc, vct, dbc, c2s, dm, bk):
    b, h, d, t = qt.shape
    g = kc.shape[1]
    nc = kc.shape[2]
    ns = c2s.shape[0]
    tq = dm.shape[1]
    nrows = dbc.shape[1]
    return pl.pallas_call(
        functools.partial(_cmp_sel_kernel, tq=tq, nc=nc, ns=ns, nrows=nrows),
        grid=(b, g, t // tq),
        in_specs=[pl.BlockSpec((1, NSA_GROUP, d, tq), lambda bi, gi, i: (bi, gi, 0, i)),
                  pl.BlockSpec((1, 1, nc, d), lambda bi, gi, i: (bi, gi, 0, 0)),
                  pl.BlockSpec((1, 1, d, nc), lambda bi, gi, i: (bi, gi, 0, 0)),
                  pl.BlockSpec((NSA_GROUP, nrows, tq), lambda bi, gi, i: (gi, 0, 0)),
                  _resident(c2s.shape), _resident(dm.shape), _resident(bk.shape)],
        out_specs=[pl.BlockSpec((1, NSA_GROUP, d, tq), lambda bi, gi, i: (bi, gi, 0, i)),
                   pl.BlockSpec((1, 1, ns, tq), lambda bi, gi, i: (bi, gi, 0, i))],
        out_shape=[jax.ShapeDtypeStruct((b, h, d, t), F32),
                   jax.ShapeDtypeStruct((b, g, ns, t), _MXU_DTYPE)],
        scratch_shapes=[pltpu.VMEM((nc + 8, tq), F32), pltpu.VMEM((NSA_GROUP * nc, tq), _MXU_DTYPE)],
        compiler_params=_cparams("parallel", "parallel", "parallel"),
        name="nsa_cmp_sel",
    )(qt, kc, vct, dbc, c2s, dm, bk)


def _combine_kernel(oc_ref, os_ref, ow_ref, g_ref, o_ref):
    g = _sigmoid(g_ref[0, 0])
    o_ref[0, 0] = (g[0:1] * oc_ref[0, 0] + g[1:2] * os_ref[0, 0]
                   + g[2:3] * ow_ref[0, 0]).astype(o_ref.dtype)


def _combine(oc, os_, ow, gl, tt=2048):
    b, h, d, t = oc.shape
    tt = min(tt, t)
    blk = pl.BlockSpec((1, 1, d, tt), lambda bi, hi, i: (bi, hi, 0, i))
    return pl.pallas_call(
        _combine_kernel,
        grid=(b, h, t // tt),
        in_specs=[blk, blk, blk, pl.BlockSpec((1, 1, 3, tt), lambda bi, hi, i: (bi, hi, 0, i))],
        out_specs=blk,
        out_shape=jax.ShapeDtypeStruct(oc.shape, _MXU_DTYPE),
        compiler_params=_cparams("parallel", "parallel", "parallel"),
        name="nsa_combine",
    )(oc, os_, ow, gl)


def _out_proj_kernel(a_ref, w_ref, r_ref, o_ref):
    o_ref[...] = r_ref[...] + _dot(a_ref[...], w_ref[...])


def _out_proj(a, w, res, tm=512):
    m, k = a.shape
    n = w.shape[1]
    return pl.pallas_call(
        _out_proj_kernel,
        grid=(m // tm,),
        in_specs=[pl.BlockSpec((tm, k), lambda i: (i, 0)), _resident((k, n)),
                  pl.BlockSpec((tm, n), lambda i: (i, 0))],
        out_specs=pl.BlockSpec((tm, n), lambda i: (i, 0)),
        out_shape=jax.ShapeDtypeStruct((m, n), F32),
        compiler_params=_cparams("parallel"),
        name="out_proj",
    )(a, w, res)


def _ffn_kernel(x_ref, gn_ref, wup_ref, cw_ref, cb_ref, wdn_ref, o_ref, carry_ref, hbuf_ref,
                acc_ref, *, tm, fc):
    @pl.when(pl.program_id(1) == 0)
    def _():
        carry_ref[...] = jnp.zeros(carry_ref.shape, F32)

    x = x_ref[...]
    y = x * lax.rsqrt(jnp.mean(x * x, axis=-1, keepdims=True) + EPS)
    xn = (y * gn_ref[...]).astype(_MXU_DTYPE)

    def conv(col0):
        cols = slice(col0, col0 + fc)
        h = _dot(xn, wup_ref[:, cols])
        hbuf_ref[0:8, :] = carry_ref[:, cols]
        hbuf_ref[8:tm + 8, :] = h
        carry_ref[:, cols] = h[tm - 8:tm, :]
        w = cw_ref[:, cols]
        return (h * w[2:3] + hbuf_ref[7:tm + 7, :] * w[1:2] + hbuf_ref[6:tm + 6, :] * w[0:1]
                + cb_ref[:, cols])

    for f in range(D_FF // fc):
        gate = conv(f * fc)
        up = conv(D_FF + f * fc)
        act = (gate * _sigmoid(gate) * up).astype(_MXU_DTYPE)
        contrib = _dot(act, wdn_ref[f * fc:(f + 1) * fc, :])
        if f == 0:
            acc_ref[...] = contrib
        else:
            acc_ref[...] += contrib
    o_ref[...] = x + acc_ref[...]


def _ffn(x, gn, wup, cw, cb, wdn, b, t, tm=512, fc=256):
    d = x.shape[1]
    nt = t // tm
    return pl.pallas_call(
        functools.partial(_ffn_kernel, tm=tm, fc=fc),
        grid=(b, nt),
        in_specs=[pl.BlockSpec((tm, d), lambda bi, i: (bi * nt + i, 0)),
                  _resident((1, d)), _resident(wup.shape), _resident(cw.shape),
                  _resident(cb.shape), _resident(wdn.shape)],
        out_specs=pl.BlockSpec((tm, d), lambda bi, i: (bi * nt + i, 0)),
        out_shape=jax.ShapeDtypeStruct(x.shape, F32),
        scratch_shapes=[pltpu.VMEM((8, 2 * D_FF), F32), pltpu.VMEM((tm + 8, fc), F32),
                        pltpu.VMEM((tm, d), F32)],
        compiler_params=_cparams("parallel", "arbitrary"),
        name="conv_ffn",
    )(x, gn, wup, cw, cb, wdn)


def _rope_tables(t, d):
    inv = ROPE_BASE ** (-np.arange(0, d, 2, dtype=np.float64) / d)
    ang = np.arange(t, dtype=np.float64)[:, None] * inv[None, :]
    return (np.concatenate([np.cos(ang)] * 2, axis=1), np.concatenate([np.sin(ang)] * 2, axis=1))


def _t5_bucket_np(dist):
    max_exact = REL_BUCKETS // 2
    d = np.maximum(dist, 1).astype(np.float64)
    log_b = max_exact + (np.log(d / max_exact) / math.log(REL_MAX_DIST / max_exact)
                         * (REL_BUCKETS - max_exact)).astype(np.int32)
    return np.where(dist < max_exact, dist, np.minimum(log_b, REL_BUCKETS - 1))


@functools.lru_cache(maxsize=None)
def _constants(t):
    c = {}
    cos, sin = _rope_tables(t, MLA_ROPE)
    pad = LANES - MLA_QK
    c["mla_cos"] = np.concatenate([np.ones((t, MLA_NOPE)), cos, np.ones((t, pad))], 1).astype(np.float32)
    c["mla_sin"] = np.concatenate([np.zeros((t, MLA_NOPE)), sin, np.zeros((t, pad))], 1).astype(np.float32)
    cos, sin = _rope_tables(t, RET_DK)
    c["ret_cos"], c["ret_sin"] = cos.astype(np.float32), sin.astype(np.float32)
    lg = np.log(1.0 - 2.0 ** (-5.0 - np.arange(RET_HEADS, dtype=np.float64)))
    idx = np.arange(RET_CHUNK, dtype=np.float64)
    diff = idx[:, None] - idx[None, :]
    c["ret_din"] = (np.exp(np.maximum(diff, 0.0) * lg[:, None, None]) * (diff >= 0)).astype(np.float32)
    qd = np.exp((idx + 1.0) * lg[:, None])[..., None]
    kd = np.exp((RET_CHUNK - 1.0 - idx) * lg[:, None])[..., None]
    c["ret_qd"] = np.broadcast_to(qd, (RET_HEADS, RET_CHUNK, RET_DK)).astype(np.float32)
    c["ret_kd"] = np.broadcast_to(kd, (RET_HEADS, RET_CHUNK, RET_DK)).astype(np.float32)
    c["ret_cd"] = np.broadcast_to(np.exp(RET_CHUNK * lg)[:, None, None], (RET_HEADS, 1, RET_DV)).astype(np.float32)
    c["bucket"] = _t5_bucket_np(np.arange(LANES)).astype(np.int32)
    kk = np.arange(ATT_TILE)[:, None]
    qq = np.arange(ATT_TILE)[None, :]
    causal = np.where(qq >= kk, 0.0, NEG_INF)
    c["causal_tab"] = np.stack([np.concatenate([causal, np.full_like(causal, NEG_INF)], 0),
                                np.concatenate([np.zeros_like(causal), causal], 0)])[None].astype(np.float32)
    nc, ns = t // CMP_STRIDE, t // SEL_BLOCK
    n_cmp = (t - CMP_BLOCK) // CMP_STRIDE + 1
    c_start = np.arange(nc) * CMP_STRIDE
    s_start = np.arange(ns) * SEL_BLOCK
    overlap = np.clip(np.minimum(c_start[:, None] + CMP_BLOCK, s_start[None, :] + SEL_BLOCK)
                      - np.maximum(c_start[:, None], s_start[None, :]), 0, None).astype(np.float64)
    overlap[n_cmp:] = 0.0
    c["c2s"] = np.tile((overlap / CMP_BLOCK).T, (1, NSA_GROUP)).astype(np.float32)
    n = np.arange(nc)[:, None]
    q = np.arange(CMP_TQ)[None, :]
    dm = q - CMP_STRIDE * n
    dm[n_cmp:] = -(2 ** 30)
    c["dm"] = dm.astype(np.int32)
    c["bk"] = (np.arange(CMP_TQ)[None, :] // SEL_BLOCK - np.arange(ns)[:, None]).astype(np.int32)
    nrows = CMP_TQ // CMP_STRIDE + 8
    dnear = q - CMP_STRIDE * np.arange(nrows)[:, None] + (CMP_STRIDE * 8 - (CMP_BLOCK - 1))
    c["dnear_idx"] = np.clip(dnear, 0, LANES - 1).astype(np.int32)
    c["dnear_ok"] = ((dnear >= 0) & (dnear < LANES))
    c["sel_onehot"] = (np.arange(t)[:, None] // SEL_BLOCK == np.arange(ns)[None, :]).astype(np.float32)
    return c


def _cols(w, pieces):
    out = []
    for p in pieces:
        if p[0] is None:
            out.append(jnp.zeros((w.shape[0], p[1]), w.dtype))
        else:
            blk = w[:, p[0]:p[1]]
            out.append(-blk if p[2] < 0 else blk)
    return jnp.concatenate(out, axis=1)


def _rot_pieces(base, d):
    return [(base + d // 2, base + d, -1), (base, base + d // 2, 1)]


def _toeplitz(w, n):
    hh, ll = w.shape
    flat = jnp.tile(w, (1, n))[:, :n * (ll - 1)]
    return flat.reshape(hh, n, ll - 1)[:, :, :n]


def _pad_to(v, n):
    return jnp.concatenate([v, jnp.zeros((n - v.shape[0],), v.dtype)])


def _layer(xf, b, t, cst, tabs, w_in, w_out, attn_norm, ffn_norm, mla_q_a_norm, mla_w_uq,
           mla_kv_a_norm, mla_w_ukv, mla_q_norm, mla_k_norm, ret_norm, pos_k, w1_k, w2_k, pos_v,
           w1_v, w2_v, nsa_q_norm, kn_cmp, kn_sel, kn_win, ffn_w_up, ffn_conv_w, ffn_conv_b,
           ffn_w_down):
    md = _MXU_DTYPE
    o = _IN_OFF
    pieces = [(o[0], o[1], 1), (None, 64), (o[1], o[2], 1),
              (None, 64), (o[2], o[3], 1), (None, 32),
              (None, 64)] + _rot_pieces(o[2], MLA_ROPE) + [(None, 32)]
    pieces += [(o[3], o[7], 1)]
    for base in (o[3], o[4]):
        for h in range(RET_HEADS):
            pieces += _rot_pieces(base + h * RET_DK, RET_DK)
    pieces += [(o[7], o[15], 1), (None, 1280 - (o[15] - o[7]))]
    w_in_r = _cols(w_in, pieces).astype(md)
    pm, pr, pn = _in_proj(xf, attn_norm[None, :], w_in_r, (640, 1536, 1280))

    wq_pieces, wq_rot = [], []
    wkv_k, wkv_v = [], []
    for h in range(MLA_HEADS):
        qb = h * MLA_QK
        wq_pieces += [(qb, qb + MLA_QK, 1), (None, LANES - MLA_QK)]
        wq_rot += [(None, MLA_NOPE)] + _rot_pieces(qb + MLA_NOPE, MLA_ROPE) + [(None, LANES - MLA_QK)]
        kb = h * (MLA_NOPE + MLA_V)
        wkv_k += [(kb, kb + MLA_NOPE, 1), (None, LANES - MLA_NOPE)]
        wkv_v += [(kb + MLA_NOPE, kb + MLA_NOPE + MLA_V, 1), (None, LANES - MLA_V)]
    wq = _cols(mla_w_uq, wq_pieces + wq_rot)
    wq = jnp.concatenate([wq, jnp.zeros((256 - MLA_Q_RANK, wq.shape[1]), wq.dtype)], axis=0).astype(md)
    wkv = _cols(mla_w_ukv, wkv_k + wkv_v).astype(md)
    half = MLA_ROPE // 2

    def rot_gain(gv):
        return jnp.concatenate([jnp.zeros((MLA_NOPE,), gv.dtype), gv[MLA_NOPE + half:],
                                gv[MLA_NOPE:MLA_NOPE + half], jnp.zeros((LANES - MLA_QK,), gv.dtype)])

    q, k, v = _mla_prep(pm, cst["mla_cos"], cst["mla_sin"], _pad_to(mla_q_a_norm, 256)[None, :],
                        mla_kv_a_norm[None, :], wq, wkv,
                        _pad_to(mla_q_norm, LANES)[None, :], rot_gain(mla_q_norm)[None, :],
                        _pad_to(mla_k_norm, LANES)[None, :], rot_gain(mla_k_norm)[None, :], b, t)
    o_mla = _flash(q.transpose(0, 1, 3, 2), k, v.transpose(0, 1, 3, 2), tabs["causal"],
                   use_far=True, name="mla_attention")
    o_mla = o_mla.transpose(0, 3, 1, 2).reshape(b * t, MLA_HEADS * MLA_V).astype(md)

    def ret_heads(cols):
        return cols.reshape(b, t, RET_HEADS, RET_DK).transpose(0, 2, 1, 3)

    rq, rk, rv, rg, rqr, rkr = [ret_heads(pr[:, j * 256:(j + 1) * 256]) for j in range(6)]
    o_ret = _retention(rq, rk, rv, rg, rqr, rkr, cst["ret_cos"], cst["ret_sin"], cst["ret_din"],
                       cst["ret_qd"], cst["ret_kd"], cst["ret_cd"], ret_norm[None, :])
    o_ret = o_ret.transpose(0, 2, 1, 3).reshape(b * t, RET_HEADS * RET_DV)

    g_, r_, d_ = NSA_KV_HEADS, NSA_GROUP, NSA_DH
    nc, ns = t // CMP_STRIDE, t // SEL_BLOCK

    def kv_heads(c0):
        return pn[:, c0:c0 + g_ * d_].reshape(b, t, g_, d_).transpose(0, 2, 1, 3)

    nq = pn[:, 0:NSA_HEADS * d_].reshape(b, t, NSA_HEADS, d_).transpose(0, 2, 1, 3)
    k_cmp, v_cmp, k_sel, v_sel, k_win, v_win = [kv_heads(384 + 128 * j) for j in range(6)]
    gate_t = pn[:, 1152:1152 + 3 * NSA_HEADS].reshape(b, t, NSA_HEADS, 3).transpose(0, 2, 3, 1)
    gains = jnp.concatenate([jnp.tile(nsa_q_norm[None, :] * (d_ ** -0.5 * LOG2E), (NSA_HEADS, 1)),
                             jnp.tile(kn_sel[None, :], (g_, 1)),
                             jnp.tile(kn_win[None, :], (g_, 1))], axis=0)[:, None, :]
    normed = _head_norm(jnp.concatenate([nq, k_sel, k_win], axis=1), gains)
    qt = normed[:, 0:NSA_HEADS].transpose(0, 1, 3, 2)
    ks_n = normed[:, NSA_HEADS:NSA_HEADS + g_]
    kw_n = normed[:, NSA_HEADS + g_:]

    def chunks(a):
        a2 = a.reshape(b * g_, nc, CMP_STRIDE * d_).astype(md)
        return a2, jnp.concatenate([a2[:, 1:], jnp.zeros_like(a2[:, :1])], axis=1)

    ak, bk_ = chunks(k_cmp)
    av, bv_ = chunks(v_cmp)
    pos8 = lambda p: jnp.broadcast_to(p.reshape(1, -1), (8, CMP_BLOCK * d_)).astype(md)
    k_c, v_c = _compress(ak, bk_, av, bv_, w1_k.astype(md), w2_k.astype(md), pos8(pos_k),
                         w1_v.astype(md), w2_v.astype(md), pos8(pos_v), kn_cmp[None, :])
    k_c = k_c.reshape(b, g_, nc, d_)
    v_ct = v_c.reshape(b, g_, nc, d_).transpose(0, 1, 3, 2)
    oc_t, selneg = _cmp_sel(qt, k_c, v_ct, tabs["cmp"], cst["c2s"].astype(md), cst["dm"], cst["bk"])

    q_aug = jnp.concatenate([qt, jnp.repeat(selneg, r_, axis=1)], axis=2)
    k_aug = jnp.concatenate([ks_n, jnp.broadcast_to(cst["sel_onehot"].astype(md), (b, g_, t, ns))],
                            axis=-1)
    bt = lambda a: a.astype(md).transpose(0, 1, 3, 2)
    os_t = _flash(q_aug, k_aug, bt(v_sel), tabs["sel"], use_far=True, name="nsa_selected")
    ow_t = _flash(qt, kw_n, bt(v_win), tabs["win"], use_far=False, name="nsa_window")
    o_nsa = _combine(oc_t, os_t, ow_t, gate_t)
    o_nsa = o_nsa.transpose(0, 3, 1, 2).reshape(b * t, NSA_HEADS * d_)

    mixed = jnp.concatenate([o_mla, o_ret, o_nsa], axis=-1)
    xf = _out_proj(mixed, w_out.astype(md), xf)
    return _ffn(xf, ffn_norm[None, :], ffn_w_up.astype(md), ffn_conv_w, ffn_conv_b[None, :],
                ffn_w_down.astype(md), b, t)


def _bias_tables(rel_bias, cst):
    n = ATT_TILE
    lut = rel_bias[cst["bucket"]].T
    delta = (lut - rel_bias[REL_BUCKETS - 1][:, None]) * LOG2E
    hh = delta.shape[0]
    dn = jnp.concatenate([delta, jnp.zeros((hh, n - LANES), F32)], axis=1)
    neg = jnp.full((hh, n), NEG_INF, F32)
    zero = jnp.zeros((hh, n), F32)
    diag = _toeplitz(jnp.concatenate([dn, neg], axis=1), n)
    prev_sel = _toeplitz(jnp.concatenate([zero, dn], axis=1), n)
    prev_win = _toeplitz(jnp.concatenate([neg, dn], axis=1), n)
    cmp_tab = jnp.where(cst["dnear_ok"][None], delta[:, cst["dnear_idx"]], 0.0)
    masked = jnp.full((hh, n, n), NEG_INF, F32)
    first = jnp.concatenate([diag, masked], axis=1)

    def near(prev):
        return jnp.stack([first, jnp.concatenate([prev, diag], axis=1)], axis=1)

    return {"sel": near(prev_sel), "win": near(prev_win), "cmp": cmp_tab,
            "causal": jnp.asarray(cst["causal_tab"])}


def kernel(x, w_in, w_out, attn_norm, ffn_norm, mla_q_a_norm, mla_w_uq, mla_kv_a_norm, mla_w_ukv, mla_q_norm, mla_k_norm, ret_norm, nsa_cmp_pos_k, nsa_cmp_w1_k, nsa_cmp_w2_k, nsa_cmp_pos_v, nsa_cmp_w1_v, nsa_cmp_w2_v, nsa_q_norm, nsa_k_norm_cmp, nsa_k_norm_sel, nsa_k_norm_win, rel_bias, ffn_w_up, ffn_conv_w, ffn_conv_b, ffn_w_down):
    b, t, d = x.shape
    assert d == D_MODEL and t % ATT_TILE == 0 and WINDOW == ATT_TILE
    cst = _constants(t)
    tabs = _bias_tables(rel_bias, cst)
    per_layer = (w_in, w_out, attn_norm, ffn_norm, mla_q_a_norm, mla_w_uq, mla_kv_a_norm, mla_w_ukv,
                 mla_q_norm, mla_k_norm, ret_norm, nsa_cmp_pos_k, nsa_cmp_w1_k, nsa_cmp_w2_k,
                 nsa_cmp_pos_v, nsa_cmp_w1_v, nsa_cmp_w2_v, nsa_q_norm, nsa_k_norm_cmp,
                 nsa_k_norm_sel, nsa_k_norm_win, ffn_w_up, ffn_conv_w, ffn_conv_b, ffn_w_down)
    xf = x.reshape(b * t, d)
    for l in range(w_in.shape[0]):
        xf = _layer(xf, b, t, cst, tabs, *[p[l] for p in per_layer])
    return xf.reshape(b, t, d)
```

```python
import functools
import math

import numpy as np
import jax
import jax.numpy as jnp
from jax import lax
from jax.experimental import pallas as pl
from jax.experimental.pallas import tpu as pltpu

D_MODEL = 1024
DEPTH = 2
MLA_HEADS = 6
MLA_Q_RANK = 192
MLA_KV_RANK = 128
MLA_NOPE = 64
MLA_ROPE = 32
MLA_V = 64
MLA_QK = MLA_NOPE + MLA_ROPE
RET_HEADS = 4
RET_DK = 64
RET_DV = 64
RET_CHUNK = 128
NSA_HEADS = 6
NSA_KV_HEADS = 2
NSA_GROUP = NSA_HEADS // NSA_KV_HEADS
NSA_DH = 64
CMP_BLOCK = 32
CMP_STRIDE = 16
CMP_HIDDEN = 256
SEL_BLOCK = 64
SEL_TOPK = 16
SEL_LOCAL = 2
WINDOW = 512
REL_BUCKETS = 32
REL_MAX_DIST = 128
D_FF = 2816
ROPE_BASE = 10000.0
EPS = 1e-6
NEG_INF = -1e30
FORCE = 1e9

_IN_SPLITS = (MLA_Q_RANK, MLA_KV_RANK, MLA_ROPE,
              RET_HEADS * RET_DK, RET_HEADS * RET_DK, RET_HEADS * RET_DV, RET_HEADS * RET_DV,
              NSA_HEADS * NSA_DH) + (NSA_KV_HEADS * NSA_DH,) * 6 + (3 * NSA_HEADS,)
_IN_OFF = [0] + [int(v) for v in np.cumsum(_IN_SPLITS)]
D_IN = _IN_OFF[-1]

LANES = 128
ATT_TILE = 512
CMP_TQ = 256
VMEM_LIMIT = 56 * 1024 * 1024

_MXU_DTYPE = jnp.bfloat16
F32 = jnp.float32
LOG2E = math.log2(math.e)


def _cparams(*sem):
    return pltpu.CompilerParams(dimension_semantics=sem, vmem_limit_bytes=VMEM_LIMIT)


def _dot(a, b):
    return jnp.dot(a, b, preferred_element_type=F32)


def _sigmoid(x):
    return 1.0 / (1.0 + jnp.exp(-x))


ONES_ROWS = 16


def _ones_rows(n, dtype):
    row = lax.broadcasted_iota(jnp.int32, (ONES_ROWS, n), 0)
    return jnp.where(row == 0, 1.0, 0.0).astype(dtype)


def _with_ones_rows(vt):
    b, h, _, t = vt.shape
    extra = jnp.concatenate([jnp.ones((b, h, 1, t), vt.dtype),
                             jnp.zeros((b, h, ONES_ROWS - 1, t), vt.dtype)], axis=2)
    return jnp.concatenate([vt, extra], axis=2)


def _resident(shape):
    nd = len(shape)
    return pl.BlockSpec(shape, lambda *_: (0,) * nd, pipeline_mode=pl.Buffered(1))


def _in_proj_kernel(x_ref, g_ref, w_ref, *o_refs, widths):
    x = x_ref[...]
    y = x * lax.rsqrt(jnp.mean(x * x, axis=-1, keepdims=True) + EPS)
    xn = (y * g_ref[...]).astype(_MXU_DTYPE)
    off = 0
    for o_ref, wd in zip(o_refs, widths):
        o_ref[...] = _dot(xn, w_ref[:, off:off + wd])
        off += wd


def _in_proj(x, g, w, widths, tm=256):
    m, d = x.shape
    n = w.shape[1]
    return pl.pallas_call(
        functools.partial(_in_proj_kernel, widths=widths),
        grid=(m // tm,),
        in_specs=[pl.BlockSpec((tm, d), lambda i: (i, 0)),
                  _resident((1, d)),
                  _resident((d, n))],
        out_specs=[pl.BlockSpec((tm, wd), lambda i: (i, 0)) for wd in widths],
        out_shape=[jax.ShapeDtypeStruct((m, wd), F32) for wd in widths],
        compiler_params=_cparams("parallel"),
        name="in_proj",
    )(x, g, w)


def _mla_prep_kernel(pm_ref, cos_ref, sin_ref, gqa_ref, gkva_ref, wq_ref, wkv_ref,
                     gq_ref, gqr_ref, gk_ref, gkr_ref, q_ref, k_ref, v_ref):
    pm = pm_ref[...]
    hs = MLA_HEADS * LANES
    cq = pm[:, 0:256]
    r = lax.rsqrt(jnp.sum(cq * cq, axis=-1, keepdims=True) * (1.0 / MLA_Q_RANK) + EPS)
    qq = _dot((cq * r * gqa_ref[...]).astype(_MXU_DTYPE), wq_ref[...])
    ckv = pm[:, 256:384]
    r = lax.rsqrt(jnp.mean(ckv * ckv, axis=-1, keepdims=True) + EPS)
    kv = _dot((ckv * r * gkva_ref[...]).astype(_MXU_DTYPE), wkv_ref[...])
    kpe = pm[:, 384:512]
    kpe_rot = pm[:, 512:640]
    cos = cos_ref[...]
    sin = sin_ref[...]
    scale = MLA_QK ** -0.5 * LOG2E
    aq = cos * gq_ref[...] * scale
    bq = sin * gqr_ref[...] * scale
    ak = cos * gk_ref[...]
    bk = sin * gkr_ref[...]
    for h in range(MLA_HEADS):
        sl = slice(h * LANES, (h + 1) * LANES)
        sr = slice(hs + h * LANES, hs + (h + 1) * LANES)
        qh = qq[:, sl]
        rq = lax.rsqrt(jnp.sum(qh * qh, axis=-1, keepdims=True) * (1.0 / MLA_QK) + EPS)
        q_ref[0, h] = ((qh * aq + qq[:, sr] * bq) * rq).T.astype(q_ref.dtype)
        kh = kv[:, sl] + kpe
        rk = lax.rsqrt(jnp.sum(kh * kh, axis=-1, keepdims=True) * (1.0 / MLA_QK) + EPS)
        k_ref[0, h] = ((kh * ak + kpe_rot * bk) * rk).astype(k_ref.dtype)
        v_ref[0, h, 0:MLA_V, :] = kv[:, sr].T[0:MLA_V].astype(v_ref.dtype)
        v_ref[0, h, MLA_V:MLA_V + ONES_ROWS, :] = _ones_rows(pm.shape[0], v_ref.dtype)


def _mla_prep(pm, cos, sin, gqa, gkva, wq, wkv, gq, gqr, gk, gkr, b, t, tm=512):
    nt = t // tm
    hs = MLA_HEADS * LANES
    vec = lambda n: _resident((1, n))
    return pl.pallas_call(
        _mla_prep_kernel,
        grid=(b, nt),
        in_specs=[pl.BlockSpec((tm, 640), lambda bi, i: (bi * nt + i, 0)),
                  pl.BlockSpec((tm, LANES), lambda bi, i: (i, 0)),
                  pl.BlockSpec((tm, LANES), lambda bi, i: (i, 0)),
                  vec(256), vec(LANES), _resident((256, 2 * hs)), _resident((LANES, 2 * hs)),
                  vec(LANES), vec(LANES), vec(LANES), vec(LANES)],
        out_specs=[pl.BlockSpec((1, MLA_HEADS, LANES, tm), lambda bi, i: (bi, 0, 0, i)),
                   pl.BlockSpec((1, MLA_HEADS, tm, LANES), lambda bi, i: (bi, 0, i, 0)),
                   pl.BlockSpec((1, MLA_HEADS, MLA_V + ONES_ROWS, tm), lambda bi, i: (bi, 0, 0, i))],
        out_shape=[jax.ShapeDtypeStruct((b, MLA_HEADS, LANES, t), _MXU_DTYPE),
                   jax.ShapeDtypeStruct((b, MLA_HEADS, t, LANES), _MXU_DTYPE),
                   jax.ShapeDtypeStruct((b, MLA_HEADS, MLA_V + ONES_ROWS, t), _MXU_DTYPE)],
        compiler_params=_cparams("parallel", "parallel"),
        name="mla_prep",
    )(pm, cos, sin, gqa, gkva, wq, wkv, gq, gqr, gk, gkr)


def _flash_kernel(*refs, use_far, sub, dv, extra):
    if extra:
        q_ref, qx_ref, k_ref, kx_ref, v_ref, tab_ref, o_ref = refs
    else:
        q_ref, k_ref, v_ref, tab_ref, o_ref = refs
    i = pl.program_id(2)
    q = q_ref[0, 0]
    if extra:
        q = jnp.concatenate([q, qx_ref[0, 0]], axis=0)
    tc = ATT_TILE
    tq = q.shape[1]

    def block(start, nkeys, with_table, m, acc):
        kb = k_ref[0, 0, pl.ds(start, nkeys), :]
        if extra:
            kb = jnp.concatenate([kb, kx_ref[pl.ds(start, nkeys), :]], axis=1)
        s = _dot(kb, q)
        if with_table:
            s = s + tab_ref[0, 0]
        for j in range(nkeys // sub):
            sj = s[j * sub:(j + 1) * sub]
            vc = v_ref[0, 0, :, pl.ds(pl.multiple_of(start + j * sub, sub), sub)]
            m_new = jnp.maximum(m, jnp.max(sj, axis=0, keepdims=True))
            p = jnp.exp2(sj - m_new).astype(_MXU_DTYPE)
            acc = jnp.exp2(m - m_new) * acc + _dot(vc, p)
            m = m_new
        return m, acc

    m = jnp.full((1, tq), -3e38, F32)
    acc = jnp.zeros((dv + ONES_ROWS, tq), F32)
    n_far = jnp.maximum(i - 1, 0)
    if use_far:
        m, acc = lax.fori_loop(
            0, n_far // 2,
            lambda c, carry: block(pl.multiple_of(c * (2 * tc), 2 * tc), 2 * tc, False, *carry),
            (m, acc))
        m, acc = lax.fori_loop(
            0, n_far % 2,
            lambda c, carry: block(pl.multiple_of((n_far - 1) * tc, tc), tc, False, *carry),
            (m, acc))
    m, acc = block(pl.multiple_of(n_far * tc, tc), 2 * tc, True, m, acc)
    o_ref[0, 0] = acc[0:dv] / acc[dv:dv + 1]


def _flash(qt, k, va, tab, *, use_far, name, qx=None, kx=None, sub=128):
    b, h, dk, t = qt.shape
    hk, dva = va.shape[1], va.shape[2]
    dv = dva - ONES_ROWS
    rep = h // hk
    ht = tab.shape[0]
    tq = ATT_TILE
    extra = qx is not None
    q_spec = pl.BlockSpec((1, 1, dk, tq), lambda bi, hi, i: (bi, hi, 0, i))
    k_spec = pl.BlockSpec((1, 1, t, dk), lambda bi, hi, i: (bi, hi // rep, 0, 0))
    rest = [pl.BlockSpec((1, 1, dva, t), lambda bi, hi, i: (bi, hi // rep, 0, 0)),
            pl.BlockSpec((1, 1, 2 * tq, tq), lambda bi, hi, i: (hi % ht, jnp.minimum(i, 1), 0, 0))]
    if extra:
        nx = qx.shape[2]
        in_specs = [q_spec, pl.BlockSpec((1, 1, nx, tq), lambda bi, hi, i: (bi, hi // rep, 0, i)),
                    k_spec, _resident((t, nx))] + rest
        args = (qt, qx, k, kx, va, tab)
    else:
        in_specs = [q_spec, k_spec] + rest
        args = (qt, k, va, tab)
    return pl.pallas_call(
        functools.partial(_flash_kernel, use_far=use_far, sub=sub, dv=dv, extra=extra),
        grid=(b, h, t // tq),
        in_specs=in_specs,
        out_specs=pl.BlockSpec((1, 1, dv, tq), lambda bi, hi, i: (bi, hi, 0, i)),
        out_shape=jax.ShapeDtypeStruct((b, h, dv, t), F32),
        compiler_params=_cparams("parallel", "parallel", "parallel"),
        name=name,
    )(*args)


def _ret_kernel(q_ref, k_ref, v_ref, g_ref, qr_ref, kr_ref, cos_ref, sin_ref, din_ref, qd_ref,
                kd_ref, cdm_ref, bd_ref, gn_ref, o_ref, state_ref, *, nchunk):
    @pl.when(pl.program_id(1) == 0)
    def _():
        state_ref[...] = jnp.zeros(state_ref.shape, F32)

    w = RET_HEADS * RET_DK
    lane = lax.broadcasted_iota(jnp.int32, (1, w), 1)
    heads = [(lane >= h * RET_DK) & (lane < (h + 1) * RET_DK) for h in range(RET_HEADS)]
    on_diag = bd_ref[...] > 0.5
    c_ = RET_CHUNK
    for c in range(nchunk):
        sl = slice(c * c_, (c + 1) * c_)
        cos = cos_ref[sl, :]
        sin = sin_ref[sl, :]
        qh = q_ref[sl, :] * cos + qr_ref[sl, :] * sin
        kh = (k_ref[sl, :] * cos + kr_ref[sl, :] * sin) * (RET_DK ** -0.5)
        kb = kh.astype(_MXU_DTYPE)
        vb = v_ref[sl, :].astype(_MXU_DTYPE)
        st = state_ref[...]
        out = _dot((qh * qd_ref[...]).astype(_MXU_DTYPE), st.astype(_MXU_DTYPE))
        for h in range(RET_HEADS):
            qm = jnp.where(heads[h], qh, 0.0).astype(_MXU_DTYPE)
            inner = lax.dot_general(qm, kb, (((1,), (1,)), ((), ())),
                                    preferred_element_type=F32) * din_ref[h]
            out = out + jnp.where(heads[h], _dot(inner.astype(_MXU_DTYPE), vb), 0.0)
        kdt = (kh * kd_ref[...]).T
        state_ref[...] = st * cdm_ref[...] + jnp.where(on_diag, _dot(kdt.astype(_MXU_DTYPE), vb), 0.0)
        o2 = out * out
        ms = jnp.zeros_like(out)
        for h in range(RET_HEADS):
            ssum = jnp.sum(jnp.where(heads[h], o2, 0.0), axis=-1, keepdims=True)
            ms = jnp.where(heads[h], ssum * (1.0 / RET_DV), ms)
        y = out * lax.rsqrt(ms + EPS) * gn_ref[...]
        gg = g_ref[sl, :]
        o_ref[sl, :] = (gg * _sigmoid(gg) * y).astype(o_ref.dtype)


def _retention(pr, cos, sin, din, qd, kd, cdm, bd, gn, b, t, tt=256):
    w = RET_HEADS * RET_DK
    nt = t // tt
    col = lambda j: pl.BlockSpec((tt, w), lambda bi, i: (bi * nt + i, j))
    pos = pl.BlockSpec((tt, w), lambda bi, i: (i, 0))
    return pl.pallas_call(
        functools.partial(_ret_kernel, nchunk=tt // RET_CHUNK),
        grid=(b, nt),
        in_specs=[col(0), col(1), col(2), col(3), col(4), col(5), pos, pos,
                  _resident(din.shape), _resident(qd.shape), _resident(kd.shape),
                  _resident(cdm.shape), _resident(bd.shape), _resident((1, w))],
        out_specs=pl.BlockSpec((tt, w), lambda bi, i: (bi * nt + i, 0)),
        out_shape=jax.ShapeDtypeStruct((b * t, w), _MXU_DTYPE),
        scratch_shapes=[pltpu.VMEM((w, w), F32)],
        compiler_params=_cparams("parallel", "arbitrary"),
        name="retention",
    )(pr, pr, pr, pr, pr, pr, cos, sin, din, qd, kd, cdm, bd, gn)


NSA_Q0, NSA_KC0, NSA_VC0, NSA_KS0, NSA_VS0, NSA_KW0, NSA_VW0, NSA_GATE0 = (
    0, 384, 512, 640, 896, 1024, 1280, 1408)
NSA_SLAB = 1536
GATE_STRIDE = 8


def _nsa_prep_kernel(pn_ref, gq_ref, gks_ref, gkw_ref, qt_ref, ks_ref, kw_ref, vs_ref, vw_ref,
                     gt_ref):
    tm = pn_ref.shape[0]
    d = NSA_DH
    dt = qt_ref.dtype
    xq = pn_ref[:, NSA_Q0:NSA_Q0 + NSA_HEADS * d].T
    for h in range(NSA_HEADS):
        blk = xq[h * d:(h + 1) * d]
        r = lax.rsqrt(jnp.mean(blk * blk, axis=0, keepdims=True) + EPS)
        qt_ref[0, h, 0:d, :] = (blk * r * gq_ref[...]).astype(dt)
        qt_ref[0, h, d:2 * d, :] = jnp.zeros((d, tm), dt)
    for g in range(NSA_KV_HEADS):
        for c0, g_ref, o_ref in ((NSA_KS0, gks_ref, ks_ref), (NSA_KW0, gkw_ref, kw_ref)):
            slot = pn_ref[:, c0 + g * LANES:c0 + (g + 1) * LANES]
            r = lax.rsqrt(jnp.sum(slot * slot, axis=-1, keepdims=True) * (1.0 / d) + EPS)
            o_ref[0, g] = (slot * r * g_ref[...]).astype(dt)
    for c0, o_ref in ((NSA_VS0, vs_ref), (NSA_VW0, vw_ref)):
        vt = pn_ref[:, c0:c0 + LANES].T
        for g in range(NSA_KV_HEADS):
            o_ref[0, g, 0:d, :] = vt[g * d:(g + 1) * d].astype(dt)
            o_ref[0, g, d:d + ONES_ROWS, :] = _ones_rows(tm, dt)
    gt = pn_ref[:, NSA_GATE0:NSA_GATE0 + LANES].T
    gt_ref[0] = _sigmoid(gt[0:NSA_HEADS * GATE_STRIDE])


def _nsa_prep(pn, gq, gks, gkw, b, t, tm=512):
    nt = t // tm
    d = NSA_DH
    g = NSA_KV_HEADS
    md = _MXU_DTYPE
    ch_major = lambda n, r: pl.BlockSpec((1, n, r, tm), lambda bi, i: (bi, 0, 0, i))
    natural = pl.BlockSpec((1, g, tm, LANES), lambda bi, i: (bi, 0, i, 0))
    return pl.pallas_call(
        _nsa_prep_kernel,
        grid=(b, nt),
        in_specs=[pl.BlockSpec((tm, NSA_SLAB), lambda bi, i: (bi * nt + i, 0)),
                  _resident((d, tm)), _resident((1, LANES)), _resident((1, LANES))],
        out_specs=[ch_major(NSA_HEADS, 2 * d), natural, natural,
                   ch_major(g, d + ONES_ROWS), ch_major(g, d + ONES_ROWS),
                   pl.BlockSpec((1, NSA_HEADS * GATE_STRIDE, tm), lambda bi, i: (bi, 0, i))],
        out_shape=[jax.ShapeDtypeStruct((b, NSA_HEADS, 2 * d, t), md),
                   jax.ShapeDtypeStruct((b, g, t, LANES), md),
                   jax.ShapeDtypeStruct((b, g, t, LANES), md),
                   jax.ShapeDtypeStruct((b, g, d + ONES_ROWS, t), md),
                   jax.ShapeDtypeStruct((b, g, d + ONES_ROWS, t), md),
                   jax.ShapeDtypeStruct((b, NSA_HEADS * GATE_STRIDE, t), F32)],
        compiler_params=_cparams("parallel", "parallel"),
        name="nsa_prep",
    )(pn, jnp.broadcast_to(gq[:, None], (d, tm)), gks, gkw)


def _gelu_tanh(x):
    return 0.5 * x * (1.0 + jnp.tanh(math.sqrt(2.0 / math.pi) * (x + 0.044715 * (x * x * x))))


def _compress_kernel(ak_ref, av_ref, w1k_ref, w2k_ref, pk_ref, w1v_ref, w2v_ref, pv_ref, gk_ref,
                     kc_ref, vc_ref):
    half = CMP_STRIDE * NSA_DH

    def comp(a_ref, w1_ref, w2_ref, p_ref):
        a = a_ref[0]
        pb = _dot(p_ref[...], w1_ref[...])[0:1]
        second = _dot(a, w1_ref[half:2 * half, :])
        nc = second.shape[0]
        hid = _dot(a, w1_ref[0:half, :]) + pltpu.roll(second, nc - 1, 0) + pb
        return _dot(_gelu_tanh(hid).astype(_MXU_DTYPE), w2_ref[...])

    kc = comp(ak_ref, w1k_ref, w2k_ref, pk_ref)
    y = kc * lax.rsqrt(jnp.mean(kc * kc, axis=-1, keepdims=True) + EPS)
    kc_ref[0] = (y * gk_ref[...]).astype(kc_ref.dtype)
    vc_ref[0] = comp(av_ref, w1v_ref, w2v_ref, pv_ref).astype(vc_ref.dtype)


def _compress(ak, av, w1k, w2k, pk, w1v, w2v, pv, gk):
    n, nc, kk = ak.shape
    blk = pl.BlockSpec((1, nc, kk), lambda i: (i, 0, 0))
    out = pl.BlockSpec((1, nc, NSA_DH), lambda i: (i, 0, 0))
    w1 = _resident((2 * kk, CMP_HIDDEN))
    w2 = _resident((CMP_HIDDEN, NSA_DH))
    pp = _resident((8, 2 * kk))
    return pl.pallas_call(
        _compress_kernel,
        grid=(n,),
        in_specs=[blk, blk, w1, w2, pp, w1, w2, pp, _resident((1, NSA_DH))],
        out_specs=[out, out],
        out_shape=[jax.ShapeDtypeStruct((n, nc, NSA_DH), _MXU_DTYPE)] * 2,
        compiler_params=_cparams("parallel"),
        name="nsa_compress",
    )(ak, av, w1k, w2k, pk, w1v, w2v, pv, gk)


def _cmp_sel_kernel(q_ref, kc_ref, vct_ref, dbc_ref, c2s_ref, dm_ref, bk_ref, oc_ref, sel_ref,
                    s_ref, p_ref, *, tq, nc, ns, nrows):
    i = pl.program_id(2)
    valid = dm_ref[...] >= (CMP_BLOCK - 1) - tq * i
    kc = kc_ref[0, 0]
    vct = vct_ref[0, 0]
    row0 = pl.multiple_of(i * (tq // CMP_STRIDE), 8)
    for r in range(NSA_GROUP):
        s_ref[8:nc + 8, :] = _dot(kc, q_ref[0, r])
        s_ref[pl.ds(row0, nrows), :] = s_ref[pl.ds(row0, nrows), :] + dbc_ref[r]
        s = jnp.where(valid, s_ref[8:nc + 8, :], NEG_INF)
        e = jnp.exp2(s - jnp.max(s, axis=0, keepdims=True))
        p = jnp.where(valid, e / jnp.sum(e, axis=0, keepdims=True), 0.0)
        pb = p.astype(_MXU_DTYPE)
        p_ref[r * nc:(r + 1) * nc, :] = pb
        oc_ref[0, r] = _dot(vct, pb)
    imp = _dot(c2s_ref[...], p_ref[...])
    back = bk_ref[...] + i * (tq // SEL_BLOCK)
    jidx = lax.broadcasted_iota(jnp.int32, (ns, tq), 0)
    forced = (jidx == 0) | ((back >= 0) & (back < SEL_LOCAL))
    imp = jnp.where(forced, FORCE, imp)
    imp = jnp.where(back >= 0, imp, NEG_INF)
    jf = jidx.astype(F32)
    chosen = jnp.zeros((ns, tq), F32)
    for _ in range(min(SEL_TOPK, ns)):
        mx = jnp.max(imp, axis=0, keepdims=True)
        first = jnp.min(jnp.where(imp == mx, jf, 1e9), axis=0, keepdims=True)
        hit = jf == first
        chosen = jnp.where(hit, 1.0, chosen)
        imp = jnp.where(hit, -3e38, imp)
    sel_ref[0, 0] = jnp.where(chosen > 0.5, 0.0, NEG_INF).astype(sel_ref.dtype)


def _cmp_sel(qt, kc, vct, dbc, c2s, dm, bk):
    b, h, _, t = qt.shape
    g, nc, d = kc.shape[1:]
    ns = c2s.shape[0]
    tq = dm.shape[1]
    nrows = dbc.shape[1]
    return pl.pallas_call(
        functools.partial(_cmp_sel_kernel, tq=tq, nc=nc, ns=ns, nrows=nrows),
        grid=(b, g, t // tq),
        in_specs=[pl.BlockSpec((1, NSA_GROUP, d, tq), lambda bi, gi, i: (bi, gi, 0, i)),
                  pl.BlockSpec((1, 1, nc, d), lambda bi, gi, i: (bi, gi, 0, 0)),
                  pl.BlockSpec((1, 1, d, nc), lambda bi, gi, i: (bi, gi, 0, 0)),
                  pl.BlockSpec((NSA_GROUP, nrows, tq), lambda bi, gi, i: (gi, 0, 0)),
                  _resident(c2s.shape), _resident(dm.shape), _resident(bk.shape)],
        out_specs=[pl.BlockSpec((1, NSA_GROUP, d, tq), lambda bi, gi, i: (bi, gi, 0, i)),
                   pl.BlockSpec((1, 1, ns, tq), lambda bi, gi, i: (bi, gi, 0, i))],
        out_shape=[jax.ShapeDtypeStruct((b, h, d, t), F32),
                   jax.ShapeDtypeStruct((b, g, ns, t), _MXU_DTYPE)],
        scratch_shapes=[pltpu.VMEM((nc + 8, tq), F32), pltpu.VMEM((NSA_GROUP * nc, tq), _MXU_DTYPE)],
        compiler_params=_cparams("parallel", "parallel", "parallel"),
        name="nsa_cmp_sel",
    )(qt, kc, vct, dbc, c2s, dm, bk)


def _combine_kernel(oc_ref, os_ref, ow_ref, g_ref, o_ref):
    g = g_ref[0]
    o_ref[0, 0] = (g[0:1] * oc_ref[0, 0] + g[1:2] * os_ref[0, 0]
                   + g[2:3] * ow_ref[0, 0]).astype(o_ref.dtype)


def _combine(oc, os_, ow, gates, tt=2048):
    b, h, d, t = oc.shape
    tt = min(tt, t)
    blk = pl.BlockSpec((1, 1, d, tt), lambda bi, hi, i: (bi, hi, 0, i))
    return pl.pallas_call(
        _combine_kernel,
        grid=(b, h, t // tt),
        in_specs=[blk, blk, blk, pl.BlockSpec((1, GATE_STRIDE, tt), lambda bi, hi, i: (bi, hi, i))],
        out_specs=blk,
        out_shape=jax.ShapeDtypeStruct(oc.shape, _MXU_DTYPE),
        compiler_params=_cparams("parallel", "parallel", "parallel"),
        name="nsa_combine",
    )(oc, os_, ow, gates)


def _out_proj_kernel(*refs, widths, transposed):
    a_refs, (w_ref, r_ref, o_ref) = refs[:len(widths)], refs[len(widths):]
    acc = r_ref[...]
    off = 0
    for a_ref, wd, tr in zip(a_refs, widths, transposed):
        a = a_ref[0].astype(F32).T if tr else a_ref[...]
        acc = acc + _dot(a.astype(_MXU_DTYPE), w_ref[off:off + wd, :])
        off += wd
    o_ref[...] = acc


def _out_proj(parts, w, res, t, tm=512):
    m = res.shape[0]
    n = w.shape[1]
    nt = t // tm
    transposed = tuple(a.ndim == 3 for a in parts)
    widths = tuple(a.shape[1] for a in parts)
    specs = [pl.BlockSpec((1, wd, tm), lambda i: (i // nt, 0, i % nt)) if tr
             else pl.BlockSpec((tm, wd), lambda i: (i, 0)) for wd, tr in zip(widths, transposed)]
    return pl.pallas_call(
        functools.partial(_out_proj_kernel, widths=widths, transposed=transposed),
        grid=(m // tm,),
        in_specs=specs + [_resident(w.shape), pl.BlockSpec((tm, n), lambda i: (i, 0))],
        out_specs=pl.BlockSpec((tm, n), lambda i: (i, 0)),
        out_shape=jax.ShapeDtypeStruct((m, n), F32),
        compiler_params=_cparams("parallel"),
        name="out_proj",
    )(*parts, w, res)


def _ffn_kernel(x_ref, gn_ref, wup_ref, cw_ref, cb_ref, wdn_ref, o_ref, carry_ref, hbuf_ref,
                acc_ref, *, tm, fc):
    @pl.when(pl.program_id(1) == 0)
    def _():
        carry_ref[...] = jnp.zeros(carry_ref.shape, F32)

    x = x_ref[...]
    y = x * lax.rsqrt(jnp.mean(x * x, axis=-1, keepdims=True) + EPS)
    xn = (y * gn_ref[...]).astype(_MXU_DTYPE)

    def conv(col0):
        cols = slice(col0, col0 + fc)
        h = _dot(xn, wup_ref[:, cols])
        hbuf_ref[0:8, :] = carry_ref[:, cols]
        hbuf_ref[8:tm + 8, :] = h
        carry_ref[:, cols] = h[tm - 8:tm, :]
        w = cw_ref[:, cols]
        return (h * w[2:3] + hbuf_ref[7:tm + 7, :] * w[1:2] + hbuf_ref[6:tm + 6, :] * w[0:1]
                + cb_ref[:, cols])

    for f in range(D_FF // fc):
        gate = conv(f * fc)
        up = conv(D_FF + f * fc)
        act = (gate * _sigmoid(gate) * up).astype(_MXU_DTYPE)
        contrib = _dot(act, wdn_ref[f * fc:(f + 1) * fc, :])
        if f == 0:
            acc_ref[...] = contrib
        else:
            acc_ref[...] += contrib
    o_ref[...] = x + acc_ref[...]


def _ffn(x, gn, wup, cw, cb, wdn, b, t, tm=512, fc=256):
    d = x.shape[1]
    nt = t // tm
    return pl.pallas_call(
        functools.partial(_ffn_kernel, tm=tm, fc=fc),
        grid=(b, nt),
        in_specs=[pl.BlockSpec((tm, d), lambda bi, i: (bi * nt + i, 0)),
                  _resident((1, d)), _resident(wup.shape), _resident(cw.shape),
                  _resident(cb.shape), _resident(wdn.shape)],
        out_specs=pl.BlockSpec((tm, d), lambda bi, i: (bi * nt + i, 0)),
        out_shape=jax.ShapeDtypeStruct(x.shape, F32),
        scratch_shapes=[pltpu.VMEM((8, 2 * D_FF), F32), pltpu.VMEM((tm + 8, fc), F32),
                        pltpu.VMEM((tm, d), F32)],
        compiler_params=_cparams("parallel", "arbitrary"),
        name="conv_ffn",
    )(x, gn, wup, cw, cb, wdn)


def _rope_tables(t, d):
    inv = ROPE_BASE ** (-np.arange(0, d, 2, dtype=np.float64) / d)
    ang = np.arange(t, dtype=np.float64)[:, None] * inv[None, :]
    return (np.concatenate([np.cos(ang)] * 2, axis=1), np.concatenate([np.sin(ang)] * 2, axis=1))


def _t5_bucket_np(dist):
    max_exact = REL_BUCKETS // 2
    d = np.maximum(dist, 1).astype(np.float64)
    log_b = max_exact + (np.log(d / max_exact) / math.log(REL_MAX_DIST / max_exact)
                         * (REL_BUCKETS - max_exact)).astype(np.int32)
    return np.where(dist < max_exact, dist, np.minimum(log_b, REL_BUCKETS - 1))


@functools.lru_cache(maxsize=None)
def _constants(t):
    c = {}
    cos, sin = _rope_tables(t, MLA_ROPE)
    pad = LANES - MLA_QK
    c["mla_cos"] = np.concatenate([np.ones((t, MLA_NOPE)), cos, np.ones((t, pad))], 1).astype(np.float32)
    c["mla_sin"] = np.concatenate([np.zeros((t, MLA_NOPE)), sin, np.zeros((t, pad))], 1).astype(np.float32)
    cos, sin = _rope_tables(t, RET_DK)
    c["ret_cos"] = np.tile(cos, (1, RET_HEADS)).astype(np.float32)
    c["ret_sin"] = np.tile(sin, (1, RET_HEADS)).astype(np.float32)
    lg = np.log(1.0 - 2.0 ** (-5.0 - np.arange(RET_HEADS, dtype=np.float64)))
    idx = np.arange(RET_CHUNK, dtype=np.float64)
    diff = idx[:, None] - idx[None, :]
    c["ret_din"] = (np.exp(np.maximum(diff, 0.0) * lg[:, None, None]) * (diff >= 0)).astype(np.float32)
    qd = np.exp((idx[:, None] + 1.0) * lg[None, :])
    kd = np.exp((RET_CHUNK - 1.0 - idx[:, None]) * lg[None, :])
    c["ret_qd"] = np.repeat(qd, RET_DK, axis=1).astype(np.float32)
    c["ret_kd"] = np.repeat(kd, RET_DK, axis=1).astype(np.float32)
    head_of = np.arange(RET_HEADS * RET_DK) // RET_DK
    c["ret_cdm"] = np.broadcast_to(np.exp(RET_CHUNK * lg)[head_of][:, None],
                                   (RET_HEADS * RET_DK, RET_HEADS * RET_DV)).astype(np.float32)
    c["ret_bd"] = (head_of[:, None] == head_of[None, :]).astype(np.float32)
    c["bucket"] = _t5_bucket_np(np.arange(LANES)).astype(np.int32)
    kk = np.arange(ATT_TILE)[:, None]
    qq = np.arange(ATT_TILE)[None, :]
    causal = np.where(qq >= kk, 0.0, NEG_INF)
    c["causal_tab"] = np.stack([np.concatenate([causal, np.full_like(causal, NEG_INF)], 0),
                                np.concatenate([np.zeros_like(causal), causal], 0)])[None].astype(np.float32)
    nc, ns = t // CMP_STRIDE, t // SEL_BLOCK
    n_cmp = (t - CMP_BLOCK) // CMP_STRIDE + 1
    c_start = np.arange(nc) * CMP_STRIDE
    s_start = np.arange(ns) * SEL_BLOCK
    overlap = np.clip(np.minimum(c_start[:, None] + CMP_BLOCK, s_start[None, :] + SEL_BLOCK)
                      - np.maximum(c_start[:, None], s_start[None, :]), 0, None).astype(np.float64)
    overlap[n_cmp:] = 0.0
    c["c2s"] = np.tile((overlap / CMP_BLOCK).T, (1, NSA_GROUP)).astype(np.float32)
    n = np.arange(nc)[:, None]
    q = np.arange(CMP_TQ)[None, :]
    dm = q - CMP_STRIDE * n
    dm[n_cmp:] = -(2 ** 30)
    c["dm"] = dm.astype(np.int32)
    c["bk"] = (np.arange(CMP_TQ)[None, :] // SEL_BLOCK - np.arange(ns)[:, None]).astype(np.int32)
    nrows = CMP_TQ // CMP_STRIDE + 8
    dnear = q - CMP_STRIDE * np.arange(nrows)[:, None] + (CMP_STRIDE * 8 - (CMP_BLOCK - 1))
    c["dnear_idx"] = np.clip(dnear, 0, LANES - 1).astype(np.int32)
    c["dnear_ok"] = ((dnear >= 0) & (dnear < LANES))
    c["sel_onehot"] = (np.arange(t)[:, None] // SEL_BLOCK == np.arange(ns)[None, :]).astype(np.float32)
    return c


def _cols(w, pieces):
    out = []
    for p in pieces:
        if p[0] is None:
            out.append(jnp.zeros((w.shape[0], p[1]), w.dtype))
        else:
            blk = w[:, p[0]:p[1]]
            out.append(-blk if p[2] < 0 else blk)
    return jnp.concatenate(out, axis=1)


def _rot_pieces(base, d):
    return [(base + d // 2, base + d, -1), (base, base + d // 2, 1)]


def _toeplitz(w, n):
    hh, ll = w.shape
    flat = jnp.tile(w, (1, n))[:, :n * (ll - 1)]
    return flat.reshape(hh, n, ll - 1)[:, :, :n]


def _pad_to(v, n):
    return jnp.concatenate([v, jnp.zeros((n - v.shape[0],), v.dtype)])


def _layer(xf, b, t, cst, tabs, w_in, w_out, attn_norm, ffn_norm, mla_q_a_norm, mla_w_uq,
           mla_kv_a_norm, mla_w_ukv, mla_q_norm, mla_k_norm, ret_norm, pos_k, w1_k, w2_k, pos_v,
           w1_v, w2_v, nsa_q_norm, kn_cmp, kn_sel, kn_win, ffn_w_up, ffn_conv_w, ffn_conv_b,
           ffn_w_down):
    md = _MXU_DTYPE
    o = _IN_OFF
    pieces = [(o[0], o[1], 1), (None, 64), (o[1], o[2], 1),
              (None, 64), (o[2], o[3], 1), (None, 32),
              (None, 64)] + _rot_pieces(o[2], MLA_ROPE) + [(None, 32)]
    pieces += [(o[3], o[7], 1)]
    for base in (o[3], o[4]):
        for h in range(RET_HEADS):
            pieces += _rot_pieces(base + h * RET_DK, RET_DK)
    d_ = NSA_DH
    pieces += [(o[7], o[10], 1)]
    for base in (o[10], o[12]):
        pieces += [(base, base + d_, 1), (None, LANES - d_), (base + d_, base + 2 * d_, 1),
                   (None, LANES - d_), (base + 2 * d_, base + 4 * d_, 1)]
    for h in range(NSA_HEADS):
        pieces += [(o[14] + 3 * h, o[14] + 3 * h + 3, 1), (None, GATE_STRIDE - 3)]
    pieces += [(None, LANES - NSA_HEADS * GATE_STRIDE)]
    w_in_r = _cols(w_in, pieces).astype(md)
    pm, pr, pn = _in_proj(xf, attn_norm[None, :], w_in_r, (640, 1536, NSA_SLAB))

    wq_pieces, wq_rot = [], []
    wkv_k, wkv_v = [], []
    for h in range(MLA_HEADS):
        qb = h * MLA_QK
        wq_pieces += [(qb, qb + MLA_QK, 1), (None, LANES - MLA_QK)]
        wq_rot += [(None, MLA_NOPE)] + _rot_pieces(qb + MLA_NOPE, MLA_ROPE) + [(None, LANES - MLA_QK)]
        kb = h * (MLA_NOPE + MLA_V)
        wkv_k += [(kb, kb + MLA_NOPE, 1), (None, LANES - MLA_NOPE)]
        wkv_v += [(kb + MLA_NOPE, kb + MLA_NOPE + MLA_V, 1), (None, LANES - MLA_V)]
    wq = _cols(mla_w_uq, wq_pieces + wq_rot)
    wq = jnp.concatenate([wq, jnp.zeros((256 - MLA_Q_RANK, wq.shape[1]), wq.dtype)], axis=0).astype(md)
    wkv = _cols(mla_w_ukv, wkv_k + wkv_v).astype(md)
    half = MLA_ROPE // 2

    def rot_gain(gv):
        return jnp.concatenate([jnp.zeros((MLA_NOPE,), gv.dtype), gv[MLA_NOPE + half:],
                                gv[MLA_NOPE:MLA_NOPE + half], jnp.zeros((LANES - MLA_QK,), gv.dtype)])

    q, k, v = _mla_prep(pm, cst["mla_cos"], cst["mla_sin"], _pad_to(mla_q_a_norm, 256)[None, :],
                        mla_kv_a_norm[None, :], wq, wkv,
                        _pad_to(mla_q_norm, LANES)[None, :], rot_gain(mla_q_norm)[None, :],
                        _pad_to(mla_k_norm, LANES)[None, :], rot_gain(mla_k_norm)[None, :], b, t)
    o_mla = _flash(q, k, v, tabs["causal"], use_far=True, name="mla_attention")
    o_mla = o_mla.reshape(b, MLA_HEADS * MLA_V, t)

    o_ret = _retention(pr, cst["ret_cos"], cst["ret_sin"], cst["ret_din"], cst["ret_qd"],
                       cst["ret_kd"], cst["ret_cdm"], cst["ret_bd"],
                       jnp.tile(ret_norm, RET_HEADS)[None, :], b, t)

    g_ = NSA_KV_HEADS
    nc = t // CMP_STRIDE
    qt, ks_n, kw_n, vs_a, vw_a, gates = _nsa_prep(
        pn, nsa_q_norm * (d_ ** -0.5 * LOG2E), _pad_to(kn_sel, LANES)[None, :],
        _pad_to(kn_win, LANES)[None, :], b, t)

    def chunks(c0):
        a = pn[:, c0:c0 + g_ * d_].reshape(b, t, g_, d_).transpose(0, 2, 1, 3)
        return a.reshape(b * g_, nc, CMP_STRIDE * d_).astype(md)

    pos8 = lambda p: jnp.broadcast_to(p.reshape(1, -1), (8, CMP_BLOCK * d_)).astype(md)
    k_c, v_c = _compress(chunks(NSA_KC0), chunks(NSA_VC0), w1_k.astype(md), w2_k.astype(md),
                         pos8(pos_k), w1_v.astype(md), w2_v.astype(md), pos8(pos_v), kn_cmp[None, :])
    k_c = k_c.reshape(b, g_, nc, d_)
    v_ct = v_c.reshape(b, g_, nc, d_).transpose(0, 1, 3, 2)
    oc_t, selneg = _cmp_sel(qt, k_c, v_ct, tabs["cmp"], cst["c2s"].astype(md), cst["dm"], cst["bk"])
    os_t = _flash(qt, ks_n, vs_a, tabs["sel"], use_far=True, name="nsa_selected",
                  qx=selneg, kx=cst["sel_onehot"].astype(md))
    ow_t = _flash(qt, kw_n, vw_a, tabs["win"], use_far=False, name="nsa_window")
    o_nsa = _combine(oc_t, os_t, ow_t, gates).reshape(b, NSA_HEADS * d_, t)

    xf = _out_proj([o_mla, o_ret, o_nsa], w_out.astype(md), xf, t)
    return _ffn(xf, ffn_norm[None, :], ffn_w_up.astype(md), ffn_conv_w, ffn_conv_b[None, :],
                ffn_w_down.astype(md), b, t)


def _bias_tables(rel_bias, cst):
    n = ATT_TILE
    lut = rel_bias[cst["bucket"]].T
    delta = (lut - rel_bias[REL_BUCKETS - 1][:, None]) * LOG2E
    hh = delta.shape[0]
    dn = jnp.concatenate([delta, jnp.zeros((hh, n - LANES), F32)], axis=1)
    neg = jnp.full((hh, n), NEG_INF, F32)
    zero = jnp.zeros((hh, n), F32)
    diag = _toeplitz(jnp.concatenate([dn, neg], axis=1), n)
    prev_sel = _toeplitz(jnp.concatenate([zero, dn], axis=1), n)
    prev_win = _toeplitz(jnp.concatenate([neg, dn], axis=1), n)
    cmp_tab = jnp.where(cst["dnear_ok"][None], delta[:, cst["dnear_idx"]], 0.0)
    masked = jnp.full((hh, n, n), NEG_INF, F32)
    first = jnp.concatenate([diag, masked], axis=1)

    def near(prev):
        return jnp.stack([first, jnp.concatenate([prev, diag], axis=1)], axis=1)

    return {"sel": near(prev_sel), "win": near(prev_win), "cmp": cmp_tab,
            "causal": jnp.asarray(cst["causal_tab"])}


def kernel(x, w_in, w_out, attn_norm, ffn_norm, mla_q_a_norm, mla_w_uq, mla_kv_a_norm, mla_w_ukv, mla_q_norm, mla_k_norm, ret_norm, nsa_cmp_pos_k, nsa_cmp_w1_k, nsa_cmp_w2_k, nsa_cmp_pos_v, nsa_cmp_w1_v, nsa_cmp_w2_v, nsa_q_norm, nsa_k_norm_cmp, nsa_k_norm_sel, nsa_k_norm_win, rel_bias, ffn_w_up, ffn_conv_w, ffn_conv_b, ffn_w_down):
    b, t, d = x.shape
    assert d == D_MODEL and t % ATT_TILE == 0 and WINDOW == ATT_TILE
    cst = _constants(t)
    tabs = _bias_tables(rel_bias, cst)
    per_layer = (w_in, w_out, attn_norm, ffn_norm, mla_q_a_norm, mla_w_uq, mla_kv_a_norm, mla_w_ukv,
                 mla_q_norm, mla_k_norm, ret_norm, nsa_cmp_pos_k, nsa_cmp_w1_k, nsa_cmp_w2_k,
                 nsa_cmp_pos_v, nsa_cmp_w1_v, nsa_cmp_w2_v, nsa_q_norm, nsa_k_norm_cmp,
                 nsa_k_norm_sel, nsa_k_norm_win, ffn_w_up, ffn_conv_w, ffn_conv_b, ffn_w_down)
    xf = x.reshape(b * t, d)
    for l in range(w_in.shape[0]):
        xf = _layer(xf, b, t, cst, tabs, *[p[l] for p in per_layer])
    return xf.reshape(b, t, d)
```

```python
import functools
import math

import numpy as np
import jax
import jax.numpy as jnp
from jax import lax
from jax.experimental import pallas as pl
from jax.experimental.pallas import tpu as pltpu

D_MODEL = 1024
DEPTH = 2
MLA_HEADS = 6
MLA_Q_RANK = 192
MLA_KV_RANK = 128
MLA_NOPE = 64
MLA_ROPE = 32
MLA_V = 64
MLA_QK = MLA_NOPE + MLA_ROPE
RET_HEADS = 4
RET_DK = 64
RET_DV = 64
RET_CHUNK = 128
NSA_HEADS = 6
NSA_KV_HEADS = 2
NSA_GROUP = NSA_HEADS // NSA_KV_HEADS
NSA_DH = 64
CMP_BLOCK = 32
CMP_STRIDE = 16
CMP_HIDDEN = 256
SEL_BLOCK = 64
SEL_TOPK = 16
SEL_LOCAL = 2
WINDOW = 512
REL_BUCKETS = 32
REL_MAX_DIST = 128
D_FF = 2816
ROPE_BASE = 10000.0
EPS = 1e-6
NEG_INF = -1e30
FORCE = 1e9

_IN_SPLITS = (MLA_Q_RANK, MLA_KV_RANK, MLA_ROPE,
              RET_HEADS * RET_DK, RET_HEADS * RET_DK, RET_HEADS * RET_DV, RET_HEADS * RET_DV,
              NSA_HEADS * NSA_DH) + (NSA_KV_HEADS * NSA_DH,) * 6 + (3 * NSA_HEADS,)
_IN_OFF = [0] + [int(v) for v in np.cumsum(_IN_SPLITS)]
D_IN = _IN_OFF[-1]

LANES = 128
ATT_TILE = 512
CMP_TQ = 256
VMEM_LIMIT = 56 * 1024 * 1024

_MXU_DTYPE = jnp.bfloat16
F32 = jnp.float32
LOG2E = math.log2(math.e)


def _cparams(*sem):
    return pltpu.CompilerParams(dimension_semantics=sem, vmem_limit_bytes=VMEM_LIMIT)


def _dot(a, b):
    return jnp.dot(a, b, preferred_element_type=F32)


def _sigmoid(x):
    return 1.0 / (1.0 + jnp.exp(-x))


ONES_ROWS = 16


def _ones_rows(n, dtype):
    row = lax.broadcasted_iota(jnp.int32, (ONES_ROWS, n), 0)
    return jnp.where(row == 0, 1.0, 0.0).astype(dtype)


def _with_ones_rows(vt):
    b, h, _, t = vt.shape
    extra = jnp.concatenate([jnp.ones((b, h, 1, t), vt.dtype),
                             jnp.zeros((b, h, ONES_ROWS - 1, t), vt.dtype)], axis=2)
    return jnp.concatenate([vt, extra], axis=2)


def _resident(shape):
    nd = len(shape)
    return pl.BlockSpec(shape, lambda *_: (0,) * nd, pipeline_mode=pl.Buffered(1))


def _in_proj_kernel(x_ref, g_ref, w_ref, *o_refs, widths):
    x = x_ref[...]
    y = x * lax.rsqrt(jnp.mean(x * x, axis=-1, keepdims=True) + EPS)
    xn = (y * g_ref[...]).astype(_MXU_DTYPE)
    off = 0
    for o_ref, wd in zip(o_refs, widths):
        o_ref[...] = _dot(xn, w_ref[:, off:off + wd])
        off += wd


def _in_proj(x, g, w, widths, tm=256):
    m, d = x.shape
    n = w.shape[1]
    return pl.pallas_call(
        functools.partial(_in_proj_kernel, widths=widths),
        grid=(m // tm,),
        in_specs=[pl.BlockSpec((tm, d), lambda i: (i, 0)),
                  _resident((1, d)),
                  _resident((d, n))],
        out_specs=[pl.BlockSpec((tm, wd), lambda i: (i, 0)) for wd in widths],
        out_shape=[jax.ShapeDtypeStruct((m, wd), F32) for wd in widths],
        compiler_params=_cparams("parallel"),
        name="in_proj",
    )(x, g, w)


def _mla_prep_kernel(pm_ref, cos_ref, sin_ref, gqa_ref, gkva_ref, wq_ref, wkv_ref,
                     gq_ref, gqr_ref, gk_ref, gkr_ref, q_ref, k_ref, v_ref):
    pm = pm_ref[...]
    hs = MLA_HEADS * LANES
    cq = pm[:, 0:256]
    r = lax.rsqrt(jnp.sum(cq * cq, axis=-1, keepdims=True) * (1.0 / MLA_Q_RANK) + EPS)
    qq = _dot((cq * r * gqa_ref[...]).astype(_MXU_DTYPE), wq_ref[...])
    ckv = pm[:, 256:384]
    r = lax.rsqrt(jnp.mean(ckv * ckv, axis=-1, keepdims=True) + EPS)
    kv = _dot((ckv * r * gkva_ref[...]).astype(_MXU_DTYPE), wkv_ref[...])
    kpe = pm[:, 384:512]
    kpe_rot = pm[:, 512:640]
    cos = cos_ref[...]
    sin = sin_ref[...]
    scale = MLA_QK ** -0.5 * LOG2E
    aq = cos * gq_ref[...] * scale
    bq = sin * gqr_ref[...] * scale
    ak = cos * gk_ref[...]
    bk = sin * gkr_ref[...]
    for h in range(MLA_HEADS):
        sl = slice(h * LANES, (h + 1) * LANES)
        sr = slice(hs + h * LANES, hs + (h + 1) * LANES)
        qh = qq[:, sl]
        rq = lax.rsqrt(jnp.sum(qh * qh, axis=-1, keepdims=True) * (1.0 / MLA_QK) + EPS)
        q_ref[0, h] = ((qh * aq + qq[:, sr] * bq) * rq).T.astype(q_ref.dtype)
        kh = kv[:, sl] + kpe
        rk = lax.rsqrt(jnp.sum(kh * kh, axis=-1, keepdims=True) * (1.0 / MLA_QK) + EPS)
        k_ref[0, h] = ((kh * ak + kpe_rot * bk) * rk).astype(k_ref.dtype)
        v_ref[0, h, 0:MLA_V, :] = kv[:, sr].T[0:MLA_V].astype(v_ref.dtype)
        v_ref[0, h, MLA_V:MLA_V + ONES_ROWS, :] = _ones_rows(pm.shape[0], v_ref.dtype)


def _mla_prep(pm, cos, sin, gqa, gkva, wq, wkv, gq, gqr, gk, gkr, b, t, tm=512):
    nt = t // tm
    hs = MLA_HEADS * LANES
    vec = lambda n: _resident((1, n))
    return pl.pallas_call(
        _mla_prep_kernel,
        grid=(b, nt),
        in_specs=[pl.BlockSpec((tm, 640), lambda bi, i: (bi * nt + i, 0)),
                  pl.BlockSpec((tm, LANES), lambda bi, i: (i, 0)),
                  pl.BlockSpec((tm, LANES), lambda bi, i: (i, 0)),
                  vec(256), vec(LANES), _resident((256, 2 * hs)), _resident((LANES, 2 * hs)),
                  vec(LANES), vec(LANES), vec(LANES), vec(LANES)],
        out_specs=[pl.BlockSpec((1, MLA_HEADS, LANES, tm), lambda bi, i: (bi, 0, 0, i)),
                   pl.BlockSpec((1, MLA_HEADS, tm, LANES), lambda bi, i: (bi, 0, i, 0)),
                   pl.BlockSpec((1, MLA_HEADS, MLA_V + ONES_ROWS, tm), lambda bi, i: (bi, 0, 0, i))],
        out_shape=[jax.ShapeDtypeStruct((b, MLA_HEADS, LANES, t), _MXU_DTYPE),
                   jax.ShapeDtypeStruct((b, MLA_HEADS, t, LANES), _MXU_DTYPE),
                   jax.ShapeDtypeStruct((b, MLA_HEADS, MLA_V + ONES_ROWS, t), _MXU_DTYPE)],
        compiler_params=_cparams("parallel", "parallel"),
        name="mla_prep",
    )(pm, cos, sin, gqa, gkva, wq, wkv, gq, gqr, gk, gkr)


def _flash_kernel(*refs, use_far, sub, dv, extra):
    if extra:
        q_ref, qx_ref, k_ref, kx_ref, v_ref, tab_ref, o_ref = refs
    else:
        q_ref, k_ref, v_ref, tab_ref, o_ref = refs
    i = pl.program_id(2)
    q = q_ref[0, 0]
    if extra:
        q = jnp.concatenate([q, qx_ref[0, 0]], axis=0)
    tc = ATT_TILE
    tq = q.shape[1]

    def block(start, nkeys, with_table, m, acc):
        kb = k_ref[0, 0, pl.ds(start, nkeys), :]
        if extra:
            kb = jnp.concatenate([kb, kx_ref[pl.ds(start, nkeys), :]], axis=1)
        s = _dot(kb, q)
        if with_table:
            s = s + tab_ref[0, 0]
        for j in range(nkeys // sub):
            sj = s[j * sub:(j + 1) * sub]
            vc = v_ref[0, 0, :, pl.ds(pl.multiple_of(start + j * sub, sub), sub)]
            m_new = jnp.maximum(m, jnp.max(sj, axis=0, keepdims=True))
            p = jnp.exp2(sj - m_new).astype(_MXU_DTYPE)
            acc = jnp.exp2(m - m_new) * acc + _dot(vc, p)
            m = m_new
        return m, acc

    m = jnp.full((1, tq), -3e38, F32)
    acc = jnp.zeros((dv + ONES_ROWS, tq), F32)
    n_far = jnp.maximum(i - 1, 0)
    if use_far:
        done = 0
        for width in (4, 2, 1):
            count = (n_far - done) // width
            base = done

            def body(c, carry, width=width, base=base):
                start = pl.multiple_of((base + c * width) * tc, tc)
                return block(start, width * tc, False, *carry)

            m, acc = lax.fori_loop(0, count, body, (m, acc))
            done = done + count * width
    m, acc = block(pl.multiple_of(n_far * tc, tc), 2 * tc, True, m, acc)
    o_ref[0, 0] = acc[0:dv] / acc[dv:dv + 1]


def _flash(qt, k, va, tab, *, use_far, name, qx=None, kx=None, sub=256):
    b, h, dk, t = qt.shape
    hk, dva = va.shape[1], va.shape[2]
    dv = dva - ONES_ROWS
    rep = h // hk
    ht = tab.shape[0]
    tq = ATT_TILE
    extra = qx is not None
    q_spec = pl.BlockSpec((1, 1, dk, tq), lambda bi, hi, i: (bi, hi, 0, i))
    k_spec = pl.BlockSpec((1, 1, t, dk), lambda bi, hi, i: (bi, hi // rep, 0, 0))
    rest = [pl.BlockSpec((1, 1, dva, t), lambda bi, hi, i: (bi, hi // rep, 0, 0)),
            pl.BlockSpec((1, 1, 2 * tq, tq), lambda bi, hi, i: (hi % ht, jnp.minimum(i, 1), 0, 0))]
    if extra:
        nx = qx.shape[2]
        in_specs = [q_spec, pl.BlockSpec((1, 1, nx, tq), lambda bi, hi, i: (bi, hi // rep, 0, i)),
                    k_spec, _resident((t, nx))] + rest
        args = (qt, qx, k, kx, va, tab)
    else:
        in_specs = [q_spec, k_spec] + rest
        args = (qt, k, va, tab)
    return pl.pallas_call(
        functools.partial(_flash_kernel, use_far=use_far, sub=sub, dv=dv, extra=extra),
        grid=(b, h, t // tq),
        in_specs=in_specs,
        out_specs=pl.BlockSpec((1, 1, dv, tq), lambda bi, hi, i: (bi, hi, 0, i)),
        out_shape=jax.ShapeDtypeStruct((b, h, dv, t), F32),
        compiler_params=_cparams("parallel", "parallel", "parallel"),
        name=name,
    )(*args)


def _ret_kernel(q_ref, k_ref, v_ref, g_ref, qr_ref, kr_ref, cos_ref, sin_ref, din_ref, qd_ref,
                kd_ref, cdm_ref, bd_ref, gn_ref, o_ref, state_ref, *, nchunk):
    @pl.when(pl.program_id(1) == 0)
    def _():
        state_ref[...] = jnp.zeros(state_ref.shape, F32)

    w = RET_HEADS * RET_DK
    lane = lax.broadcasted_iota(jnp.int32, (1, w), 1)
    heads = [(lane >= h * RET_DK) & (lane < (h + 1) * RET_DK) for h in range(RET_HEADS)]
    on_diag = bd_ref[...] > 0.5
    c_ = RET_CHUNK
    for c in range(nchunk):
        sl = slice(c * c_, (c + 1) * c_)
        cos = cos_ref[sl, :]
        sin = sin_ref[sl, :]
        qh = q_ref[sl, :] * cos + qr_ref[sl, :] * sin
        kh = (k_ref[sl, :] * cos + kr_ref[sl, :] * sin) * (RET_DK ** -0.5)
        kb = kh.astype(_MXU_DTYPE)
        vb = v_ref[sl, :].astype(_MXU_DTYPE)
        st = state_ref[...]
        out = _dot((qh * qd_ref[...]).astype(_MXU_DTYPE), st.astype(_MXU_DTYPE))
        for h in range(RET_HEADS):
            qm = jnp.where(heads[h], qh, 0.0).astype(_MXU_DTYPE)
            inner = lax.dot_general(qm, kb, (((1,), (1,)), ((), ())),
                                    preferred_element_type=F32) * din_ref[h]
            out = out + jnp.where(heads[h], _dot(inner.astype(_MXU_DTYPE), vb), 0.0)
        kdt = (kh * kd_ref[...]).T
        state_ref[...] = st * cdm_ref[...] + jnp.where(on_diag, _dot(kdt.astype(_MXU_DTYPE), vb), 0.0)
        o2 = out * out
        ms = jnp.zeros_like(out)
        for h in range(RET_HEADS):
            ssum = jnp.sum(jnp.where(heads[h], o2, 0.0), axis=-1, keepdims=True)
            ms = jnp.where(heads[h], ssum * (1.0 / RET_DV), ms)
        y = out * lax.rsqrt(ms + EPS) * gn_ref[...]
        gg = g_ref[sl, :]
        o_ref[sl, :] = (gg * _sigmoid(gg) * y).astype(o_ref.dtype)


def _retention(pr, cos, sin, din, qd, kd, cdm, bd, gn, b, t, tt=256):
    w = RET_HEADS * RET_DK
    nt = t // tt
    col = lambda j: pl.BlockSpec((tt, w), lambda bi, i: (bi * nt + i, j))
    pos = pl.BlockSpec((tt, w), lambda bi, i: (i, 0))
    return pl.pallas_call(
        functools.partial(_ret_kernel, nchunk=tt // RET_CHUNK),
        grid=(b, nt),
        in_specs=[col(0), col(1), col(2), col(3), col(4), col(5), pos, pos,
                  _resident(din.shape), _resident(qd.shape), _resident(kd.shape),
                  _resident(cdm.shape), _resident(bd.shape), _resident((1, w))],
        out_specs=pl.BlockSpec((tt, w), lambda bi, i: (bi * nt + i, 0)),
        out_shape=jax.ShapeDtypeStruct((b * t, w), _MXU_DTYPE),
        scratch_shapes=[pltpu.VMEM((w, w), F32)],
        compiler_params=_cparams("parallel", "arbitrary"),
        name="retention",
    )(pr, pr, pr, pr, pr, pr, cos, sin, din, qd, kd, cdm, bd, gn)


NSA_Q0, NSA_KC0, NSA_VC0, NSA_KS0, NSA_VS0, NSA_KW0, NSA_VW0, NSA_GATE0 = (
    0, 384, 512, 640, 896, 1024, 1280, 1408)
NSA_SLAB = 1536
GATE_STRIDE = 8


def _nsa_prep_kernel(pn_ref, gq_ref, gks_ref, gkw_ref, qt_ref, ks_ref, kw_ref, vs_ref, vw_ref,
                     gt_ref):
    tm = pn_ref.shape[0]
    d = NSA_DH
    dt = qt_ref.dtype
    xq = pn_ref[:, NSA_Q0:NSA_Q0 + NSA_HEADS * d].T
    for h in range(NSA_HEADS):
        blk = xq[h * d:(h + 1) * d]
        r = lax.rsqrt(jnp.mean(blk * blk, axis=0, keepdims=True) + EPS)
        qt_ref[0, h, 0:d, :] = (blk * r * gq_ref[...]).astype(dt)
        qt_ref[0, h, d:2 * d, :] = jnp.zeros((d, tm), dt)
    for g in range(NSA_KV_HEADS):
        for c0, g_ref, o_ref in ((NSA_KS0, gks_ref, ks_ref), (NSA_KW0, gkw_ref, kw_ref)):
            slot = pn_ref[:, c0 + g * LANES:c0 + (g + 1) * LANES]
            r = lax.rsqrt(jnp.sum(slot * slot, axis=-1, keepdims=True) * (1.0 / d) + EPS)
            o_ref[0, g] = (slot * r * g_ref[...]).astype(dt)
    for c0, o_ref in ((NSA_VS0, vs_ref), (NSA_VW0, vw_ref)):
        vt = pn_ref[:, c0:c0 + LANES].T
        for g in range(NSA_KV_HEADS):
            o_ref[0, g, 0:d, :] = vt[g * d:(g + 1) * d].astype(dt)
            o_ref[0, g, d:d + ONES_ROWS, :] = _ones_rows(tm, dt)
    gt = pn_ref[:, NSA_GATE0:NSA_GATE0 + LANES].T
    gt_ref[0] = _sigmoid(gt[0:NSA_HEADS * GATE_STRIDE])


def _nsa_prep(pn, gq, gks, gkw, b, t, tm=512):
    nt = t // tm
    d = NSA_DH
    g = NSA_KV_HEADS
    md = _MXU_DTYPE
    ch_major = lambda n, r: pl.BlockSpec((1, n, r, tm), lambda bi, i: (bi, 0, 0, i))
    natural = pl.BlockSpec((1, g, tm, LANES), lambda bi, i: (bi, 0, i, 0))
    return pl.pallas_call(
        _nsa_prep_kernel,
        grid=(b, nt),
        in_specs=[pl.BlockSpec((tm, NSA_SLAB), lambda bi, i: (bi * nt + i, 0)),
                  _resident((d, tm)), _resident((1, LANES)), _resident((1, LANES))],
        out_specs=[ch_major(NSA_HEADS, 2 * d), natural, natural,
                   ch_major(g, d + ONES_ROWS), ch_major(g, d + ONES_ROWS),
                   pl.BlockSpec((1, NSA_HEADS * GATE_STRIDE, tm), lambda bi, i: (bi, 0, i))],
        out_shape=[jax.ShapeDtypeStruct((b, NSA_HEADS, 2 * d, t), md),
                   jax.ShapeDtypeStruct((b, g, t, LANES), md),
                   jax.ShapeDtypeStruct((b, g, t, LANES), md),
                   jax.ShapeDtypeStruct((b, g, d + ONES_ROWS, t), md),
                   jax.ShapeDtypeStruct((b, g, d + ONES_ROWS, t), md),
                   jax.ShapeDtypeStruct((b, NSA_HEADS * GATE_STRIDE, t), F32)],
        compiler_params=_cparams("parallel", "parallel"),
        name="nsa_prep",
    )(pn, jnp.broadcast_to(gq[:, None], (d, tm)), gks, gkw)


def _gelu_tanh(x):
    return 0.5 * x * (1.0 + jnp.tanh(math.sqrt(2.0 / math.pi) * (x + 0.044715 * (x * x * x))))


def _compress_kernel(ak_ref, av_ref, w1k_ref, w2k_ref, pk_ref, w1v_ref, w2v_ref, pv_ref, gk_ref,
                     kc_ref, vc_ref):
    half = CMP_STRIDE * NSA_DH

    def comp(a_ref, w1_ref, w2_ref, p_ref):
        a = a_ref[0]
        pb = _dot(p_ref[...], w1_ref[...])[0:1]
        second = _dot(a, w1_ref[half:2 * half, :])
        nc = second.shape[0]
        hid = _dot(a, w1_ref[0:half, :]) + pltpu.roll(second, nc - 1, 0) + pb
        return _dot(_gelu_tanh(hid).astype(_MXU_DTYPE), w2_ref[...])

    kc = comp(ak_ref, w1k_ref, w2k_ref, pk_ref)
    y = kc * lax.rsqrt(jnp.mean(kc * kc, axis=-1, keepdims=True) + EPS)
    kc_ref[0] = (y * gk_ref[...]).astype(kc_ref.dtype)
    vc_ref[0] = comp(av_ref, w1v_ref, w2v_ref, pv_ref).astype(vc_ref.dtype)


def _compress(ak, av, w1k, w2k, pk, w1v, w2v, pv, gk):
    n, nc, kk = ak.shape
    blk = pl.BlockSpec((1, nc, kk), lambda i: (i, 0, 0))
    out = pl.BlockSpec((1, nc, NSA_DH), lambda i: (i, 0, 0))
    w1 = _resident((2 * kk, CMP_HIDDEN))
    w2 = _resident((CMP_HIDDEN, NSA_DH))
    pp = _resident((8, 2 * kk))
    return pl.pallas_call(
        _compress_kernel,
        grid=(n,),
        in_specs=[blk, blk, w1, w2, pp, w1, w2, pp, _resident((1, NSA_DH))],
        out_specs=[out, out],
        out_shape=[jax.ShapeDtypeStruct((n, nc, NSA_DH), _MXU_DTYPE)] * 2,
        compiler_params=_cparams("parallel"),
        name="nsa_compress",
    )(ak, av, w1k, w2k, pk, w1v, w2v, pv, gk)


def _cmp_sel_kernel(q_ref, kc_ref, vct_ref, dbc_ref, c2s_ref, dm_ref, bk_ref, oc_ref, sel_ref,
                    s_ref, p_ref, *, tq, nc, ns, nrows):
    i = pl.program_id(2)
    valid = dm_ref[...] >= (CMP_BLOCK - 1) - tq * i
    kc = kc_ref[0, 0]
    vct = vct_ref[0, 0]
    row0 = pl.multiple_of(i * (tq // CMP_STRIDE), 8)
    for r in range(NSA_GROUP):
        s_ref[8:nc + 8, :] = _dot(kc, q_ref[0, r])
        s_ref[pl.ds(row0, nrows), :] = s_ref[pl.ds(row0, nrows), :] + dbc_ref[r]
        s = jnp.where(valid, s_ref[8:nc + 8, :], NEG_INF)
        e = jnp.exp2(s - jnp.max(s, axis=0, keepdims=True))
        p = jnp.where(valid, e / jnp.sum(e, axis=0, keepdims=True), 0.0)
        pb = p.astype(_MXU_DTYPE)
        p_ref[r * nc:(r + 1) * nc, :] = pb
        oc_ref[0, r] = _dot(vct, pb)
    imp = _dot(c2s_ref[...], p_ref[...])
    back = bk_ref[...] + i * (tq // SEL_BLOCK)
    jidx = lax.broadcasted_iota(jnp.int32, (ns, tq), 0)
    forced = (jidx == 0) | ((back >= 0) & (back < SEL_LOCAL))
    imp = jnp.where(forced, FORCE, imp)
    imp = jnp.where(back >= 0, imp, NEG_INF)
    jf = jidx.astype(F32)
    chosen = jnp.zeros((ns, tq), F32)
    for _ in range(min(SEL_TOPK, ns)):
        mx = jnp.max(imp, axis=0, keepdims=True)
        first = jnp.min(jnp.where(imp == mx, jf, 1e9), axis=0, keepdims=True)
        hit = jf == first
        chosen = jnp.where(hit, 1.0, chosen)
        imp = jnp.where(hit, -3e38, imp)
    sel_ref[0, 0] = jnp.where(chosen > 0.5, 0.0, NEG_INF).astype(sel_ref.dtype)


def _cmp_sel(qt, kc, vct, dbc, c2s, dm, bk):
    b, h, _, t = qt.shape
    g, nc, d = kc.shape[1:]
    ns = c2s.shape[0]
    tq = dm.shape[1]
    nrows = dbc.shape[1]
    return pl.pallas_call(
        functools.partial(_cmp_sel_kernel, tq=tq, nc=nc, ns=ns, nrows=nrows),
        grid=(b, g, t // tq),
        in_specs=[pl.BlockSpec((1, NSA_GROUP, d, tq), lambda bi, gi, i: (bi, gi, 0, i)),
                  pl.BlockSpec((1, 1, nc, d), lambda bi, gi, i: (bi, gi, 0, 0)),
                  pl.BlockSpec((1, 1, d, nc), lambda bi, gi, i: (bi, gi, 0, 0)),
                  pl.BlockSpec((NSA_GROUP, nrows, tq), lambda bi, gi, i: (gi, 0, 0)),
                  _resident(c2s.shape), _resident(dm.shape), _resident(bk.shape)],
        out_specs=[pl.BlockSpec((1, NSA_GROUP, d, tq), lambda bi, gi, i: (bi, gi, 0, i)),
                   pl.BlockSpec((1, 1, ns, tq), lambda bi, gi, i: (bi, gi, 0, i))],
        out_shape=[jax.ShapeDtypeStruct((b, h, d, t), F32),
                   jax.ShapeDtypeStruct((b, g, ns, t), _MXU_DTYPE)],
        scratch_shapes=[pltpu.VMEM((nc + 8, tq), F32), pltpu.VMEM((NSA_GROUP * nc, tq), _MXU_DTYPE)],
        compiler_params=_cparams("parallel", "parallel", "parallel"),
        name="nsa_cmp_sel",
    )(qt, kc, vct, dbc, c2s, dm, bk)


def _combine_kernel(oc_ref, os_ref, ow_ref, g_ref, o_ref):
    g = g_ref[0]
    o_ref[0, 0] = (g[0:1] * oc_ref[0, 0] + g[1:2] * os_ref[0, 0]
                   + g[2:3] * ow_ref[0, 0]).astype(o_ref.dtype)


def _combine(oc, os_, ow, gates, tt=2048):
    b, h, d, t = oc.shape
    tt = min(tt, t)
    blk = pl.BlockSpec((1, 1, d, tt), lambda bi, hi, i: (bi, hi, 0, i))
    return pl.pallas_call(
        _combine_kernel,
        grid=(b, h, t // tt),
        in_specs=[blk, blk, blk, pl.BlockSpec((1, GATE_STRIDE, tt), lambda bi, hi, i: (bi, hi, i))],
        out_specs=blk,
        out_shape=jax.ShapeDtypeStruct(oc.shape, _MXU_DTYPE),
        compiler_params=_cparams("parallel", "parallel", "parallel"),
        name="nsa_combine",
    )(oc, os_, ow, gates)


def _out_proj_kernel(*refs, widths, transposed):
    a_refs, (w_ref, r_ref, o_ref) = refs[:len(widths)], refs[len(widths):]
    acc = r_ref[...]
    off = 0
    for a_ref, wd, tr in zip(a_refs, widths, transposed):
        a = a_ref[0].astype(F32).T if tr else a_ref[...]
        acc = acc + _dot(a.astype(_MXU_DTYPE), w_ref[off:off + wd, :])
        off += wd
    o_ref[...] = acc


def _out_proj(parts, w, res, t, tm=512):
    m = res.shape[0]
    n = w.shape[1]
    nt = t // tm
    transposed = tuple(a.ndim == 3 for a in parts)
    widths = tuple(a.shape[1] for a in parts)
    specs = [pl.BlockSpec((1, wd, tm), lambda i: (i // nt, 0, i % nt)) if tr
             else pl.BlockSpec((tm, wd), lambda i: (i, 0)) for wd, tr in zip(widths, transposed)]
    return pl.pallas_call(
        functools.partial(_out_proj_kernel, widths=widths, transposed=transposed),
        grid=(m // tm,),
        in_specs=specs + [_resident(w.shape), pl.BlockSpec((tm, n), lambda i: (i, 0))],
        out_specs=pl.BlockSpec((tm, n), lambda i: (i, 0)),
        out_shape=jax.ShapeDtypeStruct((m, n), F32),
        compiler_params=_cparams("parallel"),
        name="out_proj",
    )(*parts, w, res)


def _ffn_kernel(x_ref, gn_ref, wup_ref, cw_ref, cb_ref, wdn_ref, o_ref, carry_ref, hbuf_ref,
                acc_ref, *, tm, fc):
    @pl.when(pl.program_id(1) == 0)
    def _():
        carry_ref[...] = jnp.zeros(carry_ref.shape, F32)

    x = x_ref[...]
    y = x * lax.rsqrt(jnp.mean(x * x, axis=-1, keepdims=True) + EPS)
    xn = (y * gn_ref[...]).astype(_MXU_DTYPE)

    def conv(col0):
        cols = slice(col0, col0 + fc)
        h = _dot(xn, wup_ref[:, cols])
        hbuf_ref[0:8, :] = carry_ref[:, cols]
        hbuf_ref[8:tm + 8, :] = h
        carry_ref[:, cols] = h[tm - 8:tm, :]
        w = cw_ref[:, cols]
        return (h * w[2:3] + hbuf_ref[7:tm + 7, :] * w[1:2] + hbuf_ref[6:tm + 6, :] * w[0:1]
                + cb_ref[:, cols])

    for f in range(D_FF // fc):
        gate = conv(f * fc)
        up = conv(D_FF + f * fc)
        act = (gate * _sigmoid(gate) * up).astype(_MXU_DTYPE)
        contrib = _dot(act, wdn_ref[f * fc:(f + 1) * fc, :])
        if f == 0:
            acc_ref[...] = contrib
        else:
            acc_ref[...] += contrib
    o_ref[...] = x + acc_ref[...]


def _ffn(x, gn, wup, cw, cb, wdn, b, t, tm=512, fc=256):
    d = x.shape[1]
    nt = t // tm
    return pl.pallas_call(
        functools.partial(_ffn_kernel, tm=tm, fc=fc),
        grid=(b, nt),
        in_specs=[pl.BlockSpec((tm, d), lambda bi, i: (bi * nt + i, 0)),
                  _resident((1, d)), _resident(wup.shape), _resident(cw.shape),
                  _resident(cb.shape), _resident(wdn.shape)],
        out_specs=pl.BlockSpec((tm, d), lambda bi, i: (bi * nt + i, 0)),
        out_shape=jax.ShapeDtypeStruct(x.shape, F32),
        scratch_shapes=[pltpu.VMEM((8, 2 * D_FF), F32), pltpu.VMEM((tm + 8, fc), F32),
                        pltpu.VMEM((tm, d), F32)],
        compiler_params=_cparams("parallel", "arbitrary"),
        name="conv_ffn",
    )(x, gn, wup, cw, cb, wdn)


def _rope_tables(t, d):
    inv = ROPE_BASE ** (-np.arange(0, d, 2, dtype=np.float64) / d)
    ang = np.arange(t, dtype=np.float64)[:, None] * inv[None, :]
    return (np.concatenate([np.cos(ang)] * 2, axis=1), np.concatenate([np.sin(ang)] * 2, axis=1))


def _t5_bucket_np(dist):
    max_exact = REL_BUCKETS // 2
    d = np.maximum(dist, 1).astype(np.float64)
    log_b = max_exact + (np.log(d / max_exact) / math.log(REL_MAX_DIST / max_exact)
                         * (REL_BUCKETS - max_exact)).astype(np.int32)
    return np.where(dist < max_exact, dist, np.minimum(log_b, REL_BUCKETS - 1))


@functools.lru_cache(maxsize=None)
def _constants(t):
    c = {}
    cos, sin = _rope_tables(t, MLA_ROPE)
    pad = LANES - MLA_QK
    c["mla_cos"] = np.concatenate([np.ones((t, MLA_NOPE)), cos, np.ones((t, pad))], 1).astype(np.float32)
    c["mla_sin"] = np.concatenate([np.zeros((t, MLA_NOPE)), sin, np.zeros((t, pad))], 1).astype(np.float32)
    cos, sin = _rope_tables(t, RET_DK)
    c["ret_cos"] = np.tile(cos, (1, RET_HEADS)).astype(np.float32)
    c["ret_sin"] = np.tile(sin, (1, RET_HEADS)).astype(np.float32)
    lg = np.log(1.0 - 2.0 ** (-5.0 - np.arange(RET_HEADS, dtype=np.float64)))
    idx = np.arange(RET_CHUNK, dtype=np.float64)
    diff = idx[:, None] - idx[None, :]
    c["ret_din"] = (np.exp(np.maximum(diff, 0.0) * lg[:, None, None]) * (diff >= 0)).astype(np.float32)
    qd = np.exp((idx[:, None] + 1.0) * lg[None, :])
    kd = np.exp((RET_CHUNK - 1.0 - idx[:, None]) * lg[None, :])
    c["ret_qd"] = np.repeat(qd, RET_DK, axis=1).astype(np.float32)
    c["ret_kd"] = np.repeat(kd, RET_DK, axis=1).astype(np.float32)
    head_of = np.arange(RET_HEADS * RET_DK) // RET_DK
    c["ret_cdm"] = np.broadcast_to(np.exp(RET_CHUNK * lg)[head_of][:, None],
                                   (RET_HEADS * RET_DK, RET_HEADS * RET_DV)).astype(np.float32)
    c["ret_bd"] = (head_of[:, None] == head_of[None, :]).astype(np.float32)
    c["bucket"] = _t5_bucket_np(np.arange(LANES)).astype(np.int32)
    kk = np.arange(ATT_TILE)[:, None]
    qq = np.arange(ATT_TILE)[None, :]
    causal = np.where(qq >= kk, 0.0, NEG_INF)
    c["causal_tab"] = np.stack([np.concatenate([causal, np.full_like(causal, NEG_INF)], 0),
                                np.concatenate([np.zeros_like(causal), causal], 0)])[None].astype(np.float32)
    nc, ns = t // CMP_STRIDE, t // SEL_BLOCK
    n_cmp = (t - CMP_BLOCK) // CMP_STRIDE + 1
    c_start = np.arange(nc) * CMP_STRIDE
    s_start = np.arange(ns) * SEL_BLOCK
    overlap = np.clip(np.minimum(c_start[:, None] + CMP_BLOCK, s_start[None, :] + SEL_BLOCK)
                      - np.maximum(c_start[:, None], s_start[None, :]), 0, None).astype(np.float64)
    overlap[n_cmp:] = 0.0
    c["c2s"] = np.tile((overlap / CMP_BLOCK).T, (1, NSA_GROUP)).astype(np.float32)
    n = np.arange(nc)[:, None]
    q = np.arange(CMP_TQ)[None, :]
    dm = q - CMP_STRIDE * n
    dm[n_cmp:] = -(2 ** 30)
    c["dm"] = dm.astype(np.int32)
    c["bk"] = (np.arange(CMP_TQ)[None, :] // SEL_BLOCK - np.arange(ns)[:, None]).astype(np.int32)
    nrows = CMP_TQ // CMP_STRIDE + 8
    dnear = q - CMP_STRIDE * np.arange(nrows)[:, None] + (CMP_STRIDE * 8 - (CMP_BLOCK - 1))
    c["dnear_idx"] = np.clip(dnear, 0, LANES - 1).astype(np.int32)
    c["dnear_ok"] = ((dnear >= 0) & (dnear < LANES))
    c["sel_onehot"] = (np.arange(t)[:, None] // SEL_BLOCK == np.arange(ns)[None, :]).astype(np.float32)
    return c


def _cols(w, pieces):
    out = []
    for p in pieces:
        if p[0] is None:
            out.append(jnp.zeros((w.shape[0], p[1]), w.dtype))
        else:
            blk = w[:, p[0]:p[1]]
            out.append(-blk if p[2] < 0 else blk)
    return jnp.concatenate(out, axis=1)


def _rot_pieces(base, d):
    return [(base + d // 2, base + d, -1), (base, base + d // 2, 1)]


def _toeplitz(w, n):
    hh, ll = w.shape
    flat = jnp.tile(w, (1, n))[:, :n * (ll - 1)]
    return flat.reshape(hh, n, ll - 1)[:, :, :n]


def _pad_to(v, n):
    return jnp.concatenate([v, jnp.zeros((n - v.shape[0],), v.dtype)])


def _layer(xf, b, t, cst, tabs, w_in, w_out, attn_norm, ffn_norm, mla_q_a_norm, mla_w_uq,
           mla_kv_a_norm, mla_w_ukv, mla_q_norm, mla_k_norm, ret_norm, pos_k, w1_k, w2_k, pos_v,
           w1_v, w2_v, nsa_q_norm, kn_cmp, kn_sel, kn_win, ffn_w_up, ffn_conv_w, ffn_conv_b,
           ffn_w_down):
    md = _MXU_DTYPE
    o = _IN_OFF
    pieces = [(o[0], o[1], 1), (None, 64), (o[1], o[2], 1),
              (None, 64), (o[2], o[3], 1), (None, 32),
              (None, 64)] + _rot_pieces(o[2], MLA_ROPE) + [(None, 32)]
    pieces += [(o[3], o[7], 1)]
    for base in (o[3], o[4]):
        for h in range(RET_HEADS):
            pieces += _rot_pieces(base + h * RET_DK, RET_DK)
    d_ = NSA_DH
    pieces += [(o[7], o[10], 1)]
    for base in (o[10], o[12]):
        pieces += [(base, base + d_, 1), (None, LANES - d_), (base + d_, base + 2 * d_, 1),
                   (None, LANES - d_), (base + 2 * d_, base + 4 * d_, 1)]
    for h in range(NSA_HEADS):
        pieces += [(o[14] + 3 * h, o[14] + 3 * h + 3, 1), (None, GATE_STRIDE - 3)]
    pieces += [(None, LANES - NSA_HEADS * GATE_STRIDE)]
    w_in_r = _cols(w_in, pieces).astype(md)
    pm, pr, pn = _in_proj(xf, attn_norm[None, :], w_in_r, (640, 1536, NSA_SLAB))

    wq_pieces, wq_rot = [], []
    wkv_k, wkv_v = [], []
    for h in range(MLA_HEADS):
        qb = h * MLA_QK
        wq_pieces += [(qb, qb + MLA_QK, 1), (None, LANES - MLA_QK)]
        wq_rot += [(None, MLA_NOPE)] + _rot_pieces(qb + MLA_NOPE, MLA_ROPE) + [(None, LANES - MLA_QK)]
        kb = h * (MLA_NOPE + MLA_V)
        wkv_k += [(kb, kb + MLA_NOPE, 1), (None, LANES - MLA_NOPE)]
        wkv_v += [(kb + MLA_NOPE, kb + MLA_NOPE + MLA_V, 1), (None, LANES - MLA_V)]
    wq = _cols(mla_w_uq, wq_pieces + wq_rot)
    wq = jnp.concatenate([wq, jnp.zeros((256 - MLA_Q_RANK, wq.shape[1]), wq.dtype)], axis=0).astype(md)
    wkv = _cols(mla_w_ukv, wkv_k + wkv_v).astype(md)
    half = MLA_ROPE // 2

    def rot_gain(gv):
        return jnp.concatenate([jnp.zeros((MLA_NOPE,), gv.dtype), gv[MLA_NOPE + half:],
                                gv[MLA_NOPE:MLA_NOPE + half], jnp.zeros((LANES - MLA_QK,), gv.dtype)])

    q, k, v = _mla_prep(pm, cst["mla_cos"], cst["mla_sin"], _pad_to(mla_q_a_norm, 256)[None, :],
                        mla_kv_a_norm[None, :], wq, wkv,
                        _pad_to(mla_q_norm, LANES)[None, :], rot_gain(mla_q_norm)[None, :],
                        _pad_to(mla_k_norm, LANES)[None, :], rot_gain(mla_k_norm)[None, :], b, t)
    o_mla = _flash(q, k, v, tabs["causal"], use_far=True, name="mla_attention")
    o_mla = o_mla.reshape(b, MLA_HEADS * MLA_V, t)

    o_ret = _retention(pr, cst["ret_cos"], cst["ret_sin"], cst["ret_din"], cst["ret_qd"],
                       cst["ret_kd"], cst["ret_cdm"], cst["ret_bd"],
                       jnp.tile(ret_norm, RET_HEADS)[None, :], b, t)

    g_ = NSA_KV_HEADS
    nc = t // CMP_STRIDE
    qt, ks_n, kw_n, vs_a, vw_a, gates = _nsa_prep(
        pn, nsa_q_norm * (d_ ** -0.5 * LOG2E), _pad_to(kn_sel, LANES)[None, :],
        _pad_to(kn_win, LANES)[None, :], b, t)

    def chunks(c0):
        a = pn[:, c0:c0 + g_ * d_].reshape(b, t, g_, d_).transpose(0, 2, 1, 3)
        return a.reshape(b * g_, nc, CMP_STRIDE * d_).astype(md)

    pos8 = lambda p: jnp.broadcast_to(p.reshape(1, -1), (8, CMP_BLOCK * d_)).astype(md)
    k_c, v_c = _compress(chunks(NSA_KC0), chunks(NSA_VC0), w1_k.astype(md), w2_k.astype(md),
                         pos8(pos_k), w1_v.astype(md), w2_v.astype(md), pos8(pos_v), kn_cmp[None, :])
    k_c = k_c.reshape(b, g_, nc, d_)
    v_ct = v_c.reshape(b, g_, nc, d_).transpose(0, 1, 3, 2)
    oc_t, selneg = _cmp_sel(qt, k_c, v_ct, tabs["cmp"], cst["c2s"].astype(md), cst["dm"], cst["bk"])
    os_t = _flash(qt, ks_n, vs_a, tabs["sel"], use_far=True, name="nsa_selected",
                  qx=selneg, kx=cst["sel_onehot"].astype(md))
    ow_t = _flash(qt, kw_n, vw_a, tabs["win"], use_far=False, name="nsa_window")
    o_nsa = _combine(oc_t, os_t, ow_t, gates).reshape(b, NSA_HEADS * d_, t)

    xf = _out_proj([o_mla, o_ret, o_nsa], w_out.astype(md), xf, t)
    return _ffn(xf, ffn_norm[None, :], ffn_w_up.astype(md), ffn_conv_w, ffn_conv_b[None, :],
                ffn_w_down.astype(md), b, t)


def _bias_tables(rel_bias, cst):
    n = ATT_TILE
    lut = rel_bias[cst["bucket"]].T
    delta = (lut - rel_bias[REL_BUCKETS - 1][:, None]) * LOG2E
    hh = delta.shape[0]
    dn = jnp.concatenate([delta, jnp.zeros((hh, n - LANES), F32)], axis=1)
    neg = jnp.full((hh, n), NEG_INF, F32)
    zero = jnp.zeros((hh, n), F32)
    diag = _toeplitz(jnp.concatenate([dn, neg], axis=1), n)
    prev_sel = _toeplitz(jnp.concatenate([zero, dn], axis=1), n)
    prev_win = _toeplitz(jnp.concatenate([neg, dn], axis=1), n)
    cmp_tab = jnp.where(cst["dnear_ok"][None], delta[:, cst["dnear_idx"]], 0.0)
    masked = jnp.full((hh, n, n), NEG_INF, F32)
    first = jnp.concatenate([diag, masked], axis=1)

    def near(prev):
        return jnp.stack([first, jnp.concatenate([prev, diag], axis=1)], axis=1)

    return {"sel": near(prev_sel), "win": near(prev_win), "cmp": cmp_tab,
            "causal": jnp.asarray(cst["causal_tab"])}


def kernel(x, w_in, w_out, attn_norm, ffn_norm, mla_q_a_norm, mla_w_uq, mla_kv_a_norm, mla_w_ukv, mla_q_norm, mla_k_norm, ret_norm, nsa_cmp_pos_k, nsa_cmp_w1_k, nsa_cmp_w2_k, nsa_cmp_pos_v, nsa_cmp_w1_v, nsa_cmp_w2_v, nsa_q_norm, nsa_k_norm_cmp, nsa_k_norm_sel, nsa_k_norm_win, rel_bias, ffn_w_up, ffn_conv_w, ffn_conv_b, ffn_w_down):
    b, t, d = x.shape
    assert d == D_MODEL and t % ATT_TILE == 0 and WINDOW == ATT_TILE
    cst = _constants(t)
    tabs = _bias_tables(rel_bias, cst)
    per_layer = (w_in, w_out, attn_norm, ffn_norm, mla_q_a_norm, mla_w_uq, mla_kv_a_norm, mla_w_ukv,
                 mla_q_norm, mla_k_norm, ret_norm, nsa_cmp_pos_k, nsa_cmp_w1_k, nsa_cmp_w2_k,
                 nsa_cmp_pos_v, nsa_cmp_w1_v, nsa_cmp_w2_v, nsa_q_norm, nsa_k_norm_cmp,
                 nsa_k_norm_sel, nsa_k_norm_win, ffn_w_up, ffn_conv_w, ffn_conv_b, ffn_w_down)
    xf = x.reshape(b * t, d)
    for l in range(w_in.shape[0]):
        xf = _layer(xf, b, t, cst, tabs, *[p[l] for p in per_layer])
    return xf.reshape(b, t, d)
```

```python
import functools
import math

import numpy as np
import jax
import jax.numpy as jnp
from jax import lax
from jax.experimental import pallas as pl
from jax.experimental.pallas import tpu as pltpu

D_MODEL = 1024
DEPTH = 2
MLA_HEADS = 6
MLA_Q_RANK = 192
MLA_KV_RANK = 128
MLA_NOPE = 64
MLA_ROPE = 32
MLA_V = 64
MLA_QK = MLA_NOPE + MLA_ROPE
RET_HEADS = 4
RET_DK = 64
RET_DV = 64
RET_CHUNK = 128
NSA_HEADS = 6
NSA_KV_HEADS = 2
NSA_GROUP = NSA_HEADS // NSA_KV_HEADS
NSA_DH = 64
CMP_BLOCK = 32
CMP_STRIDE = 16
CMP_HIDDEN = 256
SEL_BLOCK = 64
SEL_TOPK = 16
SEL_LOCAL = 2
WINDOW = 512
REL_BUCKETS = 32
REL_MAX_DIST = 128
D_FF = 2816
ROPE_BASE = 10000.0
EPS = 1e-6
NEG_INF = -1e30
FORCE = 1e9

_IN_SPLITS = (MLA_Q_RANK, MLA_KV_RANK, MLA_ROPE,
              RET_HEADS * RET_DK, RET_HEADS * RET_DK, RET_HEADS * RET_DV, RET_HEADS * RET_DV,
              NSA_HEADS * NSA_DH) + (NSA_KV_HEADS * NSA_DH,) * 6 + (3 * NSA_HEADS,)
_IN_OFF = [0] + [int(v) for v in np.cumsum(_IN_SPLITS)]
D_IN = _IN_OFF[-1]

LANES = 128
ATT_TILE = 512
CMP_TQ = 256
VMEM_LIMIT = 56 * 1024 * 1024

_MXU_DTYPE = jnp.bfloat16
F32 = jnp.float32
LOG2E = math.log2(math.e)


def _cparams(*sem):
    return pltpu.CompilerParams(dimension_semantics=sem, vmem_limit_bytes=VMEM_LIMIT)


def _dot(a, b):
    return jnp.dot(a, b, preferred_element_type=F32)


def _sigmoid(x):
    return 1.0 / (1.0 + jnp.exp(-x))


ONES_ROWS = 16


def _ones_rows(n, dtype):
    row = lax.broadcasted_iota(jnp.int32, (ONES_ROWS, n), 0)
    return jnp.where(row == 0, 1.0, 0.0).astype(dtype)


def _with_ones_rows(vt):
    b, h, _, t = vt.shape
    extra = jnp.concatenate([jnp.ones((b, h, 1, t), vt.dtype),
                             jnp.zeros((b, h, ONES_ROWS - 1, t), vt.dtype)], axis=2)
    return jnp.concatenate([vt, extra], axis=2)


def _resident(shape):
    nd = len(shape)
    return pl.BlockSpec(shape, lambda *_: (0,) * nd, pipeline_mode=pl.Buffered(1))


def _in_proj_kernel(x_ref, g_ref, w_ref, *o_refs, widths):
    x = x_ref[...]
    y = x * lax.rsqrt(jnp.mean(x * x, axis=-1, keepdims=True) + EPS)
    xn = (y * g_ref[...]).astype(_MXU_DTYPE)
    off = 0
    for o_ref, wd in zip(o_refs, widths):
        o_ref[...] = _dot(xn, w_ref[:, off:off + wd])
        off += wd


def _in_proj(x, g, w, widths, tm=256):
    m, d = x.shape
    n = w.shape[1]
    return pl.pallas_call(
        functools.partial(_in_proj_kernel, widths=widths),
        grid=(m // tm,),
        in_specs=[pl.BlockSpec((tm, d), lambda i: (i, 0)),
                  _resident((1, d)),
                  _resident((d, n))],
        out_specs=[pl.BlockSpec((tm, wd), lambda i: (i, 0)) for wd in widths],
        out_shape=[jax.ShapeDtypeStruct((m, wd), F32) for wd in widths],
        compiler_params=_cparams("parallel"),
        name="in_proj",
    )(x, g, w)


def _mla_prep_kernel(pm_ref, cos_ref, sin_ref, gqa_ref, gkva_ref, wq_ref, wkv_ref,
                     gq_ref, gqr_ref, gk_ref, gkr_ref, q_ref, k_ref, v_ref):
    pm = pm_ref[...]
    hs = MLA_HEADS * LANES
    cq = pm[:, 0:256]
    r = lax.rsqrt(jnp.sum(cq * cq, axis=-1, keepdims=True) * (1.0 / MLA_Q_RANK) + EPS)
    qq = _dot((cq * r * gqa_ref[...]).astype(_MXU_DTYPE), wq_ref[...])
    ckv = pm[:, 256:384]
    r = lax.rsqrt(jnp.mean(ckv * ckv, axis=-1, keepdims=True) + EPS)
    kv = _dot((ckv * r * gkva_ref[...]).astype(_MXU_DTYPE), wkv_ref[...])
    kpe = pm[:, 384:512]
    kpe_rot = pm[:, 512:640]
    cos = cos_ref[...]
    sin = sin_ref[...]
    scale = MLA_QK ** -0.5 * LOG2E
    aq = cos * gq_ref[...] * scale
    bq = sin * gqr_ref[...] * scale
    ak = cos * gk_ref[...]
    bk = sin * gkr_ref[...]
    for h in range(MLA_HEADS):
        sl = slice(h * LANES, (h + 1) * LANES)
        sr = slice(hs + h * LANES, hs + (h + 1) * LANES)
        qh = qq[:, sl]
        rq = lax.rsqrt(jnp.sum(qh * qh, axis=-1, keepdims=True) * (1.0 / MLA_QK) + EPS)
        q_ref[0, h] = ((qh * aq + qq[:, sr] * bq) * rq).T.astype(q_ref.dtype)
        kh = kv[:, sl] + kpe
        rk = lax.rsqrt(jnp.sum(kh * kh, axis=-1, keepdims=True) * (1.0 / MLA_QK) + EPS)
        k_ref[0, h] = ((kh * ak + kpe_rot * bk) * rk).astype(k_ref.dtype)
        v_ref[0, h, 0:MLA_V, :] = kv[:, sr].T[0:MLA_V].astype(v_ref.dtype)
        v_ref[0, h, MLA_V:MLA_V + ONES_ROWS, :] = _ones_rows(pm.shape[0], v_ref.dtype)


def _mla_prep(pm, cos, sin, gqa, gkva, wq, wkv, gq, gqr, gk, gkr, b, t, tm=512):
    nt = t // tm
    hs = MLA_HEADS * LANES
    vec = lambda n: _resident((1, n))
    return pl.pallas_call(
        _mla_prep_kernel,
        grid=(b, nt),
        in_specs=[pl.BlockSpec((tm, 640), lambda bi, i: (bi * nt + i, 0)),
                  pl.BlockSpec((tm, LANES), lambda bi, i: (i, 0)),
                  pl.BlockSpec((tm, LANES), lambda bi, i: (i, 0)),
                  vec(256), vec(LANES), _resident((256, 2 * hs)), _resident((LANES, 2 * hs)),
                  vec(LANES), vec(LANES), vec(LANES), vec(LANES)],
        out_specs=[pl.BlockSpec((1, MLA_HEADS, LANES, tm), lambda bi, i: (bi, 0, 0, i)),
                   pl.BlockSpec((1, MLA_HEADS, tm, LANES), lambda bi, i: (bi, 0, i, 0)),
                   pl.BlockSpec((1, MLA_HEADS, MLA_V + ONES_ROWS, tm), lambda bi, i: (bi, 0, 0, i))],
        out_shape=[jax.ShapeDtypeStruct((b, MLA_HEADS, LANES, t), _MXU_DTYPE),
                   jax.ShapeDtypeStruct((b, MLA_HEADS, t, LANES), _MXU_DTYPE),
                   jax.ShapeDtypeStruct((b, MLA_HEADS, MLA_V + ONES_ROWS, t), _MXU_DTYPE)],
        compiler_params=_cparams("parallel", "parallel"),
        name="mla_prep",
    )(pm, cos, sin, gqa, gkva, wq, wkv, gq, gqr, gk, gkr)


def _flash_kernel(*refs, use_far, sub, dv, extra):
    if extra:
        q_ref, qx_ref, k_ref, kx_ref, v_ref, tab_ref, o_ref = refs
    else:
        q_ref, k_ref, v_ref, tab_ref, o_ref = refs
    i = pl.program_id(2)
    q = q_ref[0, 0]
    if extra:
        q = jnp.concatenate([q, qx_ref[0, 0]], axis=0)
    tc = ATT_TILE
    tq = q.shape[1]

    def block(start, nkeys, with_table, m, acc):
        kb = k_ref[0, 0, pl.ds(start, nkeys), :]
        if extra:
            kb = jnp.concatenate([kb, kx_ref[pl.ds(start, nkeys), :]], axis=1)
        s = _dot(kb, q)
        if with_table:
            s = s + tab_ref[0, 0]
        for j in range(nkeys // sub):
            sj = s[j * sub:(j + 1) * sub]
            vc = v_ref[0, 0, :, pl.ds(pl.multiple_of(start + j * sub, sub), sub)]
            m_new = jnp.maximum(m, jnp.max(sj, axis=0, keepdims=True))
            p = jnp.exp2(sj - m_new).astype(_MXU_DTYPE)
            acc = jnp.exp2(m - m_new) * acc + _dot(vc, p)
            m = m_new
        return m, acc

    m = jnp.full((1, tq), -3e38, F32)
    acc = jnp.zeros((dv + ONES_ROWS, tq), F32)
    n_far = jnp.maximum(i - 1, 0)
    if use_far:
        done = 0
        for width in (4, 2, 1):
            count = (n_far - done) // width
            base = done

            def body(c, carry, width=width, base=base):
                start = pl.multiple_of((base + c * width) * tc, tc)
                return block(start, width * tc, False, *carry)

            m, acc = lax.fori_loop(0, count, body, (m, acc))
            done = done + count * width
    m, acc = block(pl.multiple_of(n_far * tc, tc), 2 * tc, True, m, acc)
    o_ref[0, 0] = acc[0:dv] / acc[dv:dv + 1]


def _flash(qt, k, va, tab, *, use_far, name, qx=None, kx=None, sub=256):
    b, h, dk, t = qt.shape
    hk, dva = va.shape[1], va.shape[2]
    dv = dva - ONES_ROWS
    rep = h // hk
    ht = tab.shape[0]
    tq = ATT_TILE
    extra = qx is not None
    q_spec = pl.BlockSpec((1, 1, dk, tq), lambda bi, hi, i: (bi, hi, 0, i))
    k_spec = pl.BlockSpec((1, 1, t, dk), lambda bi, hi, i: (bi, hi // rep, 0, 0))
    rest = [pl.BlockSpec((1, 1, dva, t), lambda bi, hi, i: (bi, hi // rep, 0, 0)),
            pl.BlockSpec((1, 1, 2 * tq, tq), lambda bi, hi, i: (hi % ht, jnp.minimum(i, 1), 0, 0))]
    if extra:
        nx = qx.shape[2]
        in_specs = [q_spec, pl.BlockSpec((1, 1, nx, tq), lambda bi, hi, i: (bi, hi // rep, 0, i)),
                    k_spec, _resident((t, nx))] + rest
        args = (qt, qx, k, kx, va, tab)
    else:
        in_specs = [q_spec, k_spec] + rest
        args = (qt, k, va, tab)
    return pl.pallas_call(
        functools.partial(_flash_kernel, use_far=use_far, sub=sub, dv=dv, extra=extra),
        grid=(b, h, t // tq),
        in_specs=in_specs,
        out_specs=pl.BlockSpec((1, 1, dv, tq), lambda bi, hi, i: (bi, hi, 0, i)),
        out_shape=jax.ShapeDtypeStruct((b, h, dv, t), F32),
        compiler_params=_cparams("parallel", "parallel", "parallel"),
        name=name,
    )(*args)


def _ret_kernel(q_ref, k_ref, v_ref, g_ref, qr_ref, kr_ref, cos_ref, sin_ref, din_ref, qd_ref,
                kd_ref, cdm_ref, bd_ref, gn_ref, o_ref, state_ref, *, nchunk):
    @pl.when(pl.program_id(1) == 0)
    def _():
        state_ref[...] = jnp.zeros(state_ref.shape, F32)

    w = RET_HEADS * RET_DK
    lane = lax.broadcasted_iota(jnp.int32, (1, w), 1)
    heads = [(lane >= h * RET_DK) & (lane < (h + 1) * RET_DK) for h in range(RET_HEADS)]
    on_diag = bd_ref[...] > 0.5
    c_ = RET_CHUNK
    for c in range(nchunk):
        sl = slice(c * c_, (c + 1) * c_)
        cos = cos_ref[sl, :]
        sin = sin_ref[sl, :]
        qh = q_ref[sl, :] * cos + qr_ref[sl, :] * sin
        kh = (k_ref[sl, :] * cos + kr_ref[sl, :] * sin) * (RET_DK ** -0.5)
        kb = kh.astype(_MXU_DTYPE)
        vb = v_ref[sl, :].astype(_MXU_DTYPE)
        st = state_ref[...]
        out = _dot((qh * qd_ref[...]).astype(_MXU_DTYPE), st.astype(_MXU_DTYPE))
        for h in range(RET_HEADS):
            qm = jnp.where(heads[h], qh, 0.0).astype(_MXU_DTYPE)
            inner = lax.dot_general(qm, kb, (((1,), (1,)), ((), ())),
                                    preferred_element_type=F32) * din_ref[h]
            out = out + jnp.where(heads[h], _dot(inner.astype(_MXU_DTYPE), vb), 0.0)
        kdt = (kh * kd_ref[...]).T
        state_ref[...] = st * cdm_ref[...] + jnp.where(on_diag, _dot(kdt.astype(_MXU_DTYPE), vb), 0.0)
        o2 = out * out
        ms = jnp.zeros_like(out)
        for h in range(RET_HEADS):
            ssum = jnp.sum(jnp.where(heads[h], o2, 0.0), axis=-1, keepdims=True)
            ms = jnp.where(heads[h], ssum * (1.0 / RET_DV), ms)
        y = out * lax.rsqrt(ms + EPS) * gn_ref[...]
        gg = g_ref[sl, :]
        o_ref[sl, :] = (gg * _sigmoid(gg) * y).astype(o_ref.dtype)


def _retention(pr, cos, sin, din, qd, kd, cdm, bd, gn, b, t, tt=256):
    w = RET_HEADS * RET_DK
    nt = t // tt
    col = lambda j: pl.BlockSpec((tt, w), lambda bi, i: (bi * nt + i, j))
    pos = pl.BlockSpec((tt, w), lambda bi, i: (i, 0))
    return pl.pallas_call(
        functools.partial(_ret_kernel, nchunk=tt // RET_CHUNK),
        grid=(b, nt),
        in_specs=[col(0), col(1), col(2), col(3), col(4), col(5), pos, pos,
                  _resident(din.shape), _resident(qd.shape), _resident(kd.shape),
                  _resident(cdm.shape), _resident(bd.shape), _resident((1, w))],
        out_specs=pl.BlockSpec((tt, w), lambda bi, i: (bi * nt + i, 0)),
        out_shape=jax.ShapeDtypeStruct((b * t, w), _MXU_DTYPE),
        scratch_shapes=[pltpu.VMEM((w, w), F32)],
        compiler_params=_cparams("parallel", "arbitrary"),
        name="retention",
    )(pr, pr, pr, pr, pr, pr, cos, sin, din, qd, kd, cdm, bd, gn)


NSA_Q0, NSA_KC0, NSA_VC0, NSA_KS0, NSA_VS0, NSA_KW0, NSA_VW0, NSA_GATE0 = (
    0, 384, 512, 640, 896, 1024, 1280, 1408)
NSA_SLAB = 1536
GATE_STRIDE = 8


def _nsa_prep_kernel(pn_ref, gq_ref, gks_ref, gkw_ref, qt_ref, ks_ref, kw_ref, vs_ref, vw_ref,
                     gt_ref):
    tm = pn_ref.shape[0]
    d = NSA_DH
    dt = qt_ref.dtype
    xq = pn_ref[:, NSA_Q0:NSA_Q0 + NSA_HEADS * d].T
    for h in range(NSA_HEADS):
        blk = xq[h * d:(h + 1) * d]
        r = lax.rsqrt(jnp.mean(blk * blk, axis=0, keepdims=True) + EPS)
        qt_ref[0, h, 0:d, :] = (blk * r * gq_ref[...]).astype(dt)
        qt_ref[0, h, d:2 * d, :] = jnp.zeros((d, tm), dt)
    for g in range(NSA_KV_HEADS):
        for c0, g_ref, o_ref in ((NSA_KS0, gks_ref, ks_ref), (NSA_KW0, gkw_ref, kw_ref)):
            slot = pn_ref[:, c0 + g * LANES:c0 + (g + 1) * LANES]
            r = lax.rsqrt(jnp.sum(slot * slot, axis=-1, keepdims=True) * (1.0 / d) + EPS)
            o_ref[0, g] = (slot * r * g_ref[...]).astype(dt)
    for c0, o_ref in ((NSA_VS0, vs_ref), (NSA_VW0, vw_ref)):
        vt = pn_ref[:, c0:c0 + LANES].T
        for g in range(NSA_KV_HEADS):
            o_ref[0, g, 0:d, :] = vt[g * d:(g + 1) * d].astype(dt)
            o_ref[0, g, d:d + ONES_ROWS, :] = _ones_rows(tm, dt)
    gt = pn_ref[:, NSA_GATE0:NSA_GATE0 + LANES].T
    gt_ref[0] = _sigmoid(gt[0:NSA_HEADS * GATE_STRIDE])


def _nsa_prep(pn, gq, gks, gkw, b, t, tm=512):
    nt = t // tm
    d = NSA_DH
    g = NSA_KV_HEADS
    md = _MXU_DTYPE
    ch_major = lambda n, r: pl.BlockSpec((1, n, r, tm), lambda bi, i: (bi, 0, 0, i))
    natural = pl.BlockSpec((1, g, tm, LANES), lambda bi, i: (bi, 0, i, 0))
    return pl.pallas_call(
        _nsa_prep_kernel,
        grid=(b, nt),
        in_specs=[pl.BlockSpec((tm, NSA_SLAB), lambda bi, i: (bi * nt + i, 0)),
                  _resident((d, tm)), _resident((1, LANES)), _resident((1, LANES))],
        out_specs=[ch_major(NSA_HEADS, 2 * d), natural, natural,
                   ch_major(g, d + ONES_ROWS), ch_major(g, d + ONES_ROWS),
                   pl.BlockSpec((1, NSA_HEADS * GATE_STRIDE, tm), lambda bi, i: (bi, 0, i))],
        out_shape=[jax.ShapeDtypeStruct((b, NSA_HEADS, 2 * d, t), md),
                   jax.ShapeDtypeStruct((b, g, t, LANES), md),
                   jax.ShapeDtypeStruct((b, g, t, LANES), md),
                   jax.ShapeDtypeStruct((b, g, d + ONES_ROWS, t), md),
                   jax.ShapeDtypeStruct((b, g, d + ONES_ROWS, t), md),
                   jax.ShapeDtypeStruct((b, NSA_HEADS * GATE_STRIDE, t), F32)],
        compiler_params=_cparams("parallel", "parallel"),
        name="nsa_prep",
    )(pn, jnp.broadcast_to(gq[:, None], (d, tm)), gks, gkw)


def _gelu_tanh(x):
    return 0.5 * x * (1.0 + jnp.tanh(math.sqrt(2.0 / math.pi) * (x + 0.044715 * (x * x * x))))


def _compress_kernel(ak_ref, av_ref, w1k_ref, w2k_ref, pk_ref, w1v_ref, w2v_ref, pv_ref, gk_ref,
                     kc_ref, vc_ref):
    half = CMP_STRIDE * NSA_DH

    def comp(a_ref, w1_ref, w2_ref, p_ref):
        a = a_ref[0]
        pb = _dot(p_ref[...], w1_ref[...])[0:1]
        second = _dot(a, w1_ref[half:2 * half, :])
        nc = second.shape[0]
        hid = _dot(a, w1_ref[0:half, :]) + pltpu.roll(second, nc - 1, 0) + pb
        return _dot(_gelu_tanh(hid).astype(_MXU_DTYPE), w2_ref[...])

    kc = comp(ak_ref, w1k_ref, w2k_ref, pk_ref)
    y = kc * lax.rsqrt(jnp.mean(kc * kc, axis=-1, keepdims=True) + EPS)
    kc_ref[0] = (y * gk_ref[...]).astype(kc_ref.dtype)
    vc_ref[0] = comp(av_ref, w1v_ref, w2v_ref, pv_ref).astype(vc_ref.dtype)


def _compress(ak, av, w1k, w2k, pk, w1v, w2v, pv, gk):
    n, nc, kk = ak.shape
    blk = pl.BlockSpec((1, nc, kk), lambda i: (i, 0, 0))
    out = pl.BlockSpec((1, nc, NSA_DH), lambda i: (i, 0, 0))
    w1 = _resident((2 * kk, CMP_HIDDEN))
    w2 = _resident((CMP_HIDDEN, NSA_DH))
    pp = _resident((8, 2 * kk))
    return pl.pallas_call(
        _compress_kernel,
        grid=(n,),
        in_specs=[blk, blk, w1, w2, pp, w1, w2, pp, _resident((1, NSA_DH))],
        out_specs=[out, out],
        out_shape=[jax.ShapeDtypeStruct((n, nc, NSA_DH), _MXU_DTYPE)] * 2,
        compiler_params=_cparams("parallel"),
        name="nsa_compress",
    )(ak, av, w1k, w2k, pk, w1v, w2v, pv, gk)


def _cmp_sel_kernel(q_ref, kc_ref, vct_ref, dbc_ref, c2s_ref, dm_ref, bk_ref, oc_ref, sel_ref,
                    s_ref, p_ref, *, tq, nc, ns, nrows):
    i = pl.program_id(2)
    valid = dm_ref[...] >= (CMP_BLOCK - 1) - tq * i
    kc = kc_ref[0, 0]
    vct = vct_ref[0, 0]
    row0 = pl.multiple_of(i * (tq // CMP_STRIDE), 8)
    for r in range(NSA_GROUP):
        s_ref[8:nc + 8, :] = _dot(kc, q_ref[0, r])
        s_ref[pl.ds(row0, nrows), :] = s_ref[pl.ds(row0, nrows), :] + dbc_ref[r]
        s = jnp.where(valid, s_ref[8:nc + 8, :], NEG_INF)
        e = jnp.exp2(s - jnp.max(s, axis=0, keepdims=True))
        p = jnp.where(valid, e / jnp.sum(e, axis=0, keepdims=True), 0.0)
        pb = p.astype(_MXU_DTYPE)
        p_ref[r * nc:(r + 1) * nc, :] = pb
        oc_ref[0, r] = _dot(vct, pb)
    imp = _dot(c2s_ref[...], p_ref[...])
    back = bk_ref[...] + i * (tq // SEL_BLOCK)
    jidx = lax.broadcasted_iota(jnp.int32, (ns, tq), 0)
    forced = (jidx == 0) | ((back >= 0) & (back < SEL_LOCAL))
    imp = jnp.where(forced, FORCE, imp)
    imp = jnp.where(back >= 0, imp, NEG_INF)
    jf = jidx.astype(F32)
    chosen = jnp.zeros((ns, tq), F32)
    for _ in range(min(SEL_TOPK, ns)):
        mx = jnp.max(imp, axis=0, keepdims=True)
        first = jnp.min(jnp.where(imp == mx, jf, 1e9), axis=0, keepdims=True)
        hit = jf == first
        chosen = jnp.where(hit, 1.0, chosen)
        imp = jnp.where(hit, -3e38, imp)
    sel_ref[0, 0] = jnp.where(chosen > 0.5, 0.0, NEG_INF).astype(sel_ref.dtype)


def _cmp_sel(qt, kc, vct, dbc, c2s, dm, bk):
    b, h, _, t = qt.shape
    g, nc, d = kc.shape[1:]
    ns = c2s.shape[0]
    tq = dm.shape[1]
    nrows = dbc.shape[1]
    return pl.pallas_call(
        functools.partial(_cmp_sel_kernel, tq=tq, nc=nc, ns=ns, nrows=nrows),
        grid=(b, g, t // tq),
        in_specs=[pl.BlockSpec((1, NSA_GROUP, d, tq), lambda bi, gi, i: (bi, gi, 0, i)),
                  pl.BlockSpec((1, 1, nc, d), lambda bi, gi, i: (bi, gi, 0, 0)),
                  pl.BlockSpec((1, 1, d, nc), lambda bi, gi, i: (bi, gi, 0, 0)),
                  pl.BlockSpec((NSA_GROUP, nrows, tq), lambda bi, gi, i: (gi, 0, 0)),
                  _resident(c2s.shape), _resident(dm.shape), _resident(bk.shape)],
        out_specs=[pl.BlockSpec((1, NSA_GROUP, d, tq), lambda bi, gi, i: (bi, gi, 0, i)),
                   pl.BlockSpec((1, 1, ns, tq), lambda bi, gi, i: (bi, gi, 0, i))],
        out_shape=[jax.ShapeDtypeStruct((b, h, d, t), F32),
                   jax.ShapeDtypeStruct((b, g, ns, t), _MXU_DTYPE)],
        scratch_shapes=[pltpu.VMEM((nc + 8, tq), F32), pltpu.VMEM((NSA_GROUP * nc, tq), _MXU_DTYPE)],
        compiler_params=_cparams("parallel", "parallel", "parallel"),
        name="nsa_cmp_sel",
    )(qt, kc, vct, dbc, c2s, dm, bk)


def _out_proj_kernel(mla_ref, ret_ref, oc_ref, os_ref, ow_ref, g_ref, w_ref, r_ref, o_ref):
    d = NSA_DH
    w_mla, w_ret = MLA_HEADS * MLA_V, RET_HEADS * RET_DV
    acc = r_ref[...] + _dot(mla_ref[0].T.astype(_MXU_DTYPE), w_ref[0:w_mla, :])
    acc = acc + _dot(ret_ref[...], w_ref[w_mla:w_mla + w_ret, :])
    g = g_ref[0]
    nsa = []
    for h in range(NSA_HEADS):
        rows = slice(h * d, (h + 1) * d)
        g0 = h * GATE_STRIDE
        nsa.append(g[g0:g0 + 1] * oc_ref[0, rows, :] + g[g0 + 1:g0 + 2] * os_ref[0, rows, :]
                   + g[g0 + 2:g0 + 3] * ow_ref[0, rows, :])
    nsa = jnp.concatenate(nsa, axis=0).T.astype(_MXU_DTYPE)
    o_ref[...] = acc + _dot(nsa, w_ref[w_mla + w_ret:, :])


def _out_proj(o_mla, o_ret, oc, os_, ow, gates, w, res, t, tm=512):
    m, n = res.shape
    nt = t // tm
    ch_major = lambda c: pl.BlockSpec((1, c, tm), lambda i: (i // nt, 0, i % nt))
    tokens = lambda c: pl.BlockSpec((tm, c), lambda i: (i, 0))
    c_nsa = oc.shape[1]
    return pl.pallas_call(
        _out_proj_kernel,
        grid=(m // tm,),
        in_specs=[ch_major(o_mla.shape[1]), tokens(o_ret.shape[1]), ch_major(c_nsa), ch_major(c_nsa),
                  ch_major(c_nsa), ch_major(gates.shape[1]), _resident(w.shape), tokens(n)],
        out_specs=tokens(n),
        out_shape=jax.ShapeDtypeStruct((m, n), F32),
        compiler_params=_cparams("parallel"),
        name="out_proj",
    )(o_mla, o_ret, oc, os_, ow, gates, w, res)


def _ffn_kernel(x_ref, gn_ref, wup_ref, cw_ref, cb_ref, wdn_ref, o_ref, hbuf_ref, acc_ref, *,
                tm, fc):
    @pl.when(pl.program_id(1) == 0)
    def _():
        hbuf_ref[0:8, :] = jnp.zeros((8, hbuf_ref.shape[1]), F32)

    x = x_ref[...]
    y = x * lax.rsqrt(jnp.mean(x * x, axis=-1, keepdims=True) + EPS)
    xn = (y * gn_ref[...]).astype(_MXU_DTYPE)

    def up_proj(col0):
        cols = slice(col0, col0 + fc)
        hbuf_ref[8:tm + 8, cols] = _dot(xn, wup_ref[:, cols])

    def conv(col0):
        cols = slice(col0, col0 + fc)
        w = cw_ref[:, cols]
        h = hbuf_ref[8:tm + 8, cols]
        out = (h * w[2:3] + hbuf_ref[7:tm + 7, cols] * w[1:2] + hbuf_ref[6:tm + 6, cols] * w[0:1]
               + cb_ref[:, cols])
        hbuf_ref[0:8, cols] = hbuf_ref[tm:tm + 8, cols]
        return out

    nf = D_FF // fc
    ahead = 4
    for f in range(min(ahead, nf)):
        up_proj(f * fc)
        up_proj(D_FF + f * fc)
    for f in range(nf):
        if f + ahead < nf:
            up_proj((f + ahead) * fc)
            up_proj(D_FF + (f + ahead) * fc)
        gate = conv(f * fc)
        up = conv(D_FF + f * fc)
        act = (gate * _sigmoid(gate) * up).astype(_MXU_DTYPE)
        contrib = _dot(act, wdn_ref[f * fc:(f + 1) * fc, :])
        if f == 0:
            acc_ref[...] = contrib
        else:
            acc_ref[...] += contrib
    o_ref[...] = x + acc_ref[...]


def _ffn(x, gn, wup, cw, cb, wdn, b, t, tm=512, fc=256):
    d = x.shape[1]
    nt = t // tm
    return pl.pallas_call(
        functools.partial(_ffn_kernel, tm=tm, fc=fc),
        grid=(b, nt),
        in_specs=[pl.BlockSpec((tm, d), lambda bi, i: (bi * nt + i, 0)),
                  _resident((1, d)), _resident(wup.shape), _resident(cw.shape),
                  _resident(cb.shape), _resident(wdn.shape)],
        out_specs=pl.BlockSpec((tm, d), lambda bi, i: (bi * nt + i, 0)),
        out_shape=jax.ShapeDtypeStruct(x.shape, F32),
        scratch_shapes=[pltpu.VMEM((tm + 8, 2 * D_FF), F32), pltpu.VMEM((tm, d), F32)],
        compiler_params=_cparams("parallel", "arbitrary"),
        name="conv_ffn",
    )(x, gn, wup, cw, cb, wdn)


def _rope_tables(t, d):
    inv = ROPE_BASE ** (-np.arange(0, d, 2, dtype=np.float64) / d)
    ang = np.arange(t, dtype=np.float64)[:, None] * inv[None, :]
    return (np.concatenate([np.cos(ang)] * 2, axis=1), np.concatenate([np.sin(ang)] * 2, axis=1))


def _t5_bucket_np(dist):
    max_exact = REL_BUCKETS // 2
    d = np.maximum(dist, 1).astype(np.float64)
    log_b = max_exact + (np.log(d / max_exact) / math.log(REL_MAX_DIST / max_exact)
                         * (REL_BUCKETS - max_exact)).astype(np.int32)
    return np.where(dist < max_exact, dist, np.minimum(log_b, REL_BUCKETS - 1))


@functools.lru_cache(maxsize=None)
def _constants(t):
    c = {}
    cos, sin = _rope_tables(t, MLA_ROPE)
    pad = LANES - MLA_QK
    c["mla_cos"] = np.concatenate([np.ones((t, MLA_NOPE)), cos, np.ones((t, pad))], 1).astype(np.float32)
    c["mla_sin"] = np.concatenate([np.zeros((t, MLA_NOPE)), sin, np.zeros((t, pad))], 1).astype(np.float32)
    cos, sin = _rope_tables(t, RET_DK)
    c["ret_cos"] = np.tile(cos, (1, RET_HEADS)).astype(np.float32)
    c["ret_sin"] = np.tile(sin, (1, RET_HEADS)).astype(np.float32)
    lg = np.log(1.0 - 2.0 ** (-5.0 - np.arange(RET_HEADS, dtype=np.float64)))
    idx = np.arange(RET_CHUNK, dtype=np.float64)
    diff = idx[:, None] - idx[None, :]
    c["ret_din"] = (np.exp(np.maximum(diff, 0.0) * lg[:, None, None]) * (diff >= 0)).astype(np.float32)
    qd = np.exp((idx[:, None] + 1.0) * lg[None, :])
    kd = np.exp((RET_CHUNK - 1.0 - idx[:, None]) * lg[None, :])
    c["ret_qd"] = np.repeat(qd, RET_DK, axis=1).astype(np.float32)
    c["ret_kd"] = np.repeat(kd, RET_DK, axis=1).astype(np.float32)
    head_of = np.arange(RET_HEADS * RET_DK) // RET_DK
    c["ret_cdm"] = np.broadcast_to(np.exp(RET_CHUNK * lg)[head_of][:, None],
                                   (RET_HEADS * RET_DK, RET_HEADS * RET_DV)).astype(np.float32)
    c["ret_bd"] = (head_of[:, None] == head_of[None, :]).astype(np.float32)
    c["bucket"] = _t5_bucket_np(np.arange(LANES)).astype(np.int32)
    kk = np.arange(ATT_TILE)[:, None]
    qq = np.arange(ATT_TILE)[None, :]
    causal = np.where(qq >= kk, 0.0, NEG_INF)
    c["causal_tab"] = np.stack([np.concatenate([causal, np.full_like(causal, NEG_INF)], 0),
                                np.concatenate([np.zeros_like(causal), causal], 0)])[None].astype(np.float32)
    nc, ns = t // CMP_STRIDE, t // SEL_BLOCK
    n_cmp = (t - CMP_BLOCK) // CMP_STRIDE + 1
    c_start = np.arange(nc) * CMP_STRIDE
    s_start = np.arange(ns) * SEL_BLOCK
    overlap = np.clip(np.minimum(c_start[:, None] + CMP_BLOCK, s_start[None, :] + SEL_BLOCK)
                      - np.maximum(c_start[:, None], s_start[None, :]), 0, None).astype(np.float64)
    overlap[n_cmp:] = 0.0
    c["c2s"] = np.tile((overlap / CMP_BLOCK).T, (1, NSA_GROUP)).astype(np.float32)
    n = np.arange(nc)[:, None]
    q = np.arange(CMP_TQ)[None, :]
    dm = q - CMP_STRIDE * n
    dm[n_cmp:] = -(2 ** 30)
    c["dm"] = dm.astype(np.int32)
    c["bk"] = (np.arange(CMP_TQ)[None, :] // SEL_BLOCK - np.arange(ns)[:, None]).astype(np.int32)
    nrows = CMP_TQ // CMP_STRIDE + 8
    dnear = q - CMP_STRIDE * np.arange(nrows)[:, None] + (CMP_STRIDE * 8 - (CMP_BLOCK - 1))
    c["dnear_idx"] = np.clip(dnear, 0, LANES - 1).astype(np.int32)
    c["dnear_ok"] = ((dnear >= 0) & (dnear < LANES))
    c["sel_onehot"] = (np.arange(t)[:, None] // SEL_BLOCK == np.arange(ns)[None, :]).astype(np.float32)
    return c


def _cols(w, pieces):
    out = []
    for p in pieces:
        if p[0] is None:
            out.append(jnp.zeros((w.shape[0], p[1]), w.dtype))
        else:
            blk = w[:, p[0]:p[1]]
            out.append(-blk if p[2] < 0 else blk)
    return jnp.concatenate(out, axis=1)


def _rot_pieces(base, d):
    return [(base + d // 2, base + d, -1), (base, base + d // 2, 1)]


def _toeplitz(w, n):
    hh, ll = w.shape
    flat = jnp.tile(w, (1, n))[:, :n * (ll - 1)]
    return flat.reshape(hh, n, ll - 1)[:, :, :n]


def _pad_to(v, n):
    return jnp.concatenate([v, jnp.zeros((n - v.shape[0],), v.dtype)])


def _layer(xf, b, t, cst, tabs, w_in, w_out, attn_norm, ffn_norm, mla_q_a_norm, mla_w_uq,
           mla_kv_a_norm, mla_w_ukv, mla_q_norm, mla_k_norm, ret_norm, pos_k, w1_k, w2_k, pos_v,
           w1_v, w2_v, nsa_q_norm, kn_cmp, kn_sel, kn_win, ffn_w_up, ffn_conv_w, ffn_conv_b,
           ffn_w_down):
    md = _MXU_DTYPE
    o = _IN_OFF
    pieces = [(o[0], o[1], 1), (None, 64), (o[1], o[2], 1),
              (None, 64), (o[2], o[3], 1), (None, 32),
              (None, 64)] + _rot_pieces(o[2], MLA_ROPE) + [(None, 32)]
    pieces += [(o[3], o[7], 1)]
    for base in (o[3], o[4]):
        for h in range(RET_HEADS):
            pieces += _rot_pieces(base + h * RET_DK, RET_DK)
    d_ = NSA_DH
    pieces += [(o[7], o[10], 1)]
    for base in (o[10], o[12]):
        pieces += [(base, base + d_, 1), (None, LANES - d_), (base + d_, base + 2 * d_, 1),
                   (None, LANES - d_), (base + 2 * d_, base + 4 * d_, 1)]
    for h in range(NSA_HEADS):
        pieces += [(o[14] + 3 * h, o[14] + 3 * h + 3, 1), (None, GATE_STRIDE - 3)]
    pieces += [(None, LANES - NSA_HEADS * GATE_STRIDE)]
    w_in_r = _cols(w_in, pieces).astype(md)
    pm, pr, pn = _in_proj(xf, attn_norm[None, :], w_in_r, (640, 1536, NSA_SLAB))

    wq_pieces, wq_rot = [], []
    wkv_k, wkv_v = [], []
    for h in range(MLA_HEADS):
        qb = h * MLA_QK
        wq_pieces += [(qb, qb + MLA_QK, 1), (None, LANES - MLA_QK)]
        wq_rot += [(None, MLA_NOPE)] + _rot_pieces(qb + MLA_NOPE, MLA_ROPE) + [(None, LANES - MLA_QK)]
        kb = h * (MLA_NOPE + MLA_V)
        wkv_k += [(kb, kb + MLA_NOPE, 1), (None, LANES - MLA_NOPE)]
        wkv_v += [(kb + MLA_NOPE, kb + MLA_NOPE + MLA_V, 1), (None, LANES - MLA_V)]
    wq = _cols(mla_w_uq, wq_pieces + wq_rot)
    wq = jnp.concatenate([wq, jnp.zeros((256 - MLA_Q_RANK, wq.shape[1]), wq.dtype)], axis=0).astype(md)
    wkv = _cols(mla_w_ukv, wkv_k + wkv_v).astype(md)
    half = MLA_ROPE // 2

    def rot_gain(gv):
        return jnp.concatenate([jnp.zeros((MLA_NOPE,), gv.dtype), gv[MLA_NOPE + half:],
                                gv[MLA_NOPE:MLA_NOPE + half], jnp.zeros((LANES - MLA_QK,), gv.dtype)])

    q, k, v = _mla_prep(pm, cst["mla_cos"], cst["mla_sin"], _pad_to(mla_q_a_norm, 256)[None, :],
                        mla_kv_a_norm[None, :], wq, wkv,
                        _pad_to(mla_q_norm, LANES)[None, :], rot_gain(mla_q_norm)[None, :],
                        _pad_to(mla_k_norm, LANES)[None, :], rot_gain(mla_k_norm)[None, :], b, t)
    o_mla = _flash(q, k, v, tabs["causal"], use_far=True, name="mla_attention")
    o_mla = o_mla.reshape(b, MLA_HEADS * MLA_V, t)

    o_ret = _retention(pr, cst["ret_cos"], cst["ret_sin"], cst["ret_din"], cst["ret_qd"],
                       cst["ret_kd"], cst["ret_cdm"], cst["ret_bd"],
                       jnp.tile(ret_norm, RET_HEADS)[None, :], b, t)

    g_ = NSA_KV_HEADS
    nc = t // CMP_STRIDE
    qt, ks_n, kw_n, vs_a, vw_a, gates = _nsa_prep(
        pn, nsa_q_norm * (d_ ** -0.5 * LOG2E), _pad_to(kn_sel, LANES)[None, :],
        _pad_to(kn_win, LANES)[None, :], b, t)

    def chunks(c0):
        a = pn[:, c0:c0 + g_ * d_].reshape(b, t, g_, d_).transpose(0, 2, 1, 3)
        return a.reshape(b * g_, nc, CMP_STRIDE * d_).astype(md)

    pos8 = lambda p: jnp.broadcast_to(p.reshape(1, -1), (8, CMP_BLOCK * d_)).astype(md)
    k_c, v_c = _compress(chunks(NSA_KC0), chunks(NSA_VC0), w1_k.astype(md), w2_k.astype(md),
                         pos8(pos_k), w1_v.astype(md), w2_v.astype(md), pos8(pos_v), kn_cmp[None, :])
    k_c = k_c.reshape(b, g_, nc, d_)
    v_ct = v_c.reshape(b, g_, nc, d_).transpose(0, 1, 3, 2)
    oc_t, selneg = _cmp_sel(qt, k_c, v_ct, tabs["cmp"], cst["c2s"].astype(md), cst["dm"], cst["bk"])
    os_t = _flash(qt, ks_n, vs_a, tabs["sel"], use_far=True, name="nsa_selected",
                  qx=selneg, kx=cst["sel_onehot"].astype(md))
    ow_t = _flash(qt, kw_n, vw_a, tabs["win"], use_far=False, name="nsa_window")

    flat = lambda a: a.reshape(b, NSA_HEADS * d_, t)
    xf = _out_proj(o_mla, o_ret, flat(oc_t), flat(os_t), flat(ow_t), gates, w_out.astype(md), xf, t)
    return _ffn(xf, ffn_norm[None, :], ffn_w_up.astype(md), ffn_conv_w, ffn_conv_b[None, :],
                ffn_w_down.astype(md), b, t)


def _bias_tables(rel_bias, cst):
    n = ATT_TILE
    lut = rel_bias[cst["bucket"]].T
    delta = (lut - rel_bias[REL_BUCKETS - 1][:, None]) * LOG2E
    hh = delta.shape[0]
    dn = jnp.concatenate([delta, jnp.zeros((hh, n - LANES), F32)], axis=1)
    neg = jnp.full((hh, n), NEG_INF, F32)
    zero = jnp.zeros((hh, n), F32)
    diag = _toeplitz(jnp.concatenate([dn, neg], axis=1), n)
    prev_sel = _toeplitz(jnp.concatenate([zero, dn], axis=1), n)
    prev_win = _toeplitz(jnp.concatenate([neg, dn], axis=1), n)
    cmp_tab = jnp.where(cst["dnear_ok"][None], delta[:, cst["dnear_idx"]], 0.0)
    masked = jnp.full((hh, n, n), NEG_INF, F32)
    first = jnp.concatenate([diag, masked], axis=1)

    def near(prev):
        return jnp.stack([first, jnp.concatenate([prev, diag], axis=1)], axis=1)

    return {"sel": near(prev_sel), "win": near(prev_win), "cmp": cmp_tab,
            "causal": jnp.asarray(cst["causal_tab"])}


def kernel(x, w_in, w_out, attn_norm, ffn_norm, mla_q_a_norm, mla_w_uq, mla_kv_a_norm, mla_w_ukv, mla_q_norm, mla_k_norm, ret_norm, nsa_cmp_pos_k, nsa_cmp_w1_k, nsa_cmp_w2_k, nsa_cmp_pos_v, nsa_cmp_w1_v, nsa_cmp_w2_v, nsa_q_norm, nsa_k_norm_cmp, nsa_k_norm_sel, nsa_k_norm_win, rel_bias, ffn_w_up, ffn_conv_w, ffn_conv_b, ffn_w_down):
    b, t, d = x.shape
    assert d == D_MODEL and t % ATT_TILE == 0 and WINDOW == ATT_TILE
    cst = _constants(t)
    tabs = _bias_tables(rel_bias, cst)
    per_layer = (w_in, w_out, attn_norm, ffn_norm, mla_q_a_norm, mla_w_uq, mla_kv_a_norm, mla_w_ukv,
                 mla_q_norm, mla_k_norm, ret_norm, nsa_cmp_pos_k, nsa_cmp_w1_k, nsa_cmp_w2_k,
                 nsa_cmp_pos_v, nsa_cmp_w1_v, nsa_cmp_w2_v, nsa_q_norm, nsa_k_norm_cmp,
                 nsa_k_norm_sel, nsa_k_norm_win, ffn_w_up, ffn_conv_w, ffn_conv_b, ffn_w_down)
    xf = x.reshape(b * t, d)
    for l in range(w_in.shape[0]):
        xf = _layer(xf, b, t, cst, tabs, *[p[l] for p in per_layer])
    return xf.reshape(b, t, d)
```

```python
import functools
import math

import numpy as np
import jax
import jax.numpy as jnp
from jax import lax
from jax.experimental import pallas as pl
from jax.experimental.pallas import tpu as pltpu

D_MODEL = 1024
DEPTH = 2
MLA_HEADS = 6
MLA_Q_RANK = 192
MLA_KV_RANK = 128
MLA_NOPE = 64
MLA_ROPE = 32
MLA_V = 64
MLA_QK = MLA_NOPE + MLA_ROPE
RET_HEADS = 4
RET_DK = 64
RET_DV = 64
RET_CHUNK = 128
NSA_HEADS = 6
NSA_KV_HEADS = 2
NSA_GROUP = NSA_HEADS // NSA_KV_HEADS
NSA_DH = 64
CMP_BLOCK = 32
CMP_STRIDE = 16
CMP_HIDDEN = 256
SEL_BLOCK = 64
SEL_TOPK = 16
SEL_LOCAL = 2
WINDOW = 512
REL_BUCKETS = 32
REL_MAX_DIST = 128
D_FF = 2816
ROPE_BASE = 10000.0
EPS = 1e-6
NEG_INF = -1e30
FORCE = 1e9

_IN_SPLITS = (MLA_Q_RANK, MLA_KV_RANK, MLA_ROPE,
              RET_HEADS * RET_DK, RET_HEADS * RET_DK, RET_HEADS * RET_DV, RET_HEADS * RET_DV,
              NSA_HEADS * NSA_DH) + (NSA_KV_HEADS * NSA_DH,) * 6 + (3 * NSA_HEADS,)
_IN_OFF = [0] + [int(v) for v in np.cumsum(_IN_SPLITS)]
D_IN = _IN_OFF[-1]

LANES = 128
ATT_TILE = 512
CMP_TQ = 256
CMP_CHUNK = 128
VMEM_LIMIT = 56 * 1024 * 1024

_MXU_DTYPE = jnp.bfloat16
F32 = jnp.float32
LOG2E = math.log2(math.e)


def _cparams(*sem):
    return pltpu.CompilerParams(dimension_semantics=sem, vmem_limit_bytes=VMEM_LIMIT)


def _dot(a, b):
    return jnp.dot(a, b, preferred_element_type=F32)


def _sigmoid(x):
    return 1.0 / (1.0 + jnp.exp(-x))


ONES_ROWS = 16


def _ones_rows(n, dtype):
    row = lax.broadcasted_iota(jnp.int32, (ONES_ROWS, n), 0)
    return jnp.where(row == 0, 1.0, 0.0).astype(dtype)


def _with_ones_rows(vt):
    b, h, _, t = vt.shape
    extra = jnp.concatenate([jnp.ones((b, h, 1, t), vt.dtype),
                             jnp.zeros((b, h, ONES_ROWS - 1, t), vt.dtype)], axis=2)
    return jnp.concatenate([vt, extra], axis=2)


def _resident(shape):
    nd = len(shape)
    return pl.BlockSpec(shape, lambda *_: (0,) * nd, pipeline_mode=pl.Buffered(1))


def _in_proj_kernel(x_ref, g_ref, w_ref, *o_refs, widths):
    x = x_ref[...]
    y = x * lax.rsqrt(jnp.mean(x * x, axis=-1, keepdims=True) + EPS)
    xn = (y * g_ref[...]).astype(_MXU_DTYPE)
    off = 0
    for o_ref, wd in zip(o_refs, widths):
        o_ref[...] = _dot(xn, w_ref[:, off:off + wd])
        off += wd


def _in_proj(x, g, w, widths, tm=256):
    m, d = x.shape
    n = w.shape[1]
    return pl.pallas_call(
        functools.partial(_in_proj_kernel, widths=widths),
        grid=(m // tm,),
        in_specs=[pl.BlockSpec((tm, d), lambda i: (i, 0)),
                  _resident((1, d)),
                  _resident((d, n))],
        out_specs=[pl.BlockSpec((tm, wd), lambda i: (i, 0)) for wd in widths],
        out_shape=[jax.ShapeDtypeStruct((m, wd), F32) for wd in widths],
        compiler_params=_cparams("parallel"),
        name="in_proj",
    )(x, g, w)


def _mla_prep_kernel(pm_ref, cos_ref, sin_ref, gqa_ref, gkva_ref, wq_ref, wkv_ref,
                     gq_ref, gqr_ref, gk_ref, gkr_ref, q_ref, k_ref, v_ref):
    pm = pm_ref[...]
    hs = MLA_HEADS * LANES
    cq = pm[:, 0:256]
    r = lax.rsqrt(jnp.sum(cq * cq, axis=-1, keepdims=True) * (1.0 / MLA_Q_RANK) + EPS)
    qq = _dot((cq * r * gqa_ref[...]).astype(_MXU_DTYPE), wq_ref[...])
    ckv = pm[:, 256:384]
    r = lax.rsqrt(jnp.mean(ckv * ckv, axis=-1, keepdims=True) + EPS)
    kv = _dot((ckv * r * gkva_ref[...]).astype(_MXU_DTYPE), wkv_ref[...])
    kpe = pm[:, 384:512]
    kpe_rot = pm[:, 512:640]
    cos = cos_ref[...]
    sin = sin_ref[...]
    scale = MLA_QK ** -0.5 * LOG2E
    aq = cos * gq_ref[...] * scale
    bq = sin * gqr_ref[...] * scale
    ak = cos * gk_ref[...]
    bk = sin * gkr_ref[...]
    for h in range(MLA_HEADS):
        sl = slice(h * LANES, (h + 1) * LANES)
        sr = slice(hs + h * LANES, hs + (h + 1) * LANES)
        qh = qq[:, sl]
        rq = lax.rsqrt(jnp.sum(qh * qh, axis=-1, keepdims=True) * (1.0 / MLA_QK) + EPS)
        q_ref[0, h] = ((qh * aq + qq[:, sr] * bq) * rq).T.astype(q_ref.dtype)
        kh = kv[:, sl] + kpe
        rk = lax.rsqrt(jnp.sum(kh * kh, axis=-1, keepdims=True) * (1.0 / MLA_QK) + EPS)
        k_ref[0, h] = ((kh * ak + kpe_rot * bk) * rk).astype(k_ref.dtype)
        v_ref[0, h, 0:MLA_V, :] = kv[:, sr].T[0:MLA_V].astype(v_ref.dtype)
        v_ref[0, h, MLA_V:MLA_V + ONES_ROWS, :] = _ones_rows(pm.shape[0], v_ref.dtype)


def _mla_prep(pm, cos, sin, gqa, gkva, wq, wkv, gq, gqr, gk, gkr, b, t, tm=512):
    nt = t // tm
    hs = MLA_HEADS * LANES
    vec = lambda n: _resident((1, n))
    return pl.pallas_call(
        _mla_prep_kernel,
        grid=(b, nt),
        in_specs=[pl.BlockSpec((tm, 640), lambda bi, i: (bi * nt + i, 0)),
                  pl.BlockSpec((tm, LANES), lambda bi, i: (i, 0)),
                  pl.BlockSpec((tm, LANES), lambda bi, i: (i, 0)),
                  vec(256), vec(LANES), _resident((256, 2 * hs)), _resident((LANES, 2 * hs)),
                  vec(LANES), vec(LANES), vec(LANES), vec(LANES)],
        out_specs=[pl.BlockSpec((1, MLA_HEADS, LANES, tm), lambda bi, i: (bi, 0, 0, i)),
                   pl.BlockSpec((1, MLA_HEADS, tm, LANES), lambda bi, i: (bi, 0, i, 0)),
                   pl.BlockSpec((1, MLA_HEADS, MLA_V + ONES_ROWS, tm), lambda bi, i: (bi, 0, 0, i))],
        out_shape=[jax.ShapeDtypeStruct((b, MLA_HEADS, LANES, t), _MXU_DTYPE),
                   jax.ShapeDtypeStruct((b, MLA_HEADS, t, LANES), _MXU_DTYPE),
                   jax.ShapeDtypeStruct((b, MLA_HEADS, MLA_V + ONES_ROWS, t), _MXU_DTYPE)],
        compiler_params=_cparams("parallel", "parallel"),
        name="mla_prep",
    )(pm, cos, sin, gqa, gkva, wq, wkv, gq, gqr, gk, gkr)


def _flash_kernel(*refs, use_far, sub, dv, extra):
    if extra:
        q_ref, qx_ref, k_ref, kx_ref, v_ref, tab_ref, o_ref = refs
    else:
        q_ref, k_ref, v_ref, tab_ref, o_ref = refs
    i = pl.program_id(2)
    q = q_ref[0, 0]
    if extra:
        q = jnp.concatenate([q, qx_ref[0, 0]], axis=0)
    tc = ATT_TILE
    tq = q.shape[1]

    def block(start, nkeys, with_table, m, acc):
        kb = k_ref[0, 0, pl.ds(start, nkeys), :]
        if extra:
            kb = jnp.concatenate([kb, kx_ref[pl.ds(start, nkeys), :]], axis=1)
        s = _dot(kb, q)
        if with_table:
            s = s + tab_ref[0, 0]
        for j in range(nkeys // sub):
            sj = s[j * sub:(j + 1) * sub]
            vc = v_ref[0, 0, :, pl.ds(pl.multiple_of(start + j * sub, sub), sub)]
            m_new = jnp.maximum(m, jnp.max(sj, axis=0, keepdims=True))
            p = jnp.exp2(sj - m_new).astype(_MXU_DTYPE)
            acc = jnp.exp2(m - m_new) * acc + _dot(vc, p)
            m = m_new
        return m, acc

    m = jnp.full((1, tq), -3e38, F32)
    acc = jnp.zeros((dv + ONES_ROWS, tq), F32)
    n_far = jnp.maximum(i - 1, 0)
    if use_far:
        done = 0
        for width in (4, 2, 1):
            count = (n_far - done) // width
            base = done

            def body(c, carry, width=width, base=base):
                start = pl.multiple_of((base + c * width) * tc, tc)
                return block(start, width * tc, False, *carry)

            m, acc = lax.fori_loop(0, count, body, (m, acc))
            done = done + count * width
    m, acc = block(pl.multiple_of(n_far * tc, tc), 2 * tc, True, m, acc)
    o_ref[0, 0] = acc[0:dv] / acc[dv:dv + 1]


def _flash(qt, k, va, tab, *, use_far, name, qx=None, kx=None, sub=256):
    b, h, dk, t = qt.shape
    hk, dva = va.shape[1], va.shape[2]
    dv = dva - ONES_ROWS
    rep = h // hk
    ht = tab.shape[0]
    tq = ATT_TILE
    extra = qx is not None
    q_spec = pl.BlockSpec((1, 1, dk, tq), lambda bi, hi, i: (bi, hi, 0, i))
    k_spec = pl.BlockSpec((1, 1, t, dk), lambda bi, hi, i: (bi, hi // rep, 0, 0))
    rest = [pl.BlockSpec((1, 1, dva, t), lambda bi, hi, i: (bi, hi // rep, 0, 0)),
            pl.BlockSpec((1, 1, 2 * tq, tq), lambda bi, hi, i: (hi % ht, jnp.minimum(i, 1), 0, 0))]
    if extra:
        nx = qx.shape[2]
        in_specs = [q_spec, pl.BlockSpec((1, 1, nx, tq), lambda bi, hi, i: (bi, hi // rep, 0, i)),
                    k_spec, _resident((t, nx))] + rest
        args = (qt, qx, k, kx, va, tab)
    else:
        in_specs = [q_spec, k_spec] + rest
        args = (qt, k, va, tab)
    return pl.pallas_call(
        functools.partial(_flash_kernel, use_far=use_far, sub=sub, dv=dv, extra=extra),
        grid=(b, h, t // tq),
        in_specs=in_specs,
        out_specs=pl.BlockSpec((1, 1, dv, tq), lambda bi, hi, i: (bi, hi, 0, i)),
        out_shape=jax.ShapeDtypeStruct((b, h, dv, t), F32),
        compiler_params=_cparams("parallel", "parallel", "parallel"),
        name=name,
    )(*args)


def _ret_kernel(q_ref, k_ref, v_ref, g_ref, qr_ref, kr_ref, cos_ref, sin_ref, din_ref, qd_ref,
                kd_ref, cdm_ref, bd_ref, gn_ref, o_ref, state_ref, *, nchunk):
    @pl.when(pl.program_id(1) == 0)
    def _():
        state_ref[...] = jnp.zeros(state_ref.shape, F32)

    w = RET_HEADS * RET_DK
    lane = lax.broadcasted_iota(jnp.int32, (1, w), 1)
    heads = [(lane >= h * RET_DK) & (lane < (h + 1) * RET_DK) for h in range(RET_HEADS)]
    on_diag = bd_ref[...] > 0.5
    c_ = RET_CHUNK
    for c in range(nchunk):
        sl = slice(c * c_, (c + 1) * c_)
        cos = cos_ref[sl, :]
        sin = sin_ref[sl, :]
        qh = q_ref[sl, :] * cos + qr_ref[sl, :] * sin
        kh = (k_ref[sl, :] * cos + kr_ref[sl, :] * sin) * (RET_DK ** -0.5)
        kb = kh.astype(_MXU_DTYPE)
        vb = v_ref[sl, :].astype(_MXU_DTYPE)
        st = state_ref[...]
        out = _dot((qh * qd_ref[...]).astype(_MXU_DTYPE), st.astype(_MXU_DTYPE))
        for h in range(RET_HEADS):
            qm = jnp.where(heads[h], qh, 0.0).astype(_MXU_DTYPE)
            inner = lax.dot_general(qm, kb, (((1,), (1,)), ((), ())),
                                    preferred_element_type=F32) * din_ref[h]
            out = out + jnp.where(heads[h], _dot(inner.astype(_MXU_DTYPE), vb), 0.0)
        kdt = (kh * kd_ref[...]).T
        state_ref[...] = st * cdm_ref[...] + jnp.where(on_diag, _dot(kdt.astype(_MXU_DTYPE), vb), 0.0)
        o2 = out * out
        ms = jnp.zeros_like(out)
        for h in range(RET_HEADS):
            ssum = jnp.sum(jnp.where(heads[h], o2, 0.0), axis=-1, keepdims=True)
            ms = jnp.where(heads[h], ssum * (1.0 / RET_DV), ms)
        y = out * lax.rsqrt(ms + EPS) * gn_ref[...]
        gg = g_ref[sl, :]
        o_ref[sl, :] = (gg * _sigmoid(gg) * y).astype(o_ref.dtype)


def _retention(pr, cos, sin, din, qd, kd, cdm, bd, gn, b, t, tt=256):
    w = RET_HEADS * RET_DK
    nt = t // tt
    col = lambda j: pl.BlockSpec((tt, w), lambda bi, i: (bi * nt + i, j))
    pos = pl.BlockSpec((tt, w), lambda bi, i: (i, 0))
    return pl.pallas_call(
        functools.partial(_ret_kernel, nchunk=tt // RET_CHUNK),
        grid=(b, nt),
        in_specs=[col(0), col(1), col(2), col(3), col(4), col(5), pos, pos,
                  _resident(din.shape), _resident(qd.shape), _resident(kd.shape),
                  _resident(cdm.shape), _resident(bd.shape), _resident((1, w))],
        out_specs=pl.BlockSpec((tt, w), lambda bi, i: (bi * nt + i, 0)),
        out_shape=jax.ShapeDtypeStruct((b * t, w), _MXU_DTYPE),
        scratch_shapes=[pltpu.VMEM((w, w), F32)],
        compiler_params=_cparams("parallel", "arbitrary"),
        name="retention",
    )(pr, pr, pr, pr, pr, pr, cos, sin, din, qd, kd, cdm, bd, gn)


NSA_Q0, NSA_KC0, NSA_VC0, NSA_KS0, NSA_VS0, NSA_KW0, NSA_VW0, NSA_GATE0 = (
    0, 384, 512, 640, 896, 1024, 1280, 1408)
NSA_SLAB = 1536
GATE_STRIDE = 8


def _nsa_prep_kernel(pn_ref, gq_ref, gks_ref, gkw_ref, qt_ref, ks_ref, kw_ref, vs_ref, vw_ref,
                     gt_ref):
    tm = pn_ref.shape[0]
    d = NSA_DH
    dt = qt_ref.dtype
    xq = pn_ref[:, NSA_Q0:NSA_Q0 + NSA_HEADS * d].T
    for h in range(NSA_HEADS):
        blk = xq[h * d:(h + 1) * d]
        r = lax.rsqrt(jnp.mean(blk * blk, axis=0, keepdims=True) + EPS)
        qt_ref[0, h, 0:d, :] = (blk * r * gq_ref[...]).astype(dt)
        qt_ref[0, h, d:2 * d, :] = jnp.zeros((d, tm), dt)
    for g in range(NSA_KV_HEADS):
        for c0, g_ref, o_ref in ((NSA_KS0, gks_ref, ks_ref), (NSA_KW0, gkw_ref, kw_ref)):
            slot = pn_ref[:, c0 + g * LANES:c0 + (g + 1) * LANES]
            r = lax.rsqrt(jnp.sum(slot * slot, axis=-1, keepdims=True) * (1.0 / d) + EPS)
            o_ref[0, g] = (slot * r * g_ref[...]).astype(dt)
    for c0, o_ref in ((NSA_VS0, vs_ref), (NSA_VW0, vw_ref)):
        vt = pn_ref[:, c0:c0 + LANES].T
        for g in range(NSA_KV_HEADS):
            o_ref[0, g, 0:d, :] = vt[g * d:(g + 1) * d].astype(dt)
            o_ref[0, g, d:d + ONES_ROWS, :] = _ones_rows(tm, dt)
    gt = pn_ref[:, NSA_GATE0:NSA_GATE0 + LANES].T
    gt_ref[0] = _sigmoid(gt[0:NSA_HEADS * GATE_STRIDE])


def _nsa_prep(pn, gq, gks, gkw, b, t, tm=512):
    nt = t // tm
    d = NSA_DH
    g = NSA_KV_HEADS
    md = _MXU_DTYPE
    ch_major = lambda n, r: pl.BlockSpec((1, n, r, tm), lambda bi, i: (bi, 0, 0, i))
    natural = pl.BlockSpec((1, g, tm, LANES), lambda bi, i: (bi, 0, i, 0))
    return pl.pallas_call(
        _nsa_prep_kernel,
        grid=(b, nt),
        in_specs=[pl.BlockSpec((tm, NSA_SLAB), lambda bi, i: (bi * nt + i, 0)),
                  _resident((d, tm)), _resident((1, LANES)), _resident((1, LANES))],
        out_specs=[ch_major(NSA_HEADS, 2 * d), natural, natural,
                   ch_major(g, d + ONES_ROWS), ch_major(g, d + ONES_ROWS),
                   pl.BlockSpec((1, NSA_HEADS * GATE_STRIDE, tm), lambda bi, i: (bi, 0, i))],
        out_shape=[jax.ShapeDtypeStruct((b, NSA_HEADS, 2 * d, t), md),
                   jax.ShapeDtypeStruct((b, g, t, LANES), md),
                   jax.ShapeDtypeStruct((b, g, t, LANES), md),
                   jax.ShapeDtypeStruct((b, g, d + ONES_ROWS, t), md),
                   jax.ShapeDtypeStruct((b, g, d + ONES_ROWS, t), md),
                   jax.ShapeDtypeStruct((b, NSA_HEADS * GATE_STRIDE, t), F32)],
        compiler_params=_cparams("parallel", "parallel"),
        name="nsa_prep",
    )(pn, jnp.broadcast_to(gq[:, None], (d, tm)), gks, gkw)


def _gelu_tanh(x):
    return 0.5 * x * (1.0 + jnp.tanh(math.sqrt(2.0 / math.pi) * (x + 0.044715 * (x * x * x))))


def _compress_kernel(ak_ref, av_ref, w1k_ref, w2k_ref, pk_ref, w1v_ref, w2v_ref, pv_ref, gk_ref,
                     kc_ref, vc_ref):
    half = CMP_STRIDE * NSA_DH

    def comp(a_ref, w1_ref, w2_ref, p_ref):
        a = a_ref[0]
        pb = _dot(p_ref[...], w1_ref[...])[0:1]
        second = _dot(a, w1_ref[half:2 * half, :])
        nc = second.shape[0]
        hid = _dot(a, w1_ref[0:half, :]) + pltpu.roll(second, nc - 1, 0) + pb
        return _dot(_gelu_tanh(hid).astype(_MXU_DTYPE), w2_ref[...])

    kc = comp(ak_ref, w1k_ref, w2k_ref, pk_ref)
    y = kc * lax.rsqrt(jnp.mean(kc * kc, axis=-1, keepdims=True) + EPS)
    kc_ref[0] = (y * gk_ref[...]).astype(kc_ref.dtype)
    vc_ref[0] = comp(av_ref, w1v_ref, w2v_ref, pv_ref).astype(vc_ref.dtype)


def _compress(ak, av, w1k, w2k, pk, w1v, w2v, pv, gk):
    n, nc, kk = ak.shape
    blk = pl.BlockSpec((1, nc, kk), lambda i: (i, 0, 0))
    out = pl.BlockSpec((1, nc, NSA_DH), lambda i: (i, 0, 0))
    w1 = _resident((2 * kk, CMP_HIDDEN))
    w2 = _resident((CMP_HIDDEN, NSA_DH))
    pp = _resident((8, 2 * kk))
    return pl.pallas_call(
        _compress_kernel,
        grid=(n,),
        in_specs=[blk, blk, w1, w2, pp, w1, w2, pp, _resident((1, NSA_DH))],
        out_specs=[out, out],
        out_shape=[jax.ShapeDtypeStruct((n, nc, NSA_DH), _MXU_DTYPE)] * 2,
        compiler_params=_cparams("parallel"),
        name="nsa_compress",
    )(ak, av, w1k, w2k, pk, w1v, w2v, pv, gk)


def _cmp_sel_kernel(q_ref, kc_ref, vca_ref, tab_ref, c2s_ref, bk_ref, oc_ref, sel_ref,
                    m_ref, acco_ref, acci_ref, *, tq, ns, d):
    i = pl.program_id(2)
    ch = CMP_CHUNK
    end = (tq // CMP_STRIDE) * (i + 1)
    n_far = (end - 1) // ch
    rowi = lax.broadcasted_iota(jnp.int32, (ch, NSA_GROUP * tq), 0)
    qpos = tq * i + lax.broadcasted_iota(jnp.int32, (1, tq), 1)
    sees_any = jnp.where(qpos >= CMP_BLOCK - 1, 1.0, 0.0)

    q = jnp.concatenate([q_ref[0, r] for r in range(NSA_GROUP)], axis=1)
    m_ref[...] = jnp.full(m_ref.shape, -3e38, F32)
    acco_ref[...] = jnp.zeros(acco_ref.shape, F32)
    acci_ref[...] = jnp.zeros(acci_ref.shape, F32)

    def chunk(g, with_table):
        start = pl.multiple_of(end - ch * g, CMP_STRIDE)
        kc = kc_ref[0, 0, pl.ds(start, ch), :]
        vt = vca_ref[0, 0, pl.ds(start, ch), :].T[0:d + ONES_ROWS].astype(_MXU_DTYPE)
        ct = c2s_ref[pl.ds(start, ch), :].T.astype(_MXU_DTYPE)
        s = _dot(kc, q)
        if with_table:
            s = s + tab_ref[0]
        s = jnp.where(rowi >= ch * (g + 1) - end, s, NEG_INF)
        m_old = m_ref[...]
        m_new = jnp.maximum(m_old, jnp.max(s, axis=0, keepdims=True))
        alpha = jnp.exp2(m_old - m_new)
        e = jnp.exp2(s - m_new).astype(_MXU_DTYPE)
        acco_ref[...] = alpha * acco_ref[...] + _dot(vt, e)
        acci_ref[...] = alpha * acci_ref[...] + _dot(ct, e)
        m_ref[...] = m_new

    chunk(0, True)

    def far(g, carry):
        chunk(g, False)
        return carry

    lax.fori_loop(1, n_far + 1, far, 0)

    imp = jnp.zeros((ns, tq), F32)
    for r in range(NSA_GROUP):
        lanes = slice(r * tq, (r + 1) * tq)
        inv = sees_any / acco_ref[d:d + 1, lanes]
        oc_ref[0, r] = acco_ref[0:d, lanes] * inv
        imp = imp + acci_ref[:, lanes] * inv

    back = bk_ref[...] + i * (tq // SEL_BLOCK)
    jidx = lax.broadcasted_iota(jnp.int32, (ns, tq), 0)
    forced = (jidx == 0) | ((back >= 0) & (back < SEL_LOCAL))
    jf = jidx.astype(F32)
    taken = -3e38

    def pick(imp, rounds):
        for _ in range(rounds):
            mx = jnp.max(imp, axis=0, keepdims=True)
            first = jnp.min(jnp.where(imp == mx, jf, 1e9), axis=0, keepdims=True)
            imp = jnp.where(jf == first, taken, imp)
        sel_ref[0, 0] = jnp.where(imp == taken, 0.0, NEG_INF).astype(sel_ref.dtype)

    top_k = min(SEL_TOPK, ns)
    n_forced = 1 + SEL_LOCAL

    @pl.when(i == 0)
    def _():
        pick(jnp.where(back >= 0, jnp.where(forced, FORCE, imp), NEG_INF), top_k)

    @pl.when(i > 0)
    def _():
        pick(jnp.where(forced, taken, jnp.where(back >= 0, imp, NEG_INF)), top_k - n_forced)


def _cmp_sel(qt, kc, vca, tab, c2s, bk, tq):
    b, h, _, t = qt.shape
    g, npad, d = kc.shape[1:]
    ns = c2s.shape[1]
    assert tq >= SEL_LOCAL * SEL_BLOCK and tq // CMP_STRIDE <= CMP_CHUNK
    return pl.pallas_call(
        functools.partial(_cmp_sel_kernel, tq=tq, ns=ns, d=d),
        grid=(b, g, t // tq),
        in_specs=[pl.BlockSpec((1, NSA_GROUP, d, tq), lambda bi, gi, i: (bi, gi, 0, i)),
                  pl.BlockSpec((1, 1, npad, d), lambda bi, gi, i: (bi, gi, 0, 0)),
                  pl.BlockSpec((1, 1, npad, LANES), lambda bi, gi, i: (bi, gi, 0, 0)),
                  pl.BlockSpec((1, CMP_CHUNK, NSA_GROUP * tq), lambda bi, gi, i: (gi, 0, 0)),
                  _resident(c2s.shape), _resident(bk.shape)],
        out_specs=[pl.BlockSpec((1, NSA_GROUP, d, tq), lambda bi, gi, i: (bi, gi, 0, i)),
                   pl.BlockSpec((1, 1, ns, tq), lambda bi, gi, i: (bi, gi, 0, i))],
        out_shape=[jax.ShapeDtypeStruct((b, h, d, t), F32),
                   jax.ShapeDtypeStruct((b, g, ns, t), _MXU_DTYPE)],
        scratch_shapes=[pltpu.VMEM((1, NSA_GROUP * tq), F32),
                        pltpu.VMEM((d + ONES_ROWS, NSA_GROUP * tq), F32),
                        pltpu.VMEM((ns, NSA_GROUP * tq), F32)],
        compiler_params=_cparams("parallel", "parallel", "parallel"),
        name="nsa_cmp_sel",
    )(qt, kc, vca, tab, c2s, bk)


def _out_proj_kernel(mla_ref, ret_ref, oc_ref, os_ref, ow_ref, g_ref, w_ref, r_ref, o_ref):
    d = NSA_DH
    w_mla, w_ret = MLA_HEADS * MLA_V, RET_HEADS * RET_DV
    acc = r_ref[...] + _dot(mla_ref[0].T.astype(_MXU_DTYPE), w_ref[0:w_mla, :])
    acc = acc + _dot(ret_ref[...], w_ref[w_mla:w_mla + w_ret, :])
    g = g_ref[0]
    nsa = []
    for h in range(NSA_HEADS):
        rows = slice(h * d, (h + 1) * d)
        g0 = h * GATE_STRIDE
        nsa.append(g[g0:g0 + 1] * oc_ref[0, rows, :] + g[g0 + 1:g0 + 2] * os_ref[0, rows, :]
                   + g[g0 + 2:g0 + 3] * ow_ref[0, rows, :])
    nsa = jnp.concatenate(nsa, axis=0).T.astype(_MXU_DTYPE)
    o_ref[...] = acc + _dot(nsa, w_ref[w_mla + w_ret:, :])


def _out_proj(o_mla, o_ret, oc, os_, ow, gates, w, res, t, tm=512):
    m, n = res.shape
    nt = t // tm
    ch_major = lambda c: pl.BlockSpec((1, c, tm), lambda i: (i // nt, 0, i % nt))
    tokens = lambda c: pl.BlockSpec((tm, c), lambda i: (i, 0))
    c_nsa = oc.shape[1]
    return pl.pallas_call(
        _out_proj_kernel,
        grid=(m // tm,),
        in_specs=[ch_major(o_mla.shape[1]), tokens(o_ret.shape[1]), ch_major(c_nsa), ch_major(c_nsa),
                  ch_major(c_nsa), ch_major(gates.shape[1]), _resident(w.shape), tokens(n)],
        out_specs=tokens(n),
        out_shape=jax.ShapeDtypeStruct((m, n), F32),
        compiler_params=_cparams("parallel"),
        name="out_proj",
    )(o_mla, o_ret, oc, os_, ow, gates, w, res)


def _ffn_kernel(x_ref, gn_ref, wup_ref, cw_ref, cb_ref, wdn_ref, o_ref, hbuf_ref, acc_ref, *,
                tm, fc):
    @pl.when(pl.program_id(1) == 0)
    def _():
        hbuf_ref[0:8, :] = jnp.zeros((8, hbuf_ref.shape[1]), F32)

    x = x_ref[...]
    y = x * lax.rsqrt(jnp.mean(x * x, axis=-1, keepdims=True) + EPS)
    xn = (y * gn_ref[...]).astype(_MXU_DTYPE)

    def up_proj(col0):
        cols = slice(col0, col0 + fc)
        hbuf_ref[8:tm + 8, cols] = _dot(xn, wup_ref[:, cols])

    def conv(col0):
        cols = slice(col0, col0 + fc)
        w = cw_ref[:, cols]
        h = hbuf_ref[8:tm + 8, cols]
        out = (h * w[2:3] + hbuf_ref[7:tm + 7, cols] * w[1:2] + hbuf_ref[6:tm + 6, cols] * w[0:1]
               + cb_ref[:, cols])
        hbuf_ref[0:8, cols] = hbuf_ref[tm:tm + 8, cols]
        return out

    nf = D_FF // fc
    ahead = 4
    for f in range(min(ahead, nf)):
        up_proj(f * fc)
        up_proj(D_FF + f * fc)
    for f in range(nf):
        if f + ahead < nf:
            up_proj((f + ahead) * fc)
            up_proj(D_FF + (f + ahead) * fc)
        gate = conv(f * fc)
        up = conv(D_FF + f * fc)
        act = (gate * _sigmoid(gate) * up).astype(_MXU_DTYPE)
        contrib = _dot(act, wdn_ref[f * fc:(f + 1) * fc, :])
        if f == 0:
            acc_ref[...] = contrib
        else:
            acc_ref[...] += contrib
    o_ref[...] = x + acc_ref[...]


def _ffn(x, gn, wup, cw, cb, wdn, b, t, tm=512, fc=256):
    d = x.shape[1]
    nt = t // tm
    return pl.pallas_call(
        functools.partial(_ffn_kernel, tm=tm, fc=fc),
        grid=(b, nt),
        in_specs=[pl.BlockSpec((tm, d), lambda bi, i: (bi * nt + i, 0)),
                  _resident((1, d)), _resident(wup.shape), _resident(cw.shape),
                  _resident(cb.shape), _resident(wdn.shape)],
        out_specs=pl.BlockSpec((tm, d), lambda bi, i: (bi * nt + i, 0)),
        out_shape=jax.ShapeDtypeStruct(x.shape, F32),
        scratch_shapes=[pltpu.VMEM((tm + 8, 2 * D_FF), F32), pltpu.VMEM((tm, d), F32)],
        compiler_params=_cparams("parallel", "arbitrary"),
        name="conv_ffn",
    )(x, gn, wup, cw, cb, wdn)


def _rope_tables(t, d):
    inv = ROPE_BASE ** (-np.arange(0, d, 2, dtype=np.float64) / d)
    ang = np.arange(t, dtype=np.float64)[:, None] * inv[None, :]
    return (np.concatenate([np.cos(ang)] * 2, axis=1), np.concatenate([np.sin(ang)] * 2, axis=1))


def _t5_bucket_np(dist):
    max_exact = REL_BUCKETS // 2
    d = np.maximum(dist, 1).astype(np.float64)
    log_b = max_exact + (np.log(d / max_exact) / math.log(REL_MAX_DIST / max_exact)
                         * (REL_BUCKETS - max_exact)).astype(np.int32)
    return np.where(dist < max_exact, dist, np.minimum(log_b, REL_BUCKETS - 1))


@functools.lru_cache(maxsize=None)
def _constants(t):
    c = {}
    cos, sin = _rope_tables(t, MLA_ROPE)
    pad = LANES - MLA_QK
    c["mla_cos"] = np.concatenate([np.ones((t, MLA_NOPE)), cos, np.ones((t, pad))], 1).astype(np.float32)
    c["mla_sin"] = np.concatenate([np.zeros((t, MLA_NOPE)), sin, np.zeros((t, pad))], 1).astype(np.float32)
    cos, sin = _rope_tables(t, RET_DK)
    c["ret_cos"] = np.tile(cos, (1, RET_HEADS)).astype(np.float32)
    c["ret_sin"] = np.tile(sin, (1, RET_HEADS)).astype(np.float32)
    lg = np.log(1.0 - 2.0 ** (-5.0 - np.arange(RET_HEADS, dtype=np.float64)))
    idx = np.arange(RET_CHUNK, dtype=np.float64)
    diff = idx[:, None] - idx[None, :]
    c["ret_din"] = (np.exp(np.maximum(diff, 0.0) * lg[:, None, None]) * (diff >= 0)).astype(np.float32)
    qd = np.exp((idx[:, None] + 1.0) * lg[None, :])
    kd = np.exp((RET_CHUNK - 1.0 - idx[:, None]) * lg[None, :])
    c["ret_qd"] = np.repeat(qd, RET_DK, axis=1).astype(np.float32)
    c["ret_kd"] = np.repeat(kd, RET_DK, axis=1).astype(np.float32)
    head_of = np.arange(RET_HEADS * RET_DK) // RET_DK
    c["ret_cdm"] = np.broadcast_to(np.exp(RET_CHUNK * lg)[head_of][:, None],
                                   (RET_HEADS * RET_DK, RET_HEADS * RET_DV)).astype(np.float32)
    c["ret_bd"] = (head_of[:, None] == head_of[None, :]).astype(np.float32)
    c["bucket"] = _t5_bucket_np(np.arange(LANES)).astype(np.int32)
    kk = np.arange(ATT_TILE)[:, None]
    qq = np.arange(ATT_TILE)[None, :]
    causal = np.where(qq >= kk, 0.0, NEG_INF)
    c["causal_tab"] = np.stack([np.concatenate([causal, np.full_like(causal, NEG_INF)], 0),
                                np.concatenate([np.zeros_like(causal), causal], 0)])[None].astype(np.float32)
    nc, ns = t // CMP_STRIDE, t // SEL_BLOCK
    n_cmp = (t - CMP_BLOCK) // CMP_STRIDE + 1
    c_start = np.arange(nc) * CMP_STRIDE
    s_start = np.arange(ns) * SEL_BLOCK
    overlap = np.clip(np.minimum(c_start[:, None] + CMP_BLOCK, s_start[None, :] + SEL_BLOCK)
                      - np.maximum(c_start[:, None], s_start[None, :]), 0, None).astype(np.float64)
    overlap[n_cmp:] = 0.0
    c["c2s"] = np.concatenate([np.zeros((CMP_CHUNK, ns)), overlap / CMP_BLOCK]).astype(np.float32)
    q = np.arange(CMP_TQ)[None, :]
    c["bk"] = (q // SEL_BLOCK - np.arange(ns)[:, None]).astype(np.int32)
    c["cmp_dist"] = (q - CMP_STRIDE * np.arange(CMP_CHUNK)[:, None]
                     + (CMP_STRIDE * CMP_CHUNK - CMP_TQ - (CMP_BLOCK - 1))).astype(np.int32)
    c["sel_onehot"] = (np.arange(t)[:, None] // SEL_BLOCK == np.arange(ns)[None, :]).astype(np.float32)
    return c


def _cols(w, pieces):
    out = []
    for p in pieces:
        if p[0] is None:
            out.append(jnp.zeros((w.shape[0], p[1]), w.dtype))
        else:
            blk = w[:, p[0]:p[1]]
            out.append(-blk if p[2] < 0 else blk)
    return jnp.concatenate(out, axis=1)


def _rot_pieces(base, d):
    return [(base + d // 2, base + d, -1), (base, base + d // 2, 1)]


def _toeplitz(w, n):
    hh, ll = w.shape
    flat = jnp.tile(w, (1, n))[:, :n * (ll - 1)]
    return flat.reshape(hh, n, ll - 1)[:, :, :n]


def _pad_to(v, n):
    return jnp.concatenate([v, jnp.zeros((n - v.shape[0],), v.dtype)])


def _layer(xf, b, t, cst, tabs, w_in, w_out, attn_norm, ffn_norm, mla_q_a_norm, mla_w_uq,
           mla_kv_a_norm, mla_w_ukv, mla_q_norm, mla_k_norm, ret_norm, pos_k, w1_k, w2_k, pos_v,
           w1_v, w2_v, nsa_q_norm, kn_cmp, kn_sel, kn_win, ffn_w_up, ffn_conv_w, ffn_conv_b,
           ffn_w_down):
    md = _MXU_DTYPE
    o = _IN_OFF
    pieces = [(o[0], o[1], 1), (None, 64), (o[1], o[2], 1),
              (None, 64), (o[2], o[3], 1), (None, 32),
              (None, 64)] + _rot_pieces(o[2], MLA_ROPE) + [(None, 32)]
    pieces += [(o[3], o[7], 1)]
    for base in (o[3], o[4]):
        for h in range(RET_HEADS):
            pieces += _rot_pieces(base + h * RET_DK, RET_DK)
    d_ = NSA_DH
    pieces += [(o[7], o[10], 1)]
    for base in (o[10], o[12]):
        pieces += [(base, base + d_, 1), (None, LANES - d_), (base + d_, base + 2 * d_, 1),
                   (None, LANES - d_), (base + 2 * d_, base + 4 * d_, 1)]
    for h in range(NSA_HEADS):
        pieces += [(o[14] + 3 * h, o[14] + 3 * h + 3, 1), (None, GATE_STRIDE - 3)]
    pieces += [(None, LANES - NSA_HEADS * GATE_STRIDE)]
    w_in_r = _cols(w_in, pieces).astype(md)
    pm, pr, pn = _in_proj(xf, attn_norm[None, :], w_in_r, (640, 1536, NSA_SLAB))

    wq_pieces, wq_rot = [], []
    wkv_k, wkv_v = [], []
    for h in range(MLA_HEADS):
        qb = h * MLA_QK
        wq_pieces += [(qb, qb + MLA_QK, 1), (None, LANES - MLA_QK)]
        wq_rot += [(None, MLA_NOPE)] + _rot_pieces(qb + MLA_NOPE, MLA_ROPE) + [(None, LANES - MLA_QK)]
        kb = h * (MLA_NOPE + MLA_V)
        wkv_k += [(kb, kb + MLA_NOPE, 1), (None, LANES - MLA_NOPE)]
        wkv_v += [(kb + MLA_NOPE, kb + MLA_NOPE + MLA_V, 1), (None, LANES - MLA_V)]
    wq = _cols(mla_w_uq, wq_pieces + wq_rot)
    wq = jnp.concatenate([wq, jnp.zeros((256 - MLA_Q_RANK, wq.shape[1]), wq.dtype)], axis=0).astype(md)
    wkv = _cols(mla_w_ukv, wkv_k + wkv_v).astype(md)
    half = MLA_ROPE // 2

    def rot_gain(gv):
        return jnp.concatenate([jnp.zeros((MLA_NOPE,), gv.dtype), gv[MLA_NOPE + half:],
                                gv[MLA_NOPE:MLA_NOPE + half], jnp.zeros((LANES - MLA_QK,), gv.dtype)])

    q, k, v = _mla_prep(pm, cst["mla_cos"], cst["mla_sin"], _pad_to(mla_q_a_norm, 256)[None, :],
                        mla_kv_a_norm[None, :], wq, wkv,
                        _pad_to(mla_q_norm, LANES)[None, :], rot_gain(mla_q_norm)[None, :],
                        _pad_to(mla_k_norm, LANES)[None, :], rot_gain(mla_k_norm)[None, :], b, t)
    o_mla = _flash(q, k, v, tabs["causal"], use_far=True, name="mla_attention")
    o_mla = o_mla.reshape(b, MLA_HEADS * MLA_V, t)

    o_ret = _retention(pr, cst["ret_cos"], cst["ret_sin"], cst["ret_din"], cst["ret_qd"],
                       cst["ret_kd"], cst["ret_cdm"], cst["ret_bd"],
                       jnp.tile(ret_norm, RET_HEADS)[None, :], b, t)

    g_ = NSA_KV_HEADS
    nc = t // CMP_STRIDE
    qt, ks_n, kw_n, vs_a, vw_a, gates = _nsa_prep(
        pn, nsa_q_norm * (d_ ** -0.5 * LOG2E), _pad_to(kn_sel, LANES)[None, :],
        _pad_to(kn_win, LANES)[None, :], b, t)

    def chunks(c0):
        a = pn[:, c0:c0 + g_ * d_].reshape(b, t, g_, d_).transpose(0, 2, 1, 3)
        return a.reshape(b * g_, nc, CMP_STRIDE * d_).astype(md)

    pos8 = lambda p: jnp.broadcast_to(p.reshape(1, -1), (8, CMP_BLOCK * d_)).astype(md)
    k_c, v_c = _compress(chunks(NSA_KC0), chunks(NSA_VC0), w1_k.astype(md), w2_k.astype(md),
                         pos8(pos_k), w1_v.astype(md), w2_v.astype(md), pos8(pos_v), kn_cmp[None, :])
    front = ((0, 0), (0, 0), (CMP_CHUNK, 0), (0, 0))
    k_c = jnp.pad(k_c.reshape(b, g_, nc, d_), front)
    v_c = v_c.reshape(b, g_, nc, d_).astype(F32)
    v_ca = jnp.pad(jnp.concatenate([v_c, jnp.ones_like(v_c[..., :1]),
                                    jnp.zeros_like(v_c[..., :LANES - d_ - 1])], axis=-1), front)
    oc_t, selneg = _cmp_sel(qt, k_c, v_ca, tabs["cmp"], cst["c2s"], cst["bk"], CMP_TQ)
    os_t = _flash(qt, ks_n, vs_a, tabs["sel"], use_far=True, name="nsa_selected",
                  qx=selneg, kx=cst["sel_onehot"].astype(md))
    ow_t = _flash(qt, kw_n, vw_a, tabs["win"], use_far=False, name="nsa_window")

    flat = lambda a: a.reshape(b, NSA_HEADS * d_, t)
    xf = _out_proj(o_mla, o_ret, flat(oc_t), flat(os_t), flat(ow_t), gates, w_out.astype(md), xf, t)
    return _ffn(xf, ffn_norm[None, :], ffn_w_up.astype(md), ffn_conv_w, ffn_conv_b[None, :],
                ffn_w_down.astype(md), b, t)


def _bias_tables(rel_bias, cst):
    n = ATT_TILE
    lut = rel_bias[cst["bucket"]].T
    delta = (lut - rel_bias[REL_BUCKETS - 1][:, None]) * LOG2E
    hh = delta.shape[0]
    dn = jnp.concatenate([delta, jnp.zeros((hh, n - LANES), F32)], axis=1)
    neg = jnp.full((hh, n), NEG_INF, F32)
    zero = jnp.zeros((hh, n), F32)
    diag = _toeplitz(jnp.concatenate([dn, neg], axis=1), n)
    prev_sel = _toeplitz(jnp.concatenate([zero, dn], axis=1), n)
    prev_win = _toeplitz(jnp.concatenate([neg, dn], axis=1), n)
    dist = cst["cmp_dist"]
    cmp_tab = jnp.where((dist < 0)[None], NEG_INF,
                        jnp.where((dist < LANES)[None], delta[:, np.clip(dist, 0, LANES - 1)], 0.0))
    masked = jnp.full((hh, n, n), NEG_INF, F32)
    first = jnp.concatenate([diag, masked], axis=1)

    def near(prev):
        return jnp.stack([first, jnp.concatenate([prev, diag], axis=1)], axis=1)

    cmp_tab = cmp_tab.reshape(NSA_KV_HEADS, NSA_GROUP, *dist.shape).transpose(0, 2, 1, 3).reshape(
        NSA_KV_HEADS, dist.shape[0], NSA_GROUP * dist.shape[1])
    return {"sel": near(prev_sel), "win": near(prev_win), "cmp": cmp_tab,
            "causal": jnp.asarray(cst["causal_tab"])}


def kernel(x, w_in, w_out, attn_norm, ffn_norm, mla_q_a_norm, mla_w_uq, mla_kv_a_norm, mla_w_ukv, mla_q_norm, mla_k_norm, ret_norm, nsa_cmp_pos_k, nsa_cmp_w1_k, nsa_cmp_w2_k, nsa_cmp_pos_v, nsa_cmp_w1_v, nsa_cmp_w2_v, nsa_q_norm, nsa_k_norm_cmp, nsa_k_norm_sel, nsa_k_norm_win, rel_bias, ffn_w_up, ffn_conv_w, ffn_conv_b, ffn_w_down):
    b, t, d = x.shape
    assert d == D_MODEL and t % ATT_TILE == 0 and WINDOW == ATT_TILE
    cst = _constants(t)
    tabs = _bias_tables(rel_bias, cst)
    per_layer = (w_in, w_out, attn_norm, ffn_norm, mla_q_a_norm, mla_w_uq, mla_kv_a_norm, mla_w_ukv,
                 mla_q_norm, mla_k_norm, ret_norm, nsa_cmp_pos_k, nsa_cmp_w1_k, nsa_cmp_w2_k,
                 nsa_cmp_pos_v, nsa_cmp_w1_v, nsa_cmp_w2_v, nsa_q_norm, nsa_k_norm_cmp,
                 nsa_k_norm_sel, nsa_k_norm_win, ffn_w_up, ffn_conv_w, ffn_conv_b, ffn_w_down)
    xf = x.reshape(b * t, d)
    for l in range(w_in.shape[0]):
        xf = _layer(xf, b, t, cst, tabs, *[p[l] for p in per_layer])
    return xf.reshape(b, t, d)
```

```python
import functools
import math

import numpy as np
import jax
import jax.numpy as jnp
from jax import lax
from jax.experimental import pallas as pl
from jax.experimental.pallas import tpu as pltpu

D_MODEL = 1024
DEPTH = 2
MLA_HEADS = 6
MLA_Q_RANK = 192
MLA_KV_RANK = 128
MLA_NOPE = 64
MLA_ROPE = 32
MLA_V = 64
MLA_QK = MLA_NOPE + MLA_ROPE
RET_HEADS = 4
RET_DK = 64
RET_DV = 64
RET_CHUNK = 128
NSA_HEADS = 6
NSA_KV_HEADS = 2
NSA_GROUP = NSA_HEADS // NSA_KV_HEADS
NSA_DH = 64
CMP_BLOCK = 32
CMP_STRIDE = 16
CMP_HIDDEN = 256
SEL_BLOCK = 64
SEL_TOPK = 16
SEL_LOCAL = 2
WINDOW = 512
REL_BUCKETS = 32
REL_MAX_DIST = 128
D_FF = 2816
ROPE_BASE = 10000.0
EPS = 1e-6
NEG_INF = -1e30
FORCE = 1e9

_IN_SPLITS = (MLA_Q_RANK, MLA_KV_RANK, MLA_ROPE,
              RET_HEADS * RET_DK, RET_HEADS * RET_DK, RET_HEADS * RET_DV, RET_HEADS * RET_DV,
              NSA_HEADS * NSA_DH) + (NSA_KV_HEADS * NSA_DH,) * 6 + (3 * NSA_HEADS,)
_IN_OFF = [0] + [int(v) for v in np.cumsum(_IN_SPLITS)]
D_IN = _IN_OFF[-1]

LANES = 128
ATT_TILE = 512
CMP_TQ = 256
CMP_CHUNK = 128
VMEM_LIMIT = 56 * 1024 * 1024

_MXU_DTYPE = jnp.bfloat16
F32 = jnp.float32
LOG2E = math.log2(math.e)


def _cparams(*sem):
    return pltpu.CompilerParams(dimension_semantics=sem, vmem_limit_bytes=VMEM_LIMIT)


def _dot(a, b):
    return jnp.dot(a, b, preferred_element_type=F32)


def _sigmoid(x):
    return 1.0 / (1.0 + jnp.exp(-x))


def _cast_kernel(x_ref, o_ref):
    o_ref[...] = x_ref[...].astype(o_ref.dtype)


def _to_mxu_dtype(w, rows=256):
    r, c = w.shape
    return pl.pallas_call(
        _cast_kernel,
        grid=(r // rows,),
        in_specs=[pl.BlockSpec((rows, c), lambda i: (i, 0))],
        out_specs=pl.BlockSpec((rows, c), lambda i: (i, 0)),
        out_shape=jax.ShapeDtypeStruct(w.shape, _MXU_DTYPE),
        compiler_params=_cparams("parallel"),
        name="weight_cast",
    )(w)


ONES_ROWS = 16


def _ones_rows(n, dtype):
    row = lax.broadcasted_iota(jnp.int32, (ONES_ROWS, n), 0)
    return jnp.where(row == 0, 1.0, 0.0).astype(dtype)


def _with_ones_rows(vt):
    b, h, _, t = vt.shape
    extra = jnp.concatenate([jnp.ones((b, h, 1, t), vt.dtype),
                             jnp.zeros((b, h, ONES_ROWS - 1, t), vt.dtype)], axis=2)
    return jnp.concatenate([vt, extra], axis=2)


def _resident(shape):
    nd = len(shape)
    return pl.BlockSpec(shape, lambda *_: (0,) * nd, pipeline_mode=pl.Buffered(1))


def _in_proj_kernel(x_ref, g_ref, w_ref, *o_refs, widths):
    x = x_ref[...]
    y = x * lax.rsqrt(jnp.mean(x * x, axis=-1, keepdims=True) + EPS)
    xn = (y * g_ref[...]).astype(_MXU_DTYPE)
    off = 0
    for o_ref, wd in zip(o_refs, widths):
        o_ref[...] = _dot(xn, w_ref[:, off:off + wd])
        off += wd


def _in_proj(x, g, w, widths, tm=256):
    m, d = x.shape
    n = w.shape[1]
    return pl.pallas_call(
        functools.partial(_in_proj_kernel, widths=widths),
        grid=(m // tm,),
        in_specs=[pl.BlockSpec((tm, d), lambda i: (i, 0)),
                  _resident((1, d)),
                  _resident((d, n))],
        out_specs=[pl.BlockSpec((tm, wd), lambda i: (i, 0)) for wd in widths],
        out_shape=[jax.ShapeDtypeStruct((m, wd), F32) for wd in widths],
        compiler_params=_cparams("parallel"),
        name="in_proj",
    )(x, g, w)


def _mla_prep_kernel(pm_ref, cos_ref, sin_ref, gqa_ref, gkva_ref, wq_ref, wkv_ref,
                     gq_ref, gqr_ref, gk_ref, gkr_ref, q_ref, k_ref, v_ref):
    pm = pm_ref[...]
    hs = MLA_HEADS * LANES
    cq = pm[:, 0:256]
    r = lax.rsqrt(jnp.sum(cq * cq, axis=-1, keepdims=True) * (1.0 / MLA_Q_RANK) + EPS)
    qq = _dot((cq * r * gqa_ref[...]).astype(_MXU_DTYPE), wq_ref[...])
    ckv = pm[:, 256:384]
    r = lax.rsqrt(jnp.mean(ckv * ckv, axis=-1, keepdims=True) + EPS)
    kv = _dot((ckv * r * gkva_ref[...]).astype(_MXU_DTYPE), wkv_ref[...])
    kpe = pm[:, 384:512]
    kpe_rot = pm[:, 512:640]
    cos = cos_ref[...]
    sin = sin_ref[...]
    scale = MLA_QK ** -0.5 * LOG2E
    aq = cos * gq_ref[...] * scale
    bq = sin * gqr_ref[...] * scale
    ak = cos * gk_ref[...]
    bk = sin * gkr_ref[...]
    for h in range(MLA_HEADS):
        sl = slice(h * LANES, (h + 1) * LANES)
        sr = slice(hs + h * LANES, hs + (h + 1) * LANES)
        qh = qq[:, sl]
        rq = lax.rsqrt(jnp.sum(qh * qh, axis=-1, keepdims=True) * (1.0 / MLA_QK) + EPS)
        q_ref[0, h] = ((qh * aq + qq[:, sr] * bq) * rq).T.astype(q_ref.dtype)
        kh = kv[:, sl] + kpe
        rk = lax.rsqrt(jnp.sum(kh * kh, axis=-1, keepdims=True) * (1.0 / MLA_QK) + EPS)
        k_ref[0, h] = ((kh * ak + kpe_rot * bk) * rk).astype(k_ref.dtype)
        v_ref[0, h, 0:MLA_V, :] = kv[:, sr].T[0:MLA_V].astype(v_ref.dtype)
        v_ref[0, h, MLA_V:MLA_V + ONES_ROWS, :] = _ones_rows(pm.shape[0], v_ref.dtype)


def _mla_prep(pm, cos, sin, gqa, gkva, wq, wkv, gq, gqr, gk, gkr, b, t, tm=512):
    nt = t // tm
    hs = MLA_HEADS * LANES
    vec = lambda n: _resident((1, n))
    return pl.pallas_call(
        _mla_prep_kernel,
        grid=(b, nt),
        in_specs=[pl.BlockSpec((tm, 640), lambda bi, i: (bi * nt + i, 0)),
                  pl.BlockSpec((tm, LANES), lambda bi, i: (i, 0)),
                  pl.BlockSpec((tm, LANES), lambda bi, i: (i, 0)),
                  vec(256), vec(LANES), _resident((256, 2 * hs)), _resident((LANES, 2 * hs)),
                  vec(LANES), vec(LANES), vec(LANES), vec(LANES)],
        out_specs=[pl.BlockSpec((1, MLA_HEADS, LANES, tm), lambda bi, i: (bi, 0, 0, i)),
                   pl.BlockSpec((1, MLA_HEADS, tm, LANES), lambda bi, i: (bi, 0, i, 0)),
                   pl.BlockSpec((1, MLA_HEADS, MLA_V + ONES_ROWS, tm), lambda bi, i: (bi, 0, 0, i))],
        out_shape=[jax.ShapeDtypeStruct((b, MLA_HEADS, LANES, t), _MXU_DTYPE),
                   jax.ShapeDtypeStruct((b, MLA_HEADS, t, LANES), _MXU_DTYPE),
                   jax.ShapeDtypeStruct((b, MLA_HEADS, MLA_V + ONES_ROWS, t), _MXU_DTYPE)],
        compiler_params=_cparams("parallel", "parallel"),
        name="mla_prep",
    )(pm, cos, sin, gqa, gkva, wq, wkv, gq, gqr, gk, gkr)


def _tile_plan(nq, use_far):
    if use_far and nq % 2 == 0:
        return [lambda j: j, lambda j: nq - 1 - j]
    n = 4 if (not use_far and nq % 4 == 0) else 1
    return [functools.partial(lambda j, s: n * j + s, s=s) for s in range(n)]


def _paired_position(i, nq):
    return jnp.where(i < nq // 2, 2 * i, 2 * (nq - 1 - i) + 1)


def _flash_kernel(*refs, use_far, sub, dv, extra, tiles):
    ns = len(tiles)
    q_refs, refs = refs[:ns], refs[ns:]
    if extra:
        qx_refs, refs = refs[:ns], refs[ns:]
        k_ref, kx_ref, v_ref = refs[:3]
        refs = refs[3:]
    else:
        k_ref, v_ref = refs[:2]
        refs = refs[2:]
    tab_refs, o_ref = refs[:ns], refs[ns]
    step = pl.program_id(2)
    tc = ATT_TILE
    tq = ATT_TILE

    def block(q, tab_ref, start, nkeys, with_table, m, acc):
        kb = k_ref[0, 0, pl.ds(start, nkeys), :]
        if extra:
            kb = jnp.concatenate([kb, kx_ref[pl.ds(start, nkeys), :]], axis=1)
        s = _dot(kb, q)
        if with_table:
            s = s + tab_ref[0, 0]
        for j in range(nkeys // sub):
            sj = s[j * sub:(j + 1) * sub]
            vc = v_ref[0, 0, :, pl.ds(pl.multiple_of(start + j * sub, sub), sub)]
            m_new = jnp.maximum(m, jnp.max(sj, axis=0, keepdims=True))
            p = jnp.exp2(sj - m_new).astype(_MXU_DTYPE)
            acc = jnp.exp2(m - m_new) * acc + _dot(vc, p)
            m = m_new
        return m, acc

    state = []
    for slot in range(ns):
        i = tiles[slot](step)
        q = q_refs[slot][0, 0]
        if extra:
            q = jnp.concatenate([q, qx_refs[slot][0, 0]], axis=0)
        m = jnp.full((1, tq), -3e38, F32)
        acc = jnp.zeros((dv + ONES_ROWS, tq), F32)
        n_far = jnp.maximum(i - 1, 0)
        if use_far:
            done = 0
            for width in (4, 2, 1):
                count = (n_far - done) // width

                def body(c, carry, width=width, base=done, q=q):
                    start = pl.multiple_of((base + c * width) * tc, tc)
                    return block(q, None, start, width * tc, False, *carry)

                m, acc = lax.fori_loop(0, count, body, (m, acc))
                done = done + count * width
        state.append((q, n_far, m, acc))
    for slot, (q, n_far, m, acc) in enumerate(state):
        m, acc = block(q, tab_refs[slot], pl.multiple_of(n_far * tc, tc), 2 * tc, True, m, acc)
        o_ref[0, 0, :, slot * tq:(slot + 1) * tq] = acc[0:dv] / acc[dv:dv + 1]


def _flash(qt, k, va, tab, *, use_far, name, qx=None, kx=None, sub=256):
    b, h, dk, t = qt.shape
    hk, dva = va.shape[1], va.shape[2]
    dv = dva - ONES_ROWS
    rep = h // hk
    ht = tab.shape[0]
    tq = ATT_TILE
    extra = qx is not None
    tiles = _tile_plan(t // tq, use_far)
    ns = len(tiles)
    q_specs = [pl.BlockSpec((1, 1, dk, tq), lambda bi, hi, j, f=f: (bi, hi, 0, f(j))) for f in tiles]
    tab_specs = [pl.BlockSpec((1, 1, 2 * tq, tq),
                              lambda bi, hi, j, f=f: (hi % ht, jnp.minimum(f(j), 1), 0, 0))
                 for f in tiles]
    k_spec = pl.BlockSpec((1, 1, t, dk), lambda bi, hi, j: (bi, hi // rep, 0, 0))
    v_spec = pl.BlockSpec((1, 1, dva, t), lambda bi, hi, j: (bi, hi // rep, 0, 0))
    if extra:
        nx = qx.shape[2]
        qx_specs = [pl.BlockSpec((1, 1, nx, tq), lambda bi, hi, j, f=f: (bi, hi // rep, 0, f(j)))
                    for f in tiles]
        in_specs = q_specs + qx_specs + [k_spec, _resident((t, nx)), v_spec] + tab_specs
        args = (qt,) * ns + (qx,) * ns + (k, kx, va) + (tab,) * ns
    else:
        in_specs = q_specs + [k_spec, v_spec] + tab_specs
        args = (qt,) * ns + (k, va) + (tab,) * ns
    return pl.pallas_call(
        functools.partial(_flash_kernel, use_far=use_far, sub=sub, dv=dv, extra=extra, tiles=tiles),
        grid=(b, h, t // (tq * ns)),
        in_specs=in_specs,
        out_specs=pl.BlockSpec((1, 1, dv, ns * tq), lambda bi, hi, j: (bi, hi, 0, j)),
        out_shape=jax.ShapeDtypeStruct((b, h, dv, t), F32),
        compiler_params=_cparams("parallel", "parallel", "parallel"),
        name=name,
    )(*args)


def _ret_kernel(q_ref, k_ref, v_ref, g_ref, qr_ref, kr_ref, cos_ref, sin_ref, din_ref, qd_ref,
                kd_ref, cdm_ref, bd_ref, gn_ref, o_ref, state_ref, *, nchunk):
    @pl.when(pl.program_id(1) == 0)
    def _():
        state_ref[...] = jnp.zeros(state_ref.shape, F32)

    w = RET_HEADS * RET_DK
    lane = lax.broadcasted_iota(jnp.int32, (1, w), 1)
    heads = [(lane >= h * RET_DK) & (lane < (h + 1) * RET_DK) for h in range(RET_HEADS)]
    on_diag = bd_ref[...] > 0.5
    c_ = RET_CHUNK
    for c in range(nchunk):
        sl = slice(c * c_, (c + 1) * c_)
        cos = cos_ref[sl, :]
        sin = sin_ref[sl, :]
        qh = q_ref[sl, :] * cos + qr_ref[sl, :] * sin
        kh = (k_ref[sl, :] * cos + kr_ref[sl, :] * sin) * (RET_DK ** -0.5)
        kb = kh.astype(_MXU_DTYPE)
        vb = v_ref[sl, :].astype(_MXU_DTYPE)
        st = state_ref[...]
        out = _dot((qh * qd_ref[...]).astype(_MXU_DTYPE), st.astype(_MXU_DTYPE))
        for h in range(RET_HEADS):
            qm = jnp.where(heads[h], qh, 0.0).astype(_MXU_DTYPE)
            inner = lax.dot_general(qm, kb, (((1,), (1,)), ((), ())),
                                    preferred_element_type=F32) * din_ref[h]
            out = out + jnp.where(heads[h], _dot(inner.astype(_MXU_DTYPE), vb), 0.0)
        kdt = (kh * kd_ref[...]).T
        state_ref[...] = st * cdm_ref[...] + jnp.where(on_diag, _dot(kdt.astype(_MXU_DTYPE), vb), 0.0)
        o2 = out * out
        ms = jnp.zeros_like(out)
        for h in range(RET_HEADS):
            ssum = jnp.sum(jnp.where(heads[h], o2, 0.0), axis=-1, keepdims=True)
            ms = jnp.where(heads[h], ssum * (1.0 / RET_DV), ms)
        y = out * lax.rsqrt(ms + EPS) * gn_ref[...]
        gg = g_ref[sl, :]
        o_ref[sl, :] = (gg * _sigmoid(gg) * y).astype(o_ref.dtype)


def _retention(pr, cos, sin, din, qd, kd, cdm, bd, gn, b, t, tt=256):
    w = RET_HEADS * RET_DK
    nt = t // tt
    col = lambda j: pl.BlockSpec((tt, w), lambda bi, i: (bi * nt + i, j))
    pos = pl.BlockSpec((tt, w), lambda bi, i: (i, 0))
    return pl.pallas_call(
        functools.partial(_ret_kernel, nchunk=tt // RET_CHUNK),
        grid=(b, nt),
        in_specs=[col(0), col(1), col(2), col(3), col(4), col(5), pos, pos,
                  _resident(din.shape), _resident(qd.shape), _resident(kd.shape),
                  _resident(cdm.shape), _resident(bd.shape), _resident((1, w))],
        out_specs=pl.BlockSpec((tt, w), lambda bi, i: (bi * nt + i, 0)),
        out_shape=jax.ShapeDtypeStruct((b * t, w), _MXU_DTYPE),
        scratch_shapes=[pltpu.VMEM((w, w), F32)],
        compiler_params=_cparams("parallel", "arbitrary"),
        name="retention",
    )(pr, pr, pr, pr, pr, pr, cos, sin, din, qd, kd, cdm, bd, gn)


NSA_Q0, NSA_KC0, NSA_VC0, NSA_KS0, NSA_VS0, NSA_KW0, NSA_VW0, NSA_GATE0 = (
    0, 384, 512, 640, 896, 1024, 1280, 1408)
NSA_SLAB = 1536
GATE_STRIDE = 8


def _nsa_prep_kernel(pn_ref, gq_ref, gks_ref, gkw_ref, qt_ref, ks_ref, kw_ref, vs_ref, vw_ref,
                     gt_ref):
    tm = pn_ref.shape[0]
    d = NSA_DH
    dt = qt_ref.dtype
    xq = pn_ref[:, NSA_Q0:NSA_Q0 + NSA_HEADS * d].T
    for h in range(NSA_HEADS):
        blk = xq[h * d:(h + 1) * d]
        r = lax.rsqrt(jnp.mean(blk * blk, axis=0, keepdims=True) + EPS)
        qt_ref[0, h, 0:d, :] = (blk * r * gq_ref[...]).astype(dt)
        qt_ref[0, h, d:2 * d, :] = jnp.zeros((d, tm), dt)
    for g in range(NSA_KV_HEADS):
        for c0, g_ref, o_ref in ((NSA_KS0, gks_ref, ks_ref), (NSA_KW0, gkw_ref, kw_ref)):
            slot = pn_ref[:, c0 + g * LANES:c0 + (g + 1) * LANES]
            r = lax.rsqrt(jnp.sum(slot * slot, axis=-1, keepdims=True) * (1.0 / d) + EPS)
            o_ref[0, g] = (slot * r * g_ref[...]).astype(dt)
    for c0, o_ref in ((NSA_VS0, vs_ref), (NSA_VW0, vw_ref)):
        vt = pn_ref[:, c0:c0 + LANES].T
        for g in range(NSA_KV_HEADS):
            o_ref[0, g, 0:d, :] = vt[g * d:(g + 1) * d].astype(dt)
            o_ref[0, g, d:d + ONES_ROWS, :] = _ones_rows(tm, dt)
    gt = pn_ref[:, NSA_GATE0:NSA_GATE0 + LANES].T
    gt_ref[0] = _sigmoid(gt[0:NSA_HEADS * GATE_STRIDE])


def _nsa_prep(pn, gq, gks, gkw, b, t, tm=512):
    nt = t // tm
    d = NSA_DH
    g = NSA_KV_HEADS
    md = _MXU_DTYPE
    ch_major = lambda n, r: pl.BlockSpec((1, n, r, tm), lambda bi, i: (bi, 0, 0, i))
    natural = pl.BlockSpec((1, g, tm, LANES), lambda bi, i: (bi, 0, i, 0))
    return pl.pallas_call(
        _nsa_prep_kernel,
        grid=(b, nt),
        in_specs=[pl.BlockSpec((tm, NSA_SLAB), lambda bi, i: (bi * nt + i, 0)),
                  _resident((d, tm)), _resident((1, LANES)), _resident((1, LANES))],
        out_specs=[ch_major(NSA_HEADS, 2 * d), natural, natural,
                   ch_major(g, d + ONES_ROWS), ch_major(g, d + ONES_ROWS),
                   pl.BlockSpec((1, NSA_HEADS * GATE_STRIDE, tm), lambda bi, i: (bi, 0, i))],
        out_shape=[jax.ShapeDtypeStruct((b, NSA_HEADS, 2 * d, t), md),
                   jax.ShapeDtypeStruct((b, g, t, LANES), md),
                   jax.ShapeDtypeStruct((b, g, t, LANES), md),
                   jax.ShapeDtypeStruct((b, g, d + ONES_ROWS, t), md),
                   jax.ShapeDtypeStruct((b, g, d + ONES_ROWS, t), md),
                   jax.ShapeDtypeStruct((b, NSA_HEADS * GATE_STRIDE, t), F32)],
        compiler_params=_cparams("parallel", "parallel"),
        name="nsa_prep",
    )(pn, jnp.broadcast_to(gq[:, None], (d, tm)), gks, gkw)


def _gelu_tanh(x):
    return 0.5 * x * (1.0 + jnp.tanh(math.sqrt(2.0 / math.pi) * (x + 0.044715 * (x * x * x))))


def _compress_kernel(ak_ref, av_ref, w1k_ref, w2k_ref, pk_ref, w1v_ref, w2v_ref, pv_ref, gk_ref,
                     kc_ref, vc_ref):
    half = CMP_STRIDE * NSA_DH

    def comp(a_ref, w1_ref, w2_ref, p_ref):
        a = a_ref[0]
        pb = _dot(p_ref[...], w1_ref[...])[0:1]
        second = _dot(a, w1_ref[half:2 * half, :])
        nc = second.shape[0]
        hid = _dot(a, w1_ref[0:half, :]) + pltpu.roll(second, nc - 1, 0) + pb
        return _dot(_gelu_tanh(hid).astype(_MXU_DTYPE), w2_ref[...])

    kc = comp(ak_ref, w1k_ref, w2k_ref, pk_ref)
    y = kc * lax.rsqrt(jnp.mean(kc * kc, axis=-1, keepdims=True) + EPS)
    kc_ref[0] = (y * gk_ref[...]).astype(kc_ref.dtype)
    vc_ref[0] = comp(av_ref, w1v_ref, w2v_ref, pv_ref).astype(vc_ref.dtype)


def _compress(ak, av, w1k, w2k, pk, w1v, w2v, pv, gk):
    n, nc, kk = ak.shape
    blk = pl.BlockSpec((1, nc, kk), lambda i: (i, 0, 0))
    out = pl.BlockSpec((1, nc, NSA_DH), lambda i: (i, 0, 0))
    w1 = _resident((2 * kk, CMP_HIDDEN))
    w2 = _resident((CMP_HIDDEN, NSA_DH))
    pp = _resident((8, 2 * kk))
    return pl.pallas_call(
        _compress_kernel,
        grid=(n,),
        in_specs=[blk, blk, w1, w2, pp, w1, w2, pp, _resident((1, NSA_DH))],
        out_specs=[out, out],
        out_shape=[jax.ShapeDtypeStruct((n, nc, NSA_DH), _MXU_DTYPE)] * 2,
        compiler_params=_cparams("parallel"),
        name="nsa_compress",
    )(ak, av, w1k, w2k, pk, w1v, w2v, pv, gk)


def _cmp_sel_kernel(q_ref, kc_ref, vca_ref, tab_ref, c2s_ref, bk_ref, oc_ref, sel_ref,
                    m_ref, acco_ref, acci_ref, *, tq, ns, d):
    i = pl.program_id(2)
    ch = CMP_CHUNK
    end = (tq // CMP_STRIDE) * (i + 1)
    n_far = (end - 1) // ch
    rowi = lax.broadcasted_iota(jnp.int32, (ch, NSA_GROUP * tq), 0)
    qpos = tq * i + lax.broadcasted_iota(jnp.int32, (1, tq), 1)
    sees_any = jnp.where(qpos >= CMP_BLOCK - 1, 1.0, 0.0)

    q = jnp.concatenate([q_ref[0, r] for r in range(NSA_GROUP)], axis=1)
    m_ref[...] = jnp.full(m_ref.shape, -3e38, F32)
    acco_ref[...] = jnp.zeros(acco_ref.shape, F32)
    acci_ref[...] = jnp.zeros(acci_ref.shape, F32)

    def chunk(g, with_table):
        start = pl.multiple_of(end - ch * g, CMP_STRIDE)
        kc = kc_ref[0, 0, pl.ds(start, ch), :]
        vt = vca_ref[0, 0, pl.ds(start, ch), :].T[0:d + ONES_ROWS].astype(_MXU_DTYPE)
        ct = c2s_ref[pl.ds(start, ch), :].T.astype(_MXU_DTYPE)
        s = _dot(kc, q)
        if with_table:
            s = s + tab_ref[0]
        s = jnp.where(rowi >= ch * (g + 1) - end, s, NEG_INF)
        m_old = m_ref[...]
        m_new = jnp.maximum(m_old, jnp.max(s, axis=0, keepdims=True))
        alpha = jnp.exp2(m_old - m_new)
        e = jnp.exp2(s - m_new).astype(_MXU_DTYPE)
        acco_ref[...] = alpha * acco_ref[...] + _dot(vt, e)
        acci_ref[...] = alpha * acci_ref[...] + _dot(ct, e)
        m_ref[...] = m_new

    chunk(0, True)

    def far(g, carry):
        chunk(g, False)
        return carry

    lax.fori_loop(1, n_far + 1, far, 0)

    imp = jnp.zeros((ns, tq), F32)
    for r in range(NSA_GROUP):
        lanes = slice(r * tq, (r + 1) * tq)
        inv = sees_any / acco_ref[d:d + 1, lanes]
        oc_ref[0, r] = acco_ref[0:d, lanes] * inv
        imp = imp + acci_ref[:, lanes] * inv

    back = bk_ref[...] + i * (tq // SEL_BLOCK)
    jidx = lax.broadcasted_iota(jnp.int32, (ns, tq), 0)
    forced = (jidx == 0) | ((back >= 0) & (back < SEL_LOCAL))
    jf = jidx.astype(F32)
    taken = -3e38

    def pick(imp, rounds):
        for _ in range(rounds):
            mx = jnp.max(imp, axis=0, keepdims=True)
            first = jnp.min(jnp.where(imp == mx, jf, 1e9), axis=0, keepdims=True)
            imp = jnp.where(jf == first, taken, imp)
        sel_ref[0, 0] = jnp.where(imp == taken, 0.0, NEG_INF).astype(sel_ref.dtype)

    top_k = min(SEL_TOPK, ns)
    n_forced = 1 + SEL_LOCAL

    @pl.when(i == 0)
    def _():
        pick(jnp.where(back >= 0, jnp.where(forced, FORCE, imp), NEG_INF), top_k)

    @pl.when(i > 0)
    def _():
        pick(jnp.where(forced, taken, jnp.where(back >= 0, imp, NEG_INF)), top_k - n_forced)


def _cmp_sel(qt, kc, vca, tab, c2s, bk, tq):
    b, h, _, t = qt.shape
    g, npad, d = kc.shape[1:]
    ns = c2s.shape[1]
    assert tq >= SEL_LOCAL * SEL_BLOCK and tq // CMP_STRIDE <= CMP_CHUNK
    return pl.pallas_call(
        functools.partial(_cmp_sel_kernel, tq=tq, ns=ns, d=d),
        grid=(b, g, t // tq),
        in_specs=[pl.BlockSpec((1, NSA_GROUP, d, tq), lambda bi, gi, i: (bi, gi, 0, i)),
                  pl.BlockSpec((1, 1, npad, d), lambda bi, gi, i: (bi, gi, 0, 0)),
                  pl.BlockSpec((1, 1, npad, LANES), lambda bi, gi, i: (bi, gi, 0, 0)),
                  pl.BlockSpec((1, CMP_CHUNK, NSA_GROUP * tq), lambda bi, gi, i: (gi, 0, 0)),
                  _resident(c2s.shape), _resident(bk.shape)],
        out_specs=[pl.BlockSpec((1, NSA_GROUP, d, tq), lambda bi, gi, i: (bi, gi, 0, i)),
                   pl.BlockSpec((1, 1, ns, tq), lambda bi, gi, i: (bi, gi, 0, i))],
        out_shape=[jax.ShapeDtypeStruct((b, h, d, t), F32),
                   jax.ShapeDtypeStruct((b, g, ns, t), _MXU_DTYPE)],
        scratch_shapes=[pltpu.VMEM((1, NSA_GROUP * tq), F32),
                        pltpu.VMEM((d + ONES_ROWS, NSA_GROUP * tq), F32),
                        pltpu.VMEM((ns, NSA_GROUP * tq), F32)],
        compiler_params=_cparams("parallel", "parallel", "parallel"),
        name="nsa_cmp_sel",
    )(qt, kc, vca, tab, c2s, bk)


def _out_proj_kernel(mla_ref, ret_ref, oc_ref, os_ref, ow_ref, g_ref, w_ref, r_ref, o_ref):
    d = NSA_DH
    w_mla, w_ret = MLA_HEADS * MLA_V, RET_HEADS * RET_DV
    acc = r_ref[...] + _dot(mla_ref[0].T.astype(_MXU_DTYPE), w_ref[0:w_mla, :])
    acc = acc + _dot(ret_ref[...], w_ref[w_mla:w_mla + w_ret, :])
    g = g_ref[0]
    nsa = []
    for h in range(NSA_HEADS):
        rows = slice(h * d, (h + 1) * d)
        g0 = h * GATE_STRIDE
        nsa.append(g[g0:g0 + 1] * oc_ref[0, rows, :] + g[g0 + 1:g0 + 2] * os_ref[0, rows, :]
                   + g[g0 + 2:g0 + 3] * ow_ref[0, rows, :])
    nsa = jnp.concatenate(nsa, axis=0).T.astype(_MXU_DTYPE)
    o_ref[...] = acc + _dot(nsa, w_ref[w_mla + w_ret:, :])


def _out_proj(o_mla, o_ret, oc, os_, ow, gates, w, res, t, tm=512):
    m, n = res.shape
    assert tm == ATT_TILE
    nt = t // tm
    ch_major = lambda c: pl.BlockSpec((1, c, tm), lambda i: (i // nt, 0, i % nt))
    far_order = len(_tile_plan(nt, True)) == 2
    ch_paired = (lambda c: pl.BlockSpec((1, c, tm), lambda i: (i // nt, 0, _paired_position(i % nt, nt)))
                 ) if far_order else ch_major
    tokens = lambda c: pl.BlockSpec((tm, c), lambda i: (i, 0))
    c_nsa = oc.shape[1]
    return pl.pallas_call(
        _out_proj_kernel,
        grid=(m // tm,),
        in_specs=[ch_paired(o_mla.shape[1]), tokens(o_ret.shape[1]), ch_major(c_nsa), ch_paired(c_nsa),
                  ch_major(c_nsa), ch_major(gates.shape[1]), _resident(w.shape), tokens(n)],
        out_specs=tokens(n),
        out_shape=jax.ShapeDtypeStruct((m, n), F32),
        compiler_params=_cparams("parallel"),
        name="out_proj",
    )(o_mla, o_ret, oc, os_, ow, gates, w, res)


def _ffn_kernel(x_ref, gn_ref, wup_ref, cw_ref, cb_ref, wdn_ref, o_ref, hbuf_ref, acc_ref, *,
                tm, fc):
    @pl.when(pl.program_id(1) == 0)
    def _():
        hbuf_ref[0:8, :] = jnp.zeros((8, hbuf_ref.shape[1]), F32)

    x = x_ref[...]
    y = x * lax.rsqrt(jnp.mean(x * x, axis=-1, keepdims=True) + EPS)
    xn = (y * gn_ref[...]).astype(_MXU_DTYPE)

    def up_proj(col0):
        cols = slice(col0, col0 + fc)
        hbuf_ref[8:tm + 8, cols] = _dot(xn, wup_ref[:, cols])

    def conv(col0):
        cols = slice(col0, col0 + fc)
        w = cw_ref[:, cols]
        h = hbuf_ref[8:tm + 8, cols]
        out = (h * w[2:3] + hbuf_ref[7:tm + 7, cols] * w[1:2] + hbuf_ref[6:tm + 6, cols] * w[0:1]
               + cb_ref[:, cols])
        hbuf_ref[0:8, cols] = hbuf_ref[tm:tm + 8, cols]
        return out

    nf = D_FF // fc
    ahead = 4
    for f in range(min(ahead, nf)):
        up_proj(f * fc)
        up_proj(D_FF + f * fc)
    for f in range(nf):
        if f + ahead < nf:
            up_proj((f + ahead) * fc)
            up_proj(D_FF + (f + ahead) * fc)
        gate = conv(f * fc)
        up = conv(D_FF + f * fc)
        act = (gate * _sigmoid(gate) * up).astype(_MXU_DTYPE)
        contrib = _dot(act, wdn_ref[f * fc:(f + 1) * fc, :])
        if f == 0:
            acc_ref[...] = contrib
        else:
            acc_ref[...] += contrib
    o_ref[...] = x + acc_ref[...]


def _ffn(x, gn, wup, cw, cb, wdn, b, t, tm=512, fc=256):
    d = x.shape[1]
    nt = t // tm
    return pl.pallas_call(
        functools.partial(_ffn_kernel, tm=tm, fc=fc),
        grid=(b, nt),
        in_specs=[pl.BlockSpec((tm, d), lambda bi, i: (bi * nt + i, 0)),
                  _resident((1, d)), _resident(wup.shape), _resident(cw.shape),
                  _resident(cb.shape), _resident(wdn.shape)],
        out_specs=pl.BlockSpec((tm, d), lambda bi, i: (bi * nt + i, 0)),
        out_shape=jax.ShapeDtypeStruct(x.shape, F32),
        scratch_shapes=[pltpu.VMEM((tm + 8, 2 * D_FF), F32), pltpu.VMEM((tm, d), F32)],
        compiler_params=_cparams("parallel", "arbitrary"),
        name="conv_ffn",
    )(x, gn, wup, cw, cb, wdn)


def _rope_tables(t, d):
    inv = ROPE_BASE ** (-np.arange(0, d, 2, dtype=np.float64) / d)
    ang = np.arange(t, dtype=np.float64)[:, None] * inv[None, :]
    return (np.concatenate([np.cos(ang)] * 2, axis=1), np.concatenate([np.sin(ang)] * 2, axis=1))


def _t5_bucket_np(dist):
    max_exact = REL_BUCKETS // 2
    d = np.maximum(dist, 1).astype(np.float64)
    log_b = max_exact + (np.log(d / max_exact) / math.log(REL_MAX_DIST / max_exact)
                         * (REL_BUCKETS - max_exact)).astype(np.int32)
    return np.where(dist < max_exact, dist, np.minimum(log_b, REL_BUCKETS - 1))


@functools.lru_cache(maxsize=None)
def _constants(t):
    c = {}
    cos, sin = _rope_tables(t, MLA_ROPE)
    pad = LANES - MLA_QK
    c["mla_cos"] = np.concatenate([np.ones((t, MLA_NOPE)), cos, np.ones((t, pad))], 1).astype(np.float32)
    c["mla_sin"] = np.concatenate([np.zeros((t, MLA_NOPE)), sin, np.zeros((t, pad))], 1).astype(np.float32)
    cos, sin = _rope_tables(t, RET_DK)
    c["ret_cos"] = np.tile(cos, (1, RET_HEADS)).astype(np.float32)
    c["ret_sin"] = np.tile(sin, (1, RET_HEADS)).astype(np.float32)
    lg = np.log(1.0 - 2.0 ** (-5.0 - np.arange(RET_HEADS, dtype=np.float64)))
    idx = np.arange(RET_CHUNK, dtype=np.float64)
    diff = idx[:, None] - idx[None, :]
    c["ret_din"] = (np.exp(np.maximum(diff, 0.0) * lg[:, None, None]) * (diff >= 0)).astype(np.float32)
    qd = np.exp((idx[:, None] + 1.0) * lg[None, :])
    kd = np.exp((RET_CHUNK - 1.0 - idx[:, None]) * lg[None, :])
    c["ret_qd"] = np.repeat(qd, RET_DK, axis=1).astype(np.float32)
    c["ret_kd"] = np.repeat(kd, RET_DK, axis=1).astype(np.float32)
    head_of = np.arange(RET_HEADS * RET_DK) // RET_DK
    c["ret_cdm"] = np.broadcast_to(np.exp(RET_CHUNK * lg)[head_of][:, None],
                                   (RET_HEADS * RET_DK, RET_HEADS * RET_DV)).astype(np.float32)
    c["ret_bd"] = (head_of[:, None] == head_of[None, :]).astype(np.float32)
    c["bucket"] = _t5_bucket_np(np.arange(LANES)).astype(np.int32)
    kk = np.arange(ATT_TILE)[:, None]
    qq = np.arange(ATT_TILE)[None, :]
    causal = np.where(qq >= kk, 0.0, NEG_INF)
    c["causal_tab"] = np.stack([np.concatenate([causal, np.full_like(causal, NEG_INF)], 0),
                                np.concatenate([np.zeros_like(causal), causal], 0)])[None].astype(np.float32)
    nc, ns = t // CMP_STRIDE, t // SEL_BLOCK
    n_cmp = (t - CMP_BLOCK) // CMP_STRIDE + 1
    c_start = np.arange(nc) * CMP_STRIDE
    s_start = np.arange(ns) * SEL_BLOCK
    overlap = np.clip(np.minimum(c_start[:, None] + CMP_BLOCK, s_start[None, :] + SEL_BLOCK)
                      - np.maximum(c_start[:, None], s_start[None, :]), 0, None).astype(np.float64)
    overlap[n_cmp:] = 0.0
    c["c2s"] = np.concatenate([np.zeros((CMP_CHUNK, ns)), overlap / CMP_BLOCK]).astype(np.float32)
    q = np.arange(CMP_TQ)[None, :]
    c["bk"] = (q // SEL_BLOCK - np.arange(ns)[:, None]).astype(np.int32)
    c["cmp_dist"] = (q - CMP_STRIDE * np.arange(CMP_CHUNK)[:, None]
                     + (CMP_STRIDE * CMP_CHUNK - CMP_TQ - (CMP_BLOCK - 1))).astype(np.int32)
    c["sel_onehot"] = (np.arange(t)[:, None] // SEL_BLOCK == np.arange(ns)[None, :]).astype(np.float32)
    return c


def _cols(w, pieces):
    out = []
    for p in pieces:
        if p[0] is None:
            out.append(jnp.zeros((w.shape[0], p[1]), w.dtype))
        else:
            blk = w[:, p[0]:p[1]]
            out.append(-blk if p[2] < 0 else blk)
    return jnp.concatenate(out, axis=1)


def _rot_pieces(base, d):
    return [(base + d // 2, base + d, -1), (base, base + d // 2, 1)]


def _skew(w, rows, step, col0, cols):
    hh, ll = w.shape
    flat = jnp.tile(w, (1, rows))[:, :rows * (ll - step)]
    return flat.reshape(hh, rows, ll - step)[:, :, col0:col0 + cols]


def _toeplitz(w, n):
    return _skew(w, n, 1, 0, n)


def _pad_to(v, n):
    return jnp.concatenate([v, jnp.zeros((n - v.shape[0],), v.dtype)])


def _layer(xf, b, t, cst, tabs, w_in, w_out, attn_norm, ffn_norm, mla_q_a_norm, mla_w_uq,
           mla_kv_a_norm, mla_w_ukv, mla_q_norm, mla_k_norm, ret_norm, pos_k, w1_k, w2_k, pos_v,
           w1_v, w2_v, nsa_q_norm, kn_cmp, kn_sel, kn_win, ffn_w_up, ffn_conv_w, ffn_conv_b,
           ffn_w_down):
    md = _MXU_DTYPE
    o = _IN_OFF
    pieces = [(o[0], o[1], 1), (None, 64), (o[1], o[2], 1),
              (None, 64), (o[2], o[3], 1), (None, 32),
              (None, 64)] + _rot_pieces(o[2], MLA_ROPE) + [(None, 32)]
    pieces += [(o[3], o[7], 1)]
    for base in (o[3], o[4]):
        for h in range(RET_HEADS):
            pieces += _rot_pieces(base + h * RET_DK, RET_DK)
    d_ = NSA_DH
    pieces += [(o[7], o[10], 1)]
    for base in (o[10], o[12]):
        pieces += [(base, base + d_, 1), (None, LANES - d_), (base + d_, base + 2 * d_, 1),
                   (None, LANES - d_), (base + 2 * d_, base + 4 * d_, 1)]
    for h in range(NSA_HEADS):
        pieces += [(o[14] + 3 * h, o[14] + 3 * h + 3, 1), (None, GATE_STRIDE - 3)]
    pieces += [(None, LANES - NSA_HEADS * GATE_STRIDE)]
    w_in_r = _cols(_to_mxu_dtype(w_in), pieces)
    pm, pr, pn = _in_proj(xf, attn_norm[None, :], w_in_r, (640, 1536, NSA_SLAB))

    wq_pieces, wq_rot = [], []
    wkv_k, wkv_v = [], []
    for h in range(MLA_HEADS):
        qb = h * MLA_QK
        wq_pieces += [(qb, qb + MLA_QK, 1), (None, LANES - MLA_QK)]
        wq_rot += [(None, MLA_NOPE)] + _rot_pieces(qb + MLA_NOPE, MLA_ROPE) + [(None, LANES - MLA_QK)]
        kb = h * (MLA_NOPE + MLA_V)
        wkv_k += [(kb, kb + MLA_NOPE, 1), (None, LANES - MLA_NOPE)]
        wkv_v += [(kb + MLA_NOPE, kb + MLA_NOPE + MLA_V, 1), (None, LANES - MLA_V)]
    wq = _cols(mla_w_uq, wq_pieces + wq_rot)
    wq = jnp.concatenate([wq, jnp.zeros((256 - MLA_Q_RANK, wq.shape[1]), wq.dtype)], axis=0).astype(md)
    wkv = _cols(mla_w_ukv, wkv_k + wkv_v).astype(md)
    half = MLA_ROPE // 2

    def rot_gain(gv):
        return jnp.concatenate([jnp.zeros((MLA_NOPE,), gv.dtype), gv[MLA_NOPE + half:],
                                gv[MLA_NOPE:MLA_NOPE + half], jnp.zeros((LANES - MLA_QK,), gv.dtype)])

    q, k, v = _mla_prep(pm, cst["mla_cos"], cst["mla_sin"], _pad_to(mla_q_a_norm, 256)[None, :],
                        mla_kv_a_norm[None, :], wq, wkv,
                        _pad_to(mla_q_norm, LANES)[None, :], rot_gain(mla_q_norm)[None, :],
                        _pad_to(mla_k_norm, LANES)[None, :], rot_gain(mla_k_norm)[None, :], b, t)
    o_mla = _flash(q, k, v, tabs["causal"], use_far=True, name="mla_attention")
    o_mla = o_mla.reshape(b, MLA_HEADS * MLA_V, t)

    o_ret = _retention(pr, cst["ret_cos"], cst["ret_sin"], cst["ret_din"], cst["ret_qd"],
                       cst["ret_kd"], cst["ret_cdm"], cst["ret_bd"],
                       jnp.tile(ret_norm, RET_HEADS)[None, :], b, t)

    g_ = NSA_KV_HEADS
    nc = t // CMP_STRIDE
    qt, ks_n, kw_n, vs_a, vw_a, gates = _nsa_prep(
        pn, nsa_q_norm * (d_ ** -0.5 * LOG2E), _pad_to(kn_sel, LANES)[None, :],
        _pad_to(kn_win, LANES)[None, :], b, t)

    def chunks(c0):
        a = pn[:, c0:c0 + g_ * d_].reshape(b, t, g_, d_).transpose(0, 2, 1, 3)
        return a.reshape(b * g_, nc, CMP_STRIDE * d_).astype(md)

    pos8 = lambda p: jnp.broadcast_to(p.reshape(1, -1), (8, CMP_BLOCK * d_)).astype(md)
    k_c, v_c = _compress(chunks(NSA_KC0), chunks(NSA_VC0), _to_mxu_dtype(w1_k), w2_k.astype(md),
                         pos8(pos_k), _to_mxu_dtype(w1_v), w2_v.astype(md), pos8(pos_v),
                         kn_cmp[None, :])
    front = ((0, 0), (0, 0), (CMP_CHUNK, 0), (0, 0))
    k_c = jnp.pad(k_c.reshape(b, g_, nc, d_), front)
    v_c = v_c.reshape(b, g_, nc, d_).astype(F32)
    v_ca = jnp.pad(jnp.concatenate([v_c, jnp.ones_like(v_c[..., :1]),
                                    jnp.zeros_like(v_c[..., :LANES - d_ - 1])], axis=-1), front)
    oc_t, selneg = _cmp_sel(qt, k_c, v_ca, tabs["cmp"], cst["c2s"], cst["bk"], CMP_TQ)
    os_t = _flash(qt, ks_n, vs_a, tabs["sel"], use_far=True, name="nsa_selected",
                  qx=selneg, kx=cst["sel_onehot"].astype(md))
    ow_t = _flash(qt, kw_n, vw_a, tabs["win"], use_far=False, name="nsa_window")

    flat = lambda a: a.reshape(b, NSA_HEADS * d_, t)
    xf = _out_proj(o_mla, o_ret, flat(oc_t), flat(os_t), flat(ow_t), gates, _to_mxu_dtype(w_out),
                   xf, t)
    return _ffn(xf, ffn_norm[None, :], _to_mxu_dtype(ffn_w_up), ffn_conv_w, ffn_conv_b[None, :],
                _to_mxu_dtype(ffn_w_down), b, t)


def _bias_tables(rel_bias, cst):
    n = ATT_TILE
    lut = rel_bias[cst["bucket"]].T
    delta = (lut - rel_bias[REL_BUCKETS - 1][:, None]) * LOG2E
    hh = delta.shape[0]
    dn = jnp.concatenate([delta, jnp.zeros((hh, n - LANES), F32)], axis=1)
    neg = jnp.full((hh, n), NEG_INF, F32)
    zero = jnp.zeros((hh, n), F32)
    diag = _toeplitz(jnp.concatenate([dn, neg], axis=1), n)
    prev_sel = _toeplitz(jnp.concatenate([zero, dn], axis=1), n)
    prev_win = _toeplitz(jnp.concatenate([neg, dn], axis=1), n)
    off = CMP_STRIDE * CMP_CHUNK - CMP_TQ - (CMP_BLOCK - 1)
    d_max = off + CMP_TQ - 1
    d_min = off - CMP_STRIDE * (CMP_CHUNK - 1)
    by_dist = jnp.concatenate([delta, jnp.zeros((hh, d_max + 1 - LANES), F32),
                               jnp.full((hh, -d_min), NEG_INF, F32)], axis=1)
    cmp_tab = _skew(by_dist, CMP_CHUNK, CMP_STRIDE, off, CMP_TQ)
    dist = cst["cmp_dist"]
    masked = jnp.full((hh, n, n), NEG_INF, F32)
    first = jnp.concatenate([diag, masked], axis=1)

    def near(prev):
        return jnp.stack([first, jnp.concatenate([prev, diag], axis=1)], axis=1)

    cmp_tab = cmp_tab.reshape(NSA_KV_HEADS, NSA_GROUP, *dist.shape).transpose(0, 2, 1, 3).reshape(
        NSA_KV_HEADS, dist.shape[0], NSA_GROUP * dist.shape[1])
    return {"sel": near(prev_sel), "win": near(prev_win), "cmp": cmp_tab,
            "causal": jnp.asarray(cst["causal_tab"])}


def kernel(x, w_in, w_out, attn_norm, ffn_norm, mla_q_a_norm, mla_w_uq, mla_kv_a_norm, mla_w_ukv, mla_q_norm, mla_k_norm, ret_norm, nsa_cmp_pos_k, nsa_cmp_w1_k, nsa_cmp_w2_k, nsa_cmp_pos_v, nsa_cmp_w1_v, nsa_cmp_w2_v, nsa_q_norm, nsa_k_norm_cmp, nsa_k_norm_sel, nsa_k_norm_win, rel_bias, ffn_w_up, ffn_conv_w, ffn_conv_b, ffn_w_down):
    b, t, d = x.shape
    assert d == D_MODEL and t % ATT_TILE == 0 and WINDOW == ATT_TILE
    cst = _constants(t)
    tabs = _bias_tables(rel_bias, cst)
    per_layer = (w_in, w_out, attn_norm, ffn_norm, mla_q_a_norm, mla_w_uq, mla_kv_a_norm, mla_w_ukv,
                 mla_q_norm, mla_k_norm, ret_norm, nsa_cmp_pos_k, nsa_cmp_w1_k, nsa_cmp_w2_k,
                 nsa_cmp_pos_v, nsa_cmp_w1_v, nsa_cmp_w2_v, nsa_q_norm, nsa_k_norm_cmp,
                 nsa_k_norm_sel, nsa_k_norm_win, ffn_w_up, ffn_conv_w, ffn_conv_b, ffn_w_down)
    xf = x.reshape(b * t, d)
    for l in range(w_in.shape[0]):
        xf = _layer(xf, b, t, cst, tabs, *[p[l] for p in per_layer])
    return xf.reshape(b, t, d)
```

```python
import functools
import math

import numpy as np
import jax
import jax.numpy as jnp
from jax import lax
from jax.experimental import pallas as pl
from jax.experimental.pallas import tpu as pltpu

D_MODEL = 1024
DEPTH = 2
MLA_HEADS = 6
MLA_Q_RANK = 192
MLA_KV_RANK = 128
MLA_NOPE = 64
MLA_ROPE = 32
MLA_V = 64
MLA_QK = MLA_NOPE + MLA_ROPE
RET_HEADS = 4
RET_DK = 64
RET_DV = 64
RET_CHUNK = 128
NSA_HEADS = 6
NSA_KV_HEADS = 2
NSA_GROUP = NSA_HEADS // NSA_KV_HEADS
NSA_DH = 64
CMP_BLOCK = 32
CMP_STRIDE = 16
CMP_HIDDEN = 256
SEL_BLOCK = 64
SEL_TOPK = 16
SEL_LOCAL = 2
WINDOW = 512
REL_BUCKETS = 32
REL_MAX_DIST = 128
D_FF = 2816
ROPE_BASE = 10000.0
EPS = 1e-6
NEG_INF = -1e30
FORCE = 1e9

_IN_SPLITS = (MLA_Q_RANK, MLA_KV_RANK, MLA_ROPE,
              RET_HEADS * RET_DK, RET_HEADS * RET_DK, RET_HEADS * RET_DV, RET_HEADS * RET_DV,
              NSA_HEADS * NSA_DH) + (NSA_KV_HEADS * NSA_DH,) * 6 + (3 * NSA_HEADS,)
_IN_OFF = [0] + [int(v) for v in np.cumsum(_IN_SPLITS)]
D_IN = _IN_OFF[-1]

LANES = 128
ATT_TILE = 512
CMP_TQ = 256
CMP_CHUNK = 128
VMEM_LIMIT = 56 * 1024 * 1024

_MXU_DTYPE = jnp.bfloat16
F32 = jnp.float32
LOG2E = math.log2(math.e)


def _cparams(*sem):
    return pltpu.CompilerParams(dimension_semantics=sem, vmem_limit_bytes=VMEM_LIMIT)


def _dot(a, b):
    return jnp.dot(a, b, preferred_element_type=F32)


def _sigmoid(x):
    return 1.0 / (1.0 + jnp.exp(-x))


def _cast_kernel(x_ref, o_ref):
    o_ref[...] = x_ref[0].astype(o_ref.dtype)


def _to_mxu_dtype(w, layer, rows=256):
    _, r, c = w.shape
    return pl.pallas_call(
        _cast_kernel,
        grid=(r // rows,),
        in_specs=[pl.BlockSpec((1, rows, c), lambda i: (layer, i, 0))],
        out_specs=pl.BlockSpec((rows, c), lambda i: (i, 0)),
        out_shape=jax.ShapeDtypeStruct((r, c), _MXU_DTYPE),
        compiler_params=_cparams("parallel"),
        name="weight_cast",
    )(w)


ONES_ROWS = 16


def _ones_rows(n, dtype):
    row = lax.broadcasted_iota(jnp.int32, (ONES_ROWS, n), 0)
    return jnp.where(row == 0, 1.0, 0.0).astype(dtype)


def _with_ones_rows(vt):
    b, h, _, t = vt.shape
    extra = jnp.concatenate([jnp.ones((b, h, 1, t), vt.dtype),
                             jnp.zeros((b, h, ONES_ROWS - 1, t), vt.dtype)], axis=2)
    return jnp.concatenate([vt, extra], axis=2)


def _resident(shape):
    nd = len(shape)
    return pl.BlockSpec(shape, lambda *_: (0,) * nd, pipeline_mode=pl.Buffered(1))


def _in_proj_kernel(x_ref, g_ref, w_ref, *o_refs, widths, copy):
    x = x_ref[...]
    y = x * lax.rsqrt(jnp.mean(x * x, axis=-1, keepdims=True) + EPS)
    xn = (y * g_ref[...]).astype(_MXU_DTYPE)
    slab, c0, cw = copy
    off = 0
    for j, (o_ref, wd) in enumerate(zip(o_refs, widths)):
        res = _dot(xn, w_ref[:, off:off + wd])
        o_ref[...] = res
        if j == slab:
            o_refs[-1][...] = res[:, c0:c0 + cw].astype(o_refs[-1].dtype)
        off += wd


def _in_proj(x, g, w, widths, copy, tm=256):
    m, d = x.shape
    n = w.shape[1]
    return pl.pallas_call(
        functools.partial(_in_proj_kernel, widths=widths, copy=copy),
        grid=(m // tm,),
        in_specs=[pl.BlockSpec((tm, d), lambda i: (i, 0)),
                  _resident((1, d)),
                  _resident((d, n))],
        out_specs=[pl.BlockSpec((tm, wd), lambda i: (i, 0)) for wd in widths + (copy[2],)],
        out_shape=[jax.ShapeDtypeStruct((m, wd), F32) for wd in widths]
        + [jax.ShapeDtypeStruct((m, copy[2]), _MXU_DTYPE)],
        compiler_params=_cparams("parallel"),
        name="in_proj",
    )(x, g, w)


def _mla_prep_kernel(pm_ref, cos_ref, sin_ref, gqa_ref, gkva_ref, wq_ref, wkv_ref,
                     gq_ref, gqr_ref, gk_ref, gkr_ref, q_ref, k_ref, v_ref):
    pm = pm_ref[...]
    hs = MLA_HEADS * LANES
    cq = pm[:, 0:256]
    r = lax.rsqrt(jnp.sum(cq * cq, axis=-1, keepdims=True) * (1.0 / MLA_Q_RANK) + EPS)
    qq = _dot((cq * r * gqa_ref[...]).astype(_MXU_DTYPE), wq_ref[...])
    ckv = pm[:, 256:384]
    r = lax.rsqrt(jnp.mean(ckv * ckv, axis=-1, keepdims=True) + EPS)
    kv = _dot((ckv * r * gkva_ref[...]).astype(_MXU_DTYPE), wkv_ref[...])
    kpe = pm[:, 384:512]
    kpe_rot = pm[:, 512:640]
    cos = cos_ref[...]
    sin = sin_ref[...]
    scale = MLA_QK ** -0.5 * LOG2E
    aq = cos * gq_ref[...] * scale
    bq = sin * gqr_ref[...] * scale
    ak = cos * gk_ref[...]
    bk = sin * gkr_ref[...]
    for h in range(MLA_HEADS):
        sl = slice(h * LANES, (h + 1) * LANES)
        sr = slice(hs + h * LANES, hs + (h + 1) * LANES)
        qh = qq[:, sl]
        rq = lax.rsqrt(jnp.sum(qh * qh, axis=-1, keepdims=True) * (1.0 / MLA_QK) + EPS)
        q_ref[0, h] = ((qh * aq + qq[:, sr] * bq) * rq).T.astype(q_ref.dtype)
        kh = kv[:, sl] + kpe
        rk = lax.rsqrt(jnp.sum(kh * kh, axis=-1, keepdims=True) * (1.0 / MLA_QK) + EPS)
        k_ref[0, h] = ((kh * ak + kpe_rot * bk) * rk).astype(k_ref.dtype)
        v_ref[0, h, 0:MLA_V, :] = kv[:, sr].T[0:MLA_V].astype(v_ref.dtype)
        v_ref[0, h, MLA_V:MLA_V + ONES_ROWS, :] = _ones_rows(pm.shape[0], v_ref.dtype)


def _mla_prep(pm, cos, sin, gqa, gkva, wq, wkv, gq, gqr, gk, gkr, b, t, tm=512):
    nt = t // tm
    hs = MLA_HEADS * LANES
    vec = lambda n: _resident((1, n))
    return pl.pallas_call(
        _mla_prep_kernel,
        grid=(b, nt),
        in_specs=[pl.BlockSpec((tm, 640), lambda bi, i: (bi * nt + i, 0)),
                  pl.BlockSpec((tm, LANES), lambda bi, i: (i, 0)),
                  pl.BlockSpec((tm, LANES), lambda bi, i: (i, 0)),
                  vec(256), vec(LANES), _resident((256, 2 * hs)), _resident((LANES, 2 * hs)),
                  vec(LANES), vec(LANES), vec(LANES), vec(LANES)],
        out_specs=[pl.BlockSpec((1, MLA_HEADS, LANES, tm), lambda bi, i: (bi, 0, 0, i)),
                   pl.BlockSpec((1, MLA_HEADS, tm, LANES), lambda bi, i: (bi, 0, i, 0)),
                   pl.BlockSpec((1, MLA_HEADS, MLA_V + ONES_ROWS, tm), lambda bi, i: (bi, 0, 0, i))],
        out_shape=[jax.ShapeDtypeStruct((b, MLA_HEADS, LANES, t), _MXU_DTYPE),
                   jax.ShapeDtypeStruct((b, MLA_HEADS, t, LANES), _MXU_DTYPE),
                   jax.ShapeDtypeStruct((b, MLA_HEADS, MLA_V + ONES_ROWS, t), _MXU_DTYPE)],
        compiler_params=_cparams("parallel", "parallel"),
        name="mla_prep",
    )(pm, cos, sin, gqa, gkva, wq, wkv, gq, gqr, gk, gkr)


def _tile_plan(nq, use_far):
    if use_far and nq % 2 == 0:
        return [lambda j: j, lambda j: nq - 1 - j]
    n = 4 if (not use_far and nq % 4 == 0) else 1
    return [functools.partial(lambda j, s: n * j + s, s=s) for s in range(n)]


def _paired_position(i, nq):
    return jnp.where(i < nq // 2, 2 * i, 2 * (nq - 1 - i) + 1)


def _flash_kernel(*refs, use_far, sub, dv, extra, tiles):
    ns = len(tiles)
    q_refs, refs = refs[:ns], refs[ns:]
    if extra:
        qx_refs, refs = refs[:ns], refs[ns:]
        k_ref, kx_ref, v_ref = refs[:3]
        refs = refs[3:]
    else:
        k_ref, v_ref = refs[:2]
        refs = refs[2:]
    tab_refs, o_ref = refs[:ns], refs[ns]
    step = pl.program_id(2)
    tc = ATT_TILE
    tq = ATT_TILE

    def block(q, tab_ref, start, nkeys, with_table, m, acc):
        kb = k_ref[0, 0, pl.ds(start, nkeys), :]
        if extra:
            kb = jnp.concatenate([kb, kx_ref[pl.ds(start, nkeys), :]], axis=1)
        s = _dot(kb, q)
        if with_table:
            s = s + tab_ref[0, 0]
        for j in range(nkeys // sub):
            sj = s[j * sub:(j + 1) * sub]
            vc = v_ref[0, 0, :, pl.ds(pl.multiple_of(start + j * sub, sub), sub)]
            m_new = jnp.maximum(m, jnp.max(sj, axis=0, keepdims=True))
            p = jnp.exp2(sj - m_new).astype(_MXU_DTYPE)
            acc = jnp.exp2(m - m_new) * acc + _dot(vc, p)
            m = m_new
        return m, acc

    state = []
    for slot in range(ns):
        i = tiles[slot](step)
        q = q_refs[slot][0, 0]
        if extra:
            q = jnp.concatenate([q, qx_refs[slot][0, 0]], axis=0)
        m = jnp.full((1, tq), -3e38, F32)
        acc = jnp.zeros((dv + ONES_ROWS, tq), F32)
        n_far = jnp.maximum(i - 1, 0)
        if use_far:
            done = 0
            for width in (4, 2, 1):
                count = (n_far - done) // width

                def body(c, carry, width=width, base=done, q=q):
                    start = pl.multiple_of((base + c * width) * tc, tc)
                    return block(q, None, start, width * tc, False, *carry)

                m, acc = lax.fori_loop(0, count, body, (m, acc))
                done = done + count * width
        state.append((q, n_far, m, acc))
    for slot, (q, n_far, m, acc) in enumerate(state):
        m, acc = block(q, tab_refs[slot], pl.multiple_of(n_far * tc, tc), 2 * tc, True, m, acc)
        o_ref[0, 0, :, slot * tq:(slot + 1) * tq] = acc[0:dv] / acc[dv:dv + 1]


def _flash(qt, k, va, tab, *, use_far, name, qx=None, kx=None, sub=256):
    b, h, dk, t = qt.shape
    hk, dva = va.shape[1], va.shape[2]
    dv = dva - ONES_ROWS
    rep = h // hk
    ht = tab.shape[0]
    tq = ATT_TILE
    extra = qx is not None
    tiles = _tile_plan(t // tq, use_far)
    ns = len(tiles)
    q_specs = [pl.BlockSpec((1, 1, dk, tq), lambda bi, hi, j, f=f: (bi, hi, 0, f(j))) for f in tiles]
    tab_specs = [pl.BlockSpec((1, 1, 2 * tq, tq),
                              lambda bi, hi, j, f=f: (hi % ht, jnp.minimum(f(j), 1), 0, 0))
                 for f in tiles]
    k_spec = pl.BlockSpec((1, 1, t, dk), lambda bi, hi, j: (bi, hi // rep, 0, 0))
    v_spec = pl.BlockSpec((1, 1, dva, t), lambda bi, hi, j: (bi, hi // rep, 0, 0))
    if extra:
        nx = qx.shape[2]
        qx_specs = [pl.BlockSpec((1, 1, nx, tq), lambda bi, hi, j, f=f: (bi, hi // rep, 0, f(j)))
                    for f in tiles]
        in_specs = q_specs + qx_specs + [k_spec, _resident((t, nx)), v_spec] + tab_specs
        args = (qt,) * ns + (qx,) * ns + (k, kx, va) + (tab,) * ns
    else:
        in_specs = q_specs + [k_spec, v_spec] + tab_specs
        args = (qt,) * ns + (k, va) + (tab,) * ns
    return pl.pallas_call(
        functools.partial(_flash_kernel, use_far=use_far, sub=sub, dv=dv, extra=extra, tiles=tiles),
        grid=(b, h, t // (tq * ns)),
        in_specs=in_specs,
        out_specs=pl.BlockSpec((1, 1, dv, ns * tq), lambda bi, hi, j: (bi, hi, 0, j)),
        out_shape=jax.ShapeDtypeStruct((b, h, dv, t), F32),
        compiler_params=_cparams("parallel", "parallel", "parallel"),
        name=name,
    )(*args)


def _ret_kernel(q_ref, k_ref, v_ref, g_ref, qr_ref, kr_ref, cos_ref, sin_ref, din_ref, qd_ref,
                kd_ref, cdm_ref, bd_ref, gn_ref, o_ref, state_ref, *, nchunk):
    @pl.when(pl.program_id(1) == 0)
    def _():
        state_ref[...] = jnp.zeros(state_ref.shape, F32)

    w = RET_HEADS * RET_DK
    lane = lax.broadcasted_iota(jnp.int32, (1, w), 1)
    heads = [(lane >= h * RET_DK) & (lane < (h + 1) * RET_DK) for h in range(RET_HEADS)]
    on_diag = bd_ref[...] > 0.5
    c_ = RET_CHUNK
    for c in range(nchunk):
        sl = slice(c * c_, (c + 1) * c_)
        cos = cos_ref[sl, :]
        sin = sin_ref[sl, :]
        qh = q_ref[sl, :] * cos + qr_ref[sl, :] * sin
        kh = (k_ref[sl, :] * cos + kr_ref[sl, :] * sin) * (RET_DK ** -0.5)
        kb = kh.astype(_MXU_DTYPE)
        vb = v_ref[sl, :].astype(_MXU_DTYPE)
        st = state_ref[...]
        out = _dot((qh * qd_ref[...]).astype(_MXU_DTYPE), st.astype(_MXU_DTYPE))
        for h in range(RET_HEADS):
            qm = jnp.where(heads[h], qh, 0.0).astype(_MXU_DTYPE)
            inner = lax.dot_general(qm, kb, (((1,), (1,)), ((), ())),
                                    preferred_element_type=F32) * din_ref[h]
            out = out + jnp.where(heads[h], _dot(inner.astype(_MXU_DTYPE), vb), 0.0)
        kdt = (kh * kd_ref[...]).T
        state_ref[...] = st * cdm_ref[...] + jnp.where(on_diag, _dot(kdt.astype(_MXU_DTYPE), vb), 0.0)
        o2 = out * out
        ms = jnp.zeros_like(out)
        for h in range(RET_HEADS):
            ssum = jnp.sum(jnp.where(heads[h], o2, 0.0), axis=-1, keepdims=True)
            ms = jnp.where(heads[h], ssum * (1.0 / RET_DV), ms)
        y = out * lax.rsqrt(ms + EPS) * gn_ref[...]
        gg = g_ref[sl, :]
        o_ref[sl, :] = (gg * _sigmoid(gg) * y).astype(o_ref.dtype)


def _retention(pr, cos, sin, din, qd, kd, cdm, bd, gn, b, t, tt=256):
    w = RET_HEADS * RET_DK
    nt = t // tt
    col = lambda j: pl.BlockSpec((tt, w), lambda bi, i: (bi * nt + i, j))
    pos = pl.BlockSpec((tt, w), lambda bi, i: (i, 0))
    return pl.pallas_call(
        functools.partial(_ret_kernel, nchunk=tt // RET_CHUNK),
        grid=(b, nt),
        in_specs=[col(0), col(1), col(2), col(3), col(4), col(5), pos, pos,
                  _resident(din.shape), _resident(qd.shape), _resident(kd.shape),
                  _resident(cdm.shape), _resident(bd.shape), _resident((1, w))],
        out_specs=pl.BlockSpec((tt, w), lambda bi, i: (bi * nt + i, 0)),
        out_shape=jax.ShapeDtypeStruct((b * t, w), _MXU_DTYPE),
        scratch_shapes=[pltpu.VMEM((w, w), F32)],
        compiler_params=_cparams("parallel", "arbitrary"),
        name="retention",
    )(pr, pr, pr, pr, pr, pr, cos, sin, din, qd, kd, cdm, bd, gn)


NSA_Q0, NSA_KC0, NSA_VC0, NSA_KS0, NSA_VS0, NSA_KW0, NSA_VW0, NSA_GATE0 = (
    0, 384, 512, 640, 896, 1024, 1280, 1408)
NSA_SLAB = 1536
GATE_STRIDE = 8


def _nsa_prep_kernel(pn_ref, gq_ref, gks_ref, gkw_ref, qt_ref, ks_ref, kw_ref, vs_ref, vw_ref,
                     gt_ref):
    tm = pn_ref.shape[0]
    d = NSA_DH
    dt = qt_ref.dtype
    xq = pn_ref[:, NSA_Q0:NSA_Q0 + NSA_HEADS * d].T
    for h in range(NSA_HEADS):
        blk = xq[h * d:(h + 1) * d]
        r = lax.rsqrt(jnp.mean(blk * blk, axis=0, keepdims=True) + EPS)
        qt_ref[0, h, 0:d, :] = (blk * r * gq_ref[...]).astype(dt)
        qt_ref[0, h, d:2 * d, :] = jnp.zeros((d, tm), dt)
    for g in range(NSA_KV_HEADS):
        for c0, g_ref, o_ref in ((NSA_KS0, gks_ref, ks_ref), (NSA_KW0, gkw_ref, kw_ref)):
            slot = pn_ref[:, c0 + g * LANES:c0 + (g + 1) * LANES]
            r = lax.rsqrt(jnp.sum(slot * slot, axis=-1, keepdims=True) * (1.0 / d) + EPS)
            o_ref[0, g] = (slot * r * g_ref[...]).astype(dt)
    for c0, o_ref in ((NSA_VS0, vs_ref), (NSA_VW0, vw_ref)):
        vt = pn_ref[:, c0:c0 + LANES].T
        for g in range(NSA_KV_HEADS):
            o_ref[0, g, 0:d, :] = vt[g * d:(g + 1) * d].astype(dt)
            o_ref[0, g, d:d + ONES_ROWS, :] = _ones_rows(tm, dt)
    gt = pn_ref[:, NSA_GATE0:NSA_GATE0 + LANES].T
    gt_ref[0] = _sigmoid(gt[0:NSA_HEADS * GATE_STRIDE])


def _nsa_prep(pn, gq, gks, gkw, b, t, tm=512):
    nt = t // tm
    d = NSA_DH
    g = NSA_KV_HEADS
    md = _MXU_DTYPE
    ch_major = lambda n, r: pl.BlockSpec((1, n, r, tm), lambda bi, i: (bi, 0, 0, i))
    natural = pl.BlockSpec((1, g, tm, LANES), lambda bi, i: (bi, 0, i, 0))
    return pl.pallas_call(
        _nsa_prep_kernel,
        grid=(b, nt),
        in_specs=[pl.BlockSpec((tm, NSA_SLAB), lambda bi, i: (bi * nt + i, 0)),
                  _resident((d, tm)), _resident((1, LANES)), _resident((1, LANES))],
        out_specs=[ch_major(NSA_HEADS, 2 * d), natural, natural,
                   ch_major(g, d + ONES_ROWS), ch_major(g, d + ONES_ROWS),
                   pl.BlockSpec((1, NSA_HEADS * GATE_STRIDE, tm), lambda bi, i: (bi, 0, i))],
        out_shape=[jax.ShapeDtypeStruct((b, NSA_HEADS, 2 * d, t), md),
                   jax.ShapeDtypeStruct((b, g, t, LANES), md),
                   jax.ShapeDtypeStruct((b, g, t, LANES), md),
                   jax.ShapeDtypeStruct((b, g, d + ONES_ROWS, t), md),
                   jax.ShapeDtypeStruct((b, g, d + ONES_ROWS, t), md),
                   jax.ShapeDtypeStruct((b, NSA_HEADS * GATE_STRIDE, t), F32)],
        compiler_params=_cparams("parallel", "parallel"),
        name="nsa_prep",
    )(pn, jnp.broadcast_to(gq[:, None], (d, tm)), gks, gkw)


def _gelu_tanh(x):
    return 0.5 * x * (1.0 + jnp.tanh(math.sqrt(2.0 / math.pi) * (x + 0.044715 * (x * x * x))))


def _compress_kernel(ak_ref, av_ref, w1k_ref, w2k_ref, pk_ref, w1v_ref, w2v_ref, pv_ref, gk_ref,
                     kc_ref, vc_ref):
    half = CMP_STRIDE * NSA_DH

    def comp(a_ref, w1_ref, w2_ref, p_ref):
        a = a_ref[0]
        pb = _dot(p_ref[...], w1_ref[...])[0:1]
        second = _dot(a, w1_ref[half:2 * half, :])
        nc = second.shape[0]
        hid = _dot(a, w1_ref[0:half, :]) + pltpu.roll(second, nc - 1, 0) + pb
        return _dot(_gelu_tanh(hid).astype(_MXU_DTYPE), w2_ref[...])

    kc = comp(ak_ref, w1k_ref, w2k_ref, pk_ref)
    y = kc * lax.rsqrt(jnp.mean(kc * kc, axis=-1, keepdims=True) + EPS)
    kc_ref[0] = (y * gk_ref[...]).astype(kc_ref.dtype)
    vc_ref[0] = comp(av_ref, w1v_ref, w2v_ref, pv_ref).astype(vc_ref.dtype)


def _compress(ak, av, w1k, w2k, pk, w1v, w2v, pv, gk):
    n, nc, kk = ak.shape
    blk = pl.BlockSpec((1, nc, kk), lambda i: (i, 0, 0))
    out = pl.BlockSpec((1, nc, NSA_DH), lambda i: (i, 0, 0))
    w1 = _resident((2 * kk, CMP_HIDDEN))
    w2 = _resident((CMP_HIDDEN, NSA_DH))
    pp = _resident((8, 2 * kk))
    return pl.pallas_call(
        _compress_kernel,
        grid=(n,),
        in_specs=[blk, blk, w1, w2, pp, w1, w2, pp, _resident((1, NSA_DH))],
        out_specs=[out, out],
        out_shape=[jax.ShapeDtypeStruct((n, nc, NSA_DH), _MXU_DTYPE)] * 2,
        compiler_params=_cparams("parallel"),
        name="nsa_compress",
    )(ak, av, w1k, w2k, pk, w1v, w2v, pv, gk)


def _cmp_sel_kernel(q_ref, kc_ref, vca_ref, tab_ref, c2s_ref, bk_ref, oc_ref, sel_ref,
                    m_ref, acco_ref, acci_ref, *, tq, ns, d):
    i = pl.program_id(2)
    ch = CMP_CHUNK
    end = (tq // CMP_STRIDE) * (i + 1)
    n_far = (end - 1) // ch
    rowi = lax.broadcasted_iota(jnp.int32, (ch, NSA_GROUP * tq), 0)
    qpos = tq * i + lax.broadcasted_iota(jnp.int32, (1, tq), 1)
    sees_any = jnp.where(qpos >= CMP_BLOCK - 1, 1.0, 0.0)

    q = jnp.concatenate([q_ref[0, r] for r in range(NSA_GROUP)], axis=1)
    m_ref[...] = jnp.full(m_ref.shape, -3e38, F32)
    acco_ref[...] = jnp.zeros(acco_ref.shape, F32)
    acci_ref[...] = jnp.zeros(acci_ref.shape, F32)

    def chunk(g, with_table):
        start = pl.multiple_of(end - ch * g, CMP_STRIDE)
        kc = kc_ref[0, 0, pl.ds(start, ch), :]
        vt = vca_ref[0, 0, pl.ds(start, ch), :].T[0:d + ONES_ROWS].astype(_MXU_DTYPE)
        ct = c2s_ref[pl.ds(start, ch), :].T.astype(_MXU_DTYPE)
        s = _dot(kc, q)
        if with_table:
            s = s + tab_ref[0]
        s = jnp.where(rowi >= ch * (g + 1) - end, s, NEG_INF)
        m_old = m_ref[...]
        m_new = jnp.maximum(m_old, jnp.max(s, axis=0, keepdims=True))
        alpha = jnp.exp2(m_old - m_new)
        e = jnp.exp2(s - m_new).astype(_MXU_DTYPE)
        acco_ref[...] = alpha * acco_ref[...] + _dot(vt, e)
        acci_ref[...] = alpha * acci_ref[...] + _dot(ct, e)
        m_ref[...] = m_new

    chunk(0, True)

    def far(g, carry):
        chunk(g, False)
        return carry

    lax.fori_loop(1, n_far + 1, far, 0)

    imp = jnp.zeros((ns, tq), F32)
    for r in range(NSA_GROUP):
        lanes = slice(r * tq, (r + 1) * tq)
        inv = sees_any / acco_ref[d:d + 1, lanes]
        oc_ref[0, r] = acco_ref[0:d, lanes] * inv
        imp = imp + acci_ref[:, lanes] * inv

    back = bk_ref[...] + i * (tq // SEL_BLOCK)
    jidx = lax.broadcasted_iota(jnp.int32, (ns, tq), 0)
    forced = (jidx == 0) | ((back >= 0) & (back < SEL_LOCAL))
    jf = jidx.astype(F32)
    taken = -3e38

    def pick(imp, rounds):
        for _ in range(rounds):
            mx = jnp.max(imp, axis=0, keepdims=True)
            first = jnp.min(jnp.where(imp == mx, jf, 1e9), axis=0, keepdims=True)
            imp = jnp.where(jf == first, taken, imp)
        sel_ref[0, 0] = jnp.where(imp == taken, 0.0, NEG_INF).astype(sel_ref.dtype)

    top_k = min(SEL_TOPK, ns)
    n_forced = 1 + SEL_LOCAL

    @pl.when(i == 0)
    def _():
        pick(jnp.where(back >= 0, jnp.where(forced, FORCE, imp), NEG_INF), top_k)

    @pl.when(i > 0)
    def _():
        pick(jnp.where(forced, taken, jnp.where(back >= 0, imp, NEG_INF)), top_k - n_forced)


def _cmp_sel(qt, kc, vca, tab, c2s, bk, tq):
    b, h, _, t = qt.shape
    g, npad, d = kc.shape[1:]
    ns = c2s.shape[1]
    assert tq >= SEL_LOCAL * SEL_BLOCK and tq // CMP_STRIDE <= CMP_CHUNK
    return pl.pallas_call(
        functools.partial(_cmp_sel_kernel, tq=tq, ns=ns, d=d),
        grid=(b, g, t // tq),
        in_specs=[pl.BlockSpec((1, NSA_GROUP, d, tq), lambda bi, gi, i: (bi, gi, 0, i)),
                  pl.BlockSpec((1, 1, npad, d), lambda bi, gi, i: (bi, gi, 0, 0)),
                  pl.BlockSpec((1, 1, npad, LANES), lambda bi, gi, i: (bi, gi, 0, 0)),
                  pl.BlockSpec((1, CMP_CHUNK, NSA_GROUP * tq), lambda bi, gi, i: (gi, 0, 0)),
                  _resident(c2s.shape), _resident(bk.shape)],
        out_specs=[pl.BlockSpec((1, NSA_GROUP, d, tq), lambda bi, gi, i: (bi, gi, 0, i)),
                   pl.BlockSpec((1, 1, ns, tq), lambda bi, gi, i: (bi, gi, 0, i))],
        out_shape=[jax.ShapeDtypeStruct((b, h, d, t), F32),
                   jax.ShapeDtypeStruct((b, g, ns, t), _MXU_DTYPE)],
        scratch_shapes=[pltpu.VMEM((1, NSA_GROUP * tq), F32),
                        pltpu.VMEM((d + ONES_ROWS, NSA_GROUP * tq), F32),
                        pltpu.VMEM((ns, NSA_GROUP * tq), F32)],
        compiler_params=_cparams("parallel", "parallel", "parallel"),
        name="nsa_cmp_sel",
    )(qt, kc, vca, tab, c2s, bk)


def _out_proj_kernel(mla_ref, ret_ref, oc_ref, os_ref, ow_ref, g_ref, w_ref, r_ref, o_ref):
    d = NSA_DH
    w_mla, w_ret = MLA_HEADS * MLA_V, RET_HEADS * RET_DV
    acc = r_ref[...] + _dot(mla_ref[0].T.astype(_MXU_DTYPE), w_ref[0:w_mla, :])
    acc = acc + _dot(ret_ref[...], w_ref[w_mla:w_mla + w_ret, :])
    g = g_ref[0]
    nsa = []
    for h in range(NSA_HEADS):
        rows = slice(h * d, (h + 1) * d)
        g0 = h * GATE_STRIDE
        nsa.append(g[g0:g0 + 1] * oc_ref[0, rows, :] + g[g0 + 1:g0 + 2] * os_ref[0, rows, :]
                   + g[g0 + 2:g0 + 3] * ow_ref[0, rows, :])
    nsa = jnp.concatenate(nsa, axis=0).T.astype(_MXU_DTYPE)
    o_ref[...] = acc + _dot(nsa, w_ref[w_mla + w_ret:, :])


def _out_proj(o_mla, o_ret, oc, os_, ow, gates, w, res, t, tm=512):
    m, n = res.shape
    assert tm == ATT_TILE
    nt = t // tm
    ch_major = lambda c: pl.BlockSpec((1, c, tm), lambda i: (i // nt, 0, i % nt))
    far_order = len(_tile_plan(nt, True)) == 2
    ch_paired = (lambda c: pl.BlockSpec((1, c, tm), lambda i: (i // nt, 0, _paired_position(i % nt, nt)))
                 ) if far_order else ch_major
    tokens = lambda c: pl.BlockSpec((tm, c), lambda i: (i, 0))
    c_nsa = oc.shape[1]
    return pl.pallas_call(
        _out_proj_kernel,
        grid=(m // tm,),
        in_specs=[ch_paired(o_mla.shape[1]), tokens(o_ret.shape[1]), ch_major(c_nsa), ch_paired(c_nsa),
                  ch_major(c_nsa), ch_major(gates.shape[1]), _resident(w.shape), tokens(n)],
        out_specs=tokens(n),
        out_shape=jax.ShapeDtypeStruct((m, n), F32),
        compiler_params=_cparams("parallel"),
        name="out_proj",
    )(o_mla, o_ret, oc, os_, ow, gates, w, res)


def _ffn_kernel(x_ref, gn_ref, wup_ref, cw_ref, cb_ref, wdn_ref, o_ref, hbuf_ref, acc_ref, *,
                tm, fc):
    @pl.when(pl.program_id(1) == 0)
    def _():
        hbuf_ref[0:8, :] = jnp.zeros((8, hbuf_ref.shape[1]), F32)

    x = x_ref[...]
    y = x * lax.rsqrt(jnp.mean(x * x, axis=-1, keepdims=True) + EPS)
    xn = (y * gn_ref[...]).astype(_MXU_DTYPE)

    def up_proj(col0):
        cols = slice(col0, col0 + fc)
        hbuf_ref[8:tm + 8, cols] = _dot(xn, wup_ref[:, cols])

    def conv(col0):
        cols = slice(col0, col0 + fc)
        w = cw_ref[:, cols]
        h = hbuf_ref[8:tm + 8, cols]
        out = (h * w[2:3] + hbuf_ref[7:tm + 7, cols] * w[1:2] + hbuf_ref[6:tm + 6, cols] * w[0:1]
               + cb_ref[:, cols])
        hbuf_ref[0:8, cols] = hbuf_ref[tm:tm + 8, cols]
        return out

    nf = D_FF // fc
    ahead = 4
    for f in range(min(ahead, nf)):
        up_proj(f * fc)
        up_proj(D_FF + f * fc)
    for f in range(nf):
        if f + ahead < nf:
            up_proj((f + ahead) * fc)
            up_proj(D_FF + (f + ahead) * fc)
        gate = conv(f * fc)
        up = conv(D_FF + f * fc)
        act = (gate * _sigmoid(gate) * up).astype(_MXU_DTYPE)
        contrib = _dot(act, wdn_ref[f * fc:(f + 1) * fc, :])
        if f == 0:
            acc_ref[...] = contrib
        else:
            acc_ref[...] += contrib
    o_ref[...] = x + acc_ref[...]


def _ffn(x, gn, wup, cw, cb, wdn, b, t, tm=512, fc=256):
    d = x.shape[1]
    nt = t // tm
    return pl.pallas_call(
        functools.partial(_ffn_kernel, tm=tm, fc=fc),
        grid=(b, nt),
        in_specs=[pl.BlockSpec((tm, d), lambda bi, i: (bi * nt + i, 0)),
                  _resident((1, d)), _resident(wup.shape), _resident(cw.shape),
                  _resident(cb.shape), _resident(wdn.shape)],
        out_specs=pl.BlockSpec((tm, d), lambda bi, i: (bi * nt + i, 0)),
        out_shape=jax.ShapeDtypeStruct(x.shape, F32),
        scratch_shapes=[pltpu.VMEM((tm + 8, 2 * D_FF), F32), pltpu.VMEM((tm, d), F32)],
        compiler_params=_cparams("parallel", "arbitrary"),
        name="conv_ffn",
    )(x, gn, wup, cw, cb, wdn)


def _rope_tables(t, d):
    inv = ROPE_BASE ** (-np.arange(0, d, 2, dtype=np.float64) / d)
    ang = np.arange(t, dtype=np.float64)[:, None] * inv[None, :]
    return (np.concatenate([np.cos(ang)] * 2, axis=1), np.concatenate([np.sin(ang)] * 2, axis=1))


def _t5_bucket_np(dist):
    max_exact = REL_BUCKETS // 2
    d = np.maximum(dist, 1).astype(np.float64)
    log_b = max_exact + (np.log(d / max_exact) / math.log(REL_MAX_DIST / max_exact)
                         * (REL_BUCKETS - max_exact)).astype(np.int32)
    return np.where(dist < max_exact, dist, np.minimum(log_b, REL_BUCKETS - 1))


@functools.lru_cache(maxsize=None)
def _constants(t):
    c = {}
    cos, sin = _rope_tables(t, MLA_ROPE)
    pad = LANES - MLA_QK
    c["mla_cos"] = np.concatenate([np.ones((t, MLA_NOPE)), cos, np.ones((t, pad))], 1).astype(np.float32)
    c["mla_sin"] = np.concatenate([np.zeros((t, MLA_NOPE)), sin, np.zeros((t, pad))], 1).astype(np.float32)
    cos, sin = _rope_tables(t, RET_DK)
    c["ret_cos"] = np.tile(cos, (1, RET_HEADS)).astype(np.float32)
    c["ret_sin"] = np.tile(sin, (1, RET_HEADS)).astype(np.float32)
    lg = np.log(1.0 - 2.0 ** (-5.0 - np.arange(RET_HEADS, dtype=np.float64)))
    idx = np.arange(RET_CHUNK, dtype=np.float64)
    diff = idx[:, None] - idx[None, :]
    c["ret_din"] = (np.exp(np.maximum(diff, 0.0) * lg[:, None, None]) * (diff >= 0)).astype(np.float32)
    qd = np.exp((idx[:, None] + 1.0) * lg[None, :])
    kd = np.exp((RET_CHUNK - 1.0 - idx[:, None]) * lg[None, :])
    c["ret_qd"] = np.repeat(qd, RET_DK, axis=1).astype(np.float32)
    c["ret_kd"] = np.repeat(kd, RET_DK, axis=1).astype(np.float32)
    head_of = np.arange(RET_HEADS * RET_DK) // RET_DK
    c["ret_cdm"] = np.broadcast_to(np.exp(RET_CHUNK * lg)[head_of][:, None],
                                   (RET_HEADS * RET_DK, RET_HEADS * RET_DV)).astype(np.float32)
    c["ret_bd"] = (head_of[:, None] == head_of[None, :]).astype(np.float32)
    c["bucket"] = _t5_bucket_np(np.arange(LANES)).astype(np.int32)
    kk = np.arange(ATT_TILE)[:, None]
    qq = np.arange(ATT_TILE)[None, :]
    causal = np.where(qq >= kk, 0.0, NEG_INF)
    c["causal_tab"] = np.stack([np.concatenate([causal, np.full_like(causal, NEG_INF)], 0),
                                np.concatenate([np.zeros_like(causal), causal], 0)])[None].astype(np.float32)
    nc, ns = t // CMP_STRIDE, t // SEL_BLOCK
    n_cmp = (t - CMP_BLOCK) // CMP_STRIDE + 1
    c_start = np.arange(nc) * CMP_STRIDE
    s_start = np.arange(ns) * SEL_BLOCK
    overlap = np.clip(np.minimum(c_start[:, None] + CMP_BLOCK, s_start[None, :] + SEL_BLOCK)
                      - np.maximum(c_start[:, None], s_start[None, :]), 0, None).astype(np.float64)
    overlap[n_cmp:] = 0.0
    c["c2s"] = np.concatenate([np.zeros((CMP_CHUNK, ns)), overlap / CMP_BLOCK]).astype(np.float32)
    q = np.arange(CMP_TQ)[None, :]
    c["bk"] = (q // SEL_BLOCK - np.arange(ns)[:, None]).astype(np.int32)
    c["cmp_dist"] = (q - CMP_STRIDE * np.arange(CMP_CHUNK)[:, None]
                     + (CMP_STRIDE * CMP_CHUNK - CMP_TQ - (CMP_BLOCK - 1))).astype(np.int32)
    c["sel_onehot"] = (np.arange(t)[:, None] // SEL_BLOCK == np.arange(ns)[None, :]).astype(np.float32)
    return c


def _cols(w, pieces):
    out = []
    for p in pieces:
        if p[0] is None:
            out.append(jnp.zeros((w.shape[0], p[1]), w.dtype))
        else:
            blk = w[:, p[0]:p[1]]
            out.append(-blk if p[2] < 0 else blk)
    return jnp.concatenate(out, axis=1)


def _rot_pieces(base, d):
    return [(base + d // 2, base + d, -1), (base, base + d // 2, 1)]


def _skew(w, rows, step, col0, cols):
    hh, ll = w.shape
    flat = jnp.tile(w, (1, rows))[:, :rows * (ll - step)]
    return flat.reshape(hh, rows, ll - step)[:, :, col0:col0 + cols]


def _toeplitz(w, n):
    return _skew(w, n, 1, 0, n)


def _pad_to(v, n):
    return jnp.concatenate([v, jnp.zeros((n - v.shape[0],), v.dtype)])


def _layer(xf, b, t, cst, tabs, w_in, w_out, w1_k, w1_v, ffn_w_up, ffn_w_down,
           attn_norm, ffn_norm, mla_q_a_norm, mla_w_uq, mla_kv_a_norm, mla_w_ukv, mla_q_norm,
           mla_k_norm, ret_norm, pos_k, w2_k, pos_v, w2_v, nsa_q_norm, kn_cmp, kn_sel, kn_win,
           ffn_conv_w, ffn_conv_b):
    md = _MXU_DTYPE
    o = _IN_OFF
    pieces = [(o[0], o[1], 1), (None, 64), (o[1], o[2], 1),
              (None, 64), (o[2], o[3], 1), (None, 32),
              (None, 64)] + _rot_pieces(o[2], MLA_ROPE) + [(None, 32)]
    pieces += [(o[3], o[7], 1)]
    for base in (o[3], o[4]):
        for h in range(RET_HEADS):
            pieces += _rot_pieces(base + h * RET_DK, RET_DK)
    d_ = NSA_DH
    pieces += [(o[7], o[10], 1)]
    for base in (o[10], o[12]):
        pieces += [(base, base + d_, 1), (None, LANES - d_), (base + d_, base + 2 * d_, 1),
                   (None, LANES - d_), (base + 2 * d_, base + 4 * d_, 1)]
    for h in range(NSA_HEADS):
        pieces += [(o[14] + 3 * h, o[14] + 3 * h + 3, 1), (None, GATE_STRIDE - 3)]
    pieces += [(None, LANES - NSA_HEADS * GATE_STRIDE)]
    w_in_r = _cols(w_in, pieces)
    pm, pr, pn, pc = _in_proj(xf, attn_norm[None, :], w_in_r, (640, 1536, NSA_SLAB),
                              (2, NSA_KC0, 2 * NSA_KV_HEADS * NSA_DH))

    wq_pieces, wq_rot = [], []
    wkv_k, wkv_v = [], []
    for h in range(MLA_HEADS):
        qb = h * MLA_QK
        wq_pieces += [(qb, qb + MLA_QK, 1), (None, LANES - MLA_QK)]
        wq_rot += [(None, MLA_NOPE)] + _rot_pieces(qb + MLA_NOPE, MLA_ROPE) + [(None, LANES - MLA_QK)]
        kb = h * (MLA_NOPE + MLA_V)
        wkv_k += [(kb, kb + MLA_NOPE, 1), (None, LANES - MLA_NOPE)]
        wkv_v += [(kb + MLA_NOPE, kb + MLA_NOPE + MLA_V, 1), (None, LANES - MLA_V)]
    wq = _cols(mla_w_uq, wq_pieces + wq_rot)
    wq = jnp.concatenate([wq, jnp.zeros((256 - MLA_Q_RANK, wq.shape[1]), wq.dtype)], axis=0).astype(md)
    wkv = _cols(mla_w_ukv, wkv_k + wkv_v).astype(md)
    half = MLA_ROPE // 2

    def rot_gain(gv):
        return jnp.concatenate([jnp.zeros((MLA_NOPE,), gv.dtype), gv[MLA_NOPE + half:],
                                gv[MLA_NOPE:MLA_NOPE + half], jnp.zeros((LANES - MLA_QK,), gv.dtype)])

    q, k, v = _mla_prep(pm, cst["mla_cos"], cst["mla_sin"], _pad_to(mla_q_a_norm, 256)[None, :],
                        mla_kv_a_norm[None, :], wq, wkv,
                        _pad_to(mla_q_norm, LANES)[None, :], rot_gain(mla_q_norm)[None, :],
                        _pad_to(mla_k_norm, LANES)[None, :], rot_gain(mla_k_norm)[None, :], b, t)
    o_mla = _flash(q, k, v, tabs["causal"], use_far=True, name="mla_attention")
    o_mla = o_mla.reshape(b, MLA_HEADS * MLA_V, t)

    o_ret = _retention(pr, cst["ret_cos"], cst["ret_sin"], cst["ret_din"], cst["ret_qd"],
                       cst["ret_kd"], cst["ret_cdm"], cst["ret_bd"],
                       jnp.tile(ret_norm, RET_HEADS)[None, :], b, t)

    g_ = NSA_KV_HEADS
    nc = t // CMP_STRIDE
    qt, ks_n, kw_n, vs_a, vw_a, gates = _nsa_prep(
        pn, nsa_q_norm * (d_ ** -0.5 * LOG2E), _pad_to(kn_sel, LANES)[None, :],
        _pad_to(kn_win, LANES)[None, :], b, t)

    def chunks(c0):
        a = pc[:, c0:c0 + g_ * d_].reshape(b, t, g_, d_).transpose(0, 2, 1, 3)
        return a.reshape(b * g_, nc, CMP_STRIDE * d_)

    pos8 = lambda p: jnp.broadcast_to(p.reshape(1, -1), (8, CMP_BLOCK * d_)).astype(md)
    k_c, v_c = _compress(chunks(0), chunks(g_ * d_), w1_k, w2_k.astype(md), pos8(pos_k),
                         w1_v, w2_v.astype(md), pos8(pos_v), kn_cmp[None, :])
    front = ((0, 0), (0, 0), (CMP_CHUNK, 0), (0, 0))
    k_c = jnp.pad(k_c.reshape(b, g_, nc, d_), front)
    v_c = v_c.reshape(b, g_, nc, d_).astype(F32)
    v_ca = jnp.pad(jnp.concatenate([v_c, jnp.ones_like(v_c[..., :1]),
                                    jnp.zeros_like(v_c[..., :LANES - d_ - 1])], axis=-1), front)
    oc_t, selneg = _cmp_sel(qt, k_c, v_ca, tabs["cmp"], cst["c2s"], cst["bk"], CMP_TQ)
    os_t = _flash(qt, ks_n, vs_a, tabs["sel"], use_far=True, name="nsa_selected",
                  qx=selneg, kx=cst["sel_onehot"].astype(md))
    ow_t = _flash(qt, kw_n, vw_a, tabs["win"], use_far=False, name="nsa_window")

    flat = lambda a: a.reshape(b, NSA_HEADS * d_, t)
    xf = _out_proj(o_mla, o_ret, flat(oc_t), flat(os_t), flat(ow_t), gates, w_out, xf, t)
    return _ffn(xf, ffn_norm[None, :], ffn_w_up, ffn_conv_w, ffn_conv_b[None, :], ffn_w_down, b, t)


def _bias_tables(rel_bias, cst):
    n = ATT_TILE
    lut = rel_bias[cst["bucket"]].T
    delta = (lut - rel_bias[REL_BUCKETS - 1][:, None]) * LOG2E
    hh = delta.shape[0]
    dn = jnp.concatenate([delta, jnp.zeros((hh, n - LANES), F32)], axis=1)
    neg = jnp.full((hh, n), NEG_INF, F32)
    zero = jnp.zeros((hh, n), F32)
    diag = _toeplitz(jnp.concatenate([dn, neg], axis=1), n)
    prev_sel = _toeplitz(jnp.concatenate([zero, dn], axis=1), n)
    prev_win = _toeplitz(jnp.concatenate([neg, dn], axis=1), n)
    off = CMP_STRIDE * CMP_CHUNK - CMP_TQ - (CMP_BLOCK - 1)
    d_max = off + CMP_TQ - 1
    d_min = off - CMP_STRIDE * (CMP_CHUNK - 1)
    by_dist = jnp.concatenate([delta, jnp.zeros((hh, d_max + 1 - LANES), F32),
                               jnp.full((hh, -d_min), NEG_INF, F32)], axis=1)
    cmp_tab = _skew(by_dist, CMP_CHUNK, CMP_STRIDE, off, CMP_TQ)
    dist = cst["cmp_dist"]
    masked = jnp.full((hh, n, n), NEG_INF, F32)
    first = jnp.concatenate([diag, masked], axis=1)

    def near(prev):
        return jnp.stack([first, jnp.concatenate([prev, diag], axis=1)], axis=1)

    cmp_tab = cmp_tab.reshape(NSA_KV_HEADS, NSA_GROUP, *dist.shape).transpose(0, 2, 1, 3).reshape(
        NSA_KV_HEADS, dist.shape[0], NSA_GROUP * dist.shape[1])
    return {"sel": near(prev_sel), "win": near(prev_win), "cmp": cmp_tab,
            "causal": jnp.asarray(cst["causal_tab"])}


def kernel(x, w_in, w_out, attn_norm, ffn_norm, mla_q_a_norm, mla_w_uq, mla_kv_a_norm, mla_w_ukv, mla_q_norm, mla_k_norm, ret_norm, nsa_cmp_pos_k, nsa_cmp_w1_k, nsa_cmp_w2_k, nsa_cmp_pos_v, nsa_cmp_w1_v, nsa_cmp_w2_v, nsa_q_norm, nsa_k_norm_cmp, nsa_k_norm_sel, nsa_k_norm_win, rel_bias, ffn_w_up, ffn_conv_w, ffn_conv_b, ffn_w_down):
    b, t, d = x.shape
    assert d == D_MODEL and t % ATT_TILE == 0 and WINDOW == ATT_TILE
    cst = _constants(t)
    tabs = _bias_tables(rel_bias, cst)
    stacked = (w_in, w_out, nsa_cmp_w1_k, nsa_cmp_w1_v, ffn_w_up, ffn_w_down)
    per_layer = (attn_norm, ffn_norm, mla_q_a_norm, mla_w_uq, mla_kv_a_norm, mla_w_ukv,
                 mla_q_norm, mla_k_norm, ret_norm, nsa_cmp_pos_k, nsa_cmp_w2_k,
                 nsa_cmp_pos_v, nsa_cmp_w2_v, nsa_q_norm, nsa_k_norm_cmp,
                 nsa_k_norm_sel, nsa_k_norm_win, ffn_conv_w, ffn_conv_b)
    xf = x.reshape(b * t, d)
    for l in range(w_in.shape[0]):
        big = [_to_mxu_dtype(w, l) for w in stacked]
        xf = _layer(xf, b, t, cst, tabs, *big, *[p[l] for p in per_layer])
    return xf.reshape(b, t, d)
```

```python
import functools
import math

import numpy as np
import jax
import jax.numpy as jnp
from jax import lax
from jax.experimental import pallas as pl
from jax.experimental.pallas import tpu as pltpu

D_MODEL = 1024
DEPTH = 2
MLA_HEADS = 6
MLA_Q_RANK = 192
MLA_KV_RANK = 128
MLA_NOPE = 64
MLA_ROPE = 32
MLA_V = 64
MLA_QK = MLA_NOPE + MLA_ROPE
RET_HEADS = 4
RET_DK = 64
RET_DV = 64
RET_CHUNK = 128
NSA_HEADS = 6
NSA_KV_HEADS = 2
NSA_GROUP = NSA_HEADS // NSA_KV_HEADS
NSA_DH = 64
CMP_BLOCK = 32
CMP_STRIDE = 16
CMP_HIDDEN = 256
SEL_BLOCK = 64
SEL_TOPK = 16
SEL_LOCAL = 2
WINDOW = 512
REL_BUCKETS = 32
REL_MAX_DIST = 128
D_FF = 2816
ROPE_BASE = 10000.0
EPS = 1e-6
NEG_INF = -1e30
FORCE = 1e9

_IN_SPLITS = (MLA_Q_RANK, MLA_KV_RANK, MLA_ROPE,
              RET_HEADS * RET_DK, RET_HEADS * RET_DK, RET_HEADS * RET_DV, RET_HEADS * RET_DV,
              NSA_HEADS * NSA_DH) + (NSA_KV_HEADS * NSA_DH,) * 6 + (3 * NSA_HEADS,)
_IN_OFF = [0] + [int(v) for v in np.cumsum(_IN_SPLITS)]
D_IN = _IN_OFF[-1]

LANES = 128
ATT_TILE = 512
CMP_TQ = 256
CMP_CHUNK = 128
WIN_TILE = 512
VMEM_LIMIT = 56 * 1024 * 1024

_MXU_DTYPE = jnp.bfloat16
F32 = jnp.float32
LOG2E = math.log2(math.e)


def _cparams(*sem):
    return pltpu.CompilerParams(dimension_semantics=sem, vmem_limit_bytes=VMEM_LIMIT)


def _dot(a, b):
    return jnp.dot(a, b, preferred_element_type=F32)


def _sigmoid(x):
    return 1.0 / (1.0 + jnp.exp(-x))


def _cast_kernel(x_ref, o_ref):
    o_ref[...] = x_ref[0].astype(o_ref.dtype)


def _to_mxu_dtype(w, layer, rows=256):
    _, r, c = w.shape
    return pl.pallas_call(
        _cast_kernel,
        grid=(r // rows,),
        in_specs=[pl.BlockSpec((1, rows, c), lambda i: (layer, i, 0))],
        out_specs=pl.BlockSpec((rows, c), lambda i: (i, 0)),
        out_shape=jax.ShapeDtypeStruct((r, c), _MXU_DTYPE),
        compiler_params=_cparams("parallel"),
        name="weight_cast",
    )(w)


ONES_ROWS = 16


def _ones_rows(n, dtype):
    row = lax.broadcasted_iota(jnp.int32, (ONES_ROWS, n), 0)
    return jnp.where(row == 0, 1.0, 0.0).astype(dtype)


def _with_ones_rows(vt):
    b, h, _, t = vt.shape
    extra = jnp.concatenate([jnp.ones((b, h, 1, t), vt.dtype),
                             jnp.zeros((b, h, ONES_ROWS - 1, t), vt.dtype)], axis=2)
    return jnp.concatenate([vt, extra], axis=2)


def _resident(shape):
    nd = len(shape)
    return pl.BlockSpec(shape, lambda *_: (0,) * nd, pipeline_mode=pl.Buffered(1))


def _in_proj_kernel(x_ref, g_ref, w_ref, *o_refs, widths, copy):
    x = x_ref[...]
    y = x * lax.rsqrt(jnp.mean(x * x, axis=-1, keepdims=True) + EPS)
    xn = (y * g_ref[...]).astype(_MXU_DTYPE)
    slab, c0, cw = copy
    off = 0
    for j, (o_ref, wd) in enumerate(zip(o_refs, widths)):
        res = _dot(xn, w_ref[:, off:off + wd])
        o_ref[...] = res
        if j == slab:
            o_refs[-1][...] = res[:, c0:c0 + cw].astype(o_refs[-1].dtype)
        off += wd


def _in_proj(x, g, w, widths, copy, tm=256):
    m, d = x.shape
    n = w.shape[1]
    return pl.pallas_call(
        functools.partial(_in_proj_kernel, widths=widths, copy=copy),
        grid=(m // tm,),
        in_specs=[pl.BlockSpec((tm, d), lambda i: (i, 0)),
                  _resident((1, d)),
                  _resident((d, n))],
        out_specs=[pl.BlockSpec((tm, wd), lambda i: (i, 0)) for wd in widths + (copy[2],)],
        out_shape=[jax.ShapeDtypeStruct((m, wd), F32) for wd in widths]
        + [jax.ShapeDtypeStruct((m, copy[2]), _MXU_DTYPE)],
        compiler_params=_cparams("parallel"),
        name="in_proj",
    )(x, g, w)


def _mla_prep_kernel(pm_ref, cos_ref, sin_ref, gqa_ref, gkva_ref, wq_ref, wkv_ref,
                     gq_ref, gqr_ref, gk_ref, gkr_ref, q_ref, k_ref, v_ref):
    pm = pm_ref[...]
    hs = MLA_HEADS * LANES
    cq = pm[:, 0:256]
    r = lax.rsqrt(jnp.sum(cq * cq, axis=-1, keepdims=True) * (1.0 / MLA_Q_RANK) + EPS)
    qq = _dot((cq * r * gqa_ref[...]).astype(_MXU_DTYPE), wq_ref[...])
    ckv = pm[:, 256:384]
    r = lax.rsqrt(jnp.mean(ckv * ckv, axis=-1, keepdims=True) + EPS)
    kv = _dot((ckv * r * gkva_ref[...]).astype(_MXU_DTYPE), wkv_ref[...])
    kpe = pm[:, 384:512]
    kpe_rot = pm[:, 512:640]
    cos = cos_ref[...]
    sin = sin_ref[...]
    scale = MLA_QK ** -0.5 * LOG2E
    aq = cos * gq_ref[...] * scale
    bq = sin * gqr_ref[...] * scale
    ak = cos * gk_ref[...]
    bk = sin * gkr_ref[...]
    for h in range(MLA_HEADS):
        sl = slice(h * LANES, (h + 1) * LANES)
        sr = slice(hs + h * LANES, hs + (h + 1) * LANES)
        qh = qq[:, sl]
        rq = lax.rsqrt(jnp.sum(qh * qh, axis=-1, keepdims=True) * (1.0 / MLA_QK) + EPS)
        q_ref[0, h] = ((qh * aq + qq[:, sr] * bq) * rq).T.astype(q_ref.dtype)
        kh = kv[:, sl] + kpe
        rk = lax.rsqrt(jnp.sum(kh * kh, axis=-1, keepdims=True) * (1.0 / MLA_QK) + EPS)
        k_ref[0, h] = ((kh * ak + kpe_rot * bk) * rk).astype(k_ref.dtype)
        v_ref[0, h, 0:MLA_V, :] = kv[:, sr].T[0:MLA_V].astype(v_ref.dtype)
        v_ref[0, h, MLA_V:MLA_V + ONES_ROWS, :] = _ones_rows(pm.shape[0], v_ref.dtype)


def _mla_prep(pm, cos, sin, gqa, gkva, wq, wkv, gq, gqr, gk, gkr, b, t, tm=512):
    nt = t // tm
    hs = MLA_HEADS * LANES
    vec = lambda n: _resident((1, n))
    return pl.pallas_call(
        _mla_prep_kernel,
        grid=(b, nt),
        in_specs=[pl.BlockSpec((tm, 640), lambda bi, i: (bi * nt + i, 0)),
                  pl.BlockSpec((tm, LANES), lambda bi, i: (i, 0)),
                  pl.BlockSpec((tm, LANES), lambda bi, i: (i, 0)),
                  vec(256), vec(LANES), _resident((256, 2 * hs)), _resident((LANES, 2 * hs)),
                  vec(LANES), vec(LANES), vec(LANES), vec(LANES)],
        out_specs=[pl.BlockSpec((1, MLA_HEADS, LANES, tm), lambda bi, i: (bi, 0, 0, i)),
                   pl.BlockSpec((1, MLA_HEADS, tm, LANES), lambda bi, i: (bi, 0, i, 0)),
                   pl.BlockSpec((1, MLA_HEADS, MLA_V + ONES_ROWS, tm), lambda bi, i: (bi, 0, 0, i))],
        out_shape=[jax.ShapeDtypeStruct((b, MLA_HEADS, LANES, t), _MXU_DTYPE),
                   jax.ShapeDtypeStruct((b, MLA_HEADS, t, LANES), _MXU_DTYPE),
                   jax.ShapeDtypeStruct((b, MLA_HEADS, MLA_V + ONES_ROWS, t), _MXU_DTYPE)],
        compiler_params=_cparams("parallel", "parallel"),
        name="mla_prep",
    )(pm, cos, sin, gqa, gkva, wq, wkv, gq, gqr, gk, gkr)


def _tile_plan(nq, use_far):
    if use_far and nq % 2 == 0:
        return [lambda j: j, lambda j: nq - 1 - j]
    n = next(c for c in (8, 4, 2, 1) if nq % c == 0) if not use_far else 1
    return [functools.partial(lambda j, s: n * j + s, s=s) for s in range(n)]


def _paired_position(i, nq):
    return jnp.where(i < nq // 2, 2 * i, 2 * (nq - 1 - i) + 1)


def _flash_kernel(*refs, use_far, sub, dv, extra, tiles, tq, near):
    ns = len(tiles)
    q_refs, refs = refs[:ns], refs[ns:]
    if extra:
        qx_refs, refs = refs[:ns], refs[ns:]
        k_ref, kx_ref, v_ref = refs[:3]
        refs = refs[3:]
    else:
        k_ref, v_ref = refs[:2]
        refs = refs[2:]
    tab_refs, o_ref = refs[:ns * near], refs[ns * near]
    step = pl.program_id(2)
    tc = tq

    def block(q, table, start, nkeys, m, acc):
        kb = k_ref[0, 0, pl.ds(start, nkeys), :]
        if extra:
            kb = jnp.concatenate([kb, kx_ref[pl.ds(start, nkeys), :]], axis=1)
        s = _dot(kb, q)
        if table is not None:
            s = s + table
        for j in range(nkeys // sub):
            sj = s[j * sub:(j + 1) * sub]
            vc = v_ref[0, 0, :, pl.ds(pl.multiple_of(start + j * sub, sub), sub)]
            m_new = jnp.maximum(m, jnp.max(sj, axis=0, keepdims=True))
            p = jnp.exp2(sj - m_new).astype(_MXU_DTYPE)
            acc = jnp.exp2(m - m_new) * acc + _dot(vc, p)
            m = m_new
        return m, acc

    state = []
    for slot in range(ns):
        i = tiles[slot](step)
        q = q_refs[slot][0, 0]
        if extra:
            q = jnp.concatenate([q, qx_refs[slot][0, 0]], axis=0)
        m = jnp.full((1, tq), -3e38, F32)
        acc = jnp.zeros((dv + ONES_ROWS, tq), F32)
        n_far = jnp.maximum(i - (near - 1), 0)
        if use_far:
            done = 0
            for width in (4, 2, 1):
                count = (n_far - done) // width

                def body(c, carry, width=width, base=done, q=q):
                    start = pl.multiple_of((base + c * width) * tc, tc)
                    return block(q, None, start, width * tc, *carry)

                m, acc = lax.fori_loop(0, count, body, (m, acc))
                done = done + count * width
        state.append((q, n_far, m, acc))
    for slot, (q, n_far, m, acc) in enumerate(state):
        table = jnp.concatenate([tab_refs[slot * near + p][0, 0] for p in range(near)], axis=0)
        m, acc = block(q, table, pl.multiple_of(n_far * tc, tc), near * tc, m, acc)
        o_ref[0, 0, :, slot * tq:(slot + 1) * tq] = acc[0:dv] / acc[dv:dv + 1]


def _flash(qt, k, va, tab, *, use_far, name, qx=None, kx=None, sub=256):
    b, h, dk, t = qt.shape
    hk, dva = va.shape[1], va.shape[2]
    dv = dva - ONES_ROWS
    rep = h // hk
    ht, near, tq = tab.shape[0], tab.shape[1] - 1, tab.shape[3]
    extra = qx is not None
    tiles = _tile_plan(t // tq, use_far)
    ns = len(tiles)
    q_specs = [pl.BlockSpec((1, 1, dk, tq), lambda bi, hi, j, f=f: (bi, hi, 0, f(j))) for f in tiles]

    def piece(f, p):
        return lambda bi, hi, j: (hi % ht, jnp.minimum(p - jnp.minimum(f(j), near - 1) + near - 1, near),
                                  0, 0)

    tab_specs = [pl.BlockSpec((1, 1, tq, tq), piece(f, p)) for f in tiles for p in range(near)]
    k_spec = pl.BlockSpec((1, 1, t, dk), lambda bi, hi, j: (bi, hi // rep, 0, 0))
    v_spec = pl.BlockSpec((1, 1, dva, t), lambda bi, hi, j: (bi, hi // rep, 0, 0))
    if extra:
        nx = qx.shape[2]
        qx_specs = [pl.BlockSpec((1, 1, nx, tq), lambda bi, hi, j, f=f: (bi, hi // rep, 0, f(j)))
                    for f in tiles]
        in_specs = q_specs + qx_specs + [k_spec, _resident((t, nx)), v_spec] + tab_specs
        args = (qt,) * ns + (qx,) * ns + (k, kx, va) + (tab,) * (ns * near)
    else:
        in_specs = q_specs + [k_spec, v_spec] + tab_specs
        args = (qt,) * ns + (k, va) + (tab,) * (ns * near)
    return pl.pallas_call(
        functools.partial(_flash_kernel, use_far=use_far, sub=min(sub, tq), dv=dv, extra=extra,
                          tiles=tiles, tq=tq, near=near),
        grid=(b, h, t // (tq * ns)),
        in_specs=in_specs,
        out_specs=pl.BlockSpec((1, 1, dv, ns * tq), lambda bi, hi, j: (bi, hi, 0, j)),
        out_shape=jax.ShapeDtypeStruct((b, h, dv, t), F32),
        compiler_params=_cparams("parallel", "parallel", "parallel"),
        name=name,
    )(*args)


def _ret_kernel(q_ref, k_ref, v_ref, g_ref, qr_ref, kr_ref, cos_ref, sin_ref, din_ref, qd_ref,
                kd_ref, cdm_ref, bd_ref, gn_ref, o_ref, state_ref, *, nchunk):
    @pl.when(pl.program_id(1) == 0)
    def _():
        state_ref[...] = jnp.zeros(state_ref.shape, F32)

    w = RET_HEADS * RET_DK
    lane = lax.broadcasted_iota(jnp.int32, (1, w), 1)
    heads = [(lane >= h * RET_DK) & (lane < (h + 1) * RET_DK) for h in range(RET_HEADS)]
    on_diag = bd_ref[...] > 0.5
    c_ = RET_CHUNK
    for c in range(nchunk):
        sl = slice(c * c_, (c + 1) * c_)
        cos = cos_ref[sl, :]
        sin = sin_ref[sl, :]
        qh = q_ref[sl, :] * cos + qr_ref[sl, :] * sin
        kh = (k_ref[sl, :] * cos + kr_ref[sl, :] * sin) * (RET_DK ** -0.5)
        kb = kh.astype(_MXU_DTYPE)
        vb = v_ref[sl, :].astype(_MXU_DTYPE)
        st = state_ref[...]
        out = _dot((qh * qd_ref[...]).astype(_MXU_DTYPE), st.astype(_MXU_DTYPE))
        for h in range(RET_HEADS):
            qm = jnp.where(heads[h], qh, 0.0).astype(_MXU_DTYPE)
            inner = lax.dot_general(qm, kb, (((1,), (1,)), ((), ())),
                                    preferred_element_type=F32) * din_ref[h]
            out = out + jnp.where(heads[h], _dot(inner.astype(_MXU_DTYPE), vb), 0.0)
        kdt = (kh * kd_ref[...]).T
        state_ref[...] = st * cdm_ref[...] + jnp.where(on_diag, _dot(kdt.astype(_MXU_DTYPE), vb), 0.0)
        o2 = out * out
        ms = jnp.zeros_like(out)
        for h in range(RET_HEADS):
            ssum = jnp.sum(jnp.where(heads[h], o2, 0.0), axis=-1, keepdims=True)
            ms = jnp.where(heads[h], ssum * (1.0 / RET_DV), ms)
        y = out * lax.rsqrt(ms + EPS) * gn_ref[...]
        gg = g_ref[sl, :]
        o_ref[sl, :] = (gg * _sigmoid(gg) * y).astype(o_ref.dtype)


def _retention(pr, cos, sin, din, qd, kd, cdm, bd, gn, b, t, tt=256):
    w = RET_HEADS * RET_DK
    nt = t // tt
    col = lambda j: pl.BlockSpec((tt, w), lambda bi, i: (bi * nt + i, j))
    pos = pl.BlockSpec((tt, w), lambda bi, i: (i, 0))
    return pl.pallas_call(
        functools.partial(_ret_kernel, nchunk=tt // RET_CHUNK),
        grid=(b, nt),
        in_specs=[col(0), col(1), col(2), col(3), col(4), col(5), pos, pos,
                  _resident(din.shape), _resident(qd.shape), _resident(kd.shape),
                  _resident(cdm.shape), _resident(bd.shape), _resident((1, w))],
        out_specs=pl.BlockSpec((tt, w), lambda bi, i: (bi * nt + i, 0)),
        out_shape=jax.ShapeDtypeStruct((b * t, w), _MXU_DTYPE),
        scratch_shapes=[pltpu.VMEM((w, w), F32)],
        compiler_params=_cparams("parallel", "arbitrary"),
        name="retention",
    )(pr, pr, pr, pr, pr, pr, cos, sin, din, qd, kd, cdm, bd, gn)


NSA_Q0, NSA_KC0, NSA_VC0, NSA_KS0, NSA_VS0, NSA_KW0, NSA_VW0, NSA_GATE0 = (
    0, 384, 512, 640, 896, 1024, 1280, 1408)
NSA_SLAB = 1536
GATE_STRIDE = 8


def _nsa_prep_kernel(pn_ref, gq_ref, gks_ref, gkw_ref, qt_ref, ks_ref, kw_ref, vs_ref, vw_ref,
                     gt_ref):
    tm = pn_ref.shape[0]
    d = NSA_DH
    dt = qt_ref.dtype
    xq = pn_ref[:, NSA_Q0:NSA_Q0 + NSA_HEADS * d].T
    for h in range(NSA_HEADS):
        blk = xq[h * d:(h + 1) * d]
        r = lax.rsqrt(jnp.mean(blk * blk, axis=0, keepdims=True) + EPS)
        qt_ref[0, h, 0:d, :] = (blk * r * gq_ref[...]).astype(dt)
        qt_ref[0, h, d:2 * d, :] = jnp.zeros((d, tm), dt)
    for g in range(NSA_KV_HEADS):
        for c0, g_ref, o_ref in ((NSA_KS0, gks_ref, ks_ref), (NSA_KW0, gkw_ref, kw_ref)):
            slot = pn_ref[:, c0 + g * LANES:c0 + (g + 1) * LANES]
            r = lax.rsqrt(jnp.sum(slot * slot, axis=-1, keepdims=True) * (1.0 / d) + EPS)
            o_ref[0, g] = (slot * r * g_ref[...]).astype(dt)
    for c0, o_ref in ((NSA_VS0, vs_ref), (NSA_VW0, vw_ref)):
        vt = pn_ref[:, c0:c0 + LANES].T
        for g in range(NSA_KV_HEADS):
            o_ref[0, g, 0:d, :] = vt[g * d:(g + 1) * d].astype(dt)
            o_ref[0, g, d:d + ONES_ROWS, :] = _ones_rows(tm, dt)
    gt = pn_ref[:, NSA_GATE0:NSA_GATE0 + LANES].T
    gt_ref[0] = _sigmoid(gt[0:NSA_HEADS * GATE_STRIDE])


def _nsa_prep(pn, gq, gks, gkw, b, t, tm=512):
    nt = t // tm
    d = NSA_DH
    g = NSA_KV_HEADS
    md = _MXU_DTYPE
    ch_major = lambda n, r: pl.BlockSpec((1, n, r, tm), lambda bi, i: (bi, 0, 0, i))
    natural = pl.BlockSpec((1, g, tm, LANES), lambda bi, i: (bi, 0, i, 0))
    return pl.pallas_call(
        _nsa_prep_kernel,
        grid=(b, nt),
        in_specs=[pl.BlockSpec((tm, NSA_SLAB), lambda bi, i: (bi * nt + i, 0)),
                  _resident((d, tm)), _resident((1, LANES)), _resident((1, LANES))],
        out_specs=[ch_major(NSA_HEADS, 2 * d), natural, natural,
                   ch_major(g, d + ONES_ROWS), ch_major(g, d + ONES_ROWS),
                   pl.BlockSpec((1, NSA_HEADS * GATE_STRIDE, tm), lambda bi, i: (bi, 0, i))],
        out_shape=[jax.ShapeDtypeStruct((b, NSA_HEADS, 2 * d, t), md),
                   jax.ShapeDtypeStruct((b, g, t, LANES), md),
                   jax.ShapeDtypeStruct((b, g, t, LANES), md),
                   jax.ShapeDtypeStruct((b, g, d + ONES_ROWS, t), md),
                   jax.ShapeDtypeStruct((b, g, d + ONES_ROWS, t), md),
                   jax.ShapeDtypeStruct((b, NSA_HEADS * GATE_STRIDE, t), F32)],
        compiler_params=_cparams("parallel", "parallel"),
        name="nsa_prep",
    )(pn, jnp.broadcast_to(gq[:, None], (d, tm)), gks, gkw)


def _gelu_tanh(x):
    return 0.5 * x * (1.0 + jnp.tanh(math.sqrt(2.0 / math.pi) * (x + 0.044715 * (x * x * x))))


def _compress_kernel(ak_ref, av_ref, w1k_ref, w2k_ref, pk_ref, w1v_ref, w2v_ref, pv_ref, gk_ref,
                     kc_ref, vc_ref):
    half = CMP_STRIDE * NSA_DH

    def comp(a_ref, w1_ref, w2_ref, p_ref):
        a = a_ref[0]
        pb = _dot(p_ref[...], w1_ref[...])[0:1]
        second = _dot(a, w1_ref[half:2 * half, :])
        nc = second.shape[0]
        hid = _dot(a, w1_ref[0:half, :]) + pltpu.roll(second, nc - 1, 0) + pb
        return _dot(_gelu_tanh(hid).astype(_MXU_DTYPE), w2_ref[...])

    kc = comp(ak_ref, w1k_ref, w2k_ref, pk_ref)
    y = kc * lax.rsqrt(jnp.mean(kc * kc, axis=-1, keepdims=True) + EPS)
    kc_ref[0] = (y * gk_ref[...]).astype(kc_ref.dtype)
    vc_ref[0] = comp(av_ref, w1v_ref, w2v_ref, pv_ref).astype(vc_ref.dtype)


def _compress(ak, av, w1k, w2k, pk, w1v, w2v, pv, gk):
    n, nc, kk = ak.shape
    blk = pl.BlockSpec((1, nc, kk), lambda i: (i, 0, 0))
    out = pl.BlockSpec((1, nc, NSA_DH), lambda i: (i, 0, 0))
    w1 = _resident((2 * kk, CMP_HIDDEN))
    w2 = _resident((CMP_HIDDEN, NSA_DH))
    pp = _resident((8, 2 * kk))
    return pl.pallas_call(
        _compress_kernel,
        grid=(n,),
        in_specs=[blk, blk, w1, w2, pp, w1, w2, pp, _resident((1, NSA_DH))],
        out_specs=[out, out],
        out_shape=[jax.ShapeDtypeStruct((n, nc, NSA_DH), _MXU_DTYPE)] * 2,
        compiler_params=_cparams("parallel"),
        name="nsa_compress",
    )(ak, av, w1k, w2k, pk, w1v, w2v, pv, gk)


def _cmp_sel_kernel(q_ref, kc_ref, vca_ref, tab_ref, c2s_ref, bk_ref, oc_ref, sel_ref,
                    m_ref, acco_ref, acci_ref, *, tq, ns, d):
    i = pl.program_id(2)
    ch = CMP_CHUNK
    end = (tq // CMP_STRIDE) * (i + 1)
    n_far = (end - 1) // ch
    rowi = lax.broadcasted_iota(jnp.int32, (ch, NSA_GROUP * tq), 0)
    qpos = tq * i + lax.broadcasted_iota(jnp.int32, (1, tq), 1)
    sees_any = jnp.where(qpos >= CMP_BLOCK - 1, 1.0, 0.0)

    q = jnp.concatenate([q_ref[0, r] for r in range(NSA_GROUP)], axis=1)
    m_ref[...] = jnp.full(m_ref.shape, -3e38, F32)
    acco_ref[...] = jnp.zeros(acco_ref.shape, F32)
    acci_ref[...] = jnp.zeros(acci_ref.shape, F32)

    def chunk(g, with_table):
        start = pl.multiple_of(end - ch * g, CMP_STRIDE)
        kc = kc_ref[0, 0, pl.ds(start, ch), :]
        vt = vca_ref[0, 0, pl.ds(start, ch), :].T[0:d + ONES_ROWS].astype(_MXU_DTYPE)
        ct = c2s_ref[pl.ds(start, ch), :].T.astype(_MXU_DTYPE)
        s = _dot(kc, q)
        if with_table:
            s = s + tab_ref[0]
        s = jnp.where(rowi >= ch * (g + 1) - end, s, NEG_INF)
        m_old = m_ref[...]
        m_new = jnp.maximum(m_old, jnp.max(s, axis=0, keepdims=True))
        alpha = jnp.exp2(m_old - m_new)
        e = jnp.exp2(s - m_new).astype(_MXU_DTYPE)
        acco_ref[...] = alpha * acco_ref[...] + _dot(vt, e)
        acci_ref[...] = alpha * acci_ref[...] + _dot(ct, e)
        m_ref[...] = m_new

    chunk(0, True)

    def far(g, carry):
        chunk(g, False)
        return carry

    lax.fori_loop(1, n_far + 1, far, 0)

    imp = jnp.zeros((ns, tq), F32)
    for r in range(NSA_GROUP):
        lanes = slice(r * tq, (r + 1) * tq)
        inv = sees_any / acco_ref[d:d + 1, lanes]
        oc_ref[0, r] = acco_ref[0:d, lanes] * inv
        imp = imp + acci_ref[:, lanes] * inv

    back = bk_ref[...] + i * (tq // SEL_BLOCK)
    jidx = lax.broadcasted_iota(jnp.int32, (ns, tq), 0)
    forced = (jidx == 0) | ((back >= 0) & (back < SEL_LOCAL))
    jf = jidx.astype(F32)
    taken = -3e38

    def pick(imp, rounds):
        for _ in range(rounds):
            mx = jnp.max(imp, axis=0, keepdims=True)
            first = jnp.min(jnp.where(imp == mx, jf, 1e9), axis=0, keepdims=True)
            imp = jnp.where(jf == first, taken, imp)
        sel_ref[0, 0] = jnp.where(imp == taken, 0.0, NEG_INF).astype(sel_ref.dtype)

    top_k = min(SEL_TOPK, ns)
    n_forced = 1 + SEL_LOCAL

    @pl.when(i == 0)
    def _():
        pick(jnp.where(back >= 0, jnp.where(forced, FORCE, imp), NEG_INF), top_k)

    @pl.when(i > 0)
    def _():
        pick(jnp.where(forced, taken, jnp.where(back >= 0, imp, NEG_INF)), top_k - n_forced)


def _cmp_sel(qt, kc, vca, tab, c2s, bk, tq):
    b, h, _, t = qt.shape
    g, npad, d = kc.shape[1:]
    ns = c2s.shape[1]
    assert tq >= SEL_LOCAL * SEL_BLOCK and tq // CMP_STRIDE <= CMP_CHUNK
    return pl.pallas_call(
        functools.partial(_cmp_sel_kernel, tq=tq, ns=ns, d=d),
        grid=(b, g, t // tq),
        in_specs=[pl.BlockSpec((1, NSA_GROUP, d, tq), lambda bi, gi, i: (bi, gi, 0, i)),
                  pl.BlockSpec((1, 1, npad, d), lambda bi, gi, i: (bi, gi, 0, 0)),
                  pl.BlockSpec((1, 1, npad, LANES), lambda bi, gi, i: (bi, gi, 0, 0)),
                  pl.BlockSpec((1, CMP_CHUNK, NSA_GROUP * tq), lambda bi, gi, i: (gi, 0, 0)),
                  _resident(c2s.shape), _resident(bk.shape)],
        out_specs=[pl.BlockSpec((1, NSA_GROUP, d, tq), lambda bi, gi, i: (bi, gi, 0, i)),
                   pl.BlockSpec((1, 1, ns, tq), lambda bi, gi, i: (bi, gi, 0, i))],
        out_shape=[jax.ShapeDtypeStruct((b, h, d, t), F32),
                   jax.ShapeDtypeStruct((b, g, ns, t), _MXU_DTYPE)],
        scratch_shapes=[pltpu.VMEM((1, NSA_GROUP * tq), F32),
                        pltpu.VMEM((d + ONES_ROWS, NSA_GROUP * tq), F32),
                        pltpu.VMEM((ns, NSA_GROUP * tq), F32)],
        compiler_params=_cparams("parallel", "parallel", "parallel"),
        name="nsa_cmp_sel",
    )(qt, kc, vca, tab, c2s, bk)


def _out_proj_kernel(mla_ref, ret_ref, oc_ref, os_ref, ow_ref, g_ref, w_ref, r_ref, o_ref):
    d = NSA_DH
    w_mla, w_ret = MLA_HEADS * MLA_V, RET_HEADS * RET_DV
    acc = r_ref[...] + _dot(mla_ref[0].T.astype(_MXU_DTYPE), w_ref[0:w_mla, :])
    acc = acc + _dot(ret_ref[...], w_ref[w_mla:w_mla + w_ret, :])
    g = g_ref[0]
    nsa = []
    for h in range(NSA_HEADS):
        rows = slice(h * d, (h + 1) * d)
        g0 = h * GATE_STRIDE
        nsa.append(g[g0:g0 + 1] * oc_ref[0, rows, :] + g[g0 + 1:g0 + 2] * os_ref[0, rows, :]
                   + g[g0 + 2:g0 + 3] * ow_ref[0, rows, :])
    nsa = jnp.concatenate(nsa, axis=0).T.astype(_MXU_DTYPE)
    o_ref[...] = acc + _dot(nsa, w_ref[w_mla + w_ret:, :])


def _out_proj(o_mla, o_ret, oc, os_, ow, gates, w, res, t, tm=512):
    m, n = res.shape
    assert tm == ATT_TILE
    nt = t // tm
    ch_major = lambda c: pl.BlockSpec((1, c, tm), lambda i: (i // nt, 0, i % nt))
    far_order = len(_tile_plan(nt, True)) == 2
    ch_paired = (lambda c: pl.BlockSpec((1, c, tm), lambda i: (i // nt, 0, _paired_position(i % nt, nt)))
                 ) if far_order else ch_major
    tokens = lambda c: pl.BlockSpec((tm, c), lambda i: (i, 0))
    c_nsa = oc.shape[1]
    return pl.pallas_call(
        _out_proj_kernel,
        grid=(m // tm,),
        in_specs=[ch_paired(o_mla.shape[1]), tokens(o_ret.shape[1]), ch_major(c_nsa), ch_paired(c_nsa),
                  ch_major(c_nsa), ch_major(gates.shape[1]), _resident(w.shape), tokens(n)],
        out_specs=tokens(n),
        out_shape=jax.ShapeDtypeStruct((m, n), F32),
        compiler_params=_cparams("parallel"),
        name="out_proj",
    )(o_mla, o_ret, oc, os_, ow, gates, w, res)


def _ffn_kernel(x_ref, gn_ref, wup_ref, cw_ref, cb_ref, wdn_ref, o_ref, hbuf_ref, acc_ref, *,
                tm, fc):
    @pl.when(pl.program_id(1) == 0)
    def _():
        hbuf_ref[0:8, :] = jnp.zeros((8, hbuf_ref.shape[1]), F32)

    x = x_ref[...]
    y = x * lax.rsqrt(jnp.mean(x * x, axis=-1, keepdims=True) + EPS)
    xn = (y * gn_ref[...]).astype(_MXU_DTYPE)

    def up_proj(col0):
        cols = slice(col0, col0 + fc)
        hbuf_ref[8:tm + 8, cols] = _dot(xn, wup_ref[:, cols])

    def conv(col0):
        cols = slice(col0, col0 + fc)
        w = cw_ref[:, cols]
        h = hbuf_ref[8:tm + 8, cols]
        out = (h * w[2:3] + hbuf_ref[7:tm + 7, cols] * w[1:2] + hbuf_ref[6:tm + 6, cols] * w[0:1]
               + cb_ref[:, cols])
        hbuf_ref[0:8, cols] = hbuf_ref[tm:tm + 8, cols]
        return out

    nf = D_FF // fc
    ahead = 4
    for f in range(min(ahead, nf)):
        up_proj(f * fc)
        up_proj(D_FF + f * fc)
    for f in range(nf):
        if f + ahead < nf:
            up_proj((f + ahead) * fc)
            up_proj(D_FF + (f + ahead) * fc)
        gate = conv(f * fc)
        up = conv(D_FF + f * fc)
        act = (gate * _sigmoid(gate) * up).astype(_MXU_DTYPE)
        contrib = _dot(act, wdn_ref[f * fc:(f + 1) * fc, :])
        if f == 0:
            acc_ref[...] = contrib
        else:
            acc_ref[...] += contrib
    o_ref[...] = x + acc_ref[...]


def _ffn(x, gn, wup, cw, cb, wdn, b, t, tm=512, fc=256):
    d = x.shape[1]
    nt = t // tm
    return pl.pallas_call(
        functools.partial(_ffn_kernel, tm=tm, fc=fc),
        grid=(b, nt),
        in_specs=[pl.BlockSpec((tm, d), lambda bi, i: (bi * nt + i, 0)),
                  _resident((1, d)), _resident(wup.shape), _resident(cw.shape),
                  _resident(cb.shape), _resident(wdn.shape)],
        out_specs=pl.BlockSpec((tm, d), lambda bi, i: (bi * nt + i, 0)),
        out_shape=jax.ShapeDtypeStruct(x.shape, F32),
        scratch_shapes=[pltpu.VMEM((tm + 8, 2 * D_FF), F32), pltpu.VMEM((tm, d), F32)],
        compiler_params=_cparams("parallel", "arbitrary"),
        name="conv_ffn",
    )(x, gn, wup, cw, cb, wdn)


def _rope_tables(t, d):
    inv = ROPE_BASE ** (-np.arange(0, d, 2, dtype=np.float64) / d)
    ang = np.arange(t, dtype=np.float64)[:, None] * inv[None, :]
    return (np.concatenate([np.cos(ang)] * 2, axis=1), np.concatenate([np.sin(ang)] * 2, axis=1))


def _t5_bucket_np(dist):
    max_exact = REL_BUCKETS // 2
    d = np.maximum(dist, 1).astype(np.float64)
    log_b = max_exact + (np.log(d / max_exact) / math.log(REL_MAX_DIST / max_exact)
                         * (REL_BUCKETS - max_exact)).astype(np.int32)
    return np.where(dist < max_exact, dist, np.minimum(log_b, REL_BUCKETS - 1))


@functools.lru_cache(maxsize=None)
def _constants(t):
    c = {}
    cos, sin = _rope_tables(t, MLA_ROPE)
    pad = LANES - MLA_QK
    c["mla_cos"] = np.concatenate([np.ones((t, MLA_NOPE)), cos, np.ones((t, pad))], 1).astype(np.float32)
    c["mla_sin"] = np.concatenate([np.zeros((t, MLA_NOPE)), sin, np.zeros((t, pad))], 1).astype(np.float32)
    cos, sin = _rope_tables(t, RET_DK)
    c["ret_cos"] = np.tile(cos, (1, RET_HEADS)).astype(np.float32)
    c["ret_sin"] = np.tile(sin, (1, RET_HEADS)).astype(np.float32)
    lg = np.log(1.0 - 2.0 ** (-5.0 - np.arange(RET_HEADS, dtype=np.float64)))
    idx = np.arange(RET_CHUNK, dtype=np.float64)
    diff = idx[:, None] - idx[None, :]
    c["ret_din"] = (np.exp(np.maximum(diff, 0.0) * lg[:, None, None]) * (diff >= 0)).astype(np.float32)
    qd = np.exp((idx[:, None] + 1.0) * lg[None, :])
    kd = np.exp((RET_CHUNK - 1.0 - idx[:, None]) * lg[None, :])
    c["ret_qd"] = np.repeat(qd, RET_DK, axis=1).astype(np.float32)
    c["ret_kd"] = np.repeat(kd, RET_DK, axis=1).astype(np.float32)
    head_of = np.arange(RET_HEADS * RET_DK) // RET_DK
    c["ret_cdm"] = np.broadcast_to(np.exp(RET_CHUNK * lg)[head_of][:, None],
                                   (RET_HEADS * RET_DK, RET_HEADS * RET_DV)).astype(np.float32)
    c["ret_bd"] = (head_of[:, None] == head_of[None, :]).astype(np.float32)
    c["bucket"] = _t5_bucket_np(np.arange(LANES)).astype(np.int32)
    kk = np.arange(ATT_TILE)[:, None]
    qq = np.arange(ATT_TILE)[None, :]
    causal = np.where(qq >= kk, 0.0, NEG_INF)
    c["causal_tab"] = np.stack([np.zeros_like(causal), causal,
                                np.full_like(causal, NEG_INF)])[None].astype(np.float32)
    nc, ns = t // CMP_STRIDE, t // SEL_BLOCK
    n_cmp = (t - CMP_BLOCK) // CMP_STRIDE + 1
    c_start = np.arange(nc) * CMP_STRIDE
    s_start = np.arange(ns) * SEL_BLOCK
    overlap = np.clip(np.minimum(c_start[:, None] + CMP_BLOCK, s_start[None, :] + SEL_BLOCK)
                      - np.maximum(c_start[:, None], s_start[None, :]), 0, None).astype(np.float64)
    overlap[n_cmp:] = 0.0
    c["c2s"] = np.concatenate([np.zeros((CMP_CHUNK, ns)), overlap / CMP_BLOCK]).astype(np.float32)
    q = np.arange(CMP_TQ)[None, :]
    c["bk"] = (q // SEL_BLOCK - np.arange(ns)[:, None]).astype(np.int32)
    c["cmp_dist"] = (q - CMP_STRIDE * np.arange(CMP_CHUNK)[:, None]
                     + (CMP_STRIDE * CMP_CHUNK - CMP_TQ - (CMP_BLOCK - 1))).astype(np.int32)
    c["sel_onehot"] = (np.arange(t)[:, None] // SEL_BLOCK == np.arange(ns)[None, :]).astype(np.float32)
    return c


def _cols(w, pieces):
    out = []
    for p in pieces:
        if p[0] is None:
            out.append(jnp.zeros((w.shape[0], p[1]), w.dtype))
        else:
            blk = w[:, p[0]:p[1]]
            out.append(-blk if p[2] < 0 else blk)
    return jnp.concatenate(out, axis=1)


def _rot_pieces(base, d):
    return [(base + d // 2, base + d, -1), (base, base + d // 2, 1)]


def _skew(w, rows, step, col0, cols):
    hh, ll = w.shape
    flat = jnp.tile(w, (1, rows))[:, :rows * (ll - step)]
    return flat.reshape(hh, rows, ll - step)[:, :, col0:col0 + cols]


def _toeplitz(w, n):
    return _skew(w, n, 1, 0, n)


def _pad_to(v, n):
    return jnp.concatenate([v, jnp.zeros((n - v.shape[0],), v.dtype)])


def _layer(xf, b, t, cst, tabs, w_in, w_out, w1_k, w1_v, ffn_w_up, ffn_w_down,
           attn_norm, ffn_norm, mla_q_a_norm, mla_w_uq, mla_kv_a_norm, mla_w_ukv, mla_q_norm,
           mla_k_norm, ret_norm, pos_k, w2_k, pos_v, w2_v, nsa_q_norm, kn_cmp, kn_sel, kn_win,
           ffn_conv_w, ffn_conv_b):
    md = _MXU_DTYPE
    o = _IN_OFF
    pieces = [(o[0], o[1], 1), (None, 64), (o[1], o[2], 1),
              (None, 64), (o[2], o[3], 1), (None, 32),
              (None, 64)] + _rot_pieces(o[2], MLA_ROPE) + [(None, 32)]
    pieces += [(o[3], o[7], 1)]
    for base in (o[3], o[4]):
        for h in range(RET_HEADS):
            pieces += _rot_pieces(base + h * RET_DK, RET_DK)
    d_ = NSA_DH
    pieces += [(o[7], o[10], 1)]
    for base in (o[10], o[12]):
        pieces += [(base, base + d_, 1), (None, LANES - d_), (base + d_, base + 2 * d_, 1),
                   (None, LANES - d_), (base + 2 * d_, base + 4 * d_, 1)]
    for h in range(NSA_HEADS):
        pieces += [(o[14] + 3 * h, o[14] + 3 * h + 3, 1), (None, GATE_STRIDE - 3)]
    pieces += [(None, LANES - NSA_HEADS * GATE_STRIDE)]
    w_in_r = _cols(w_in, pieces)
    pm, pr, pn, pc = _in_proj(xf, attn_norm[None, :], w_in_r, (640, 1536, NSA_SLAB),
                              (2, NSA_KC0, 2 * NSA_KV_HEADS * NSA_DH))

    wq_pieces, wq_rot = [], []
    wkv_k, wkv_v = [], []
    for h in range(MLA_HEADS):
        qb = h * MLA_QK
        wq_pieces += [(qb, qb + MLA_QK, 1), (None, LANES - MLA_QK)]
        wq_rot += [(None, MLA_NOPE)] + _rot_pieces(qb + MLA_NOPE, MLA_ROPE) + [(None, LANES - MLA_QK)]
        kb = h * (MLA_NOPE + MLA_V)
        wkv_k += [(kb, kb + MLA_NOPE, 1), (None, LANES - MLA_NOPE)]
        wkv_v += [(kb + MLA_NOPE, kb + MLA_NOPE + MLA_V, 1), (None, LANES - MLA_V)]
    wq = _cols(mla_w_uq, wq_pieces + wq_rot)
    wq = jnp.concatenate([wq, jnp.zeros((256 - MLA_Q_RANK, wq.shape[1]), wq.dtype)], axis=0).astype(md)
    wkv = _cols(mla_w_ukv, wkv_k + wkv_v).astype(md)
    half = MLA_ROPE // 2

    def rot_gain(gv):
        return jnp.concatenate([jnp.zeros((MLA_NOPE,), gv.dtype), gv[MLA_NOPE + half:],
                                gv[MLA_NOPE:MLA_NOPE + half], jnp.zeros((LANES - MLA_QK,), gv.dtype)])

    q, k, v = _mla_prep(pm, cst["mla_cos"], cst["mla_sin"], _pad_to(mla_q_a_norm, 256)[None, :],
                        mla_kv_a_norm[None, :], wq, wkv,
                        _pad_to(mla_q_norm, LANES)[None, :], rot_gain(mla_q_norm)[None, :],
                        _pad_to(mla_k_norm, LANES)[None, :], rot_gain(mla_k_norm)[None, :], b, t)
    o_mla = _flash(q, k, v, tabs["causal"], use_far=True, name="mla_attention")
    o_mla = o_mla.reshape(b, MLA_HEADS * MLA_V, t)

    o_ret = _retention(pr, cst["ret_cos"], cst["ret_sin"], cst["ret_din"], cst["ret_qd"],
                       cst["ret_kd"], cst["ret_cdm"], cst["ret_bd"],
                       jnp.tile(ret_norm, RET_HEADS)[None, :], b, t)

    g_ = NSA_KV_HEADS
    nc = t // CMP_STRIDE
    qt, ks_n, kw_n, vs_a, vw_a, gates = _nsa_prep(
        pn, nsa_q_norm * (d_ ** -0.5 * LOG2E), _pad_to(kn_sel, LANES)[None, :],
        _pad_to(kn_win, LANES)[None, :], b, t)

    def chunks(c0):
        a = pc[:, c0:c0 + g_ * d_].reshape(b, t, g_, d_).transpose(0, 2, 1, 3)
        return a.reshape(b * g_, nc, CMP_STRIDE * d_)

    pos8 = lambda p: jnp.broadcast_to(p.reshape(1, -1), (8, CMP_BLOCK * d_)).astype(md)
    k_c, v_c = _compress(chunks(0), chunks(g_ * d_), w1_k, w2_k.astype(md), pos8(pos_k),
                         w1_v, w2_v.astype(md), pos8(pos_v), kn_cmp[None, :])
    front = ((0, 0), (0, 0), (CMP_CHUNK, 0), (0, 0))
    k_c = jnp.pad(k_c.reshape(b, g_, nc, d_), front)
    v_c = v_c.reshape(b, g_, nc, d_).astype(F32)
    v_ca = jnp.pad(jnp.concatenate([v_c, jnp.ones_like(v_c[..., :1]),
                                    jnp.zeros_like(v_c[..., :LANES - d_ - 1])], axis=-1), front)
    oc_t, selneg = _cmp_sel(qt, k_c, v_ca, tabs["cmp"], cst["c2s"], cst["bk"], CMP_TQ)
    os_t = _flash(qt, ks_n, vs_a, tabs["sel"], use_far=True, name="nsa_selected",
                  qx=selneg, kx=cst["sel_onehot"].astype(md))
    ow_t = _flash(qt, kw_n, vw_a, tabs["win"], use_far=False, name="nsa_window")

    flat = lambda a: a.reshape(b, NSA_HEADS * d_, t)
    xf = _out_proj(o_mla, o_ret, flat(oc_t), flat(os_t), flat(ow_t), gates, w_out, xf, t)
    return _ffn(xf, ffn_norm[None, :], ffn_w_up, ffn_conv_w, ffn_conv_b[None, :], ffn_w_down, b, t)


def _bias_tables(rel_bias, cst):
    n = ATT_TILE
    lut = rel_bias[cst["bucket"]].T
    delta = (lut - rel_bias[REL_BUCKETS - 1][:, None]) * LOG2E
    hh = delta.shape[0]
    dn = jnp.concatenate([delta, jnp.zeros((hh, n - LANES), F32)], axis=1)
    neg = jnp.full((hh, n), NEG_INF, F32)
    zero = jnp.zeros((hh, n), F32)
    diag = _toeplitz(jnp.concatenate([dn, neg], axis=1), n)
    prev_sel = _toeplitz(jnp.concatenate([zero, dn], axis=1), n)
    off = CMP_STRIDE * CMP_CHUNK - CMP_TQ - (CMP_BLOCK - 1)
    d_max = off + CMP_TQ - 1
    d_min = off - CMP_STRIDE * (CMP_CHUNK - 1)
    by_dist = jnp.concatenate([delta, jnp.zeros((hh, d_max + 1 - LANES), F32),
                               jnp.full((hh, -d_min), NEG_INF, F32)], axis=1)
    cmp_tab = _skew(by_dist, CMP_CHUNK, CMP_STRIDE, off, CMP_TQ)
    dist = cst["cmp_dist"]
    masked = jnp.full((hh, n, n), NEG_INF, F32)
    nw = WIN_TILE
    back = WINDOW // nw
    dnw, negw, zerow = dn[:, :nw], neg[:, :nw], zero[:, :nw]
    win = [_toeplitz(jnp.concatenate([negw if r == back else zerow, dnw if r == 1 else zerow], axis=1), nw)
           for r in range(back, 0, -1)]
    win += [_toeplitz(jnp.concatenate([dnw, negw], axis=1), nw), masked[:, :nw, :nw]]
    cmp_tab = cmp_tab.reshape(NSA_KV_HEADS, NSA_GROUP, *dist.shape).transpose(0, 2, 1, 3).reshape(
        NSA_KV_HEADS, dist.shape[0], NSA_GROUP * dist.shape[1])
    return {"sel": jnp.stack([prev_sel, diag, masked], axis=1), "win": jnp.stack(win, axis=1),
            "cmp": cmp_tab, "causal": jnp.asarray(cst["causal_tab"])}


def kernel(x, w_in, w_out, attn_norm, ffn_norm, mla_q_a_norm, mla_w_uq, mla_kv_a_norm, mla_w_ukv, mla_q_norm, mla_k_norm, ret_norm, nsa_cmp_pos_k, nsa_cmp_w1_k, nsa_cmp_w2_k, nsa_cmp_pos_v, nsa_cmp_w1_v, nsa_cmp_w2_v, nsa_q_norm, nsa_k_norm_cmp, nsa_k_norm_sel, nsa_k_norm_win, rel_bias, ffn_w_up, ffn_conv_w, ffn_conv_b, ffn_w_down):
    b, t, d = x.shape
    assert d == D_MODEL and t % (2 * ATT_TILE) == 0 and WINDOW % WIN_TILE == 0 and WIN_TILE >= LANES
    cst = _constants(t)
    tabs = _bias_tables(rel_bias, cst)
    stacked = (w_in, w_out, nsa_cmp_w1_k, nsa_cmp_w1_v, ffn_w_up, ffn_w_down)
    per_layer = (attn_norm, ffn_norm, mla_q_a_norm, mla_w_uq, mla_kv_a_norm, mla_w_ukv,
                 mla_q_norm, mla_k_norm, ret_norm, nsa_cmp_pos_k, nsa_cmp_w2_k,
                 nsa_cmp_pos_v, nsa_cmp_w2_v, nsa_q_norm, nsa_k_norm_cmp,
                 nsa_k_norm_sel, nsa_k_norm_win, ffn_conv_w, ffn_conv_b)
    xf = x.reshape(b * t, d)
    for l in range(w_in.shape[0]):
        big = [_to_mxu_dtype(w, l) for w in stacked]
        xf = _layer(xf, b, t, cst, tabs, *big, *[p[l] for p in per_layer])
    return xf.reshape(b, t, d)
```

```python
import functools
import math

import numpy as np
import jax
import jax.numpy as jnp
from jax import lax
from jax.experimental import pallas as pl
from jax.experimental.pallas import tpu as pltpu

D_MODEL = 1024
DEPTH = 2
MLA_HEADS = 6
MLA_Q_RANK = 192
MLA_KV_RANK = 128
MLA_NOPE = 64
MLA_ROPE = 32
MLA_V = 64
MLA_QK = MLA_NOPE + MLA_ROPE
RET_HEADS = 4
RET_DK = 64
RET_DV = 64
RET_CHUNK = 128
NSA_HEADS = 6
NSA_KV_HEADS = 2
NSA_GROUP = NSA_HEADS // NSA_KV_HEADS
NSA_DH = 64
CMP_BLOCK = 32
CMP_STRIDE = 16
CMP_HIDDEN = 256
SEL_BLOCK = 64
SEL_TOPK = 16
SEL_LOCAL = 2
WINDOW = 512
REL_BUCKETS = 32
REL_MAX_DIST = 128
D_FF = 2816
ROPE_BASE = 10000.0
EPS = 1e-6
NEG_INF = -1e30
FORCE = 1e9

_IN_SPLITS = (MLA_Q_RANK, MLA_KV_RANK, MLA_ROPE,
              RET_HEADS * RET_DK, RET_HEADS * RET_DK, RET_HEADS * RET_DV, RET_HEADS * RET_DV,
              NSA_HEADS * NSA_DH) + (NSA_KV_HEADS * NSA_DH,) * 6 + (3 * NSA_HEADS,)
_IN_OFF = [0] + [int(v) for v in np.cumsum(_IN_SPLITS)]
D_IN = _IN_OFF[-1]

LANES = 128
ATT_TILE = 512
CMP_TQ = 256
CMP_CHUNK = 128
WIN_TILE = 512
VMEM_LIMIT = 56 * 1024 * 1024

_MXU_DTYPE = jnp.bfloat16
F32 = jnp.float32
LOG2E = math.log2(math.e)


def _cparams(*sem):
    return pltpu.CompilerParams(dimension_semantics=sem, vmem_limit_bytes=VMEM_LIMIT)


def _dot(a, b):
    return jnp.dot(a, b, preferred_element_type=F32)


def _sigmoid(x):
    return 1.0 / (1.0 + jnp.exp(-x))


def _cast_kernel(x_ref, o_ref):
    o_ref[...] = x_ref[0].astype(o_ref.dtype)


def _to_mxu_dtype(w, layer, rows=256):
    _, r, c = w.shape
    return pl.pallas_call(
        _cast_kernel,
        grid=(r // rows,),
        in_specs=[pl.BlockSpec((1, rows, c), lambda i: (layer, i, 0))],
        out_specs=pl.BlockSpec((rows, c), lambda i: (i, 0)),
        out_shape=jax.ShapeDtypeStruct((r, c), _MXU_DTYPE),
        compiler_params=_cparams("parallel"),
        name="weight_cast",
    )(w)


ONES_ROWS = 16


def _ones_rows(n, dtype):
    row = lax.broadcasted_iota(jnp.int32, (ONES_ROWS, n), 0)
    return jnp.where(row == 0, 1.0, 0.0).astype(dtype)


def _with_ones_rows(vt):
    b, h, _, t = vt.shape
    extra = jnp.concatenate([jnp.ones((b, h, 1, t), vt.dtype),
                             jnp.zeros((b, h, ONES_ROWS - 1, t), vt.dtype)], axis=2)
    return jnp.concatenate([vt, extra], axis=2)


def _resident(shape):
    nd = len(shape)
    return pl.BlockSpec(shape, lambda *_: (0,) * nd, pipeline_mode=pl.Buffered(1))


def _in_proj_kernel(x_ref, g_ref, w_ref, *o_refs, widths, copy):
    x = x_ref[...]
    y = x * lax.rsqrt(jnp.mean(x * x, axis=-1, keepdims=True) + EPS)
    xn = (y * g_ref[...]).astype(_MXU_DTYPE)
    slab, c0, cw = copy
    off = 0
    for j, (o_ref, wd) in enumerate(zip(o_refs, widths)):
        res = _dot(xn, w_ref[:, off:off + wd])
        o_ref[...] = res
        if j == slab:
            o_refs[-1][...] = res[:, c0:c0 + cw].astype(o_refs[-1].dtype)
        off += wd


def _in_proj(x, g, w, widths, copy, tm=256):
    m, d = x.shape
    n = w.shape[1]
    return pl.pallas_call(
        functools.partial(_in_proj_kernel, widths=widths, copy=copy),
        grid=(m // tm,),
        in_specs=[pl.BlockSpec((tm, d), lambda i: (i, 0)),
                  _resident((1, d)),
                  _resident((d, n))],
        out_specs=[pl.BlockSpec((tm, wd), lambda i: (i, 0)) for wd in widths + (copy[2],)],
        out_shape=[jax.ShapeDtypeStruct((m, wd), F32) for wd in widths]
        + [jax.ShapeDtypeStruct((m, copy[2]), _MXU_DTYPE)],
        compiler_params=_cparams("parallel"),
        name="in_proj",
    )(x, g, w)


def _mla_prep_kernel(pm_ref, cos_ref, sin_ref, gqa_ref, gkva_ref, wq_ref, wkv_ref,
                     gq_ref, gqr_ref, gk_ref, gkr_ref, q_ref, k_ref, v_ref):
    pm = pm_ref[...]
    hs = MLA_HEADS * LANES
    cq = pm[:, 0:256]
    r = lax.rsqrt(jnp.sum(cq * cq, axis=-1, keepdims=True) * (1.0 / MLA_Q_RANK) + EPS)
    qq = _dot((cq * r * gqa_ref[...]).astype(_MXU_DTYPE), wq_ref[...])
    ckv = pm[:, 256:384]
    r = lax.rsqrt(jnp.mean(ckv * ckv, axis=-1, keepdims=True) + EPS)
    kv = _dot((ckv * r * gkva_ref[...]).astype(_MXU_DTYPE), wkv_ref[...])
    kpe = pm[:, 384:512]
    kpe_rot = pm[:, 512:640]
    cos = cos_ref[...]
    sin = sin_ref[...]
    scale = MLA_QK ** -0.5 * LOG2E
    aq = cos * gq_ref[...] * scale
    bq = sin * gqr_ref[...] * scale
    ak = cos * gk_ref[...]
    bk = sin * gkr_ref[...]
    for h in range(MLA_HEADS):
        sl = slice(h * LANES, (h + 1) * LANES)
        sr = slice(hs + h * LANES, hs + (h + 1) * LANES)
        qh = qq[:, sl]
        rq = lax.rsqrt(jnp.sum(qh * qh, axis=-1, keepdims=True) * (1.0 / MLA_QK) + EPS)
        q_ref[0, h] = ((qh * aq + qq[:, sr] * bq) * rq).T.astype(q_ref.dtype)
        kh = kv[:, sl] + kpe
        rk = lax.rsqrt(jnp.sum(kh * kh, axis=-1, keepdims=True) * (1.0 / MLA_QK) + EPS)
        k_ref[0, h] = ((kh * ak + kpe_rot * bk) * rk).astype(k_ref.dtype)
        v_ref[0, h, 0:MLA_V, :] = kv[:, sr].T[0:MLA_V].astype(v_ref.dtype)
        v_ref[0, h, MLA_V:MLA_V + ONES_ROWS, :] = _ones_rows(pm.shape[0], v_ref.dtype)


def _mla_prep(pm, cos, sin, gqa, gkva, wq, wkv, gq, gqr, gk, gkr, b, t, tm=512):
    nt = t // tm
    hs = MLA_HEADS * LANES
    vec = lambda n: _resident((1, n))
    return pl.pallas_call(
        _mla_prep_kernel,
        grid=(b, nt),
        in_specs=[pl.BlockSpec((tm, 640), lambda bi, i: (bi * nt + i, 0)),
                  pl.BlockSpec((tm, LANES), lambda bi, i: (i, 0)),
                  pl.BlockSpec((tm, LANES), lambda bi, i: (i, 0)),
                  vec(256), vec(LANES), _resident((256, 2 * hs)), _resident((LANES, 2 * hs)),
                  vec(LANES), vec(LANES), vec(LANES), vec(LANES)],
        out_specs=[pl.BlockSpec((1, MLA_HEADS, LANES, tm), lambda bi, i: (bi, 0, 0, i)),
                   pl.BlockSpec((1, MLA_HEADS, tm, LANES), lambda bi, i: (bi, 0, i, 0)),
                   pl.BlockSpec((1, MLA_HEADS, MLA_V + ONES_ROWS, tm), lambda bi, i: (bi, 0, 0, i))],
        out_shape=[jax.ShapeDtypeStruct((b, MLA_HEADS, LANES, t), _MXU_DTYPE),
                   jax.ShapeDtypeStruct((b, MLA_HEADS, t, LANES), _MXU_DTYPE),
                   jax.ShapeDtypeStruct((b, MLA_HEADS, MLA_V + ONES_ROWS, t), _MXU_DTYPE)],
        compiler_params=_cparams("parallel", "parallel"),
        name="mla_prep",
    )(pm, cos, sin, gqa, gkva, wq, wkv, gq, gqr, gk, gkr)


def _tile_plan(nq, use_far):
    if use_far and nq % 2 == 0:
        return [lambda j: j, lambda j: nq - 1 - j]
    n = next(c for c in (4, 2, 1) if nq % c == 0) if not use_far else 1
    return [functools.partial(lambda j, s: n * j + s, s=s) for s in range(n)]


def _paired_position(i, nq):
    return jnp.where(i < nq // 2, 2 * i, 2 * (nq - 1 - i) + 1)


def _flash_kernel(*refs, use_far, sub, dv, extra, tiles, tq, near):
    ns = len(tiles)
    q_refs, refs = refs[:ns], refs[ns:]
    if extra:
        qx_refs, refs = refs[:ns], refs[ns:]
        k_ref, kx_ref, v_ref = refs[:3]
        refs = refs[3:]
    else:
        k_ref, v_ref = refs[:2]
        refs = refs[2:]
    tab_refs, o_ref = refs[:ns * near], refs[ns * near]
    step = pl.program_id(2)
    tc = tq

    def block(q, table, start, nkeys, m, acc):
        kb = k_ref[0, 0, pl.ds(start, nkeys), :]
        if extra:
            kb = jnp.concatenate([kb, kx_ref[pl.ds(start, nkeys), :]], axis=1)
        s = _dot(kb, q)
        if table is not None:
            s = s + table
        for j in range(nkeys // sub):
            sj = s[j * sub:(j + 1) * sub]
            vc = v_ref[0, 0, :, pl.ds(pl.multiple_of(start + j * sub, sub), sub)]
            m_new = jnp.maximum(m, jnp.max(sj, axis=0, keepdims=True))
            p = jnp.exp2(sj - m_new).astype(_MXU_DTYPE)
            acc = jnp.exp2(m - m_new) * acc + _dot(vc, p)
            m = m_new
        return m, acc

    state = []
    for slot in range(ns):
        i = tiles[slot](step)
        q = q_refs[slot][0, 0]
        if extra:
            q = jnp.concatenate([q, qx_refs[slot][0, 0]], axis=0)
        m = jnp.full((1, tq), -3e38, F32)
        acc = jnp.zeros((dv + ONES_ROWS, tq), F32)
        n_far = jnp.maximum(i - (near - 1), 0)
        if use_far:
            done = 0
            for width in (4, 2, 1):
                count = (n_far - done) // width

                def body(c, carry, width=width, base=done, q=q):
                    start = pl.multiple_of((base + c * width) * tc, tc)
                    return block(q, None, start, width * tc, *carry)

                m, acc = lax.fori_loop(0, count, body, (m, acc))
                done = done + count * width
        state.append((q, n_far, m, acc))
    for slot, (q, n_far, m, acc) in enumerate(state):
        table = jnp.concatenate([tab_refs[slot * near + p][0, 0] for p in range(near)], axis=0)
        m, acc = block(q, table, pl.multiple_of(n_far * tc, tc), near * tc, m, acc)
        o_ref[0, 0, :, slot * tq:(slot + 1) * tq] = acc[0:dv] / acc[dv:dv + 1]


def _flash(qt, k, va, tab, *, use_far, name, qx=None, kx=None, sub=256):
    b, h, dk, t = qt.shape
    hk, dva = va.shape[1], va.shape[2]
    dv = dva - ONES_ROWS
    rep = h // hk
    ht, near, tq = tab.shape[0], tab.shape[1] - 1, tab.shape[3]
    extra = qx is not None
    tiles = _tile_plan(t // tq, use_far)
    ns = len(tiles)
    q_specs = [pl.BlockSpec((1, 1, dk, tq), lambda bi, hi, j, f=f: (bi, hi, 0, f(j))) for f in tiles]

    def piece(f, p):
        return lambda bi, hi, j: (hi % ht, jnp.minimum(p - jnp.minimum(f(j), near - 1) + near - 1, near),
                                  0, 0)

    tab_specs = [pl.BlockSpec((1, 1, tq, tq), piece(f, p)) for f in tiles for p in range(near)]
    k_spec = pl.BlockSpec((1, 1, t, dk), lambda bi, hi, j: (bi, hi // rep, 0, 0))
    v_spec = pl.BlockSpec((1, 1, dva, t), lambda bi, hi, j: (bi, hi // rep, 0, 0))
    if extra:
        nx = qx.shape[2]
        qx_specs = [pl.BlockSpec((1, 1, nx, tq), lambda bi, hi, j, f=f: (bi, hi // rep, 0, f(j)))
                    for f in tiles]
        in_specs = q_specs + qx_specs + [k_spec, _resident((t, nx)), v_spec] + tab_specs
        args = (qt,) * ns + (qx,) * ns + (k, kx, va) + (tab,) * (ns * near)
    else:
        in_specs = q_specs + [k_spec, v_spec] + tab_specs
        args = (qt,) * ns + (k, va) + (tab,) * (ns * near)
    return pl.pallas_call(
        functools.partial(_flash_kernel, use_far=use_far, sub=min(sub, tq), dv=dv, extra=extra,
                          tiles=tiles, tq=tq, near=near),
        grid=(b, h, t // (tq * ns)),
        in_specs=in_specs,
        out_specs=pl.BlockSpec((1, 1, dv, ns * tq), lambda bi, hi, j: (bi, hi, 0, j)),
        out_shape=jax.ShapeDtypeStruct((b, h, dv, t), F32),
        compiler_params=_cparams("parallel", "parallel", "parallel"),
        name=name,
    )(*args)


def _ret_kernel(q_ref, k_ref, v_ref, g_ref, qr_ref, kr_ref, cos_ref, sin_ref, din_ref, qd_ref,
                kd_ref, cdm_ref, bd_ref, gn_ref, o_ref, state_ref, *, nchunk):
    @pl.when(pl.program_id(1) == 0)
    def _():
        state_ref[...] = jnp.zeros(state_ref.shape, F32)

    w = RET_HEADS * RET_DK
    lane = lax.broadcasted_iota(jnp.int32, (1, w), 1)
    heads = [(lane >= h * RET_DK) & (lane < (h + 1) * RET_DK) for h in range(RET_HEADS)]
    on_diag = bd_ref[...] > 0.5
    c_ = RET_CHUNK
    for c in range(nchunk):
        sl = slice(c * c_, (c + 1) * c_)
        cos = cos_ref[sl, :]
        sin = sin_ref[sl, :]
        qh = q_ref[sl, :] * cos + qr_ref[sl, :] * sin
        kh = (k_ref[sl, :] * cos + kr_ref[sl, :] * sin) * (RET_DK ** -0.5)
        kb = kh.astype(_MXU_DTYPE)
        vb = v_ref[sl, :].astype(_MXU_DTYPE)
        st = state_ref[...]
        out = _dot((qh * qd_ref[...]).astype(_MXU_DTYPE), st.astype(_MXU_DTYPE))
        for h in range(RET_HEADS):
            qm = jnp.where(heads[h], qh, 0.0).astype(_MXU_DTYPE)
            inner = lax.dot_general(qm, kb, (((1,), (1,)), ((), ())),
                                    preferred_element_type=F32) * din_ref[h]
            out = out + jnp.where(heads[h], _dot(inner.astype(_MXU_DTYPE), vb), 0.0)
        kdt = (kh * kd_ref[...]).T
        state_ref[...] = st * cdm_ref[...] + jnp.where(on_diag, _dot(kdt.astype(_MXU_DTYPE), vb), 0.0)
        o2 = out * out
        ms = jnp.zeros_like(out)
        for h in range(RET_HEADS):
            ssum = jnp.sum(jnp.where(heads[h], o2, 0.0), axis=-1, keepdims=True)
            ms = jnp.where(heads[h], ssum * (1.0 / RET_DV), ms)
        y = out * lax.rsqrt(ms + EPS) * gn_ref[...]
        gg = g_ref[sl, :]
        o_ref[sl, :] = (gg * _sigmoid(gg) * y).astype(o_ref.dtype)


def _retention(pr, cos, sin, din, qd, kd, cdm, bd, gn, b, t, tt=256):
    w = RET_HEADS * RET_DK
    nt = t // tt
    col = lambda j: pl.BlockSpec((tt, w), lambda bi, i: (bi * nt + i, j))
    pos = pl.BlockSpec((tt, w), lambda bi, i: (i, 0))
    return pl.pallas_call(
        functools.partial(_ret_kernel, nchunk=tt // RET_CHUNK),
        grid=(b, nt),
        in_specs=[col(0), col(1), col(2), col(3), col(4), col(5), pos, pos,
                  _resident(din.shape), _resident(qd.shape), _resident(kd.shape),
                  _resident(cdm.shape), _resident(bd.shape), _resident((1, w))],
        out_specs=pl.BlockSpec((tt, w), lambda bi, i: (bi * nt + i, 0)),
        out_shape=jax.ShapeDtypeStruct((b * t, w), _MXU_DTYPE),
        scratch_shapes=[pltpu.VMEM((w, w), F32)],
        compiler_params=_cparams("parallel", "arbitrary"),
        name="retention",
    )(pr, pr, pr, pr, pr, pr, cos, sin, din, qd, kd, cdm, bd, gn)


NSA_Q0, NSA_KC0, NSA_VC0, NSA_KS0, NSA_VS0, NSA_KW0, NSA_VW0, NSA_GATE0 = (
    0, 384, 512, 640, 896, 1024, 1280, 1408)
NSA_SLAB = 1536
GATE_STRIDE = 8


def _nsa_prep_kernel(pn_ref, gq_ref, gks_ref, gkw_ref, qt_ref, ks_ref, kw_ref, vs_ref, vw_ref,
                     gt_ref):
    tm = pn_ref.shape[0]
    d = NSA_DH
    dt = qt_ref.dtype
    xq = pn_ref[:, NSA_Q0:NSA_Q0 + NSA_HEADS * d].T
    for h in range(NSA_HEADS):
        blk = xq[h * d:(h + 1) * d]
        r = lax.rsqrt(jnp.mean(blk * blk, axis=0, keepdims=True) + EPS)
        qt_ref[0, h, 0:d, :] = (blk * r * gq_ref[...]).astype(dt)
        qt_ref[0, h, d:2 * d, :] = jnp.zeros((d, tm), dt)
    for g in range(NSA_KV_HEADS):
        for c0, g_ref, o_ref in ((NSA_KS0, gks_ref, ks_ref), (NSA_KW0, gkw_ref, kw_ref)):
            slot = pn_ref[:, c0 + g * LANES:c0 + (g + 1) * LANES]
            r = lax.rsqrt(jnp.sum(slot * slot, axis=-1, keepdims=True) * (1.0 / d) + EPS)
            o_ref[0, g] = (slot * r * g_ref[...]).astype(dt)
    for c0, o_ref in ((NSA_VS0, vs_ref), (NSA_VW0, vw_ref)):
        vt = pn_ref[:, c0:c0 + LANES].T
        for g in range(NSA_KV_HEADS):
            o_ref[0, g, 0:d, :] = vt[g * d:(g + 1) * d].astype(dt)
            o_ref[0, g, d:d + ONES_ROWS, :] = _ones_rows(tm, dt)
    gt = pn_ref[:, NSA_GATE0:NSA_GATE0 + LANES].T
    gt_ref[0] = _sigmoid(gt[0:NSA_HEADS * GATE_STRIDE])


def _nsa_prep(pn, gq, gks, gkw, b, t, tm=512):
    nt = t // tm
    d = NSA_DH
    g = NSA_KV_HEADS
    md = _MXU_DTYPE
    ch_major = lambda n, r: pl.BlockSpec((1, n, r, tm), lambda bi, i: (bi, 0, 0, i))
    natural = pl.BlockSpec((1, g, tm, LANES), lambda bi, i: (bi, 0, i, 0))
    return pl.pallas_call(
        _nsa_prep_kernel,
        grid=(b, nt),
        in_specs=[pl.BlockSpec((tm, NSA_SLAB), lambda bi, i: (bi * nt + i, 0)),
                  _resident((d, tm)), _resident((1, LANES)), _resident((1, LANES))],
        out_specs=[ch_major(NSA_HEADS, 2 * d), natural, natural,
                   ch_major(g, d + ONES_ROWS), ch_major(g, d + ONES_ROWS),
                   pl.BlockSpec((1, NSA_HEADS * GATE_STRIDE, tm), lambda bi, i: (bi, 0, i))],
        out_shape=[jax.ShapeDtypeStruct((b, NSA_HEADS, 2 * d, t), md),
                   jax.ShapeDtypeStruct((b, g, t, LANES), md),
                   jax.ShapeDtypeStruct((b, g, t, LANES), md),
                   jax.ShapeDtypeStruct((b, g, d + ONES_ROWS, t), md),
                   jax.ShapeDtypeStruct((b, g, d + ONES_ROWS, t), md),
                   jax.ShapeDtypeStruct((b, NSA_HEADS * GATE_STRIDE, t), F32)],
        compiler_params=_cparams("parallel", "parallel"),
        name="nsa_prep",
    )(pn, jnp.broadcast_to(gq[:, None], (d, tm)), gks, gkw)


def _gelu_tanh(x):
    return 0.5 * x * (1.0 + jnp.tanh(math.sqrt(2.0 / math.pi) * (x + 0.044715 * (x * x * x))))


def _compress_kernel(ak_ref, av_ref, w1k_ref, w2k_ref, pk_ref, w1v_ref, w2v_ref, pv_ref, gk_ref,
                     kc_ref, vc_ref):
    half = CMP_STRIDE * NSA_DH

    def comp(a_ref, w1_ref, w2_ref, p_ref):
        a = a_ref[0]
        pb = _dot(p_ref[...], w1_ref[...])[0:1]
        second = _dot(a, w1_ref[half:2 * half, :])
        nc = second.shape[0]
        hid = _dot(a, w1_ref[0:half, :]) + pltpu.roll(second, nc - 1, 0) + pb
        return _dot(_gelu_tanh(hid).astype(_MXU_DTYPE), w2_ref[...])

    kc = comp(ak_ref, w1k_ref, w2k_ref, pk_ref)
    y = kc * lax.rsqrt(jnp.mean(kc * kc, axis=-1, keepdims=True) + EPS)
    kc_ref[0] = (y * gk_ref[...]).astype(kc_ref.dtype)
    vc_ref[0] = comp(av_ref, w1v_ref, w2v_ref, pv_ref).astype(vc_ref.dtype)


def _compress(ak, av, w1k, w2k, pk, w1v, w2v, pv, gk):
    n, nc, kk = ak.shape
    blk = pl.BlockSpec((1, nc, kk), lambda i: (i, 0, 0))
    out = pl.BlockSpec((1, nc, NSA_DH), lambda i: (i, 0, 0))
    w1 = _resident((2 * kk, CMP_HIDDEN))
    w2 = _resident((CMP_HIDDEN, NSA_DH))
    pp = _resident((8, 2 * kk))
    return pl.pallas_call(
        _compress_kernel,
        grid=(n,),
        in_specs=[blk, blk, w1, w2, pp, w1, w2, pp, _resident((1, NSA_DH))],
        out_specs=[out, out],
        out_shape=[jax.ShapeDtypeStruct((n, nc, NSA_DH), _MXU_DTYPE)] * 2,
        compiler_params=_cparams("parallel"),
        name="nsa_compress",
    )(ak, av, w1k, w2k, pk, w1v, w2v, pv, gk)


def _cmp_sel_kernel(q_ref, kc_ref, vca_ref, tab_ref, c2s_ref, bk_ref, oc_ref, sel_ref,
                    m_ref, acco_ref, acci_ref, *, tq, ns, d):
    i = pl.program_id(2)
    ch = CMP_CHUNK
    end = (tq // CMP_STRIDE) * (i + 1)
    n_far = (end - 1) // ch
    rowi = lax.broadcasted_iota(jnp.int32, (ch, NSA_GROUP * tq), 0)
    qpos = tq * i + lax.broadcasted_iota(jnp.int32, (1, tq), 1)
    sees_any = jnp.where(qpos >= CMP_BLOCK - 1, 1.0, 0.0)

    q = jnp.concatenate([q_ref[0, r] for r in range(NSA_GROUP)], axis=1)
    m_ref[...] = jnp.full(m_ref.shape, -3e38, F32)
    acco_ref[...] = jnp.zeros(acco_ref.shape, F32)
    acci_ref[...] = jnp.zeros(acci_ref.shape, F32)

    def chunk(g, with_table):
        start = pl.multiple_of(end - ch * g, CMP_STRIDE)
        kc = kc_ref[0, 0, pl.ds(start, ch), :]
        vt = vca_ref[0, 0, pl.ds(start, ch), :].T[0:d + ONES_ROWS].astype(_MXU_DTYPE)
        ct = c2s_ref[pl.ds(start, ch), :].T.astype(_MXU_DTYPE)
        s = _dot(kc, q)
        if with_table:
            s = s + tab_ref[0]
        s = jnp.where(rowi >= ch * (g + 1) - end, s, NEG_INF)
        m_old = m_ref[...]
        m_new = jnp.maximum(m_old, jnp.max(s, axis=0, keepdims=True))
        alpha = jnp.exp2(m_old - m_new)
        e = jnp.exp2(s - m_new).astype(_MXU_DTYPE)
        acco_ref[...] = alpha * acco_ref[...] + _dot(vt, e)
        acci_ref[...] = alpha * acci_ref[...] + _dot(ct, e)
        m_ref[...] = m_new

    chunk(0, True)

    def far(g, carry):
        chunk(g, False)
        return carry

    lax.fori_loop(1, n_far + 1, far, 0)

    imp = jnp.zeros((ns, tq), F32)
    for r in range(NSA_GROUP):
        lanes = slice(r * tq, (r + 1) * tq)
        inv = sees_any / acco_ref[d:d + 1, lanes]
        oc_ref[0, r] = acco_ref[0:d, lanes] * inv
        imp = imp + acci_ref[:, lanes] * inv

    back = bk_ref[...] + i * (tq // SEL_BLOCK)
    jidx = lax.broadcasted_iota(jnp.int32, (ns, tq), 0)
    forced = (jidx == 0) | ((back >= 0) & (back < SEL_LOCAL))
    jf = jidx.astype(F32)
    taken = -3e38

    def pick(imp, rounds):
        for _ in range(rounds):
            mx = jnp.max(imp, axis=0, keepdims=True)
            first = jnp.min(jnp.where(imp == mx, jf, 1e9), axis=0, keepdims=True)
            imp = jnp.where(jf == first, taken, imp)
        sel_ref[0, 0] = jnp.where(imp == taken, 0.0, NEG_INF).astype(sel_ref.dtype)

    top_k = min(SEL_TOPK, ns)
    n_forced = 1 + SEL_LOCAL

    @pl.when(i == 0)
    def _():
        pick(jnp.where(back >= 0, jnp.where(forced, FORCE, imp), NEG_INF), top_k)

    @pl.when(i > 0)
    def _():
        pick(jnp.where(forced, taken, jnp.where(back >= 0, imp, NEG_INF)), top_k - n_forced)


def _cmp_sel(qt, kc, vca, tab, c2s, bk, tq):
    b, h, _, t = qt.shape
    g, npad, d = kc.shape[1:]
    ns = c2s.shape[1]
    assert tq >= SEL_LOCAL * SEL_BLOCK and tq // CMP_STRIDE <= CMP_CHUNK
    return pl.pallas_call(
        functools.partial(_cmp_sel_kernel, tq=tq, ns=ns, d=d),
        grid=(b, g, t // tq),
        in_specs=[pl.BlockSpec((1, NSA_GROUP, d, tq), lambda bi, gi, i: (bi, gi, 0, i)),
                  pl.BlockSpec((1, 1, npad, d), lambda bi, gi, i: (bi, gi, 0, 0)),
                  pl.BlockSpec((1, 1, npad, LANES), lambda bi, gi, i: (bi, gi, 0, 0)),
                  pl.BlockSpec((1, CMP_CHUNK, NSA_GROUP * tq), lambda bi, gi, i: (gi, 0, 0)),
                  _resident(c2s.shape), _resident(bk.shape)],
        out_specs=[pl.BlockSpec((1, NSA_GROUP, d, tq), lambda bi, gi, i: (bi, gi, 0, i)),
                   pl.BlockSpec((1, 1, ns, tq), lambda bi, gi, i: (bi, gi, 0, i))],
        out_shape=[jax.ShapeDtypeStruct((b, h, d, t), F32),
                   jax.ShapeDtypeStruct((b, g, ns, t), _MXU_DTYPE)],
        scratch_shapes=[pltpu.VMEM((1, NSA_GROUP * tq), F32),
                        pltpu.VMEM((d + ONES_ROWS, NSA_GROUP * tq), F32),
                        pltpu.VMEM((ns, NSA_GROUP * tq), F32)],
        compiler_params=_cparams("parallel", "parallel", "parallel"),
        name="nsa_cmp_sel",
    )(qt, kc, vca, tab, c2s, bk)


def _out_proj_kernel(mla_ref, ret_ref, oc_ref, os_ref, ow_ref, g_ref, w_ref, r_ref, o_ref):
    d = NSA_DH
    w_mla, w_ret = MLA_HEADS * MLA_V, RET_HEADS * RET_DV
    acc = r_ref[...] + _dot(mla_ref[0].T.astype(_MXU_DTYPE), w_ref[0:w_mla, :])
    acc = acc + _dot(ret_ref[...], w_ref[w_mla:w_mla + w_ret, :])
    g = g_ref[0]
    nsa = []
    for h in range(NSA_HEADS):
        rows = slice(h * d, (h + 1) * d)
        g0 = h * GATE_STRIDE
        nsa.append(g[g0:g0 + 1] * oc_ref[0, rows, :] + g[g0 + 1:g0 + 2] * os_ref[0, rows, :]
                   + g[g0 + 2:g0 + 3] * ow_ref[0, rows, :])
    nsa = jnp.concatenate(nsa, axis=0).T.astype(_MXU_DTYPE)
    o_ref[...] = acc + _dot(nsa, w_ref[w_mla + w_ret:, :])


def _out_proj(o_mla, o_ret, oc, os_, ow, gates, w, res, t, tm=512):
    m, n = res.shape
    assert tm == ATT_TILE
    nt = t // tm
    ch_major = lambda c: pl.BlockSpec((1, c, tm), lambda i: (i // nt, 0, i % nt))
    far_order = len(_tile_plan(nt, True)) == 2
    ch_paired = (lambda c: pl.BlockSpec((1, c, tm), lambda i: (i // nt, 0, _paired_position(i % nt, nt)))
                 ) if far_order else ch_major
    tokens = lambda c: pl.BlockSpec((tm, c), lambda i: (i, 0))
    c_nsa = oc.shape[1]
    return pl.pallas_call(
        _out_proj_kernel,
        grid=(m // tm,),
        in_specs=[ch_paired(o_mla.shape[1]), tokens(o_ret.shape[1]), ch_major(c_nsa), ch_paired(c_nsa),
                  ch_major(c_nsa), ch_major(gates.shape[1]), _resident(w.shape), tokens(n)],
        out_specs=tokens(n),
        out_shape=jax.ShapeDtypeStruct((m, n), F32),
        compiler_params=_cparams("parallel"),
        name="out_proj",
    )(o_mla, o_ret, oc, os_, ow, gates, w, res)


def _ffn_kernel(x_ref, gn_ref, wup_ref, cw_ref, cb_ref, wdn_ref, o_ref, hbuf_ref, acc_ref, *,
                tm, fc):
    @pl.when(pl.program_id(1) == 0)
    def _():
        hbuf_ref[0:8, :] = jnp.zeros((8, hbuf_ref.shape[1]), F32)

    x = x_ref[...]
    y = x * lax.rsqrt(jnp.mean(x * x, axis=-1, keepdims=True) + EPS)
    xn = (y * gn_ref[...]).astype(_MXU_DTYPE)

    def up_proj(col0):
        cols = slice(col0, col0 + fc)
        hbuf_ref[8:tm + 8, cols] = _dot(xn, wup_ref[:, cols])

    def conv(col0):
        cols = slice(col0, col0 + fc)
        w = cw_ref[:, cols]
        h = hbuf_ref[8:tm + 8, cols]
        out = (h * w[2:3] + hbuf_ref[7:tm + 7, cols] * w[1:2] + hbuf_ref[6:tm + 6, cols] * w[0:1]
               + cb_ref[:, cols])
        hbuf_ref[0:8, cols] = hbuf_ref[tm:tm + 8, cols]
        return out

    nf = D_FF // fc
    ahead = 4
    for f in range(min(ahead, nf)):
        up_proj(f * fc)
        up_proj(D_FF + f * fc)
    for f in range(nf):
        if f + ahead < nf:
            up_proj((f + ahead) * fc)
            up_proj(D_FF + (f + ahead) * fc)
        gate = conv(f * fc)
        up = conv(D_FF + f * fc)
        act = (gate * _sigmoid(gate) * up).astype(_MXU_DTYPE)
        contrib = _dot(act, wdn_ref[f * fc:(f + 1) * fc, :])
        if f == 0:
            acc_ref[...] = contrib
        else:
            acc_ref[...] += contrib
    o_ref[...] = x + acc_ref[...]


def _ffn(x, gn, wup, cw, cb, wdn, b, t, tm=512, fc=256):
    d = x.shape[1]
    nt = t // tm
    return pl.pallas_call(
        functools.partial(_ffn_kernel, tm=tm, fc=fc),
        grid=(b, nt),
        in_specs=[pl.BlockSpec((tm, d), lambda bi, i: (bi * nt + i, 0)),
                  _resident((1, d)), _resident(wup.shape), _resident(cw.shape),
                  _resident(cb.shape), _resident(wdn.shape)],
        out_specs=pl.BlockSpec((tm, d), lambda bi, i: (bi * nt + i, 0)),
        out_shape=jax.ShapeDtypeStruct(x.shape, F32),
        scratch_shapes=[pltpu.VMEM((tm + 8, 2 * D_FF), F32), pltpu.VMEM((tm, d), F32)],
        compiler_params=_cparams("parallel", "arbitrary"),
        name="conv_ffn",
    )(x, gn, wup, cw, cb, wdn)


def _rope_tables(t, d):
    inv = ROPE_BASE ** (-np.arange(0, d, 2, dtype=np.float64) / d)
    ang = np.arange(t, dtype=np.float64)[:, None] * inv[None, :]
    return (np.concatenate([np.cos(ang)] * 2, axis=1), np.concatenate([np.sin(ang)] * 2, axis=1))


def _t5_bucket_np(dist):
    max_exact = REL_BUCKETS // 2
    d = np.maximum(dist, 1).astype(np.float64)
    log_b = max_exact + (np.log(d / max_exact) / math.log(REL_MAX_DIST / max_exact)
                         * (REL_BUCKETS - max_exact)).astype(np.int32)
    return np.where(dist < max_exact, dist, np.minimum(log_b, REL_BUCKETS - 1))


@functools.lru_cache(maxsize=None)
def _constants(t):
    c = {}
    cos, sin = _rope_tables(t, MLA_ROPE)
    pad = LANES - MLA_QK
    c["mla_cos"] = np.concatenate([np.ones((t, MLA_NOPE)), cos, np.ones((t, pad))], 1).astype(np.float32)
    c["mla_sin"] = np.concatenate([np.zeros((t, MLA_NOPE)), sin, np.zeros((t, pad))], 1).astype(np.float32)
    cos, sin = _rope_tables(t, RET_DK)
    c["ret_cos"] = np.tile(cos, (1, RET_HEADS)).astype(np.float32)
    c["ret_sin"] = np.tile(sin, (1, RET_HEADS)).astype(np.float32)
    lg = np.log(1.0 - 2.0 ** (-5.0 - np.arange(RET_HEADS, dtype=np.float64)))
    idx = np.arange(RET_CHUNK, dtype=np.float64)
    diff = idx[:, None] - idx[None, :]
    c["ret_din"] = (np.exp(np.maximum(diff, 0.0) * lg[:, None, None]) * (diff >= 0)).astype(np.float32)
    qd = np.exp((idx[:, None] + 1.0) * lg[None, :])
    kd = np.exp((RET_CHUNK - 1.0 - idx[:, None]) * lg[None, :])
    c["ret_qd"] = np.repeat(qd, RET_DK, axis=1).astype(np.float32)
    c["ret_kd"] = np.repeat(kd, RET_DK, axis=1).astype(np.float32)
    head_of = np.arange(RET_HEADS * RET_DK) // RET_DK
    c["ret_cdm"] = np.broadcast_to(np.exp(RET_CHUNK * lg)[head_of][:, None],
                                   (RET_HEADS * RET_DK, RET_HEADS * RET_DV)).astype(np.float32)
    c["ret_bd"] = (head_of[:, None] == head_of[None, :]).astype(np.float32)
    c["bucket"] = _t5_bucket_np(np.arange(LANES)).astype(np.int32)
    kk = np.arange(ATT_TILE)[:, None]
    qq = np.arange(ATT_TILE)[None, :]
    causal = np.where(qq >= kk, 0.0, NEG_INF)
    c["causal_tab"] = np.stack([np.zeros_like(causal), causal,
                                np.full_like(causal, NEG_INF)])[None].astype(np.float32)
    nc, ns = t // CMP_STRIDE, t // SEL_BLOCK
    n_cmp = (t - CMP_BLOCK) // CMP_STRIDE + 1
    c_start = np.arange(nc) * CMP_STRIDE
    s_start = np.arange(ns) * SEL_BLOCK
    overlap = np.clip(np.minimum(c_start[:, None] + CMP_BLOCK, s_start[None, :] + SEL_BLOCK)
                      - np.maximum(c_start[:, None], s_start[None, :]), 0, None).astype(np.float64)
    overlap[n_cmp:] = 0.0
    c["c2s"] = np.concatenate([np.zeros((CMP_CHUNK, ns)), overlap / CMP_BLOCK]).astype(np.float32)
    q = np.arange(CMP_TQ)[None, :]
    c["bk"] = (q // SEL_BLOCK - np.arange(ns)[:, None]).astype(np.int32)
    c["cmp_dist"] = (q - CMP_STRIDE * np.arange(CMP_CHUNK)[:, None]
                     + (CMP_STRIDE * CMP_CHUNK - CMP_TQ - (CMP_BLOCK - 1))).astype(np.int32)
    c["sel_onehot"] = (np.arange(t)[:, None] // SEL_BLOCK == np.arange(ns)[None, :]).astype(np.float32)
    return c


def _cols(w, pieces):
    out = []
    for p in pieces:
        if p[0] is None:
            out.append(jnp.zeros((w.shape[0], p[1]), w.dtype))
        else:
            blk = w[:, p[0]:p[1]]
            out.append(-blk if p[2] < 0 else blk)
    return jnp.concatenate(out, axis=1)


def _rot_pieces(base, d):
    return [(base + d // 2, base + d, -1), (base, base + d // 2, 1)]


def _skew(w, rows, step, col0, cols):
    hh, ll = w.shape
    flat = jnp.tile(w, (1, rows))[:, :rows * (ll - step)]
    return flat.reshape(hh, rows, ll - step)[:, :, col0:col0 + cols]


def _toeplitz(w, n):
    return _skew(w, n, 1, 0, n)


def _pad_to(v, n):
    return jnp.concatenate([v, jnp.zeros((n - v.shape[0],), v.dtype)])


def _layer(xf, b, t, cst, tabs, w_in, w_out, w1_k, w1_v, ffn_w_up, ffn_w_down,
           attn_norm, ffn_norm, mla_q_a_norm, mla_w_uq, mla_kv_a_norm, mla_w_ukv, mla_q_norm,
           mla_k_norm, ret_norm, pos_k, w2_k, pos_v, w2_v, nsa_q_norm, kn_cmp, kn_sel, kn_win,
           ffn_conv_w, ffn_conv_b):
    md = _MXU_DTYPE
    o = _IN_OFF
    pieces = [(o[0], o[1], 1), (None, 64), (o[1], o[2], 1),
              (None, 64), (o[2], o[3], 1), (None, 32),
              (None, 64)] + _rot_pieces(o[2], MLA_ROPE) + [(None, 32)]
    pieces += [(o[3], o[7], 1)]
    for base in (o[3], o[4]):
        for h in range(RET_HEADS):
            pieces += _rot_pieces(base + h * RET_DK, RET_DK)
    d_ = NSA_DH
    pieces += [(o[7], o[10], 1)]
    for base in (o[10], o[12]):
        pieces += [(base, base + d_, 1), (None, LANES - d_), (base + d_, base + 2 * d_, 1),
                   (None, LANES - d_), (base + 2 * d_, base + 4 * d_, 1)]
    for h in range(NSA_HEADS):
        pieces += [(o[14] + 3 * h, o[14] + 3 * h + 3, 1), (None, GATE_STRIDE - 3)]
    pieces += [(None, LANES - NSA_HEADS * GATE_STRIDE)]
    w_in_r = _cols(w_in, pieces)
    pm, pr, pn, pc = _in_proj(xf, attn_norm[None, :], w_in_r, (640, 1536, NSA_SLAB),
                              (2, NSA_KC0, 2 * NSA_KV_HEADS * NSA_DH))

    wq_pieces, wq_rot = [], []
    wkv_k, wkv_v = [], []
    for h in range(MLA_HEADS):
        qb = h * MLA_QK
        wq_pieces += [(qb, qb + MLA_QK, 1), (None, LANES - MLA_QK)]
        wq_rot += [(None, MLA_NOPE)] + _rot_pieces(qb + MLA_NOPE, MLA_ROPE) + [(None, LANES - MLA_QK)]
        kb = h * (MLA_NOPE + MLA_V)
        wkv_k += [(kb, kb + MLA_NOPE, 1), (None, LANES - MLA_NOPE)]
        wkv_v += [(kb + MLA_NOPE, kb + MLA_NOPE + MLA_V, 1), (None, LANES - MLA_V)]
    wq = _cols(mla_w_uq, wq_pieces + wq_rot)
    wq = jnp.concatenate([wq, jnp.zeros((256 - MLA_Q_RANK, wq.shape[1]), wq.dtype)], axis=0).astype(md)
    wkv = _cols(mla_w_ukv, wkv_k + wkv_v).astype(md)
    half = MLA_ROPE // 2

    def rot_gain(gv):
        return jnp.concatenate([jnp.zeros((MLA_NOPE,), gv.dtype), gv[MLA_NOPE + half:],
                                gv[MLA_NOPE:MLA_NOPE + half], jnp.zeros((LANES - MLA_QK,), gv.dtype)])

    q, k, v = _mla_prep(pm, cst["mla_cos"], cst["mla_sin"], _pad_to(mla_q_a_norm, 256)[None, :],
                        mla_kv_a_norm[None, :], wq, wkv,
                        _pad_to(mla_q_norm, LANES)[None, :], rot_gain(mla_q_norm)[None, :],
                        _pad_to(mla_k_norm, LANES)[None, :], rot_gain(mla_k_norm)[None, :], b, t)
    o_mla = _flash(q, k, v, tabs["causal"], use_far=True, name="mla_attention")
    o_mla = o_mla.reshape(b, MLA_HEADS * MLA_V, t)

    o_ret = _retention(pr, cst["ret_cos"], cst["ret_sin"], cst["ret_din"], cst["ret_qd"],
                       cst["ret_kd"], cst["ret_cdm"], cst["ret_bd"],
                       jnp.tile(ret_norm, RET_HEADS)[None, :], b, t)

    g_ = NSA_KV_HEADS
    nc = t // CMP_STRIDE
    qt, ks_n, kw_n, vs_a, vw_a, gates = _nsa_prep(
        pn, nsa_q_norm * (d_ ** -0.5 * LOG2E), _pad_to(kn_sel, LANES)[None, :],
        _pad_to(kn_win, LANES)[None, :], b, t)

    def chunks(c0):
        a = pc[:, c0:c0 + g_ * d_].reshape(b, t, g_, d_).transpose(0, 2, 1, 3)
        return a.reshape(b * g_, nc, CMP_STRIDE * d_)

    pos8 = lambda p: jnp.broadcast_to(p.reshape(1, -1), (8, CMP_BLOCK * d_)).astype(md)
    k_c, v_c = _compress(chunks(0), chunks(g_ * d_), w1_k, w2_k.astype(md), pos8(pos_k),
                         w1_v, w2_v.astype(md), pos8(pos_v), kn_cmp[None, :])
    front = ((0, 0), (0, 0), (CMP_CHUNK, 0), (0, 0))
    k_c = jnp.pad(k_c.reshape(b, g_, nc, d_), front)
    v_c = v_c.reshape(b, g_, nc, d_).astype(F32)
    v_ca = jnp.pad(jnp.concatenate([v_c, jnp.ones_like(v_c[..., :1]),
                                    jnp.zeros_like(v_c[..., :LANES - d_ - 1])], axis=-1), front)
    oc_t, selneg = _cmp_sel(qt, k_c, v_ca, tabs["cmp"], cst["c2s"], cst["bk"], CMP_TQ)
    os_t = _flash(qt, ks_n, vs_a, tabs["sel"], use_far=True, name="nsa_selected",
                  qx=selneg, kx=cst["sel_onehot"].astype(md))
    ow_t = _flash(qt, kw_n, vw_a, tabs["win"], use_far=False, name="nsa_window")

    flat = lambda a: a.reshape(b, NSA_HEADS * d_, t)
    xf = _out_proj(o_mla, o_ret, flat(oc_t), flat(os_t), flat(ow_t), gates, w_out, xf, t)
    return _ffn(xf, ffn_norm[None, :], ffn_w_up, ffn_conv_w, ffn_conv_b[None, :], ffn_w_down, b, t)


def _bias_tables(rel_bias, cst):
    n = ATT_TILE
    lut = rel_bias[cst["bucket"]].T
    delta = (lut - rel_bias[REL_BUCKETS - 1][:, None]) * LOG2E
    hh = delta.shape[0]
    dn = jnp.concatenate([delta, jnp.zeros((hh, n - LANES), F32)], axis=1)
    neg = jnp.full((hh, n), NEG_INF, F32)
    zero = jnp.zeros((hh, n), F32)
    diag = _toeplitz(jnp.concatenate([dn, neg], axis=1), n)
    prev_sel = _toeplitz(jnp.concatenate([zero, dn], axis=1), n)
    off = CMP_STRIDE * CMP_CHUNK - CMP_TQ - (CMP_BLOCK - 1)
    d_max = off + CMP_TQ - 1
    d_min = off - CMP_STRIDE * (CMP_CHUNK - 1)
    by_dist = jnp.concatenate([delta, jnp.zeros((hh, d_max + 1 - LANES), F32),
                               jnp.full((hh, -d_min), NEG_INF, F32)], axis=1)
    cmp_tab = _skew(by_dist, CMP_CHUNK, CMP_STRIDE, off, CMP_TQ)
    dist = cst["cmp_dist"]
    masked = jnp.full((hh, n, n), NEG_INF, F32)
    nw = WIN_TILE
    back = WINDOW // nw
    dnw, negw, zerow = dn[:, :nw], neg[:, :nw], zero[:, :nw]
    win = [_toeplitz(jnp.concatenate([negw if r == back else zerow, dnw if r == 1 else zerow], axis=1), nw)
           for r in range(back, 0, -1)]
    win += [_toeplitz(jnp.concatenate([dnw, negw], axis=1), nw), masked[:, :nw, :nw]]
    cmp_tab = cmp_tab.reshape(NSA_KV_HEADS, NSA_GROUP, *dist.shape).transpose(0, 2, 1, 3).reshape(
        NSA_KV_HEADS, dist.shape[0], NSA_GROUP * dist.shape[1])
    return {"sel": jnp.stack([prev_sel, diag, masked], axis=1), "win": jnp.stack(win, axis=1),
            "cmp": cmp_tab, "causal": jnp.asarray(cst["causal_tab"])}


def kernel(x, w_in, w_out, attn_norm, ffn_norm, mla_q_a_norm, mla_w_uq, mla_kv_a_norm, mla_w_ukv, mla_q_norm, mla_k_norm, ret_norm, nsa_cmp_pos_k, nsa_cmp_w1_k, nsa_cmp_w2_k, nsa_cmp_pos_v, nsa_cmp_w1_v, nsa_cmp_w2_v, nsa_q_norm, nsa_k_norm_cmp, nsa_k_norm_sel, nsa_k_norm_win, rel_bias, ffn_w_up, ffn_conv_w, ffn_conv_b, ffn_w_down):
    b, t, d = x.shape
    assert d == D_MODEL and t % (2 * ATT_TILE) == 0 and WINDOW % WIN_TILE == 0 and WIN_TILE >= LANES
    cst = _constants(t)
    tabs = _bias_tables(rel_bias, cst)
    stacked = (w_in, w_out, nsa_cmp_w1_k, nsa_cmp_w1_v, ffn_w_up, ffn_w_down)
    per_layer = (attn_norm, ffn_norm, mla_q_a_norm, mla_w_uq, mla_kv_a_norm, mla_w_ukv,
                 mla_q_norm, mla_k_norm, ret_norm, nsa_cmp_pos_k, nsa_cmp_w2_k,
                 nsa_cmp_pos_v, nsa_cmp_w2_v, nsa_q_norm, nsa_k_norm_cmp,
                 nsa_k_norm_sel, nsa_k_norm_win, ffn_conv_w, ffn_conv_b)
    xf = x.reshape(b * t, d)
    for l in range(w_in.shape[0]):
        big = [_to_mxu_dtype(w, l) for w in stacked]
        xf = _layer(xf, b, t, cst, tabs, *big, *[p[l] for p in per_layer])
    return xf.reshape(b, t, d)
```

```python
import functools
import math

import numpy as np
import jax
import jax.numpy as jnp
from jax import lax
from jax.experimental import pallas as pl
from jax.experimental.pallas import tpu as pltpu

D_MODEL = 1024
DEPTH = 2
MLA_HEADS = 6
MLA_Q_RANK = 192
MLA_KV_RANK = 128
MLA_NOPE = 64
MLA_ROPE = 32
MLA_V = 64
MLA_QK = MLA_NOPE + MLA_ROPE
RET_HEADS = 4
RET_DK = 64
RET_DV = 64
RET_CHUNK = 128
NSA_HEADS = 6
NSA_KV_HEADS = 2
NSA_GROUP = NSA_HEADS // NSA_KV_HEADS
NSA_DH = 64
CMP_BLOCK = 32
CMP_STRIDE = 16
CMP_HIDDEN = 256
SEL_BLOCK = 64
SEL_TOPK = 16
SEL_LOCAL = 2
WINDOW = 512
REL_BUCKETS = 32
REL_MAX_DIST = 128
D_FF = 2816
ROPE_BASE = 10000.0
EPS = 1e-6
NEG_INF = -1e30
FORCE = 1e9

_IN_SPLITS = (MLA_Q_RANK, MLA_KV_RANK, MLA_ROPE,
              RET_HEADS * RET_DK, RET_HEADS * RET_DK, RET_HEADS * RET_DV, RET_HEADS * RET_DV,
              NSA_HEADS * NSA_DH) + (NSA_KV_HEADS * NSA_DH,) * 6 + (3 * NSA_HEADS,)
_IN_OFF = [0] + [int(v) for v in np.cumsum(_IN_SPLITS)]
D_IN = _IN_OFF[-1]

LANES = 128
ATT_TILE = 512
CMP_TQ = 256
CMP_CHUNK = 128
WIN_TILE = 512
VMEM_LIMIT = 56 * 1024 * 1024

_MXU_DTYPE = jnp.bfloat16
F32 = jnp.float32
LOG2E = math.log2(math.e)


def _cparams(*sem):
    return pltpu.CompilerParams(dimension_semantics=sem, vmem_limit_bytes=VMEM_LIMIT)


def _dot(a, b):
    return jnp.dot(a, b, preferred_element_type=F32)


def _sigmoid(x):
    return 1.0 / (1.0 + jnp.exp(-x))


def _cast_kernel(x_ref, o_ref):
    o_ref[...] = x_ref[0].astype(o_ref.dtype)


def _to_mxu_dtype(w, layer, rows=256):
    _, r, c = w.shape
    return pl.pallas_call(
        _cast_kernel,
        grid=(r // rows,),
        in_specs=[pl.BlockSpec((1, rows, c), lambda i: (layer, i, 0))],
        out_specs=pl.BlockSpec((rows, c), lambda i: (i, 0)),
        out_shape=jax.ShapeDtypeStruct((r, c), _MXU_DTYPE),
        compiler_params=_cparams("parallel"),
        name="weight_cast",
    )(w)


ONES_ROWS = 16


def _ones_rows(n, dtype):
    row = lax.broadcasted_iota(jnp.int32, (ONES_ROWS, n), 0)
    return jnp.where(row == 0, 1.0, 0.0).astype(dtype)


def _with_ones_rows(vt):
    b, h, _, t = vt.shape
    extra = jnp.concatenate([jnp.ones((b, h, 1, t), vt.dtype),
                             jnp.zeros((b, h, ONES_ROWS - 1, t), vt.dtype)], axis=2)
    return jnp.concatenate([vt, extra], axis=2)


def _resident(shape):
    nd = len(shape)
    return pl.BlockSpec(shape, lambda *_: (0,) * nd, pipeline_mode=pl.Buffered(1))


def _in_proj_kernel(x_ref, g_ref, w_ref, *o_refs, widths, copy):
    x = x_ref[...]
    y = x * lax.rsqrt(jnp.mean(x * x, axis=-1, keepdims=True) + EPS)
    xn = (y * g_ref[...]).astype(_MXU_DTYPE)
    slab, c0, cw = copy
    off = 0
    for j, (o_ref, wd) in enumerate(zip(o_refs, widths)):
        res = _dot(xn, w_ref[:, off:off + wd])
        o_ref[...] = res
        if j == slab:
            o_refs[-1][...] = res[:, c0:c0 + cw].astype(o_refs[-1].dtype)
        off += wd


def _in_proj(x, g, w, widths, copy, tm=256):
    m, d = x.shape
    n = w.shape[1]
    return pl.pallas_call(
        functools.partial(_in_proj_kernel, widths=widths, copy=copy),
        grid=(m // tm,),
        in_specs=[pl.BlockSpec((tm, d), lambda i: (i, 0)),
                  _resident((1, d)),
                  _resident((d, n))],
        out_specs=[pl.BlockSpec((tm, wd), lambda i: (i, 0)) for wd in widths + (copy[2],)],
        out_shape=[jax.ShapeDtypeStruct((m, wd), F32) for wd in widths]
        + [jax.ShapeDtypeStruct((m, copy[2]), _MXU_DTYPE)],
        compiler_params=_cparams("parallel"),
        name="in_proj",
    )(x, g, w)


def _mla_prep_kernel(pm_ref, cos_ref, sin_ref, gqa_ref, gkva_ref, wq_ref, wkv_ref,
                     gq_ref, gqr_ref, gk_ref, gkr_ref, q_ref, k_ref, v_ref):
    pm = pm_ref[...]
    hs = MLA_HEADS * LANES
    cq = pm[:, 0:256]
    r = lax.rsqrt(jnp.sum(cq * cq, axis=-1, keepdims=True) * (1.0 / MLA_Q_RANK) + EPS)
    qq = _dot((cq * r * gqa_ref[...]).astype(_MXU_DTYPE), wq_ref[...])
    ckv = pm[:, 256:384]
    r = lax.rsqrt(jnp.mean(ckv * ckv, axis=-1, keepdims=True) + EPS)
    kv = _dot((ckv * r * gkva_ref[...]).astype(_MXU_DTYPE), wkv_ref[...])
    kpe = pm[:, 384:512]
    kpe_rot = pm[:, 512:640]
    cos = cos_ref[...]
    sin = sin_ref[...]
    scale = MLA_QK ** -0.5 * LOG2E
    aq = cos * gq_ref[...] * scale
    bq = sin * gqr_ref[...] * scale
    ak = cos * gk_ref[...]
    bk = sin * gkr_ref[...]
    for h in range(MLA_HEADS):
        sl = slice(h * LANES, (h + 1) * LANES)
        sr = slice(hs + h * LANES, hs + (h + 1) * LANES)
        qh = qq[:, sl]
        rq = lax.rsqrt(jnp.sum(qh * qh, axis=-1, keepdims=True) * (1.0 / MLA_QK) + EPS)
        q_ref[0, h] = ((qh * aq + qq[:, sr] * bq) * rq).T.astype(q_ref.dtype)
        kh = kv[:, sl] + kpe
        rk = lax.rsqrt(jnp.sum(kh * kh, axis=-1, keepdims=True) * (1.0 / MLA_QK) + EPS)
        k_ref[0, h] = ((kh * ak + kpe_rot * bk) * rk).astype(k_ref.dtype)
        v_ref[0, h, 0:MLA_V, :] = kv[:, sr].T[0:MLA_V].astype(v_ref.dtype)
        v_ref[0, h, MLA_V:MLA_V + ONES_ROWS, :] = _ones_rows(pm.shape[0], v_ref.dtype)


def _mla_prep(pm, cos, sin, gqa, gkva, wq, wkv, gq, gqr, gk, gkr, b, t, tm=512):
    nt = t // tm
    hs = MLA_HEADS * LANES
    vec = lambda n: _resident((1, n))
    return pl.pallas_call(
        _mla_prep_kernel,
        grid=(b, nt),
        in_specs=[pl.BlockSpec((tm, 640), lambda bi, i: (bi * nt + i, 0)),
                  pl.BlockSpec((tm, LANES), lambda bi, i: (i, 0)),
                  pl.BlockSpec((tm, LANES), lambda bi, i: (i, 0)),
                  vec(256), vec(LANES), _resident((256, 2 * hs)), _resident((LANES, 2 * hs)),
                  vec(LANES), vec(LANES), vec(LANES), vec(LANES)],
        out_specs=[pl.BlockSpec((1, MLA_HEADS, LANES, tm), lambda bi, i: (bi, 0, 0, i)),
                   pl.BlockSpec((1, MLA_HEADS, tm, LANES), lambda bi, i: (bi, 0, i, 0)),
                   pl.BlockSpec((1, MLA_HEADS, MLA_V + ONES_ROWS, tm), lambda bi, i: (bi, 0, 0, i))],
        out_shape=[jax.ShapeDtypeStruct((b, MLA_HEADS, LANES, t), _MXU_DTYPE),
                   jax.ShapeDtypeStruct((b, MLA_HEADS, t, LANES), _MXU_DTYPE),
                   jax.ShapeDtypeStruct((b, MLA_HEADS, MLA_V + ONES_ROWS, t), _MXU_DTYPE)],
        compiler_params=_cparams("parallel", "parallel"),
        name="mla_prep",
    )(pm, cos, sin, gqa, gkva, wq, wkv, gq, gqr, gk, gkr)


def _tile_plan(nq, use_far):
    if use_far and nq % 2 == 0:
        return [lambda j: j, lambda j: nq - 1 - j]
    n = next(c for c in (4, 2, 1) if nq % c == 0) if not use_far else 1
    return [functools.partial(lambda j, s: n * j + s, s=s) for s in range(n)]


def _paired_position(i, nq):
    return jnp.where(i < nq // 2, 2 * i, 2 * (nq - 1 - i) + 1)


def _flash_kernel(*refs, use_far, sub, dv, extra, tiles, tq, near):
    ns = len(tiles)
    q_refs, refs = refs[:ns], refs[ns:]
    if extra:
        qx_refs, refs = refs[:ns], refs[ns:]
        k_ref, kx_ref, v_ref = refs[:3]
        refs = refs[3:]
    else:
        k_ref, v_ref = refs[:2]
        refs = refs[2:]
    tab_refs, o_ref = refs[:ns * near], refs[ns * near]
    step = pl.program_id(2)
    tc = tq

    def block(q, table, start, nkeys, m, acc):
        kb = k_ref[0, 0, pl.ds(start, nkeys), :]
        if extra:
            kb = jnp.concatenate([kb, kx_ref[pl.ds(start, nkeys), :]], axis=1)
        s = _dot(kb, q)
        if table is not None:
            s = s + table
        for j in range(nkeys // sub):
            sj = s[j * sub:(j + 1) * sub]
            vc = v_ref[0, 0, :, pl.ds(pl.multiple_of(start + j * sub, sub), sub)]
            m_new = jnp.maximum(m, jnp.max(sj, axis=0, keepdims=True))
            p = jnp.exp2(sj - m_new).astype(_MXU_DTYPE)
            acc = jnp.exp2(m - m_new) * acc + _dot(vc, p)
            m = m_new
        return m, acc

    state = []
    for slot in range(ns):
        i = tiles[slot](step)
        q = q_refs[slot][0, 0]
        if extra:
            q = jnp.concatenate([q, qx_refs[slot][0, 0]], axis=0)
        m = jnp.full((1, tq), -3e38, F32)
        acc = jnp.zeros((dv + ONES_ROWS, tq), F32)
        n_far = jnp.maximum(i - (near - 1), 0)
        if use_far:
            done = 0
            for width in (8, 4, 2, 1):
                count = (n_far - done) // width

                def body(c, carry, width=width, base=done, q=q):
                    start = pl.multiple_of((base + c * width) * tc, tc)
                    return block(q, None, start, width * tc, *carry)

                m, acc = lax.fori_loop(0, count, body, (m, acc))
                done = done + count * width
        state.append((q, n_far, m, acc))
    for slot, (q, n_far, m, acc) in enumerate(state):
        table = jnp.concatenate([tab_refs[slot * near + p][0, 0] for p in range(near)], axis=0)
        m, acc = block(q, table, pl.multiple_of(n_far * tc, tc), near * tc, m, acc)
        o_ref[0, 0, :, slot * tq:(slot + 1) * tq] = acc[0:dv] / acc[dv:dv + 1]


def _flash(qt, k, va, tab, *, use_far, name, qx=None, kx=None, sub=256):
    b, h, dk, t = qt.shape
    hk, dva = va.shape[1], va.shape[2]
    dv = dva - ONES_ROWS
    rep = h // hk
    ht, near, tq = tab.shape[0], tab.shape[1] - 1, tab.shape[3]
    extra = qx is not None
    tiles = _tile_plan(t // tq, use_far)
    ns = len(tiles)
    q_specs = [pl.BlockSpec((1, 1, dk, tq), lambda bi, hi, j, f=f: (bi, hi, 0, f(j))) for f in tiles]

    def piece(f, p):
        return lambda bi, hi, j: (hi % ht, jnp.minimum(p - jnp.minimum(f(j), near - 1) + near - 1, near),
                                  0, 0)

    tab_specs = [pl.BlockSpec((1, 1, tq, tq), piece(f, p)) for f in tiles for p in range(near)]
    k_spec = pl.BlockSpec((1, 1, t, dk), lambda bi, hi, j: (bi, hi // rep, 0, 0))
    v_spec = pl.BlockSpec((1, 1, dva, t), lambda bi, hi, j: (bi, hi // rep, 0, 0))
    if extra:
        nx = qx.shape[2]
        qx_specs = [pl.BlockSpec((1, 1, nx, tq), lambda bi, hi, j, f=f: (bi, hi // rep, 0, f(j)))
                    for f in tiles]
        in_specs = q_specs + qx_specs + [k_spec, _resident((t, nx)), v_spec] + tab_specs
        args = (qt,) * ns + (qx,) * ns + (k, kx, va) + (tab,) * (ns * near)
    else:
        in_specs = q_specs + [k_spec, v_spec] + tab_specs
        args = (qt,) * ns + (k, va) + (tab,) * (ns * near)
    return pl.pallas_call(
        functools.partial(_flash_kernel, use_far=use_far, sub=min(sub, tq), dv=dv, extra=extra,
                          tiles=tiles, tq=tq, near=near),
        grid=(b, h, t // (tq * ns)),
        in_specs=in_specs,
        out_specs=pl.BlockSpec((1, 1, dv, ns * tq), lambda bi, hi, j: (bi, hi, 0, j)),
        out_shape=jax.ShapeDtypeStruct((b, h, dv, t), F32),
        compiler_params=_cparams("parallel", "parallel", "parallel"),
        name=name,
    )(*args)


def _ret_kernel(q_ref, k_ref, v_ref, g_ref, qr_ref, kr_ref, cos_ref, sin_ref, din_ref, qd_ref,
                kd_ref, cdm_ref, bd_ref, gn_ref, o_ref, state_ref, *, nchunk):
    @pl.when(pl.program_id(1) == 0)
    def _():
        state_ref[...] = jnp.zeros(state_ref.shape, F32)

    w = RET_HEADS * RET_DK
    lane = lax.broadcasted_iota(jnp.int32, (1, w), 1)
    heads = [(lane >= h * RET_DK) & (lane < (h + 1) * RET_DK) for h in range(RET_HEADS)]
    on_diag = bd_ref[...] > 0.5
    c_ = RET_CHUNK
    for c in range(nchunk):
        sl = slice(c * c_, (c + 1) * c_)
        cos = cos_ref[sl, :]
        sin = sin_ref[sl, :]
        qh = q_ref[sl, :] * cos + qr_ref[sl, :] * sin
        kh = (k_ref[sl, :] * cos + kr_ref[sl, :] * sin) * (RET_DK ** -0.5)
        kb = kh.astype(_MXU_DTYPE)
        vb = v_ref[sl, :].astype(_MXU_DTYPE)
        st = state_ref[...]
        out = _dot((qh * qd_ref[...]).astype(_MXU_DTYPE), st.astype(_MXU_DTYPE))
        for h in range(RET_HEADS):
            qm = jnp.where(heads[h], qh, 0.0).astype(_MXU_DTYPE)
            inner = lax.dot_general(qm, kb, (((1,), (1,)), ((), ())),
                                    preferred_element_type=F32) * din_ref[h]
            out = out + jnp.where(heads[h], _dot(inner.astype(_MXU_DTYPE), vb), 0.0)
        kdt = (kh * kd_ref[...]).T
        state_ref[...] = st * cdm_ref[...] + jnp.where(on_diag, _dot(kdt.astype(_MXU_DTYPE), vb), 0.0)
        o2 = out * out
        ms = jnp.zeros_like(out)
        for h in range(RET_HEADS):
            ssum = jnp.sum(jnp.where(heads[h], o2, 0.0), axis=-1, keepdims=True)
            ms = jnp.where(heads[h], ssum * (1.0 / RET_DV), ms)
        y = out * lax.rsqrt(ms + EPS) * gn_ref[...]
        gg = g_ref[sl, :]
        o_ref[sl, :] = (gg * _sigmoid(gg) * y).astype(o_ref.dtype)


def _retention(pr, cos, sin, din, qd, kd, cdm, bd, gn, b, t, tt=256):
    w = RET_HEADS * RET_DK
    nt = t // tt
    col = lambda j: pl.BlockSpec((tt, w), lambda bi, i: (bi * nt + i, j))
    pos = pl.BlockSpec((tt, w), lambda bi, i: (i, 0))
    return pl.pallas_call(
        functools.partial(_ret_kernel, nchunk=tt // RET_CHUNK),
        grid=(b, nt),
        in_specs=[col(0), col(1), col(2), col(3), col(4), col(5), pos, pos,
                  _resident(din.shape), _resident(qd.shape), _resident(kd.shape),
                  _resident(cdm.shape), _resident(bd.shape), _resident((1, w))],
        out_specs=pl.BlockSpec((tt, w), lambda bi, i: (bi * nt + i, 0)),
        out_shape=jax.ShapeDtypeStruct((b * t, w), _MXU_DTYPE),
        scratch_shapes=[pltpu.VMEM((w, w), F32)],
        compiler_params=_cparams("parallel", "arbitrary"),
        name="retention",
    )(pr, pr, pr, pr, pr, pr, cos, sin, din, qd, kd, cdm, bd, gn)


NSA_Q0, NSA_KC0, NSA_VC0, NSA_KS0, NSA_VS0, NSA_KW0, NSA_VW0, NSA_GATE0 = (
    0, 384, 512, 640, 896, 1024, 1280, 1408)
NSA_SLAB = 1536
GATE_STRIDE = 8


def _nsa_prep_kernel(pn_ref, gq_ref, gks_ref, gkw_ref, qt_ref, ks_ref, kw_ref, vs_ref, vw_ref,
                     gt_ref):
    tm = pn_ref.shape[0]
    d = NSA_DH
    dt = qt_ref.dtype
    xq = pn_ref[:, NSA_Q0:NSA_Q0 + NSA_HEADS * d].T
    for h in range(NSA_HEADS):
        blk = xq[h * d:(h + 1) * d]
        r = lax.rsqrt(jnp.mean(blk * blk, axis=0, keepdims=True) + EPS)
        qt_ref[0, h, 0:d, :] = (blk * r * gq_ref[...]).astype(dt)
        qt_ref[0, h, d:2 * d, :] = jnp.zeros((d, tm), dt)
    for g in range(NSA_KV_HEADS):
        for c0, g_ref, o_ref in ((NSA_KS0, gks_ref, ks_ref), (NSA_KW0, gkw_ref, kw_ref)):
            slot = pn_ref[:, c0 + g * LANES:c0 + (g + 1) * LANES]
            r = lax.rsqrt(jnp.sum(slot * slot, axis=-1, keepdims=True) * (1.0 / d) + EPS)
            o_ref[0, g] = (slot * r * g_ref[...]).astype(dt)
    for c0, o_ref in ((NSA_VS0, vs_ref), (NSA_VW0, vw_ref)):
        vt = pn_ref[:, c0:c0 + LANES].T
        for g in range(NSA_KV_HEADS):
            o_ref[0, g, 0:d, :] = vt[g * d:(g + 1) * d].astype(dt)
            o_ref[0, g, d:d + ONES_ROWS, :] = _ones_rows(tm, dt)
    gt = pn_ref[:, NSA_GATE0:NSA_GATE0 + LANES].T
    gt_ref[0] = _sigmoid(gt[0:NSA_HEADS * GATE_STRIDE])


def _nsa_prep(pn, gq, gks, gkw, b, t, tm=512):
    nt = t // tm
    d = NSA_DH
    g = NSA_KV_HEADS
    md = _MXU_DTYPE
    ch_major = lambda n, r: pl.BlockSpec((1, n, r, tm), lambda bi, i: (bi, 0, 0, i))
    natural = pl.BlockSpec((1, g, tm, LANES), lambda bi, i: (bi, 0, i, 0))
    return pl.pallas_call(
        _nsa_prep_kernel,
        grid=(b, nt),
        in_specs=[pl.BlockSpec((tm, NSA_SLAB), lambda bi, i: (bi * nt + i, 0)),
                  _resident((d, tm)), _resident((1, LANES)), _resident((1, LANES))],
        out_specs=[ch_major(NSA_HEADS, 2 * d), natural, natural,
                   ch_major(g, d + ONES_ROWS), ch_major(g, d + ONES_ROWS),
                   pl.BlockSpec((1, NSA_HEADS * GATE_STRIDE, tm), lambda bi, i: (bi, 0, i))],
        out_shape=[jax.ShapeDtypeStruct((b, NSA_HEADS, 2 * d, t), md),
                   jax.ShapeDtypeStruct((b, g, t, LANES), md),
                   jax.ShapeDtypeStruct((b, g, t, LANES), md),
                   jax.ShapeDtypeStruct((b, g, d + ONES_ROWS, t), md),
                   jax.ShapeDtypeStruct((b, g, d + ONES_ROWS, t), md),
                   jax.ShapeDtypeStruct((b, NSA_HEADS * GATE_STRIDE, t), F32)],
        compiler_params=_cparams("parallel", "parallel"),
        name="nsa_prep",
    )(pn, jnp.broadcast_to(gq[:, None], (d, tm)), gks, gkw)


def _gelu_tanh(x):
    return 0.5 * x * (1.0 + jnp.tanh(math.sqrt(2.0 / math.pi) * (x + 0.044715 * (x * x * x))))


def _compress_kernel(ak_ref, av_ref, w1k_ref, w2k_ref, pk_ref, w1v_ref, w2v_ref, pv_ref, gk_ref,
                     kc_ref, vc_ref):
    half = CMP_STRIDE * NSA_DH

    def comp(a_ref, w1_ref, w2_ref, p_ref):
        a = a_ref[0]
        pb = _dot(p_ref[...], w1_ref[...])[0:1]
        second = _dot(a, w1_ref[half:2 * half, :])
        nc = second.shape[0]
        hid = _dot(a, w1_ref[0:half, :]) + pltpu.roll(second, nc - 1, 0) + pb
        return _dot(_gelu_tanh(hid).astype(_MXU_DTYPE), w2_ref[...])

    kc = comp(ak_ref, w1k_ref, w2k_ref, pk_ref)
    y = kc * lax.rsqrt(jnp.mean(kc * kc, axis=-1, keepdims=True) + EPS)
    kc_ref[0] = (y * gk_ref[...]).astype(kc_ref.dtype)
    vc_ref[0] = comp(av_ref, w1v_ref, w2v_ref, pv_ref).astype(vc_ref.dtype)


def _compress(ak, av, w1k, w2k, pk, w1v, w2v, pv, gk):
    n, nc, kk = ak.shape
    blk = pl.BlockSpec((1, nc, kk), lambda i: (i, 0, 0))
    out = pl.BlockSpec((1, nc, NSA_DH), lambda i: (i, 0, 0))
    w1 = _resident((2 * kk, CMP_HIDDEN))
    w2 = _resident((CMP_HIDDEN, NSA_DH))
    pp = _resident((8, 2 * kk))
    return pl.pallas_call(
        _compress_kernel,
        grid=(n,),
        in_specs=[blk, blk, w1, w2, pp, w1, w2, pp, _resident((1, NSA_DH))],
        out_specs=[out, out],
        out_shape=[jax.ShapeDtypeStruct((n, nc, NSA_DH), _MXU_DTYPE)] * 2,
        compiler_params=_cparams("parallel"),
        name="nsa_compress",
    )(ak, av, w1k, w2k, pk, w1v, w2v, pv, gk)


def _cmp_sel_kernel(q_ref, kc_ref, vca_ref, tab_ref, c2s_ref, bk_ref, oc_ref, sel_ref,
                    m_ref, acco_ref, acci_ref, *, tq, ns, d):
    i = pl.program_id(2)
    ch = CMP_CHUNK
    end = (tq // CMP_STRIDE) * (i + 1)
    n_far = (end - 1) // ch
    rowi = lax.broadcasted_iota(jnp.int32, (ch, NSA_GROUP * tq), 0)
    qpos = tq * i + lax.broadcasted_iota(jnp.int32, (1, tq), 1)
    sees_any = jnp.where(qpos >= CMP_BLOCK - 1, 1.0, 0.0)

    q = jnp.concatenate([q_ref[0, r] for r in range(NSA_GROUP)], axis=1)
    m_ref[...] = jnp.full(m_ref.shape, -3e38, F32)
    acco_ref[...] = jnp.zeros(acco_ref.shape, F32)
    acci_ref[...] = jnp.zeros(acci_ref.shape, F32)

    def chunk(g, with_table):
        start = pl.multiple_of(end - ch * g, CMP_STRIDE)
        kc = kc_ref[0, 0, pl.ds(start, ch), :]
        vt = vca_ref[0, 0, pl.ds(start, ch), :].T[0:d + ONES_ROWS].astype(_MXU_DTYPE)
        ct = c2s_ref[pl.ds(start, ch), :].T.astype(_MXU_DTYPE)
        s = _dot(kc, q)
        if with_table:
            s = s + tab_ref[0]
        s = jnp.where(rowi >= ch * (g + 1) - end, s, NEG_INF)
        m_old = m_ref[...]
        m_new = jnp.maximum(m_old, jnp.max(s, axis=0, keepdims=True))
        alpha = jnp.exp2(m_old - m_new)
        e = jnp.exp2(s - m_new).astype(_MXU_DTYPE)
        acco_ref[...] = alpha * acco_ref[...] + _dot(vt, e)
        acci_ref[...] = alpha * acci_ref[...] + _dot(ct, e)
        m_ref[...] = m_new

    chunk(0, True)

    def far(g, carry):
        chunk(g, False)
        return carry

    lax.fori_loop(1, n_far + 1, far, 0)

    imp = jnp.zeros((ns, tq), F32)
    for r in range(NSA_GROUP):
        lanes = slice(r * tq, (r + 1) * tq)
        inv = sees_any / acco_ref[d:d + 1, lanes]
        oc_ref[0, r] = acco_ref[0:d, lanes] * inv
        imp = imp + acci_ref[:, lanes] * inv

    back = bk_ref[...] + i * (tq // SEL_BLOCK)
    jidx = lax.broadcasted_iota(jnp.int32, (ns, tq), 0)
    forced = (jidx == 0) | ((back >= 0) & (back < SEL_LOCAL))
    jf = jidx.astype(F32)
    taken = -3e38

    def pick(imp, rounds):
        for _ in range(rounds):
            mx = jnp.max(imp, axis=0, keepdims=True)
            first = jnp.min(jnp.where(imp == mx, jf, 1e9), axis=0, keepdims=True)
            imp = jnp.where(jf == first, taken, imp)
        sel_ref[0, 0] = jnp.where(imp == taken, 0.0, NEG_INF).astype(sel_ref.dtype)

    top_k = min(SEL_TOPK, ns)
    n_forced = 1 + SEL_LOCAL

    @pl.when(i == 0)
    def _():
        pick(jnp.where(back >= 0, jnp.where(forced, FORCE, imp), NEG_INF), top_k)

    @pl.when(i > 0)
    def _():
        pick(jnp.where(forced, taken, jnp.where(back >= 0, imp, NEG_INF)), top_k - n_forced)


def _cmp_sel(qt, kc, vca, tab, c2s, bk, tq):
    b, h, _, t = qt.shape
    g, npad, d = kc.shape[1:]
    ns = c2s.shape[1]
    assert tq >= SEL_LOCAL * SEL_BLOCK and tq // CMP_STRIDE <= CMP_CHUNK
    return pl.pallas_call(
        functools.partial(_cmp_sel_kernel, tq=tq, ns=ns, d=d),
        grid=(b, g, t // tq),
        in_specs=[pl.BlockSpec((1, NSA_GROUP, d, tq), lambda bi, gi, i: (bi, gi, 0, i)),
                  pl.BlockSpec((1, 1, npad, d), lambda bi, gi, i: (bi, gi, 0, 0)),
                  pl.BlockSpec((1, 1, npad, LANES), lambda bi, gi, i: (bi, gi, 0, 0)),
                  pl.BlockSpec((1, CMP_CHUNK, NSA_GROUP * tq), lambda bi, gi, i: (gi, 0, 0)),
                  _resident(c2s.shape), _resident(bk.shape)],
        out_specs=[pl.BlockSpec((1, NSA_GROUP, d, tq), lambda bi, gi, i: (bi, gi, 0, i)),
                   pl.BlockSpec((1, 1, ns, tq), lambda bi, gi, i: (bi, gi, 0, i))],
        out_shape=[jax.ShapeDtypeStruct((b, h, d, t), F32),
                   jax.ShapeDtypeStruct((b, g, ns, t), _MXU_DTYPE)],
        scratch_shapes=[pltpu.VMEM((1, NSA_GROUP * tq), F32),
                        pltpu.VMEM((d + ONES_ROWS, NSA_GROUP * tq), F32),
                        pltpu.VMEM((ns, NSA_GROUP * tq), F32)],
        compiler_params=_cparams("parallel", "parallel", "parallel"),
        name="nsa_cmp_sel",
    )(qt, kc, vca, tab, c2s, bk)


def _out_proj_kernel(mla_ref, ret_ref, oc_ref, os_ref, ow_ref, g_ref, w_ref, r_ref, o_ref):
    d = NSA_DH
    w_mla, w_ret = MLA_HEADS * MLA_V, RET_HEADS * RET_DV
    acc = r_ref[...] + _dot(mla_ref[0].T.astype(_MXU_DTYPE), w_ref[0:w_mla, :])
    acc = acc + _dot(ret_ref[...], w_ref[w_mla:w_mla + w_ret, :])
    g = g_ref[0]
    nsa = []
    for h in range(NSA_HEADS):
        rows = slice(h * d, (h + 1) * d)
        g0 = h * GATE_STRIDE
        nsa.append(g[g0:g0 + 1] * oc_ref[0, rows, :] + g[g0 + 1:g0 + 2] * os_ref[0, rows, :]
                   + g[g0 + 2:g0 + 3] * ow_ref[0, rows, :])
    nsa = jnp.concatenate(nsa, axis=0).T.astype(_MXU_DTYPE)
    o_ref[...] = acc + _dot(nsa, w_ref[w_mla + w_ret:, :])


def _out_proj(o_mla, o_ret, oc, os_, ow, gates, w, res, t, tm=512):
    m, n = res.shape
    assert tm == ATT_TILE
    nt = t // tm
    ch_major = lambda c: pl.BlockSpec((1, c, tm), lambda i: (i // nt, 0, i % nt))
    far_order = len(_tile_plan(nt, True)) == 2
    ch_paired = (lambda c: pl.BlockSpec((1, c, tm), lambda i: (i // nt, 0, _paired_position(i % nt, nt)))
                 ) if far_order else ch_major
    tokens = lambda c: pl.BlockSpec((tm, c), lambda i: (i, 0))
    c_nsa = oc.shape[1]
    return pl.pallas_call(
        _out_proj_kernel,
        grid=(m // tm,),
        in_specs=[ch_paired(o_mla.shape[1]), tokens(o_ret.shape[1]), ch_major(c_nsa), ch_paired(c_nsa),
                  ch_major(c_nsa), ch_major(gates.shape[1]), _resident(w.shape), tokens(n)],
        out_specs=tokens(n),
        out_shape=jax.ShapeDtypeStruct((m, n), F32),
        compiler_params=_cparams("parallel"),
        name="out_proj",
    )(o_mla, o_ret, oc, os_, ow, gates, w, res)


def _ffn_kernel(x_ref, gn_ref, wup_ref, cw_ref, cb_ref, wdn_ref, o_ref, hbuf_ref, acc_ref, *,
                tm, fc):
    @pl.when(pl.program_id(1) == 0)
    def _():
        hbuf_ref[0:8, :] = jnp.zeros((8, hbuf_ref.shape[1]), F32)

    x = x_ref[...]
    y = x * lax.rsqrt(jnp.mean(x * x, axis=-1, keepdims=True) + EPS)
    xn = (y * gn_ref[...]).astype(_MXU_DTYPE)

    def up_proj(col0):
        cols = slice(col0, col0 + fc)
        hbuf_ref[8:tm + 8, cols] = _dot(xn, wup_ref[:, cols])

    def conv(col0):
        cols = slice(col0, col0 + fc)
        w = cw_ref[:, cols]
        h = hbuf_ref[8:tm + 8, cols]
        out = (h * w[2:3] + hbuf_ref[7:tm + 7, cols] * w[1:2] + hbuf_ref[6:tm + 6, cols] * w[0:1]
               + cb_ref[:, cols])
        hbuf_ref[0:8, cols] = hbuf_ref[tm:tm + 8, cols]
        return out

    nf = D_FF // fc
    ahead = 5
    for f in range(min(ahead, nf)):
        up_proj(f * fc)
        up_proj(D_FF + f * fc)
    for f in range(nf):
        if f + ahead < nf:
            up_proj((f + ahead) * fc)
            up_proj(D_FF + (f + ahead) * fc)
        gate = conv(f * fc)
        up = conv(D_FF + f * fc)
        act = (gate * _sigmoid(gate) * up).astype(_MXU_DTYPE)
        contrib = _dot(act, wdn_ref[f * fc:(f + 1) * fc, :])
        if f == 0:
            acc_ref[...] = contrib
        else:
            acc_ref[...] += contrib
    o_ref[...] = x + acc_ref[...]


def _ffn(x, gn, wup, cw, cb, wdn, b, t, tm=512, fc=256):
    d = x.shape[1]
    nt = t // tm
    return pl.pallas_call(
        functools.partial(_ffn_kernel, tm=tm, fc=fc),
        grid=(b, nt),
        in_specs=[pl.BlockSpec((tm, d), lambda bi, i: (bi * nt + i, 0)),
                  _resident((1, d)), _resident(wup.shape), _resident(cw.shape),
                  _resident(cb.shape), _resident(wdn.shape)],
        out_specs=pl.BlockSpec((tm, d), lambda bi, i: (bi * nt + i, 0)),
        out_shape=jax.ShapeDtypeStruct(x.shape, F32),
        scratch_shapes=[pltpu.VMEM((tm + 8, 2 * D_FF), F32), pltpu.VMEM((tm, d), F32)],
        compiler_params=_cparams("parallel", "arbitrary"),
        name="conv_ffn",
    )(x, gn, wup, cw, cb, wdn)


def _rope_tables(t, d):
    inv = ROPE_BASE ** (-np.arange(0, d, 2, dtype=np.float64) / d)
    ang = np.arange(t, dtype=np.float64)[:, None] * inv[None, :]
    return (np.concatenate([np.cos(ang)] * 2, axis=1), np.concatenate([np.sin(ang)] * 2, axis=1))


def _t5_bucket_np(dist):
    max_exact = REL_BUCKETS // 2
    d = np.maximum(dist, 1).astype(np.float64)
    log_b = max_exact + (np.log(d / max_exact) / math.log(REL_MAX_DIST / max_exact)
                         * (REL_BUCKETS - max_exact)).astype(np.int32)
    return np.where(dist < max_exact, dist, np.minimum(log_b, REL_BUCKETS - 1))


@functools.lru_cache(maxsize=None)
def _constants(t):
    c = {}
    cos, sin = _rope_tables(t, MLA_ROPE)
    pad = LANES - MLA_QK
    c["mla_cos"] = np.concatenate([np.ones((t, MLA_NOPE)), cos, np.ones((t, pad))], 1).astype(np.float32)
    c["mla_sin"] = np.concatenate([np.zeros((t, MLA_NOPE)), sin, np.zeros((t, pad))], 1).astype(np.float32)
    cos, sin = _rope_tables(t, RET_DK)
    c["ret_cos"] = np.tile(cos, (1, RET_HEADS)).astype(np.float32)
    c["ret_sin"] = np.tile(sin, (1, RET_HEADS)).astype(np.float32)
    lg = np.log(1.0 - 2.0 ** (-5.0 - np.arange(RET_HEADS, dtype=np.float64)))
    idx = np.arange(RET_CHUNK, dtype=np.float64)
    diff = idx[:, None] - idx[None, :]
    c["ret_din"] = (np.exp(np.maximum(diff, 0.0) * lg[:, None, None]) * (diff >= 0)).astype(np.float32)
    qd = np.exp((idx[:, None] + 1.0) * lg[None, :])
    kd = np.exp((RET_CHUNK - 1.0 - idx[:, None]) * lg[None, :])
    c["ret_qd"] = np.repeat(qd, RET_DK, axis=1).astype(np.float32)
    c["ret_kd"] = np.repeat(kd, RET_DK, axis=1).astype(np.float32)
    head_of = np.arange(RET_HEADS * RET_DK) // RET_DK
    c["ret_cdm"] = np.broadcast_to(np.exp(RET_CHUNK * lg)[head_of][:, None],
                                   (RET_HEADS * RET_DK, RET_HEADS * RET_DV)).astype(np.float32)
    c["ret_bd"] = (head_of[:, None] == head_of[None, :]).astype(np.float32)
    c["bucket"] = _t5_bucket_np(np.arange(LANES)).astype(np.int32)
    kk = np.arange(ATT_TILE)[:, None]
    qq = np.arange(ATT_TILE)[None, :]
    causal = np.where(qq >= kk, 0.0, NEG_INF)
    c["causal_tab"] = np.stack([np.zeros_like(causal), causal,
                                np.full_like(causal, NEG_INF)])[None].astype(np.float32)
    nc, ns = t // CMP_STRIDE, t // SEL_BLOCK
    n_cmp = (t - CMP_BLOCK) // CMP_STRIDE + 1
    c_start = np.arange(nc) * CMP_STRIDE
    s_start = np.arange(ns) * SEL_BLOCK
    overlap = np.clip(np.minimum(c_start[:, None] + CMP_BLOCK, s_start[None, :] + SEL_BLOCK)
                      - np.maximum(c_start[:, None], s_start[None, :]), 0, None).astype(np.float64)
    overlap[n_cmp:] = 0.0
    c["c2s"] = np.concatenate([np.zeros((CMP_CHUNK, ns)), overlap / CMP_BLOCK]).astype(np.float32)
    q = np.arange(CMP_TQ)[None, :]
    c["bk"] = (q // SEL_BLOCK - np.arange(ns)[:, None]).astype(np.int32)
    c["cmp_dist"] = (q - CMP_STRIDE * np.arange(CMP_CHUNK)[:, None]
                     + (CMP_STRIDE * CMP_CHUNK - CMP_TQ - (CMP_BLOCK - 1))).astype(np.int32)
    c["sel_onehot"] = (np.arange(t)[:, None] // SEL_BLOCK == np.arange(ns)[None, :]).astype(np.float32)
    return c


def _cols(w, pieces):
    out = []
    for p in pieces:
        if p[0] is None:
            out.append(jnp.zeros((w.shape[0], p[1]), w.dtype))
        else:
            blk = w[:, p[0]:p[1]]
            out.append(-blk if p[2] < 0 else blk)
    return jnp.concatenate(out, axis=1)


def _rot_pieces(base, d):
    return [(base + d // 2, base + d, -1), (base, base + d // 2, 1)]


def _skew(w, rows, step, col0, cols):
    hh, ll = w.shape
    flat = jnp.tile(w, (1, rows))[:, :rows * (ll - step)]
    return flat.reshape(hh, rows, ll - step)[:, :, col0:col0 + cols]


def _toeplitz(w, n):
    return _skew(w, n, 1, 0, n)


def _pad_to(v, n):
    return jnp.concatenate([v, jnp.zeros((n - v.shape[0],), v.dtype)])


def _layer(xf, b, t, cst, tabs, w_in, w_out, w1_k, w1_v, ffn_w_up, ffn_w_down,
           attn_norm, ffn_norm, mla_q_a_norm, mla_w_uq, mla_kv_a_norm, mla_w_ukv, mla_q_norm,
           mla_k_norm, ret_norm, pos_k, w2_k, pos_v, w2_v, nsa_q_norm, kn_cmp, kn_sel, kn_win,
           ffn_conv_w, ffn_conv_b):
    md = _MXU_DTYPE
    o = _IN_OFF
    pieces = [(o[0], o[1], 1), (None, 64), (o[1], o[2], 1),
              (None, 64), (o[2], o[3], 1), (None, 32),
              (None, 64)] + _rot_pieces(o[2], MLA_ROPE) + [(None, 32)]
    pieces += [(o[3], o[7], 1)]
    for base in (o[3], o[4]):
        for h in range(RET_HEADS):
            pieces += _rot_pieces(base + h * RET_DK, RET_DK)
    d_ = NSA_DH
    pieces += [(o[7], o[10], 1)]
    for base in (o[10], o[12]):
        pieces += [(base, base + d_, 1), (None, LANES - d_), (base + d_, base + 2 * d_, 1),
                   (None, LANES - d_), (base + 2 * d_, base + 4 * d_, 1)]
    for h in range(NSA_HEADS):
        pieces += [(o[14] + 3 * h, o[14] + 3 * h + 3, 1), (None, GATE_STRIDE - 3)]
    pieces += [(None, LANES - NSA_HEADS * GATE_STRIDE)]
    w_in_r = _cols(w_in, pieces)
    pm, pr, pn, pc = _in_proj(xf, attn_norm[None, :], w_in_r, (640, 1536, NSA_SLAB),
                              (2, NSA_KC0, 2 * NSA_KV_HEADS * NSA_DH))

    wq_pieces, wq_rot = [], []
    wkv_k, wkv_v = [], []
    for h in range(MLA_HEADS):
        qb = h * MLA_QK
        wq_pieces += [(qb, qb + MLA_QK, 1), (None, LANES - MLA_QK)]
        wq_rot += [(None, MLA_NOPE)] + _rot_pieces(qb + MLA_NOPE, MLA_ROPE) + [(None, LANES - MLA_QK)]
        kb = h * (MLA_NOPE + MLA_V)
        wkv_k += [(kb, kb + MLA_NOPE, 1), (None, LANES - MLA_NOPE)]
        wkv_v += [(kb + MLA_NOPE, kb + MLA_NOPE + MLA_V, 1), (None, LANES - MLA_V)]
    wq = _cols(mla_w_uq, wq_pieces + wq_rot)
    wq = jnp.concatenate([wq, jnp.zeros((256 - MLA_Q_RANK, wq.shape[1]), wq.dtype)], axis=0).astype(md)
    wkv = _cols(mla_w_ukv, wkv_k + wkv_v).astype(md)
    half = MLA_ROPE // 2

    def rot_gain(gv):
        return jnp.concatenate([jnp.zeros((MLA_NOPE,), gv.dtype), gv[MLA_NOPE + half:],
                                gv[MLA_NOPE:MLA_NOPE + half], jnp.zeros((LANES - MLA_QK,), gv.dtype)])

    q, k, v = _mla_prep(pm, cst["mla_cos"], cst["mla_sin"], _pad_to(mla_q_a_norm, 256)[None, :],
                        mla_kv_a_norm[None, :], wq, wkv,
                        _pad_to(mla_q_norm, LANES)[None, :], rot_gain(mla_q_norm)[None, :],
                        _pad_to(mla_k_norm, LANES)[None, :], rot_gain(mla_k_norm)[None, :], b, t)
    o_mla = _flash(q, k, v, tabs["causal"], use_far=True, name="mla_attention")
    o_mla = o_mla.reshape(b, MLA_HEADS * MLA_V, t)

    o_ret = _retention(pr, cst["ret_cos"], cst["ret_sin"], cst["ret_din"], cst["ret_qd"],
                       cst["ret_kd"], cst["ret_cdm"], cst["ret_bd"],
                       jnp.tile(ret_norm, RET_HEADS)[None, :], b, t)

    g_ = NSA_KV_HEADS
    nc = t // CMP_STRIDE
    qt, ks_n, kw_n, vs_a, vw_a, gates = _nsa_prep(
        pn, nsa_q_norm * (d_ ** -0.5 * LOG2E), _pad_to(kn_sel, LANES)[None, :],
        _pad_to(kn_win, LANES)[None, :], b, t)

    def chunks(c0):
        a = pc[:, c0:c0 + g_ * d_].reshape(b, t, g_, d_).transpose(0, 2, 1, 3)
        return a.reshape(b * g_, nc, CMP_STRIDE * d_)

    pos8 = lambda p: jnp.broadcast_to(p.reshape(1, -1), (8, CMP_BLOCK * d_)).astype(md)
    k_c, v_c = _compress(chunks(0), chunks(g_ * d_), w1_k, w2_k.astype(md), pos8(pos_k),
                         w1_v, w2_v.astype(md), pos8(pos_v), kn_cmp[None, :])
    front = ((0, 0), (0, 0), (CMP_CHUNK, 0), (0, 0))
    k_c = jnp.pad(k_c.reshape(b, g_, nc, d_), front)
    v_c = v_c.reshape(b, g_, nc, d_).astype(F32)
    v_ca = jnp.pad(jnp.concatenate([v_c, jnp.ones_like(v_c[..., :1]),
                                    jnp.zeros_like(v_c[..., :LANES - d_ - 1])], axis=-1), front)
    oc_t, selneg = _cmp_sel(qt, k_c, v_ca, tabs["cmp"], cst["c2s"], cst["bk"], CMP_TQ)
    os_t = _flash(qt, ks_n, vs_a, tabs["sel"], use_far=True, name="nsa_selected",
                  qx=selneg, kx=cst["sel_onehot"].astype(md))
    ow_t = _flash(qt, kw_n, vw_a, tabs["win"], use_far=False, name="nsa_window")

    flat = lambda a: a.reshape(b, NSA_HEADS * d_, t)
    xf = _out_proj(o_mla, o_ret, flat(oc_t), flat(os_t), flat(ow_t), gates, w_out, xf, t)
    return _ffn(xf, ffn_norm[None, :], ffn_w_up, ffn_conv_w, ffn_conv_b[None, :], ffn_w_down, b, t)


def _bias_tables(rel_bias, cst):
    n = ATT_TILE
    lut = rel_bias[cst["bucket"]].T
    delta = (lut - rel_bias[REL_BUCKETS - 1][:, None]) * LOG2E
    hh = delta.shape[0]
    dn = jnp.concatenate([delta, jnp.zeros((hh, n - LANES), F32)], axis=1)
    neg = jnp.full((hh, n), NEG_INF, F32)
    zero = jnp.zeros((hh, n), F32)
    diag = _toeplitz(jnp.concatenate([dn, neg], axis=1), n)
    prev_sel = _toeplitz(jnp.concatenate([zero, dn], axis=1), n)
    off = CMP_STRIDE * CMP_CHUNK - CMP_TQ - (CMP_BLOCK - 1)
    d_max = off + CMP_TQ - 1
    d_min = off - CMP_STRIDE * (CMP_CHUNK - 1)
    by_dist = jnp.concatenate([delta, jnp.zeros((hh, d_max + 1 - LANES), F32),
                               jnp.full((hh, -d_min), NEG_INF, F32)], axis=1)
    cmp_tab = _skew(by_dist, CMP_CHUNK, CMP_STRIDE, off, CMP_TQ)
    dist = cst["cmp_dist"]
    masked = jnp.full((hh, n, n), NEG_INF, F32)
    nw = WIN_TILE
    back = WINDOW // nw
    dnw, negw, zerow = dn[:, :nw], neg[:, :nw], zero[:, :nw]
    win = [_toeplitz(jnp.concatenate([negw if r == back else zerow, dnw if r == 1 else zerow], axis=1), nw)
           for r in range(back, 0, -1)]
    win += [_toeplitz(jnp.concatenate([dnw, negw], axis=1), nw), masked[:, :nw, :nw]]
    cmp_tab = cmp_tab.reshape(NSA_KV_HEADS, NSA_GROUP, *dist.shape).transpose(0, 2, 1, 3).reshape(
        NSA_KV_HEADS, dist.shape[0], NSA_GROUP * dist.shape[1])
    return {"sel": jnp.stack([prev_sel, diag, masked], axis=1), "win": jnp.stack(win, axis=1),
            "cmp": cmp_tab, "causal": jnp.asarray(cst["causal_tab"])}


def kernel(x, w_in, w_out, attn_norm, ffn_norm, mla_q_a_norm, mla_w_uq, mla_kv_a_norm, mla_w_ukv, mla_q_norm, mla_k_norm, ret_norm, nsa_cmp_pos_k, nsa_cmp_w1_k, nsa_cmp_w2_k, nsa_cmp_pos_v, nsa_cmp_w1_v, nsa_cmp_w2_v, nsa_q_norm, nsa_k_norm_cmp, nsa_k_norm_sel, nsa_k_norm_win, rel_bias, ffn_w_up, ffn_conv_w, ffn_conv_b, ffn_w_down):
    b, t, d = x.shape
    assert d == D_MODEL and t % (2 * ATT_TILE) == 0 and WINDOW % WIN_TILE == 0 and WIN_TILE >= LANES
    cst = _constants(t)
    tabs = _bias_tables(rel_bias, cst)
    stacked = (w_in, w_out, nsa_cmp_w1_k, nsa_cmp_w1_v, ffn_w_up, ffn_w_down)
    per_layer = (attn_norm, ffn_norm, mla_q_a_norm, mla_w_uq, mla_kv_a_norm, mla_w_ukv,
                 mla_q_norm, mla_k_norm, ret_norm, nsa_cmp_pos_k, nsa_cmp_w2_k,
                 nsa_cmp_pos_v, nsa_cmp_w2_v, nsa_q_norm, nsa_k_norm_cmp,
                 nsa_k_norm_sel, nsa_k_norm_win, ffn_conv_w, ffn_conv_b)
    xf = x.reshape(b * t, d)
    for l in range(w_in.shape[0]):
        big = [_to_mxu_dtype(w, l) for w in stacked]
        xf = _layer(xf, b, t, cst, tabs, *big, *[p[l] for p in per_layer])
    return xf.reshape(b, t, d)
```

```python
import functools
import math

import numpy as np
import jax
import jax.numpy as jnp
from jax import lax
from jax.experimental import pallas as pl
from jax.experimental.pallas import tpu as pltpu

D_MODEL = 1024
DEPTH = 2
MLA_HEADS = 6
MLA_Q_RANK = 192
MLA_KV_RANK = 128
MLA_NOPE = 64
MLA_ROPE = 32
MLA_V = 64
MLA_QK = MLA_NOPE + MLA_ROPE
RET_HEADS = 4
RET_DK = 64
RET_DV = 64
RET_CHUNK = 128
NSA_HEADS = 6
NSA_KV_HEADS = 2
NSA_GROUP = NSA_HEADS // NSA_KV_HEADS
NSA_DH = 64
CMP_BLOCK = 32
CMP_STRIDE = 16
CMP_HIDDEN = 256
SEL_BLOCK = 64
SEL_TOPK = 16
SEL_LOCAL = 2
WINDOW = 512
REL_BUCKETS = 32
REL_MAX_DIST = 128
D_FF = 2816
ROPE_BASE = 10000.0
EPS = 1e-6
NEG_INF = -1e30
FORCE = 1e9

_IN_SPLITS = (MLA_Q_RANK, MLA_KV_RANK, MLA_ROPE,
              RET_HEADS * RET_DK, RET_HEADS * RET_DK, RET_HEADS * RET_DV, RET_HEADS * RET_DV,
              NSA_HEADS * NSA_DH) + (NSA_KV_HEADS * NSA_DH,) * 6 + (3 * NSA_HEADS,)
_IN_OFF = [0] + [int(v) for v in np.cumsum(_IN_SPLITS)]
D_IN = _IN_OFF[-1]

LANES = 128
ATT_TILE = 512
CMP_TQ = 512
CMP_CHUNK = 128
WIN_TILE = 512
VMEM_LIMIT = 56 * 1024 * 1024

_MXU_DTYPE = jnp.bfloat16
F32 = jnp.float32
LOG2E = math.log2(math.e)


def _cparams(*sem):
    return pltpu.CompilerParams(dimension_semantics=sem, vmem_limit_bytes=VMEM_LIMIT)


def _dot(a, b):
    return jnp.dot(a, b, preferred_element_type=F32)


def _sigmoid(x):
    return 1.0 / (1.0 + jnp.exp(-x))


def _cast_kernel(x_ref, o_ref):
    o_ref[...] = x_ref[0].astype(o_ref.dtype)


def _to_mxu_dtype(w, layer, rows=256):
    _, r, c = w.shape
    return pl.pallas_call(
        _cast_kernel,
        grid=(r // rows,),
        in_specs=[pl.BlockSpec((1, rows, c), lambda i: (layer, i, 0))],
        out_specs=pl.BlockSpec((rows, c), lambda i: (i, 0)),
        out_shape=jax.ShapeDtypeStruct((r, c), _MXU_DTYPE),
        compiler_params=_cparams("parallel"),
        name="weight_cast",
    )(w)


ONES_ROWS = 16


def _ones_rows(n, dtype):
    row = lax.broadcasted_iota(jnp.int32, (ONES_ROWS, n), 0)
    return jnp.where(row == 0, 1.0, 0.0).astype(dtype)


def _with_ones_rows(vt):
    b, h, _, t = vt.shape
    extra = jnp.concatenate([jnp.ones((b, h, 1, t), vt.dtype),
                             jnp.zeros((b, h, ONES_ROWS - 1, t), vt.dtype)], axis=2)
    return jnp.concatenate([vt, extra], axis=2)


def _resident(shape):
    nd = len(shape)
    return pl.BlockSpec(shape, lambda *_: (0,) * nd, pipeline_mode=pl.Buffered(1))


def _in_proj_kernel(x_ref, g_ref, w_ref, *o_refs, widths, copy):
    x = x_ref[...]
    y = x * lax.rsqrt(jnp.mean(x * x, axis=-1, keepdims=True) + EPS)
    xn = (y * g_ref[...]).astype(_MXU_DTYPE)
    slab, c0, cw = copy
    off = 0
    for j, (o_ref, wd) in enumerate(zip(o_refs, widths)):
        res = _dot(xn, w_ref[:, off:off + wd])
        o_ref[...] = res
        if j == slab:
            o_refs[-1][...] = res[:, c0:c0 + cw].astype(o_refs[-1].dtype)
        off += wd


def _in_proj(x, g, w, widths, copy, tm=256):
    m, d = x.shape
    n = w.shape[1]
    return pl.pallas_call(
        functools.partial(_in_proj_kernel, widths=widths, copy=copy),
        grid=(m // tm,),
        in_specs=[pl.BlockSpec((tm, d), lambda i: (i, 0)),
                  _resident((1, d)),
                  _resident((d, n))],
        out_specs=[pl.BlockSpec((tm, wd), lambda i: (i, 0)) for wd in widths + (copy[2],)],
        out_shape=[jax.ShapeDtypeStruct((m, wd), F32) for wd in widths]
        + [jax.ShapeDtypeStruct((m, copy[2]), _MXU_DTYPE)],
        compiler_params=_cparams("parallel"),
        name="in_proj",
    )(x, g, w)


def _mla_prep_kernel(pm_ref, cos_ref, sin_ref, gqa_ref, gkva_ref, wq_ref, wkv_ref,
                     gq_ref, gqr_ref, gk_ref, gkr_ref, q_ref, k_ref, v_ref):
    pm = pm_ref[...]
    hs = MLA_HEADS * LANES
    cq = pm[:, 0:256]
    r = lax.rsqrt(jnp.sum(cq * cq, axis=-1, keepdims=True) * (1.0 / MLA_Q_RANK) + EPS)
    qq = _dot((cq * r * gqa_ref[...]).astype(_MXU_DTYPE), wq_ref[...])
    ckv = pm[:, 256:384]
    r = lax.rsqrt(jnp.mean(ckv * ckv, axis=-1, keepdims=True) + EPS)
    kv = _dot((ckv * r * gkva_ref[...]).astype(_MXU_DTYPE), wkv_ref[...])
    kpe = pm[:, 384:512]
    kpe_rot = pm[:, 512:640]
    cos = cos_ref[...]
    sin = sin_ref[...]
    scale = MLA_QK ** -0.5 * LOG2E
    aq = cos * gq_ref[...] * scale
    bq = sin * gqr_ref[...] * scale
    ak = cos * gk_ref[...]
    bk = sin * gkr_ref[...]
    for h in range(MLA_HEADS):
        sl = slice(h * LANES, (h + 1) * LANES)
        sr = slice(hs + h * LANES, hs + (h + 1) * LANES)
        qh = qq[:, sl]
        rq = lax.rsqrt(jnp.sum(qh * qh, axis=-1, keepdims=True) * (1.0 / MLA_QK) + EPS)
        q_ref[0, h] = ((qh * aq + qq[:, sr] * bq) * rq).T.astype(q_ref.dtype)
        kh = kv[:, sl] + kpe
        rk = lax.rsqrt(jnp.sum(kh * kh, axis=-1, keepdims=True) * (1.0 / MLA_QK) + EPS)
        k_ref[0, h] = ((kh * ak + kpe_rot * bk) * rk).astype(k_ref.dtype)
        v_ref[0, h, 0:MLA_V, :] = kv[:, sr].T[0:MLA_V].astype(v_ref.dtype)
        v_ref[0, h, MLA_V:MLA_V + ONES_ROWS, :] = _ones_rows(pm.shape[0], v_ref.dtype)


def _mla_prep(pm, cos, sin, gqa, gkva, wq, wkv, gq, gqr, gk, gkr, b, t, tm=512):
    nt = t // tm
    hs = MLA_HEADS * LANES
    vec = lambda n: _resident((1, n))
    return pl.pallas_call(
        _mla_prep_kernel,
        grid=(b, nt),
        in_specs=[pl.BlockSpec((tm, 640), lambda bi, i: (bi * nt + i, 0)),
                  pl.BlockSpec((tm, LANES), lambda bi, i: (i, 0)),
                  pl.BlockSpec((tm, LANES), lambda bi, i: (i, 0)),
                  vec(256), vec(LANES), _resident((256, 2 * hs)), _resident((LANES, 2 * hs)),
                  vec(LANES), vec(LANES), vec(LANES), vec(LANES)],
        out_specs=[pl.BlockSpec((1, MLA_HEADS, LANES, tm), lambda bi, i: (bi, 0, 0, i)),
                   pl.BlockSpec((1, MLA_HEADS, tm, LANES), lambda bi, i: (bi, 0, i, 0)),
                   pl.BlockSpec((1, MLA_HEADS, MLA_V + ONES_ROWS, tm), lambda bi, i: (bi, 0, 0, i))],
        out_shape=[jax.ShapeDtypeStruct((b, MLA_HEADS, LANES, t), _MXU_DTYPE),
                   jax.ShapeDtypeStruct((b, MLA_HEADS, t, LANES), _MXU_DTYPE),
                   jax.ShapeDtypeStruct((b, MLA_HEADS, MLA_V + ONES_ROWS, t), _MXU_DTYPE)],
        compiler_params=_cparams("parallel", "parallel"),
        name="mla_prep",
    )(pm, cos, sin, gqa, gkva, wq, wkv, gq, gqr, gk, gkr)


def _tile_plan(nq, use_far):
    if use_far and nq % 2 == 0:
        return [lambda j: j, lambda j: nq - 1 - j]
    n = next(c for c in (4, 2, 1) if nq % c == 0) if not use_far else 1
    return [functools.partial(lambda j, s: n * j + s, s=s) for s in range(n)]


def _paired_position(i, nq):
    return jnp.where(i < nq // 2, 2 * i, 2 * (nq - 1 - i) + 1)


def _flash_kernel(*refs, use_far, sub, dv, extra, tiles, tq, near):
    ns = len(tiles)
    q_refs, refs = refs[:ns], refs[ns:]
    if extra:
        qx_refs, refs = refs[:ns], refs[ns:]
        k_ref, kx_ref, v_ref = refs[:3]
        refs = refs[3:]
    else:
        k_ref, v_ref = refs[:2]
        refs = refs[2:]
    tab_refs, o_ref = refs[:ns * near], refs[ns * near]
    step = pl.program_id(2)
    tc = tq

    def block(q, table, start, nkeys, m, acc):
        kb = k_ref[0, 0, pl.ds(start, nkeys), :]
        if extra:
            kb = jnp.concatenate([kb, kx_ref[pl.ds(start, nkeys), :]], axis=1)
        s = _dot(kb, q)
        if table is not None:
            s = s + table
        for j in range(nkeys // sub):
            sj = s[j * sub:(j + 1) * sub]
            vc = v_ref[0, 0, :, pl.ds(pl.multiple_of(start + j * sub, sub), sub)]
            m_new = jnp.maximum(m, jnp.max(sj, axis=0, keepdims=True))
            p = jnp.exp2(sj - m_new).astype(_MXU_DTYPE)
            acc = jnp.exp2(m - m_new) * acc + _dot(vc, p)
            m = m_new
        return m, acc

    state = []
    for slot in range(ns):
        i = tiles[slot](step)
        q = q_refs[slot][0, 0]
        if extra:
            q = jnp.concatenate([q, qx_refs[slot][0, 0]], axis=0)
        m = jnp.full((1, tq), -3e38, F32)
        acc = jnp.zeros((dv + ONES_ROWS, tq), F32)
        n_far = jnp.maximum(i - (near - 1), 0)
        if use_far:
            done = 0
            for width in (8, 4, 2, 1):
                count = (n_far - done) // width

                def body(c, carry, width=width, base=done, q=q):
                    start = pl.multiple_of((base + c * width) * tc, tc)
                    return block(q, None, start, width * tc, *carry)

                m, acc = lax.fori_loop(0, count, body, (m, acc))
                done = done + count * width
        state.append((q, n_far, m, acc))
    for slot, (q, n_far, m, acc) in enumerate(state):
        table = jnp.concatenate([tab_refs[slot * near + p][0, 0] for p in range(near)], axis=0)
        m, acc = block(q, table, pl.multiple_of(n_far * tc, tc), near * tc, m, acc)
        o_ref[0, 0, :, slot * tq:(slot + 1) * tq] = acc[0:dv] / acc[dv:dv + 1]


def _flash(qt, k, va, tab, *, use_far, name, qx=None, kx=None, sub=256):
    b, h, dk, t = qt.shape
    hk, dva = va.shape[1], va.shape[2]
    dv = dva - ONES_ROWS
    rep = h // hk
    ht, near, tq = tab.shape[0], tab.shape[1] - 1, tab.shape[3]
    extra = qx is not None
    tiles = _tile_plan(t // tq, use_far)
    ns = len(tiles)
    q_specs = [pl.BlockSpec((1, 1, dk, tq), lambda bi, hi, j, f=f: (bi, hi, 0, f(j))) for f in tiles]

    def piece(f, p):
        return lambda bi, hi, j: (hi % ht, jnp.minimum(p - jnp.minimum(f(j), near - 1) + near - 1, near),
                                  0, 0)

    tab_specs = [pl.BlockSpec((1, 1, tq, tq), piece(f, p)) for f in tiles for p in range(near)]
    k_spec = pl.BlockSpec((1, 1, t, dk), lambda bi, hi, j: (bi, hi // rep, 0, 0))
    v_spec = pl.BlockSpec((1, 1, dva, t), lambda bi, hi, j: (bi, hi // rep, 0, 0))
    if extra:
        nx = qx.shape[2]
        qx_specs = [pl.BlockSpec((1, 1, nx, tq), lambda bi, hi, j, f=f: (bi, hi // rep, 0, f(j)))
                    for f in tiles]
        in_specs = q_specs + qx_specs + [k_spec, _resident((t, nx)), v_spec] + tab_specs
        args = (qt,) * ns + (qx,) * ns + (k, kx, va) + (tab,) * (ns * near)
    else:
        in_specs = q_specs + [k_spec, v_spec] + tab_specs
        args = (qt,) * ns + (k, va) + (tab,) * (ns * near)
    return pl.pallas_call(
        functools.partial(_flash_kernel, use_far=use_far, sub=min(sub, tq), dv=dv, extra=extra,
                          tiles=tiles, tq=tq, near=near),
        grid=(b, h, t // (tq * ns)),
        in_specs=in_specs,
        out_specs=pl.BlockSpec((1, 1, dv, ns * tq), lambda bi, hi, j: (bi, hi, 0, j)),
        out_shape=jax.ShapeDtypeStruct((b, h, dv, t), F32),
        compiler_params=_cparams("parallel", "parallel", "parallel"),
        name=name,
    )(*args)


def _ret_kernel(q_ref, k_ref, v_ref, g_ref, qr_ref, kr_ref, cos_ref, sin_ref, din_ref, qd_ref,
                kd_ref, cdm_ref, bd_ref, gn_ref, o_ref, state_ref, *, nchunk):
    @pl.when(pl.program_id(1) == 0)
    def _():
        state_ref[...] = jnp.zeros(state_ref.shape, F32)

    w = RET_HEADS * RET_DK
    lane = lax.broadcasted_iota(jnp.int32, (1, w), 1)
    heads = [(lane >= h * RET_DK) & (lane < (h + 1) * RET_DK) for h in range(RET_HEADS)]
    on_diag = bd_ref[...] > 0.5
    c_ = RET_CHUNK
    for c in range(nchunk):
        sl = slice(c * c_, (c + 1) * c_)
        cos = cos_ref[sl, :]
        sin = sin_ref[sl, :]
        qh = q_ref[sl, :] * cos + qr_ref[sl, :] * sin
        kh = (k_ref[sl, :] * cos + kr_ref[sl, :] * sin) * (RET_DK ** -0.5)
        kb = kh.astype(_MXU_DTYPE)
        vb = v_ref[sl, :].astype(_MXU_DTYPE)
        st = state_ref[...]
        out = _dot((qh * qd_ref[...]).astype(_MXU_DTYPE), st.astype(_MXU_DTYPE))
        for h in range(RET_HEADS):
            qm = jnp.where(heads[h], qh, 0.0).astype(_MXU_DTYPE)
            inner = lax.dot_general(qm, kb, (((1,), (1,)), ((), ())),
                                    preferred_element_type=F32) * din_ref[h]
            out = out + jnp.where(heads[h], _dot(inner.astype(_MXU_DTYPE), vb), 0.0)
        kdt = (kh * kd_ref[...]).T
        state_ref[...] = st * cdm_ref[...] + jnp.where(on_diag, _dot(kdt.astype(_MXU_DTYPE), vb), 0.0)
        o2 = out * out
        ms = jnp.zeros_like(out)
        for h in range(RET_HEADS):
            ssum = jnp.sum(jnp.where(heads[h], o2, 0.0), axis=-1, keepdims=True)
            ms = jnp.where(heads[h], ssum * (1.0 / RET_DV), ms)
        y = out * lax.rsqrt(ms + EPS) * gn_ref[...]
        gg = g_ref[sl, :]
        o_ref[sl, :] = (gg * _sigmoid(gg) * y).astype(o_ref.dtype)


def _retention(pr, cos, sin, din, qd, kd, cdm, bd, gn, b, t, tt=256):
    w = RET_HEADS * RET_DK
    nt = t // tt
    col = lambda j: pl.BlockSpec((tt, w), lambda bi, i: (bi * nt + i, j))
    pos = pl.BlockSpec((tt, w), lambda bi, i: (i, 0))
    return pl.pallas_call(
        functools.partial(_ret_kernel, nchunk=tt // RET_CHUNK),
        grid=(b, nt),
        in_specs=[col(0), col(1), col(2), col(3), col(4), col(5), pos, pos,
                  _resident(din.shape), _resident(qd.shape), _resident(kd.shape),
                  _resident(cdm.shape), _resident(bd.shape), _resident((1, w))],
        out_specs=pl.BlockSpec((tt, w), lambda bi, i: (bi * nt + i, 0)),
        out_shape=jax.ShapeDtypeStruct((b * t, w), _MXU_DTYPE),
        scratch_shapes=[pltpu.VMEM((w, w), F32)],
        compiler_params=_cparams("parallel", "arbitrary"),
        name="retention",
    )(pr, pr, pr, pr, pr, pr, cos, sin, din, qd, kd, cdm, bd, gn)


NSA_Q0, NSA_KC0, NSA_VC0, NSA_KS0, NSA_VS0, NSA_KW0, NSA_VW0, NSA_GATE0 = (
    0, 384, 512, 640, 896, 1024, 1280, 1408)
NSA_SLAB = 1536
GATE_STRIDE = 8


def _nsa_prep_kernel(pn_ref, gq_ref, gks_ref, gkw_ref, qt_ref, ks_ref, kw_ref, vs_ref, vw_ref,
                     gt_ref):
    tm = pn_ref.shape[0]
    d = NSA_DH
    dt = qt_ref.dtype
    xq = pn_ref[:, NSA_Q0:NSA_Q0 + NSA_HEADS * d].T
    for h in range(NSA_HEADS):
        blk = xq[h * d:(h + 1) * d]
        r = lax.rsqrt(jnp.mean(blk * blk, axis=0, keepdims=True) + EPS)
        qt_ref[0, h, 0:d, :] = (blk * r * gq_ref[...]).astype(dt)
        qt_ref[0, h, d:2 * d, :] = jnp.zeros((d, tm), dt)
    for g in range(NSA_KV_HEADS):
        for c0, g_ref, o_ref in ((NSA_KS0, gks_ref, ks_ref), (NSA_KW0, gkw_ref, kw_ref)):
            slot = pn_ref[:, c0 + g * LANES:c0 + (g + 1) * LANES]
            r = lax.rsqrt(jnp.sum(slot * slot, axis=-1, keepdims=True) * (1.0 / d) + EPS)
            o_ref[0, g] = (slot * r * g_ref[...]).astype(dt)
    for c0, o_ref in ((NSA_VS0, vs_ref), (NSA_VW0, vw_ref)):
        vt = pn_ref[:, c0:c0 + LANES].T
        for g in range(NSA_KV_HEADS):
            o_ref[0, g, 0:d, :] = vt[g * d:(g + 1) * d].astype(dt)
            o_ref[0, g, d:d + ONES_ROWS, :] = _ones_rows(tm, dt)
    gt = pn_ref[:, NSA_GATE0:NSA_GATE0 + LANES].T
    gt_ref[0] = _sigmoid(gt[0:NSA_HEADS * GATE_STRIDE])


def _nsa_prep(pn, gq, gks, gkw, b, t, tm=512):
    nt = t // tm
    d = NSA_DH
    g = NSA_KV_HEADS
    md = _MXU_DTYPE
    ch_major = lambda n, r: pl.BlockSpec((1, n, r, tm), lambda bi, i: (bi, 0, 0, i))
    natural = pl.BlockSpec((1, g, tm, LANES), lambda bi, i: (bi, 0, i, 0))
    return pl.pallas_call(
        _nsa_prep_kernel,
        grid=(b, nt),
        in_specs=[pl.BlockSpec((tm, NSA_SLAB), lambda bi, i: (bi * nt + i, 0)),
                  _resident((d, tm)), _resident((1, LANES)), _resident((1, LANES))],
        out_specs=[ch_major(NSA_HEADS, 2 * d), natural, natural,
                   ch_major(g, d + ONES_ROWS), ch_major(g, d + ONES_ROWS),
                   pl.BlockSpec((1, NSA_HEADS * GATE_STRIDE, tm), lambda bi, i: (bi, 0, i))],
        out_shape=[jax.ShapeDtypeStruct((b, NSA_HEADS, 2 * d, t), md),
                   jax.ShapeDtypeStruct((b, g, t, LANES), md),
                   jax.ShapeDtypeStruct((b, g, t, LANES), md),
                   jax.ShapeDtypeStruct((b, g, d + ONES_ROWS, t), md),
                   jax.ShapeDtypeStruct((b, g, d + ONES_ROWS, t), md),
                   jax.ShapeDtypeStruct((b, NSA_HEADS * GATE_STRIDE, t), F32)],
        compiler_params=_cparams("parallel", "parallel"),
        name="nsa_prep",
    )(pn, jnp.broadcast_to(gq[:, None], (d, tm)), gks, gkw)


def _gelu_tanh(x):
    return 0.5 * x * (1.0 + jnp.tanh(math.sqrt(2.0 / math.pi) * (x + 0.044715 * (x * x * x))))


def _compress_kernel(ak_ref, av_ref, w1k_ref, w2k_ref, pk_ref, w1v_ref, w2v_ref, pv_ref, gk_ref,
                     kc_ref, vc_ref):
    half = CMP_STRIDE * NSA_DH

    def comp(a_ref, w1_ref, w2_ref, p_ref):
        a = a_ref[0]
        pb = _dot(p_ref[...], w1_ref[...])[0:1]
        second = _dot(a, w1_ref[half:2 * half, :])
        nc = second.shape[0]
        hid = _dot(a, w1_ref[0:half, :]) + pltpu.roll(second, nc - 1, 0) + pb
        return _dot(_gelu_tanh(hid).astype(_MXU_DTYPE), w2_ref[...])

    kc = comp(ak_ref, w1k_ref, w2k_ref, pk_ref)
    y = kc * lax.rsqrt(jnp.mean(kc * kc, axis=-1, keepdims=True) + EPS)
    kc_ref[0] = (y * gk_ref[...]).astype(kc_ref.dtype)
    vc_ref[0] = comp(av_ref, w1v_ref, w2v_ref, pv_ref).astype(vc_ref.dtype)


def _compress(ak, av, w1k, w2k, pk, w1v, w2v, pv, gk):
    n, nc, kk = ak.shape
    blk = pl.BlockSpec((1, nc, kk), lambda i: (i, 0, 0))
    out = pl.BlockSpec((1, nc, NSA_DH), lambda i: (i, 0, 0))
    w1 = _resident((2 * kk, CMP_HIDDEN))
    w2 = _resident((CMP_HIDDEN, NSA_DH))
    pp = _resident((8, 2 * kk))
    return pl.pallas_call(
        _compress_kernel,
        grid=(n,),
        in_specs=[blk, blk, w1, w2, pp, w1, w2, pp, _resident((1, NSA_DH))],
        out_specs=[out, out],
        out_shape=[jax.ShapeDtypeStruct((n, nc, NSA_DH), _MXU_DTYPE)] * 2,
        compiler_params=_cparams("parallel"),
        name="nsa_compress",
    )(ak, av, w1k, w2k, pk, w1v, w2v, pv, gk)


def _cmp_sel_kernel(q_ref, kc_ref, vca_ref, tab_ref, c2s_ref, bk_ref, oc_ref, sel_ref,
                    m_ref, acco_ref, acci_ref, *, tq, ns, d):
    i = pl.program_id(2)
    ch = CMP_CHUNK
    end = (tq // CMP_STRIDE) * (i + 1)
    n_far = (end - 1) // ch
    rowi = lax.broadcasted_iota(jnp.int32, (ch, NSA_GROUP * tq), 0)
    qpos = tq * i + lax.broadcasted_iota(jnp.int32, (1, tq), 1)
    sees_any = jnp.where(qpos >= CMP_BLOCK - 1, 1.0, 0.0)

    q = jnp.concatenate([q_ref[0, r] for r in range(NSA_GROUP)], axis=1)
    m_ref[...] = jnp.full(m_ref.shape, -3e38, F32)
    acco_ref[...] = jnp.zeros(acco_ref.shape, F32)
    acci_ref[...] = jnp.zeros(acci_ref.shape, F32)

    def chunk(g, with_table):
        start = pl.multiple_of(end - ch * g, CMP_STRIDE)
        kc = kc_ref[0, 0, pl.ds(start, ch), :]
        vt = vca_ref[0, 0, pl.ds(start, ch), :].T[0:d + ONES_ROWS].astype(_MXU_DTYPE)
        ct = c2s_ref[pl.ds(start, ch), :].T.astype(_MXU_DTYPE)
        s = _dot(kc, q)
        if with_table:
            s = s + tab_ref[0]
        s = jnp.where(rowi >= ch * (g + 1) - end, s, NEG_INF)
        m_old = m_ref[...]
        m_new = jnp.maximum(m_old, jnp.max(s, axis=0, keepdims=True))
        alpha = jnp.exp2(m_old - m_new)
        e = jnp.exp2(s - m_new).astype(_MXU_DTYPE)
        acco_ref[...] = alpha * acco_ref[...] + _dot(vt, e)
        acci_ref[...] = alpha * acci_ref[...] + _dot(ct, e)
        m_ref[...] = m_new

    chunk(0, True)

    def far(g, carry):
        chunk(g, False)
        return carry

    lax.fori_loop(1, n_far + 1, far, 0)

    imp = jnp.zeros((ns, tq), F32)
    for r in range(NSA_GROUP):
        lanes = slice(r * tq, (r + 1) * tq)
        inv = sees_any / acco_ref[d:d + 1, lanes]
        oc_ref[0, r] = acco_ref[0:d, lanes] * inv
        imp = imp + acci_ref[:, lanes] * inv

    back = bk_ref[...] + i * (tq // SEL_BLOCK)
    jidx = lax.broadcasted_iota(jnp.int32, (ns, tq), 0)
    forced = (jidx == 0) | ((back >= 0) & (back < SEL_LOCAL))
    jf = jidx.astype(F32)
    taken = -3e38

    def pick(imp, rounds):
        for _ in range(rounds):
            mx = jnp.max(imp, axis=0, keepdims=True)
            first = jnp.min(jnp.where(imp == mx, jf, 1e9), axis=0, keepdims=True)
            imp = jnp.where(jf == first, taken, imp)
        sel_ref[0, 0] = jnp.where(imp == taken, 0.0, NEG_INF).astype(sel_ref.dtype)

    top_k = min(SEL_TOPK, ns)
    n_forced = 1 + SEL_LOCAL

    @pl.when(i == 0)
    def _():
        pick(jnp.where(back >= 0, jnp.where(forced, FORCE, imp), NEG_INF), top_k)

    @pl.when(i > 0)
    def _():
        pick(jnp.where(forced, taken, jnp.where(back >= 0, imp, NEG_INF)), top_k - n_forced)


def _cmp_sel(qt, kc, vca, tab, c2s, bk, tq):
    b, h, _, t = qt.shape
    g, npad, d = kc.shape[1:]
    ns = c2s.shape[1]
    assert tq >= SEL_LOCAL * SEL_BLOCK and tq // CMP_STRIDE <= CMP_CHUNK
    return pl.pallas_call(
        functools.partial(_cmp_sel_kernel, tq=tq, ns=ns, d=d),
        grid=(b, g, t // tq),
        in_specs=[pl.BlockSpec((1, NSA_GROUP, d, tq), lambda bi, gi, i: (bi, gi, 0, i)),
                  pl.BlockSpec((1, 1, npad, d), lambda bi, gi, i: (bi, gi, 0, 0)),
                  pl.BlockSpec((1, 1, npad, LANES), lambda bi, gi, i: (bi, gi, 0, 0)),
                  pl.BlockSpec((1, CMP_CHUNK, NSA_GROUP * tq), lambda bi, gi, i: (gi, 0, 0)),
                  _resident(c2s.shape), _resident(bk.shape)],
        out_specs=[pl.BlockSpec((1, NSA_GROUP, d, tq), lambda bi, gi, i: (bi, gi, 0, i)),
                   pl.BlockSpec((1, 1, ns, tq), lambda bi, gi, i: (bi, gi, 0, i))],
        out_shape=[jax.ShapeDtypeStruct((b, h, d, t), F32),
                   jax.ShapeDtypeStruct((b, g, ns, t), _MXU_DTYPE)],
        scratch_shapes=[pltpu.VMEM((1, NSA_GROUP * tq), F32),
                        pltpu.VMEM((d + ONES_ROWS, NSA_GROUP * tq), F32),
                        pltpu.VMEM((ns, NSA_GROUP * tq), F32)],
        compiler_params=_cparams("parallel", "parallel", "parallel"),
        name="nsa_cmp_sel",
    )(qt, kc, vca, tab, c2s, bk)


def _out_proj_kernel(mla_ref, ret_ref, oc_ref, os_ref, ow_ref, g_ref, w_ref, r_ref, o_ref):
    d = NSA_DH
    w_mla, w_ret = MLA_HEADS * MLA_V, RET_HEADS * RET_DV
    acc = r_ref[...] + _dot(mla_ref[0].T.astype(_MXU_DTYPE), w_ref[0:w_mla, :])
    acc = acc + _dot(ret_ref[...], w_ref[w_mla:w_mla + w_ret, :])
    g = g_ref[0]
    nsa = []
    for h in range(NSA_HEADS):
        rows = slice(h * d, (h + 1) * d)
        g0 = h * GATE_STRIDE
        nsa.append(g[g0:g0 + 1] * oc_ref[0, rows, :] + g[g0 + 1:g0 + 2] * os_ref[0, rows, :]
                   + g[g0 + 2:g0 + 3] * ow_ref[0, rows, :])
    nsa = jnp.concatenate(nsa, axis=0).T.astype(_MXU_DTYPE)
    o_ref[...] = acc + _dot(nsa, w_ref[w_mla + w_ret:, :])


def _out_proj(o_mla, o_ret, oc, os_, ow, gates, w, res, t, tm=512):
    m, n = res.shape
    assert tm == ATT_TILE
    nt = t // tm
    ch_major = lambda c: pl.BlockSpec((1, c, tm), lambda i: (i // nt, 0, i % nt))
    far_order = len(_tile_plan(nt, True)) == 2
    ch_paired = (lambda c: pl.BlockSpec((1, c, tm), lambda i: (i // nt, 0, _paired_position(i % nt, nt)))
                 ) if far_order else ch_major
    tokens = lambda c: pl.BlockSpec((tm, c), lambda i: (i, 0))
    c_nsa = oc.shape[1]
    return pl.pallas_call(
        _out_proj_kernel,
        grid=(m // tm,),
        in_specs=[ch_paired(o_mla.shape[1]), tokens(o_ret.shape[1]), ch_major(c_nsa), ch_paired(c_nsa),
                  ch_major(c_nsa), ch_major(gates.shape[1]), _resident(w.shape), tokens(n)],
        out_specs=tokens(n),
        out_shape=jax.ShapeDtypeStruct((m, n), F32),
        compiler_params=_cparams("parallel"),
        name="out_proj",
    )(o_mla, o_ret, oc, os_, ow, gates, w, res)


def _ffn_kernel(x_ref, gn_ref, wup_ref, cw_ref, cb_ref, wdn_ref, o_ref, hbuf_ref, acc_ref, *,
                tm, fc):
    @pl.when(pl.program_id(1) == 0)
    def _():
        hbuf_ref[0:8, :] = jnp.zeros((8, hbuf_ref.shape[1]), F32)

    x = x_ref[...]
    y = x * lax.rsqrt(jnp.mean(x * x, axis=-1, keepdims=True) + EPS)
    xn = (y * gn_ref[...]).astype(_MXU_DTYPE)

    def up_proj(col0):
        cols = slice(col0, col0 + fc)
        hbuf_ref[8:tm + 8, cols] = _dot(xn, wup_ref[:, cols])

    def conv(col0):
        cols = slice(col0, col0 + fc)
        w = cw_ref[:, cols]
        h = hbuf_ref[8:tm + 8, cols]
        out = (h * w[2:3] + hbuf_ref[7:tm + 7, cols] * w[1:2] + hbuf_ref[6:tm + 6, cols] * w[0:1]
               + cb_ref[:, cols])
        hbuf_ref[0:8, cols] = hbuf_ref[tm:tm + 8, cols]
        return out

    nf = D_FF // fc
    ahead = 5
    for f in range(min(ahead, nf)):
        up_proj(f * fc)
        up_proj(D_FF + f * fc)
    for f in range(nf):
        if f + ahead < nf:
            up_proj((f + ahead) * fc)
            up_proj(D_FF + (f + ahead) * fc)
        gate = conv(f * fc)
        up = conv(D_FF + f * fc)
        act = (gate * _sigmoid(gate) * up).astype(_MXU_DTYPE)
        contrib = _dot(act, wdn_ref[f * fc:(f + 1) * fc, :])
        if f == 0:
            acc_ref[...] = contrib
        else:
            acc_ref[...] += contrib
    o_ref[...] = x + acc_ref[...]


def _ffn(x, gn, wup, cw, cb, wdn, b, t, tm=512, fc=256):
    d = x.shape[1]
    nt = t // tm
    return pl.pallas_call(
        functools.partial(_ffn_kernel, tm=tm, fc=fc),
        grid=(b, nt),
        in_specs=[pl.BlockSpec((tm, d), lambda bi, i: (bi * nt + i, 0)),
                  _resident((1, d)), _resident(wup.shape), _resident(cw.shape),
                  _resident(cb.shape), _resident(wdn.shape)],
        out_specs=pl.BlockSpec((tm, d), lambda bi, i: (bi * nt + i, 0)),
        out_shape=jax.ShapeDtypeStruct(x.shape, F32),
        scratch_shapes=[pltpu.VMEM((tm + 8, 2 * D_FF), F32), pltpu.VMEM((tm, d), F32)],
        compiler_params=_cparams("parallel", "arbitrary"),
        name="conv_ffn",
    )(x, gn, wup, cw, cb, wdn)


def _rope_tables(t, d):
    inv = ROPE_BASE ** (-np.arange(0, d, 2, dtype=np.float64) / d)
    ang = np.arange(t, dtype=np.float64)[:, None] * inv[None, :]
    return (np.concatenate([np.cos(ang)] * 2, axis=1), np.concatenate([np.sin(ang)] * 2, axis=1))


def _t5_bucket_np(dist):
    max_exact = REL_BUCKETS // 2
    d = np.maximum(dist, 1).astype(np.float64)
    log_b = max_exact + (np.log(d / max_exact) / math.log(REL_MAX_DIST / max_exact)
                         * (REL_BUCKETS - max_exact)).astype(np.int32)
    return np.where(dist < max_exact, dist, np.minimum(log_b, REL_BUCKETS - 1))


@functools.lru_cache(maxsize=None)
def _constants(t):
    c = {}
    cos, sin = _rope_tables(t, MLA_ROPE)
    pad = LANES - MLA_QK
    c["mla_cos"] = np.concatenate([np.ones((t, MLA_NOPE)), cos, np.ones((t, pad))], 1).astype(np.float32)
    c["mla_sin"] = np.concatenate([np.zeros((t, MLA_NOPE)), sin, np.zeros((t, pad))], 1).astype(np.float32)
    cos, sin = _rope_tables(t, RET_DK)
    c["ret_cos"] = np.tile(cos, (1, RET_HEADS)).astype(np.float32)
    c["ret_sin"] = np.tile(sin, (1, RET_HEADS)).astype(np.float32)
    lg = np.log(1.0 - 2.0 ** (-5.0 - np.arange(RET_HEADS, dtype=np.float64)))
    idx = np.arange(RET_CHUNK, dtype=np.float64)
    diff = idx[:, None] - idx[None, :]
    c["ret_din"] = (np.exp(np.maximum(diff, 0.0) * lg[:, None, None]) * (diff >= 0)).astype(np.float32)
    qd = np.exp((idx[:, None] + 1.0) * lg[None, :])
    kd = np.exp((RET_CHUNK - 1.0 - idx[:, None]) * lg[None, :])
    c["ret_qd"] = np.repeat(qd, RET_DK, axis=1).astype(np.float32)
    c["ret_kd"] = np.repeat(kd, RET_DK, axis=1).astype(np.float32)
    head_of = np.arange(RET_HEADS * RET_DK) // RET_DK
    c["ret_cdm"] = np.broadcast_to(np.exp(RET_CHUNK * lg)[head_of][:, None],
                                   (RET_HEADS * RET_DK, RET_HEADS * RET_DV)).astype(np.float32)
    c["ret_bd"] = (head_of[:, None] == head_of[None, :]).astype(np.float32)
    c["bucket"] = _t5_bucket_np(np.arange(LANES)).astype(np.int32)
    kk = np.arange(ATT_TILE)[:, None]
    qq = np.arange(ATT_TILE)[None, :]
    causal = np.where(qq >= kk, 0.0, NEG_INF)
    c["causal_tab"] = np.stack([np.zeros_like(causal), causal,
                                np.full_like(causal, NEG_INF)])[None].astype(np.float32)
    nc, ns = t // CMP_STRIDE, t // SEL_BLOCK
    n_cmp = (t - CMP_BLOCK) // CMP_STRIDE + 1
    c_start = np.arange(nc) * CMP_STRIDE
    s_start = np.arange(ns) * SEL_BLOCK
    overlap = np.clip(np.minimum(c_start[:, None] + CMP_BLOCK, s_start[None, :] + SEL_BLOCK)
                      - np.maximum(c_start[:, None], s_start[None, :]), 0, None).astype(np.float64)
    overlap[n_cmp:] = 0.0
    c["c2s"] = np.concatenate([np.zeros((CMP_CHUNK, ns)), overlap / CMP_BLOCK]).astype(np.float32)
    q = np.arange(CMP_TQ)[None, :]
    c["bk"] = (q // SEL_BLOCK - np.arange(ns)[:, None]).astype(np.int32)
    c["cmp_dist"] = (q - CMP_STRIDE * np.arange(CMP_CHUNK)[:, None]
                     + (CMP_STRIDE * CMP_CHUNK - CMP_TQ - (CMP_BLOCK - 1))).astype(np.int32)
    c["sel_onehot"] = (np.arange(t)[:, None] // SEL_BLOCK == np.arange(ns)[None, :]).astype(np.float32)
    return c


def _cols(w, pieces):
    out = []
    for p in pieces:
        if p[0] is None:
            out.append(jnp.zeros((w.shape[0], p[1]), w.dtype))
        else:
            blk = w[:, p[0]:p[1]]
            out.append(-blk if p[2] < 0 else blk)
    return jnp.concatenate(out, axis=1)


def _rot_pieces(base, d):
    return [(base + d // 2, base + d, -1), (base, base + d // 2, 1)]


def _skew(w, rows, step, col0, cols):
    hh, ll = w.shape
    flat = jnp.tile(w, (1, rows))[:, :rows * (ll - step)]
    return flat.reshape(hh, rows, ll - step)[:, :, col0:col0 + cols]


def _toeplitz(w, n):
    return _skew(w, n, 1, 0, n)


def _pad_to(v, n):
    return jnp.concatenate([v, jnp.zeros((n - v.shape[0],), v.dtype)])


def _layer(xf, b, t, cst, tabs, w_in, w_out, w1_k, w1_v, ffn_w_up, ffn_w_down,
           attn_norm, ffn_norm, mla_q_a_norm, mla_w_uq, mla_kv_a_norm, mla_w_ukv, mla_q_norm,
           mla_k_norm, ret_norm, pos_k, w2_k, pos_v, w2_v, nsa_q_norm, kn_cmp, kn_sel, kn_win,
           ffn_conv_w, ffn_conv_b):
    md = _MXU_DTYPE
    o = _IN_OFF
    pieces = [(o[0], o[1], 1), (None, 64), (o[1], o[2], 1),
              (None, 64), (o[2], o[3], 1), (None, 32),
              (None, 64)] + _rot_pieces(o[2], MLA_ROPE) + [(None, 32)]
    pieces += [(o[3], o[7], 1)]
    for base in (o[3], o[4]):
        for h in range(RET_HEADS):
            pieces += _rot_pieces(base + h * RET_DK, RET_DK)
    d_ = NSA_DH
    pieces += [(o[7], o[10], 1)]
    for base in (o[10], o[12]):
        pieces += [(base, base + d_, 1), (None, LANES - d_), (base + d_, base + 2 * d_, 1),
                   (None, LANES - d_), (base + 2 * d_, base + 4 * d_, 1)]
    for h in range(NSA_HEADS):
        pieces += [(o[14] + 3 * h, o[14] + 3 * h + 3, 1), (None, GATE_STRIDE - 3)]
    pieces += [(None, LANES - NSA_HEADS * GATE_STRIDE)]
    w_in_r = _cols(w_in, pieces)
    pm, pr, pn, pc = _in_proj(xf, attn_norm[None, :], w_in_r, (640, 1536, NSA_SLAB),
                              (2, NSA_KC0, 2 * NSA_KV_HEADS * NSA_DH))

    wq_pieces, wq_rot = [], []
    wkv_k, wkv_v = [], []
    for h in range(MLA_HEADS):
        qb = h * MLA_QK
        wq_pieces += [(qb, qb + MLA_QK, 1), (None, LANES - MLA_QK)]
        wq_rot += [(None, MLA_NOPE)] + _rot_pieces(qb + MLA_NOPE, MLA_ROPE) + [(None, LANES - MLA_QK)]
        kb = h * (MLA_NOPE + MLA_V)
        wkv_k += [(kb, kb + MLA_NOPE, 1), (None, LANES - MLA_NOPE)]
        wkv_v += [(kb + MLA_NOPE, kb + MLA_NOPE + MLA_V, 1), (None, LANES - MLA_V)]
    wq = _cols(mla_w_uq, wq_pieces + wq_rot)
    wq = jnp.concatenate([wq, jnp.zeros((256 - MLA_Q_RANK, wq.shape[1]), wq.dtype)], axis=0).astype(md)
    wkv = _cols(mla_w_ukv, wkv_k + wkv_v).astype(md)
    half = MLA_ROPE // 2

    def rot_gain(gv):
        return jnp.concatenate([jnp.zeros((MLA_NOPE,), gv.dtype), gv[MLA_NOPE + half:],
                                gv[MLA_NOPE:MLA_NOPE + half], jnp.zeros((LANES - MLA_QK,), gv.dtype)])

    q, k, v = _mla_prep(pm, cst["mla_cos"], cst["mla_sin"], _pad_to(mla_q_a_norm, 256)[None, :],
                        mla_kv_a_norm[None, :], wq, wkv,
                        _pad_to(mla_q_norm, LANES)[None, :], rot_gain(mla_q_norm)[None, :],
                        _pad_to(mla_k_norm, LANES)[None, :], rot_gain(mla_k_norm)[None, :], b, t)
    o_mla = _flash(q, k, v, tabs["causal"], use_far=True, name="mla_attention")
    o_mla = o_mla.reshape(b, MLA_HEADS * MLA_V, t)

    o_ret = _retention(pr, cst["ret_cos"], cst["ret_sin"], cst["ret_din"], cst["ret_qd"],
                       cst["ret_kd"], cst["ret_cdm"], cst["ret_bd"],
                       jnp.tile(ret_norm, RET_HEADS)[None, :], b, t)

    g_ = NSA_KV_HEADS
    nc = t // CMP_STRIDE
    qt, ks_n, kw_n, vs_a, vw_a, gates = _nsa_prep(
        pn, nsa_q_norm * (d_ ** -0.5 * LOG2E), _pad_to(kn_sel, LANES)[None, :],
        _pad_to(kn_win, LANES)[None, :], b, t)

    def chunks(c0):
        a = pc[:, c0:c0 + g_ * d_].reshape(b, t, g_, d_).transpose(0, 2, 1, 3)
        return a.reshape(b * g_, nc, CMP_STRIDE * d_)

    pos8 = lambda p: jnp.broadcast_to(p.reshape(1, -1), (8, CMP_BLOCK * d_)).astype(md)
    k_c, v_c = _compress(chunks(0), chunks(g_ * d_), w1_k, w2_k.astype(md), pos8(pos_k),
                         w1_v, w2_v.astype(md), pos8(pos_v), kn_cmp[None, :])
    front = ((0, 0), (0, 0), (CMP_CHUNK, 0), (0, 0))
    k_c = jnp.pad(k_c.reshape(b, g_, nc, d_), front)
    v_c = v_c.reshape(b, g_, nc, d_).astype(F32)
    v_ca = jnp.pad(jnp.concatenate([v_c, jnp.ones_like(v_c[..., :1]),
                                    jnp.zeros_like(v_c[..., :LANES - d_ - 1])], axis=-1), front)
    oc_t, selneg = _cmp_sel(qt, k_c, v_ca, tabs["cmp"], cst["c2s"], cst["bk"], CMP_TQ)
    os_t = _flash(qt, ks_n, vs_a, tabs["sel"], use_far=True, name="nsa_selected",
                  qx=selneg, kx=cst["sel_onehot"].astype(md))
    ow_t = _flash(qt, kw_n, vw_a, tabs["win"], use_far=False, name="nsa_window")

    flat = lambda a: a.reshape(b, NSA_HEADS * d_, t)
    xf = _out_proj(o_mla, o_ret, flat(oc_t), flat(os_t), flat(ow_t), gates, w_out, xf, t)
    return _ffn(xf, ffn_norm[None, :], ffn_w_up, ffn_conv_w, ffn_conv_b[None, :], ffn_w_down, b, t)


def _bias_tables(rel_bias, cst):
    n = ATT_TILE
    lut = rel_bias[cst["bucket"]].T
    delta = (lut - rel_bias[REL_BUCKETS - 1][:, None]) * LOG2E
    hh = delta.shape[0]
    dn = jnp.concatenate([delta, jnp.zeros((hh, n - LANES), F32)], axis=1)
    neg = jnp.full((hh, n), NEG_INF, F32)
    zero = jnp.zeros((hh, n), F32)
    diag = _toeplitz(jnp.concatenate([dn, neg], axis=1), n)
    prev_sel = _toeplitz(jnp.concatenate([zero, dn], axis=1), n)
    off = CMP_STRIDE * CMP_CHUNK - CMP_TQ - (CMP_BLOCK - 1)
    d_max = off + CMP_TQ - 1
    d_min = off - CMP_STRIDE * (CMP_CHUNK - 1)
    by_dist = jnp.concatenate([delta, jnp.zeros((hh, d_max + 1 - LANES), F32),
                               jnp.full((hh, -d_min), NEG_INF, F32)], axis=1)
    cmp_tab = _skew(by_dist, CMP_CHUNK, CMP_STRIDE, off, CMP_TQ)
    dist = cst["cmp_dist"]
    masked = jnp.full((hh, n, n), NEG_INF, F32)
    nw = WIN_TILE
    back = WINDOW // nw
    dnw, negw, zerow = dn[:, :nw], neg[:, :nw], zero[:, :nw]
    win = [_toeplitz(jnp.concatenate([negw if r == back else zerow, dnw if r == 1 else zerow], axis=1), nw)
           for r in range(back, 0, -1)]
    win += [_toeplitz(jnp.concatenate([dnw, negw], axis=1), nw), masked[:, :nw, :nw]]
    cmp_tab = cmp_tab.reshape(NSA_KV_HEADS, NSA_GROUP, *dist.shape).transpose(0, 2, 1, 3).reshape(
        NSA_KV_HEADS, dist.shape[0], NSA_GROUP * dist.shape[1])
    return {"sel": jnp.stack([prev_sel, diag, masked], axis=1), "win": jnp.stack(win, axis=1),
            "cmp": cmp_tab, "causal": jnp.asarray(cst["causal_tab"])}


def kernel(x, w_in, w_out, attn_norm, ffn_norm, mla_q_a_norm, mla_w_uq, mla_kv_a_norm, mla_w_ukv, mla_q_norm, mla_k_norm, ret_norm, nsa_cmp_pos_k, nsa_cmp_w1_k, nsa_cmp_w2_k, nsa_cmp_pos_v, nsa_cmp_w1_v, nsa_cmp_w2_v, nsa_q_norm, nsa_k_norm_cmp, nsa_k_norm_sel, nsa_k_norm_win, rel_bias, ffn_w_up, ffn_conv_w, ffn_conv_b, ffn_w_down):
    b, t, d = x.shape
    assert d == D_MODEL and t % (2 * ATT_TILE) == 0 and WINDOW % WIN_TILE == 0 and WIN_TILE >= LANES
    cst = _constants(t)
    tabs = _bias_tables(rel_bias, cst)
    stacked = (w_in, w_out, nsa_cmp_w1_k, nsa_cmp_w1_v, ffn_w_up, ffn_w_down)
    per_layer = (attn_norm, ffn_norm, mla_q_a_norm, mla_w_uq, mla_kv_a_norm, mla_w_ukv,
                 mla_q_norm, mla_k_norm, ret_norm, nsa_cmp_pos_k, nsa_cmp_w2_k,
                 nsa_cmp_pos_v, nsa_cmp_w2_v, nsa_q_norm, nsa_k_norm_cmp,
                 nsa_k_norm_sel, nsa_k_norm_win, ffn_conv_w, ffn_conv_b)
    xf = x.reshape(b * t, d)
    for l in range(w_in.shape[0]):
        big = [_to_mxu_dtype(w, l) for w in stacked]
        xf = _layer(xf, b, t, cst, tabs, *big, *[p[l] for p in per_layer])
    return xf.reshape(b, t, d)
```

```python
import functools
import math

import numpy as np
import jax
import jax.numpy as jnp
from jax import lax
from jax.experimental import pallas as pl
from jax.experimental.pallas import tpu as pltpu

D_MODEL = 1024
DEPTH = 2
MLA_HEADS = 6
MLA_Q_RANK = 192
MLA_KV_RANK = 128
MLA_NOPE = 64
MLA_ROPE = 32
MLA_V = 64
MLA_QK = MLA_NOPE + MLA_ROPE
RET_HEADS = 4
RET_DK = 64
RET_DV = 64
RET_CHUNK = 128
NSA_HEADS = 6
NSA_KV_HEADS = 2
NSA_GROUP = NSA_HEADS // NSA_KV_HEADS
NSA_DH = 64
CMP_BLOCK = 32
CMP_STRIDE = 16
CMP_HIDDEN = 256
SEL_BLOCK = 64
SEL_TOPK = 16
SEL_LOCAL = 2
WINDOW = 512
REL_BUCKETS = 32
REL_MAX_DIST = 128
D_FF = 2816
ROPE_BASE = 10000.0
EPS = 1e-6
NEG_INF = -1e30
FORCE = 1e9

_IN_SPLITS = (MLA_Q_RANK, MLA_KV_RANK, MLA_ROPE,
              RET_HEADS * RET_DK, RET_HEADS * RET_DK, RET_HEADS * RET_DV, RET_HEADS * RET_DV,
              NSA_HEADS * NSA_DH) + (NSA_KV_HEADS * NSA_DH,) * 6 + (3 * NSA_HEADS,)
_IN_OFF = [0] + [int(v) for v in np.cumsum(_IN_SPLITS)]
D_IN = _IN_OFF[-1]

LANES = 128
ATT_TILE = 512
CMP_TQ = 512
CMP_CHUNK = 128
WIN_TILE = 512
VMEM_LIMIT = 56 * 1024 * 1024

_MXU_DTYPE = jnp.bfloat16
F32 = jnp.float32
LOG2E = math.log2(math.e)


def _cparams(*sem):
    return pltpu.CompilerParams(dimension_semantics=sem, vmem_limit_bytes=VMEM_LIMIT)


def _dot(a, b):
    return jnp.dot(a, b, preferred_element_type=F32)


def _sigmoid(x):
    return 1.0 / (1.0 + jnp.exp(-x))


def _cast_kernel(x_ref, o_ref):
    o_ref[...] = x_ref[0].astype(o_ref.dtype)


def _to_mxu_dtype(w, layer, rows=256):
    _, r, c = w.shape
    return pl.pallas_call(
        _cast_kernel,
        grid=(r // rows,),
        in_specs=[pl.BlockSpec((1, rows, c), lambda i: (layer, i, 0))],
        out_specs=pl.BlockSpec((rows, c), lambda i: (i, 0)),
        out_shape=jax.ShapeDtypeStruct((r, c), _MXU_DTYPE),
        compiler_params=_cparams("parallel"),
        name="weight_cast",
    )(w)


ONES_ROWS = 16


def _ones_rows(n, dtype):
    row = lax.broadcasted_iota(jnp.int32, (ONES_ROWS, n), 0)
    return jnp.where(row == 0, 1.0, 0.0).astype(dtype)


def _with_ones_rows(vt):
    b, h, _, t = vt.shape
    extra = jnp.concatenate([jnp.ones((b, h, 1, t), vt.dtype),
                             jnp.zeros((b, h, ONES_ROWS - 1, t), vt.dtype)], axis=2)
    return jnp.concatenate([vt, extra], axis=2)


def _resident(shape):
    nd = len(shape)
    return pl.BlockSpec(shape, lambda *_: (0,) * nd, pipeline_mode=pl.Buffered(1))


def _in_proj_kernel(x_ref, g_ref, w_ref, *o_refs, widths, copy):
    x = x_ref[...]
    y = x * lax.rsqrt(jnp.mean(x * x, axis=-1, keepdims=True) + EPS)
    xn = (y * g_ref[...]).astype(_MXU_DTYPE)
    slab, c0, cw = copy
    off = 0
    for j, (o_ref, wd) in enumerate(zip(o_refs, widths)):
        res = _dot(xn, w_ref[:, off:off + wd])
        o_ref[...] = res
        if j == slab:
            o_refs[-1][...] = res[:, c0:c0 + cw].astype(o_refs[-1].dtype)
        off += wd


def _in_proj(x, g, w, widths, copy, tm=256):
    m, d = x.shape
    n = w.shape[1]
    return pl.pallas_call(
        functools.partial(_in_proj_kernel, widths=widths, copy=copy),
        grid=(m // tm,),
        in_specs=[pl.BlockSpec((tm, d), lambda i: (i, 0)),
                  _resident((1, d)),
                  _resident((d, n))],
        out_specs=[pl.BlockSpec((tm, wd), lambda i: (i, 0)) for wd in widths + (copy[2],)],
        out_shape=[jax.ShapeDtypeStruct((m, wd), F32) for wd in widths]
        + [jax.ShapeDtypeStruct((m, copy[2]), _MXU_DTYPE)],
        compiler_params=_cparams("parallel"),
        name="in_proj",
    )(x, g, w)


def _mla_prep_kernel(pm_ref, cos_ref, sin_ref, gqa_ref, gkva_ref, wq_ref, wkv_ref,
                     gq_ref, gqr_ref, gk_ref, gkr_ref, q_ref, k_ref, v_ref):
    pm = pm_ref[...]
    hs = MLA_HEADS * LANES
    cq = pm[:, 0:256]
    r = lax.rsqrt(jnp.sum(cq * cq, axis=-1, keepdims=True) * (1.0 / MLA_Q_RANK) + EPS)
    qq = _dot((cq * r * gqa_ref[...]).astype(_MXU_DTYPE), wq_ref[...])
    ckv = pm[:, 256:384]
    r = lax.rsqrt(jnp.mean(ckv * ckv, axis=-1, keepdims=True) + EPS)
    kv = _dot((ckv * r * gkva_ref[...]).astype(_MXU_DTYPE), wkv_ref[...])
    kpe = pm[:, 384:512]
    kpe_rot = pm[:, 512:640]
    cos = cos_ref[...]
    sin = sin_ref[...]
    scale = MLA_QK ** -0.5 * LOG2E
    aq = cos * gq_ref[...] * scale
    bq = sin * gqr_ref[...] * scale
    ak = cos * gk_ref[...]
    bk = sin * gkr_ref[...]
    for h in range(MLA_HEADS):
        sl = slice(h * LANES, (h + 1) * LANES)
        sr = slice(hs + h * LANES, hs + (h + 1) * LANES)
        qh = qq[:, sl]
        rq = lax.rsqrt(jnp.sum(qh * qh, axis=-1, keepdims=True) * (1.0 / MLA_QK) + EPS)
        q_ref[0, h] = ((qh * aq + qq[:, sr] * bq) * rq).T.astype(q_ref.dtype)
        kh = kv[:, sl] + kpe
        rk = lax.rsqrt(jnp.sum(kh * kh, axis=-1, keepdims=True) * (1.0 / MLA_QK) + EPS)
        k_ref[0, h] = ((kh * ak + kpe_rot * bk) * rk).astype(k_ref.dtype)
        v_ref[0, h, 0:MLA_V, :] = kv[:, sr].T[0:MLA_V].astype(v_ref.dtype)
        v_ref[0, h, MLA_V:MLA_V + ONES_ROWS, :] = _ones_rows(pm.shape[0], v_ref.dtype)


def _mla_prep(pm, cos, sin, gqa, gkva, wq, wkv, gq, gqr, gk, gkr, b, t, tm=512):
    nt = t // tm
    hs = MLA_HEADS * LANES
    vec = lambda n: _resident((1, n))
    return pl.pallas_call(
        _mla_prep_kernel,
        grid=(b, nt),
        in_specs=[pl.BlockSpec((tm, 640), lambda bi, i: (bi * nt + i, 0)),
                  pl.BlockSpec((tm, LANES), lambda bi, i: (i, 0)),
                  pl.BlockSpec((tm, LANES), lambda bi, i: (i, 0)),
                  vec(256), vec(LANES), _resident((256, 2 * hs)), _resident((LANES, 2 * hs)),
                  vec(LANES), vec(LANES), vec(LANES), vec(LANES)],
        out_specs=[pl.BlockSpec((1, MLA_HEADS, LANES, tm), lambda bi, i: (bi, 0, 0, i)),
                   pl.BlockSpec((1, MLA_HEADS, tm, LANES), lambda bi, i: (bi, 0, i, 0)),
                   pl.BlockSpec((1, MLA_HEADS, MLA_V + ONES_ROWS, tm), lambda bi, i: (bi, 0, 0, i))],
        out_shape=[jax.ShapeDtypeStruct((b, MLA_HEADS, LANES, t), _MXU_DTYPE),
                   jax.ShapeDtypeStruct((b, MLA_HEADS, t, LANES), _MXU_DTYPE),
                   jax.ShapeDtypeStruct((b, MLA_HEADS, MLA_V + ONES_ROWS, t), _MXU_DTYPE)],
        compiler_params=_cparams("parallel", "parallel"),
        name="mla_prep",
    )(pm, cos, sin, gqa, gkva, wq, wkv, gq, gqr, gk, gkr)


def _tile_plan(nq, use_far):
    if use_far and nq % 2 == 0:
        return [lambda j: j, lambda j: nq - 1 - j]
    n = next(c for c in (4, 2, 1) if nq % c == 0) if not use_far else 1
    return [functools.partial(lambda j, s: n * j + s, s=s) for s in range(n)]


def _paired_position(i, nq):
    return jnp.where(i < nq // 2, 2 * i, 2 * (nq - 1 - i) + 1)


def _flash_kernel(*refs, use_far, sub, dv, extra, tiles, tq, near):
    ns = len(tiles)
    q_refs, refs = refs[:ns], refs[ns:]
    if extra:
        qx_refs, refs = refs[:ns], refs[ns:]
        k_ref, kx_ref, v_ref = refs[:3]
        refs = refs[3:]
    else:
        k_ref, v_ref = refs[:2]
        refs = refs[2:]
    tab_refs, o_ref = refs[:ns * near], refs[ns * near]
    step = pl.program_id(2)
    tc = tq

    def block(q, table, start, nkeys, m, acc):
        kb = k_ref[0, 0, pl.ds(start, nkeys), :]
        if extra:
            kb = jnp.concatenate([kb, kx_ref[pl.ds(start, nkeys), :]], axis=1)
        s = _dot(kb, q)
        if table is not None:
            s = s + table
        for j in range(nkeys // sub):
            sj = s[j * sub:(j + 1) * sub]
            vc = v_ref[0, 0, :, pl.ds(pl.multiple_of(start + j * sub, sub), sub)]
            m_new = jnp.maximum(m, jnp.max(sj, axis=0, keepdims=True))
            p = jnp.exp2(sj - m_new).astype(_MXU_DTYPE)
            acc = jnp.exp2(m - m_new) * acc + _dot(vc, p)
            m = m_new
        return m, acc

    state = []
    for slot in range(ns):
        i = tiles[slot](step)
        q = q_refs[slot][0, 0]
        if extra:
            q = jnp.concatenate([q, qx_refs[slot][0, 0]], axis=0)
        m = jnp.full((1, tq), -3e38, F32)
        acc = jnp.zeros((dv + ONES_ROWS, tq), F32)
        n_far = jnp.maximum(i - (near - 1), 0)
        if use_far:
            done = 0
            for width in (8, 4, 2, 1):
                count = (n_far - done) // width

                def body(c, carry, width=width, base=done, q=q):
                    start = pl.multiple_of((base + c * width) * tc, tc)
                    return block(q, None, start, width * tc, *carry)

                m, acc = lax.fori_loop(0, count, body, (m, acc))
                done = done + count * width
        state.append((q, n_far, m, acc))
    for slot, (q, n_far, m, acc) in enumerate(state):
        table = jnp.concatenate([tab_refs[slot * near + p][0, 0] for p in range(near)], axis=0)
        m, acc = block(q, table, pl.multiple_of(n_far * tc, tc), near * tc, m, acc)
        o_ref[0, 0, :, slot * tq:(slot + 1) * tq] = acc[0:dv] / acc[dv:dv + 1]


def _flash(qt, k, va, tab, *, use_far, name, qx=None, kx=None, sub=256):
    b, h, dk, t = qt.shape
    hk, dva = va.shape[1], va.shape[2]
    dv = dva - ONES_ROWS
    rep = h // hk
    ht, near, tq = tab.shape[0], tab.shape[1] - 1, tab.shape[3]
    extra = qx is not None
    tiles = _tile_plan(t // tq, use_far)
    ns = len(tiles)
    q_specs = [pl.BlockSpec((1, 1, dk, tq), lambda bi, hi, j, f=f: (bi, hi, 0, f(j))) for f in tiles]

    def piece(f, p):
        return lambda bi, hi, j: (hi % ht, jnp.minimum(p - jnp.minimum(f(j), near - 1) + near - 1, near),
                                  0, 0)

    tab_specs = [pl.BlockSpec((1, 1, tq, tq), piece(f, p)) for f in tiles for p in range(near)]
    k_spec = pl.BlockSpec((1, 1, t, dk), lambda bi, hi, j: (bi, hi // rep, 0, 0))
    v_spec = pl.BlockSpec((1, 1, dva, t), lambda bi, hi, j: (bi, hi // rep, 0, 0))
    if extra:
        nx = qx.shape[2]
        qx_specs = [pl.BlockSpec((1, 1, nx, tq), lambda bi, hi, j, f=f: (bi, hi // rep, 0, f(j)))
                    for f in tiles]
        in_specs = q_specs + qx_specs + [k_spec, _resident((t, nx)), v_spec] + tab_specs
        args = (qt,) * ns + (qx,) * ns + (k, kx, va) + (tab,) * (ns * near)
    else:
        in_specs = q_specs + [k_spec, v_spec] + tab_specs
        args = (qt,) * ns + (k, va) + (tab,) * (ns * near)
    return pl.pallas_call(
        functools.partial(_flash_kernel, use_far=use_far, sub=min(sub, tq), dv=dv, extra=extra,
                          tiles=tiles, tq=tq, near=near),
        grid=(b, h, t // (tq * ns)),
        in_specs=in_specs,
        out_specs=pl.BlockSpec((1, 1, dv, ns * tq), lambda bi, hi, j: (bi, hi, 0, j)),
        out_shape=jax.ShapeDtypeStruct((b, h, dv, t), F32),
        compiler_params=_cparams("parallel", "parallel", "parallel"),
        name=name,
    )(*args)


def _ret_kernel(q_ref, k_ref, v_ref, g_ref, qr_ref, kr_ref, cos_ref, sin_ref, din_ref, qd_ref,
                kd_ref, cdm_ref, bd_ref, gn_ref, o_ref, state_ref, *, nchunk):
    @pl.when(pl.program_id(1) == 0)
    def _():
        state_ref[...] = jnp.zeros(state_ref.shape, F32)

    w = RET_HEADS * RET_DK
    lane = lax.broadcasted_iota(jnp.int32, (1, w), 1)
    heads = [(lane >= h * RET_DK) & (lane < (h + 1) * RET_DK) for h in range(RET_HEADS)]
    on_diag = bd_ref[...] > 0.5
    c_ = RET_CHUNK
    for c in range(nchunk):
        sl = slice(c * c_, (c + 1) * c_)
        cos = cos_ref[sl, :]
        sin = sin_ref[sl, :]
        qh = q_ref[sl, :] * cos + qr_ref[sl, :] * sin
        kh = (k_ref[sl, :] * cos + kr_ref[sl, :] * sin) * (RET_DK ** -0.5)
        kb = kh.astype(_MXU_DTYPE)
        vb = v_ref[sl, :].astype(_MXU_DTYPE)
        st = state_ref[...]
        out = _dot((qh * qd_ref[...]).astype(_MXU_DTYPE), st.astype(_MXU_DTYPE))
        for h in range(RET_HEADS):
            qm = jnp.where(heads[h], qh, 0.0).astype(_MXU_DTYPE)
            inner = lax.dot_general(qm, kb, (((1,), (1,)), ((), ())),
                                    preferred_element_type=F32) * din_ref[h]
            out = out + jnp.where(heads[h], _dot(inner.astype(_MXU_DTYPE), vb), 0.0)
        kdt = (kh * kd_ref[...]).T
        state_ref[...] = st * cdm_ref[...] + jnp.where(on_diag, _dot(kdt.astype(_MXU_DTYPE), vb), 0.0)
        o2 = out * out
        ms = jnp.zeros_like(out)
        for h in range(RET_HEADS):
            ssum = jnp.sum(jnp.where(heads[h], o2, 0.0), axis=-1, keepdims=True)
            ms = jnp.where(heads[h], ssum * (1.0 / RET_DV), ms)
        y = out * lax.rsqrt(ms + EPS) * gn_ref[...]
        gg = g_ref[sl, :]
        o_ref[sl, :] = (gg * _sigmoid(gg) * y).astype(o_ref.dtype)


def _retention(pr, cos, sin, din, qd, kd, cdm, bd, gn, b, t, tt=256):
    w = RET_HEADS * RET_DK
    nt = t // tt
    col = lambda j: pl.BlockSpec((tt, w), lambda bi, i: (bi * nt + i, j))
    pos = pl.BlockSpec((tt, w), lambda bi, i: (i, 0))
    return pl.pallas_call(
        functools.partial(_ret_kernel, nchunk=tt // RET_CHUNK),
        grid=(b, nt),
        in_specs=[col(0), col(1), col(2), col(3), col(4), col(5), pos, pos,
                  _resident(din.shape), _resident(qd.shape), _resident(kd.shape),
                  _resident(cdm.shape), _resident(bd.shape), _resident((1, w))],
        out_specs=pl.BlockSpec((tt, w), lambda bi, i: (bi * nt + i, 0)),
        out_shape=jax.ShapeDtypeStruct((b * t, w), _MXU_DTYPE),
        scratch_shapes=[pltpu.VMEM((w, w), F32)],
        compiler_params=_cparams("parallel", "arbitrary"),
        name="retention",
    )(pr, pr, pr, pr, pr, pr, cos, sin, din, qd, kd, cdm, bd, gn)


NSA_Q0, NSA_KC0, NSA_VC0, NSA_KS0, NSA_VS0, NSA_KW0, NSA_VW0, NSA_GATE0 = (
    0, 384, 512, 640, 896, 1024, 1280, 1408)
NSA_SLAB = 1536
GATE_STRIDE = 8


def _nsa_prep_kernel(pn_ref, gq_ref, gks_ref, gkw_ref, qt_ref, ks_ref, kw_ref, vs_ref, vw_ref,
                     gt_ref):
    tm = pn_ref.shape[0]
    d = NSA_DH
    dt = qt_ref.dtype
    xq = pn_ref[:, NSA_Q0:NSA_Q0 + NSA_HEADS * d].T
    for h in range(NSA_HEADS):
        blk = xq[h * d:(h + 1) * d]
        r = lax.rsqrt(jnp.mean(blk * blk, axis=0, keepdims=True) + EPS)
        qt_ref[0, h, 0:d, :] = (blk * r * gq_ref[...]).astype(dt)
        qt_ref[0, h, d:2 * d, :] = jnp.zeros((d, tm), dt)
    for g in range(NSA_KV_HEADS):
        for c0, g_ref, o_ref in ((NSA_KS0, gks_ref, ks_ref), (NSA_KW0, gkw_ref, kw_ref)):
            slot = pn_ref[:, c0 + g * LANES:c0 + (g + 1) * LANES]
            r = lax.rsqrt(jnp.sum(slot * slot, axis=-1, keepdims=True) * (1.0 / d) + EPS)
            o_ref[0, g] = (slot * r * g_ref[...]).astype(dt)
    for c0, o_ref in ((NSA_VS0, vs_ref), (NSA_VW0, vw_ref)):
        vt = pn_ref[:, c0:c0 + LANES].T
        for g in range(NSA_KV_HEADS):
            o_ref[0, g, 0:d, :] = vt[g * d:(g + 1) * d].astype(dt)
            o_ref[0, g, d:d + ONES_ROWS, :] = _ones_rows(tm, dt)
    gt = pn_ref[:, NSA_GATE0:NSA_GATE0 + LANES].T
    gt_ref[0] = _sigmoid(gt[0:NSA_HEADS * GATE_STRIDE])


def _nsa_prep(pn, gq, gks, gkw, b, t, tm=512):
    nt = t // tm
    d = NSA_DH
    g = NSA_KV_HEADS
    md = _MXU_DTYPE
    ch_major = lambda n, r: pl.BlockSpec((1, n, r, tm), lambda bi, i: (bi, 0, 0, i))
    natural = pl.BlockSpec((1, g, tm, LANES), lambda bi, i: (bi, 0, i, 0))
    return pl.pallas_call(
        _nsa_prep_kernel,
        grid=(b, nt),
        in_specs=[pl.BlockSpec((tm, NSA_SLAB), lambda bi, i: (bi * nt + i, 0)),
                  _resident((d, tm)), _resident((1, LANES)), _resident((1, LANES))],
        out_specs=[ch_major(NSA_HEADS, 2 * d), natural, natural,
                   ch_major(g, d + ONES_ROWS), ch_major(g, d + ONES_ROWS),
                   pl.BlockSpec((1, NSA_HEADS * GATE_STRIDE, tm), lambda bi, i: (bi, 0, i))],
        out_shape=[jax.ShapeDtypeStruct((b, NSA_HEADS, 2 * d, t), md),
                   jax.ShapeDtypeStruct((b, g, t, LANES), md),
                   jax.ShapeDtypeStruct((b, g, t, LANES), md),
                   jax.ShapeDtypeStruct((b, g, d + ONES_ROWS, t), md),
                   jax.ShapeDtypeStruct((b, g, d + ONES_ROWS, t), md),
                   jax.ShapeDtypeStruct((b, NSA_HEADS * GATE_STRIDE, t), F32)],
        compiler_params=_cparams("parallel", "parallel"),
        name="nsa_prep",
    )(pn, jnp.broadcast_to(gq[:, None], (d, tm)), gks, gkw)


def _gelu_tanh(x):
    return 0.5 * x * (1.0 + jnp.tanh(math.sqrt(2.0 / math.pi) * (x + 0.044715 * (x * x * x))))


def _compress_kernel(ak_ref, av_ref, w1k_ref, w2k_ref, pk_ref, w1v_ref, w2v_ref, pv_ref, gk_ref,
                     kc_ref, vc_ref):
    half = CMP_STRIDE * NSA_DH

    def comp(a_ref, w1_ref, w2_ref, p_ref):
        a = a_ref[0]
        pb = _dot(p_ref[...], w1_ref[...])[0:1]
        second = _dot(a, w1_ref[half:2 * half, :])
        nc = second.shape[0]
        hid = _dot(a, w1_ref[0:half, :]) + pltpu.roll(second, nc - 1, 0) + pb
        return _dot(_gelu_tanh(hid).astype(_MXU_DTYPE), w2_ref[...])

    kc = comp(ak_ref, w1k_ref, w2k_ref, pk_ref)
    y = kc * lax.rsqrt(jnp.mean(kc * kc, axis=-1, keepdims=True) + EPS)
    kc_ref[0] = (y * gk_ref[...]).astype(kc_ref.dtype)
    vc_ref[0] = comp(av_ref, w1v_ref, w2v_ref, pv_ref).astype(vc_ref.dtype)


def _compress(ak, av, w1k, w2k, pk, w1v, w2v, pv, gk):
    n, nc, kk = ak.shape
    blk = pl.BlockSpec((1, nc, kk), lambda i: (i, 0, 0))
    out = pl.BlockSpec((1, nc, NSA_DH), lambda i: (i, 0, 0))
    w1 = _resident((2 * kk, CMP_HIDDEN))
    w2 = _resident((CMP_HIDDEN, NSA_DH))
    pp = _resident((8, 2 * kk))
    return pl.pallas_call(
        _compress_kernel,
        grid=(n,),
        in_specs=[blk, blk, w1, w2, pp, w1, w2, pp, _resident((1, NSA_DH))],
        out_specs=[out, out],
        out_shape=[jax.ShapeDtypeStruct((n, nc, NSA_DH), _MXU_DTYPE)] * 2,
        compiler_params=_cparams("parallel"),
        name="nsa_compress",
    )(ak, av, w1k, w2k, pk, w1v, w2v, pv, gk)


def _cmp_sel_kernel(q_ref, kc_ref, vca_ref, tab_ref, c2s_ref, bk_ref, oc_ref, sel_ref,
                    m_ref, acco_ref, acci_ref, *, tq, ns, d):
    i = pl.program_id(2)
    ch = CMP_CHUNK
    end = (tq // CMP_STRIDE) * (i + 1)
    n_far = (end - 1) // ch
    rowi = lax.broadcasted_iota(jnp.int32, (ch, NSA_GROUP * tq), 0)
    qpos = tq * i + lax.broadcasted_iota(jnp.int32, (1, tq), 1)
    sees_any = jnp.where(qpos >= CMP_BLOCK - 1, 1.0, 0.0)

    q = jnp.concatenate([q_ref[0, r] for r in range(NSA_GROUP)], axis=1)
    m_ref[...] = jnp.full(m_ref.shape, -3e38, F32)
    acco_ref[...] = jnp.zeros(acco_ref.shape, F32)
    acci_ref[...] = jnp.zeros(acci_ref.shape, F32)

    def chunk(g, with_table):
        start = pl.multiple_of(end - ch * g, CMP_STRIDE)
        kc = kc_ref[0, 0, pl.ds(start, ch), :]
        vt = vca_ref[0, 0, pl.ds(start, ch), :].T[0:d + ONES_ROWS].astype(_MXU_DTYPE)
        ct = c2s_ref[pl.ds(start, ch), :].T.astype(_MXU_DTYPE)
        s = _dot(kc, q)
        if with_table:
            s = s + tab_ref[0]
        s = jnp.where(rowi >= ch * (g + 1) - end, s, NEG_INF)
        m_old = m_ref[...]
        m_new = jnp.maximum(m_old, jnp.max(s, axis=0, keepdims=True))
        alpha = jnp.exp2(m_old - m_new)
        e = jnp.exp2(s - m_new).astype(_MXU_DTYPE)
        acco_ref[...] = alpha * acco_ref[...] + _dot(vt, e)
        acci_ref[...] = alpha * acci_ref[...] + _dot(ct, e)
        m_ref[...] = m_new

    chunk(0, True)

    def far(g, carry):
        chunk(g, False)
        return carry

    lax.fori_loop(1, n_far + 1, far, 0)

    imp = jnp.zeros((ns, tq), F32)
    for r in range(NSA_GROUP):
        lanes = slice(r * tq, (r + 1) * tq)
        inv = sees_any / acco_ref[d:d + 1, lanes]
        oc_ref[0, r] = acco_ref[0:d, lanes] * inv
        imp = imp + acci_ref[:, lanes] * inv

    back = bk_ref[...] + i * (tq // SEL_BLOCK)
    jidx = lax.broadcasted_iota(jnp.int32, (ns, tq), 0)
    forced = (jidx == 0) | ((back >= 0) & (back < SEL_LOCAL))
    taken = -3e38

    def pick(imp, rounds, rows):
        imp = imp[0:rows]
        idx = lax.broadcasted_iota(jnp.int32, (rows, tq), 0).astype(F32)
        for _ in range(rounds):
            mx = jnp.max(imp, axis=0, keepdims=True)
            first = jnp.min(jnp.where(imp == mx, idx, 1e9), axis=0, keepdims=True)
            imp = jnp.where(idx == first, taken, imp)
        sel = jnp.where(imp == taken, 0.0, NEG_INF)
        if rows < ns:
            sel = jnp.concatenate([sel, jnp.full((ns - rows, tq), NEG_INF, F32)], axis=0)
        sel_ref[0, 0] = sel.astype(sel_ref.dtype)

    top_k = min(SEL_TOPK, ns)
    n_forced = 1 + SEL_LOCAL
    per_tile = tq // SEL_BLOCK
    half = ns // per_tile // 2

    @pl.when(i == 0)
    def _():
        pick(jnp.where(back >= 0, jnp.where(forced, FORCE, imp), NEG_INF), top_k,
             min(ns, max(16, per_tile)))

    later = jnp.where(forced, taken, jnp.where(back >= 0, imp, NEG_INF))

    @pl.when((i > 0) & (i < half))
    def _():
        pick(later, top_k - n_forced, max(16, ns // 2))

    @pl.when((i > 0) & (i >= half))
    def _():
        pick(later, top_k - n_forced, ns)


def _cmp_sel(qt, kc, vca, tab, c2s, bk, tq):
    b, h, _, t = qt.shape
    g, npad, d = kc.shape[1:]
    ns = c2s.shape[1]
    assert tq >= SEL_LOCAL * SEL_BLOCK and tq // CMP_STRIDE <= CMP_CHUNK
    return pl.pallas_call(
        functools.partial(_cmp_sel_kernel, tq=tq, ns=ns, d=d),
        grid=(b, g, t // tq),
        in_specs=[pl.BlockSpec((1, NSA_GROUP, d, tq), lambda bi, gi, i: (bi, gi, 0, i)),
                  pl.BlockSpec((1, 1, npad, d), lambda bi, gi, i: (bi, gi, 0, 0)),
                  pl.BlockSpec((1, 1, npad, LANES), lambda bi, gi, i: (bi, gi, 0, 0)),
                  pl.BlockSpec((1, CMP_CHUNK, NSA_GROUP * tq), lambda bi, gi, i: (gi, 0, 0)),
                  _resident(c2s.shape), _resident(bk.shape)],
        out_specs=[pl.BlockSpec((1, NSA_GROUP, d, tq), lambda bi, gi, i: (bi, gi, 0, i)),
                   pl.BlockSpec((1, 1, ns, tq), lambda bi, gi, i: (bi, gi, 0, i))],
        out_shape=[jax.ShapeDtypeStruct((b, h, d, t), F32),
                   jax.ShapeDtypeStruct((b, g, ns, t), _MXU_DTYPE)],
        scratch_shapes=[pltpu.VMEM((1, NSA_GROUP * tq), F32),
                        pltpu.VMEM((d + ONES_ROWS, NSA_GROUP * tq), F32),
                        pltpu.VMEM((ns, NSA_GROUP * tq), F32)],
        compiler_params=_cparams("parallel", "parallel", "parallel"),
        name="nsa_cmp_sel",
    )(qt, kc, vca, tab, c2s, bk)


def _out_proj_kernel(mla_ref, ret_ref, oc_ref, os_ref, ow_ref, g_ref, w_ref, r_ref, o_ref):
    d = NSA_DH
    w_mla, w_ret = MLA_HEADS * MLA_V, RET_HEADS * RET_DV
    acc = r_ref[...] + _dot(mla_ref[0].T.astype(_MXU_DTYPE), w_ref[0:w_mla, :])
    acc = acc + _dot(ret_ref[...], w_ref[w_mla:w_mla + w_ret, :])
    g = g_ref[0]
    nsa = []
    for h in range(NSA_HEADS):
        rows = slice(h * d, (h + 1) * d)
        g0 = h * GATE_STRIDE
        nsa.append(g[g0:g0 + 1] * oc_ref[0, rows, :] + g[g0 + 1:g0 + 2] * os_ref[0, rows, :]
                   + g[g0 + 2:g0 + 3] * ow_ref[0, rows, :])
    nsa = jnp.concatenate(nsa, axis=0).T.astype(_MXU_DTYPE)
    o_ref[...] = acc + _dot(nsa, w_ref[w_mla + w_ret:, :])


def _out_proj(o_mla, o_ret, oc, os_, ow, gates, w, res, t, tm=512):
    m, n = res.shape
    assert tm == ATT_TILE
    nt = t // tm
    ch_major = lambda c: pl.BlockSpec((1, c, tm), lambda i: (i // nt, 0, i % nt))
    far_order = len(_tile_plan(nt, True)) == 2
    ch_paired = (lambda c: pl.BlockSpec((1, c, tm), lambda i: (i // nt, 0, _paired_position(i % nt, nt)))
                 ) if far_order else ch_major
    tokens = lambda c: pl.BlockSpec((tm, c), lambda i: (i, 0))
    c_nsa = oc.shape[1]
    return pl.pallas_call(
        _out_proj_kernel,
        grid=(m // tm,),
        in_specs=[ch_paired(o_mla.shape[1]), tokens(o_ret.shape[1]), ch_major(c_nsa), ch_paired(c_nsa),
                  ch_major(c_nsa), ch_major(gates.shape[1]), _resident(w.shape), tokens(n)],
        out_specs=tokens(n),
        out_shape=jax.ShapeDtypeStruct((m, n), F32),
        compiler_params=_cparams("parallel"),
        name="out_proj",
    )(o_mla, o_ret, oc, os_, ow, gates, w, res)


def _ffn_kernel(x_ref, gn_ref, wup_ref, cw_ref, cb_ref, wdn_ref, o_ref, hbuf_ref, acc_ref, *,
                tm, fc):
    @pl.when(pl.program_id(1) == 0)
    def _():
        hbuf_ref[0:8, :] = jnp.zeros((8, hbuf_ref.shape[1]), F32)

    x = x_ref[...]
    y = x * lax.rsqrt(jnp.mean(x * x, axis=-1, keepdims=True) + EPS)
    xn = (y * gn_ref[...]).astype(_MXU_DTYPE)

    def up_proj(col0):
        cols = slice(col0, col0 + fc)
        hbuf_ref[8:tm + 8, cols] = _dot(xn, wup_ref[:, cols])

    def conv(col0):
        cols = slice(col0, col0 + fc)
        w = cw_ref[:, cols]
        h = hbuf_ref[8:tm + 8, cols]
        out = (h * w[2:3] + hbuf_ref[7:tm + 7, cols] * w[1:2] + hbuf_ref[6:tm + 6, cols] * w[0:1]
               + cb_ref[:, cols])
        hbuf_ref[0:8, cols] = hbuf_ref[tm:tm + 8, cols]
        return out

    nf = D_FF // fc
    ahead = 5
    for f in range(min(ahead, nf)):
        up_proj(f * fc)
        up_proj(D_FF + f * fc)
    for f in range(nf):
        if f + ahead < nf:
            up_proj((f + ahead) * fc)
            up_proj(D_FF + (f + ahead) * fc)
        gate = conv(f * fc)
        up = conv(D_FF + f * fc)
        act = (gate * _sigmoid(gate) * up).astype(_MXU_DTYPE)
        contrib = _dot(act, wdn_ref[f * fc:(f + 1) * fc, :])
        if f == 0:
            acc_ref[...] = contrib
        else:
            acc_ref[...] += contrib
    o_ref[...] = x + acc_ref[...]


def _ffn(x, gn, wup, cw, cb, wdn, b, t, tm=512, fc=256):
    d = x.shape[1]
    nt = t // tm
    return pl.pallas_call(
        functools.partial(_ffn_kernel, tm=tm, fc=fc),
        grid=(b, nt),
        in_specs=[pl.BlockSpec((tm, d), lambda bi, i: (bi * nt + i, 0)),
                  _resident((1, d)), _resident(wup.shape), _resident(cw.shape),
                  _resident(cb.shape), _resident(wdn.shape)],
        out_specs=pl.BlockSpec((tm, d), lambda bi, i: (bi * nt + i, 0)),
        out_shape=jax.ShapeDtypeStruct(x.shape, F32),
        scratch_shapes=[pltpu.VMEM((tm + 8, 2 * D_FF), F32), pltpu.VMEM((tm, d), F32)],
        compiler_params=_cparams("parallel", "arbitrary"),
        name="conv_ffn",
    )(x, gn, wup, cw, cb, wdn)


def _rope_tables(t, d):
    inv = ROPE_BASE ** (-np.arange(0, d, 2, dtype=np.float64) / d)
    ang = np.arange(t, dtype=np.float64)[:, None] * inv[None, :]
    return (np.concatenate([np.cos(ang)] * 2, axis=1), np.concatenate([np.sin(ang)] * 2, axis=1))


def _t5_bucket_np(dist):
    max_exact = REL_BUCKETS // 2
    d = np.maximum(dist, 1).astype(np.float64)
    log_b = max_exact + (np.log(d / max_exact) / math.log(REL_MAX_DIST / max_exact)
                         * (REL_BUCKETS - max_exact)).astype(np.int32)
    return np.where(dist < max_exact, dist, np.minimum(log_b, REL_BUCKETS - 1))


@functools.lru_cache(maxsize=None)
def _constants(t):
    c = {}
    cos, sin = _rope_tables(t, MLA_ROPE)
    pad = LANES - MLA_QK
    c["mla_cos"] = np.concatenate([np.ones((t, MLA_NOPE)), cos, np.ones((t, pad))], 1).astype(np.float32)
    c["mla_sin"] = np.concatenate([np.zeros((t, MLA_NOPE)), sin, np.zeros((t, pad))], 1).astype(np.float32)
    cos, sin = _rope_tables(t, RET_DK)
    c["ret_cos"] = np.tile(cos, (1, RET_HEADS)).astype(np.float32)
    c["ret_sin"] = np.tile(sin, (1, RET_HEADS)).astype(np.float32)
    lg = np.log(1.0 - 2.0 ** (-5.0 - np.arange(RET_HEADS, dtype=np.float64)))
    idx = np.arange(RET_CHUNK, dtype=np.float64)
    diff = idx[:, None] - idx[None, :]
    c["ret_din"] = (np.exp(np.maximum(diff, 0.0) * lg[:, None, None]) * (diff >= 0)).astype(np.float32)
    qd = np.exp((idx[:, None] + 1.0) * lg[None, :])
    kd = np.exp((RET_CHUNK - 1.0 - idx[:, None]) * lg[None, :])
    c["ret_qd"] = np.repeat(qd, RET_DK, axis=1).astype(np.float32)
    c["ret_kd"] = np.repeat(kd, RET_DK, axis=1).astype(np.float32)
    head_of = np.arange(RET_HEADS * RET_DK) // RET_DK
    c["ret_cdm"] = np.broadcast_to(np.exp(RET_CHUNK * lg)[head_of][:, None],
                                   (RET_HEADS * RET_DK, RET_HEADS * RET_DV)).astype(np.float32)
    c["ret_bd"] = (head_of[:, None] == head_of[None, :]).astype(np.float32)
    c["bucket"] = _t5_bucket_np(np.arange(LANES)).astype(np.int32)
    kk = np.arange(ATT_TILE)[:, None]
    qq = np.arange(ATT_TILE)[None, :]
    causal = np.where(qq >= kk, 0.0, NEG_INF)
    c["causal_tab"] = np.stack([np.zeros_like(causal), causal,
                                np.full_like(causal, NEG_INF)])[None].astype(np.float32)
    nc, ns = t // CMP_STRIDE, t // SEL_BLOCK
    n_cmp = (t - CMP_BLOCK) // CMP_STRIDE + 1
    c_start = np.arange(nc) * CMP_STRIDE
    s_start = np.arange(ns) * SEL_BLOCK
    overlap = np.clip(np.minimum(c_start[:, None] + CMP_BLOCK, s_start[None, :] + SEL_BLOCK)
                      - np.maximum(c_start[:, None], s_start[None, :]), 0, None).astype(np.float64)
    overlap[n_cmp:] = 0.0
    c["c2s"] = np.concatenate([np.zeros((CMP_CHUNK, ns)), overlap / CMP_BLOCK]).astype(np.float32)
    q = np.arange(CMP_TQ)[None, :]
    c["bk"] = (q // SEL_BLOCK - np.arange(ns)[:, None]).astype(np.int32)
    c["cmp_dist"] = (q - CMP_STRIDE * np.arange(CMP_CHUNK)[:, None]
                     + (CMP_STRIDE * CMP_CHUNK - CMP_TQ - (CMP_BLOCK - 1))).astype(np.int32)
    c["sel_onehot"] = (np.arange(t)[:, None] // SEL_BLOCK == np.arange(ns)[None, :]).astype(np.float32)
    return c


def _cols(w, pieces):
    out = []
    for p in pieces:
        if p[0] is None:
            out.append(jnp.zeros((w.shape[0], p[1]), w.dtype))
        else:
            blk = w[:, p[0]:p[1]]
            out.append(-blk if p[2] < 0 else blk)
    return jnp.concatenate(out, axis=1)


def _rot_pieces(base, d):
    return [(base + d // 2, base + d, -1), (base, base + d // 2, 1)]


def _skew(w, rows, step, col0, cols):
    hh, ll = w.shape
    flat = jnp.tile(w, (1, rows))[:, :rows * (ll - step)]
    return flat.reshape(hh, rows, ll - step)[:, :, col0:col0 + cols]


def _toeplitz(w, n):
    return _skew(w, n, 1, 0, n)


def _pad_to(v, n):
    return jnp.concatenate([v, jnp.zeros((n - v.shape[0],), v.dtype)])


def _layer(xf, b, t, cst, tabs, w_in, w_out, w1_k, w1_v, ffn_w_up, ffn_w_down,
           attn_norm, ffn_norm, mla_q_a_norm, mla_w_uq, mla_kv_a_norm, mla_w_ukv, mla_q_norm,
           mla_k_norm, ret_norm, pos_k, w2_k, pos_v, w2_v, nsa_q_norm, kn_cmp, kn_sel, kn_win,
           ffn_conv_w, ffn_conv_b):
    md = _MXU_DTYPE
    o = _IN_OFF
    pieces = [(o[0], o[1], 1), (None, 64), (o[1], o[2], 1),
              (None, 64), (o[2], o[3], 1), (None, 32),
              (None, 64)] + _rot_pieces(o[2], MLA_ROPE) + [(None, 32)]
    pieces += [(o[3], o[7], 1)]
    for base in (o[3], o[4]):
        for h in range(RET_HEADS):
            pieces += _rot_pieces(base + h * RET_DK, RET_DK)
    d_ = NSA_DH
    pieces += [(o[7], o[10], 1)]
    for base in (o[10], o[12]):
        pieces += [(base, base + d_, 1), (None, LANES - d_), (base + d_, base + 2 * d_, 1),
                   (None, LANES - d_), (base + 2 * d_, base + 4 * d_, 1)]
    for h in range(NSA_HEADS):
        pieces += [(o[14] + 3 * h, o[14] + 3 * h + 3, 1), (None, GATE_STRIDE - 3)]
    pieces += [(None, LANES - NSA_HEADS * GATE_STRIDE)]
    w_in_r = _cols(w_in, pieces)
    pm, pr, pn, pc = _in_proj(xf, attn_norm[None, :], w_in_r, (640, 1536, NSA_SLAB),
                              (2, NSA_KC0, 2 * NSA_KV_HEADS * NSA_DH))

    wq_pieces, wq_rot = [], []
    wkv_k, wkv_v = [], []
    for h in range(MLA_HEADS):
        qb = h * MLA_QK
        wq_pieces += [(qb, qb + MLA_QK, 1), (None, LANES - MLA_QK)]
        wq_rot += [(None, MLA_NOPE)] + _rot_pieces(qb + MLA_NOPE, MLA_ROPE) + [(None, LANES - MLA_QK)]
        kb = h * (MLA_NOPE + MLA_V)
        wkv_k += [(kb, kb + MLA_NOPE, 1), (None, LANES - MLA_NOPE)]
        wkv_v += [(kb + MLA_NOPE, kb + MLA_NOPE + MLA_V, 1), (None, LANES - MLA_V)]
    wq = _cols(mla_w_uq, wq_pieces + wq_rot)
    wq = jnp.concatenate([wq, jnp.zeros((256 - MLA_Q_RANK, wq.shape[1]), wq.dtype)], axis=0).astype(md)
    wkv = _cols(mla_w_ukv, wkv_k + wkv_v).astype(md)
    half = MLA_ROPE // 2

    def rot_gain(gv):
        return jnp.concatenate([jnp.zeros((MLA_NOPE,), gv.dtype), gv[MLA_NOPE + half:],
                                gv[MLA_NOPE:MLA_NOPE + half], jnp.zeros((LANES - MLA_QK,), gv.dtype)])

    q, k, v = _mla_prep(pm, cst["mla_cos"], cst["mla_sin"], _pad_to(mla_q_a_norm, 256)[None, :],
                        mla_kv_a_norm[None, :], wq, wkv,
                        _pad_to(mla_q_norm, LANES)[None, :], rot_gain(mla_q_norm)[None, :],
                        _pad_to(mla_k_norm, LANES)[None, :], rot_gain(mla_k_norm)[None, :], b, t)
    o_mla = _flash(q, k, v, tabs["causal"], use_far=True, name="mla_attention")
    o_mla = o_mla.reshape(b, MLA_HEADS * MLA_V, t)

    o_ret = _retention(pr, cst["ret_cos"], cst["ret_sin"], cst["ret_din"], cst["ret_qd"],
                       cst["ret_kd"], cst["ret_cdm"], cst["ret_bd"],
                       jnp.tile(ret_norm, RET_HEADS)[None, :], b, t)

    g_ = NSA_KV_HEADS
    nc = t // CMP_STRIDE
    qt, ks_n, kw_n, vs_a, vw_a, gates = _nsa_prep(
        pn, nsa_q_norm * (d_ ** -0.5 * LOG2E), _pad_to(kn_sel, LANES)[None, :],
        _pad_to(kn_win, LANES)[None, :], b, t)

    def chunks(c0):
        a = pc[:, c0:c0 + g_ * d_].reshape(b, t, g_, d_).transpose(0, 2, 1, 3)
        return a.reshape(b * g_, nc, CMP_STRIDE * d_)

    pos8 = lambda p: jnp.broadcast_to(p.reshape(1, -1), (8, CMP_BLOCK * d_)).astype(md)
    k_c, v_c = _compress(chunks(0), chunks(g_ * d_), w1_k, w2_k.astype(md), pos8(pos_k),
                         w1_v, w2_v.astype(md), pos8(pos_v), kn_cmp[None, :])
    front = ((0, 0), (0, 0), (CMP_CHUNK, 0), (0, 0))
    k_c = jnp.pad(k_c.reshape(b, g_, nc, d_), front)
    v_c = v_c.reshape(b, g_, nc, d_).astype(F32)
    v_ca = jnp.pad(jnp.concatenate([v_c, jnp.ones_like(v_c[..., :1]),
                                    jnp.zeros_like(v_c[..., :LANES - d_ - 1])], axis=-1), front)
    oc_t, selneg = _cmp_sel(qt, k_c, v_ca, tabs["cmp"], cst["c2s"], cst["bk"], CMP_TQ)
    os_t = _flash(qt, ks_n, vs_a, tabs["sel"], use_far=True, name="nsa_selected",
                  qx=selneg, kx=cst["sel_onehot"].astype(md))
    ow_t = _flash(qt, kw_n, vw_a, tabs["win"], use_far=False, name="nsa_window")

    flat = lambda a: a.reshape(b, NSA_HEADS * d_, t)
    xf = _out_proj(o_mla, o_ret, flat(oc_t), flat(os_t), flat(ow_t), gates, w_out, xf, t)
    return _ffn(xf, ffn_norm[None, :], ffn_w_up, ffn_conv_w, ffn_conv_b[None, :], ffn_w_down, b, t)


def _bias_tables(rel_bias, cst):
    n = ATT_TILE
    lut = rel_bias[cst["bucket"]].T
    delta = (lut - rel_bias[REL_BUCKETS - 1][:, None]) * LOG2E
    hh = delta.shape[0]
    dn = jnp.concatenate([delta, jnp.zeros((hh, n - LANES), F32)], axis=1)
    neg = jnp.full((hh, n), NEG_INF, F32)
    zero = jnp.zeros((hh, n), F32)
    diag = _toeplitz(jnp.concatenate([dn, neg], axis=1), n)
    prev_sel = _toeplitz(jnp.concatenate([zero, dn], axis=1), n)
    off = CMP_STRIDE * CMP_CHUNK - CMP_TQ - (CMP_BLOCK - 1)
    d_max = off + CMP_TQ - 1
    d_min = off - CMP_STRIDE * (CMP_CHUNK - 1)
    by_dist = jnp.concatenate([delta, jnp.zeros((hh, d_max + 1 - LANES), F32),
                               jnp.full((hh, -d_min), NEG_INF, F32)], axis=1)
    cmp_tab = _skew(by_dist, CMP_CHUNK, CMP_STRIDE, off, CMP_TQ)
    dist = cst["cmp_dist"]
    masked = jnp.full((hh, n, n), NEG_INF, F32)
    nw = WIN_TILE
    back = WINDOW // nw
    dnw, negw, zerow = dn[:, :nw], neg[:, :nw], zero[:, :nw]
    win = [_toeplitz(jnp.concatenate([negw if r == back else zerow, dnw if r == 1 else zerow], axis=1), nw)
           for r in range(back, 0, -1)]
    win += [_toeplitz(jnp.concatenate([dnw, negw], axis=1), nw), masked[:, :nw, :nw]]
    cmp_tab = cmp_tab.reshape(NSA_KV_HEADS, NSA_GROUP, *dist.shape).transpose(0, 2, 1, 3).reshape(
        NSA_KV_HEADS, dist.shape[0], NSA_GROUP * dist.shape[1])
    return {"sel": jnp.stack([prev_sel, diag, masked], axis=1), "win": jnp.stack(win, axis=1),
            "cmp": cmp_tab, "causal": jnp.asarray(cst["causal_tab"])}


def kernel(x, w_in, w_out, attn_norm, ffn_norm, mla_q_a_norm, mla_w_uq, mla_kv_a_norm, mla_w_ukv, mla_q_norm, mla_k_norm, ret_norm, nsa_cmp_pos_k, nsa_cmp_w1_k, nsa_cmp_w2_k, nsa_cmp_pos_v, nsa_cmp_w1_v, nsa_cmp_w2_v, nsa_q_norm, nsa_k_norm_cmp, nsa_k_norm_sel, nsa_k_norm_win, rel_bias, ffn_w_up, ffn_conv_w, ffn_conv_b, ffn_w_down):
    b, t, d = x.shape
    assert d == D_MODEL and t % (2 * ATT_TILE) == 0 and WINDOW % WIN_TILE == 0 and WIN_TILE >= LANES
    cst = _constants(t)
    tabs = _bias_tables(rel_bias, cst)
    stacked = (w_in, w_out, nsa_cmp_w1_k, nsa_cmp_w1_v, ffn_w_up, ffn_w_down)
    per_layer = (attn_norm, ffn_norm, mla_q_a_norm, mla_w_uq, mla_kv_a_norm, mla_w_ukv,
                 mla_q_norm, mla_k_norm, ret_norm, nsa_cmp_pos_k, nsa_cmp_w2_k,
                 nsa_cmp_pos_v, nsa_cmp_w2_v, nsa_q_norm, nsa_k_norm_cmp,
                 nsa_k_norm_sel, nsa_k_norm_win, ffn_conv_w, ffn_conv_b)
    xf = x.reshape(b * t, d)
    for l in range(w_in.shape[0]):
        big = [_to_mxu_dtype(w, l) for w in stacked]
        xf = _layer(xf, b, t, cst, tabs, *big, *[p[l] for p in per_layer])
    return xf.reshape(b, t, d)
```

```python
import functools
import math

import numpy as np
import jax
import jax.numpy as jnp
from jax import lax
from jax.experimental import pallas as pl
from jax.experimental.pallas import tpu as pltpu

D_MODEL = 1024
DEPTH = 2
MLA_HEADS = 6
MLA_Q_RANK = 192
MLA_KV_RANK = 128
MLA_NOPE = 64
MLA_ROPE = 32
MLA_V = 64
MLA_QK = MLA_NOPE + MLA_ROPE
RET_HEADS = 4
RET_DK = 64
RET_DV = 64
RET_CHUNK = 128
NSA_HEADS = 6
NSA_KV_HEADS = 2
NSA_GROUP = NSA_HEADS // NSA_KV_HEADS
NSA_DH = 64
CMP_BLOCK = 32
CMP_STRIDE = 16
CMP_HIDDEN = 256
SEL_BLOCK = 64
SEL_TOPK = 16
SEL_LOCAL = 2
WINDOW = 512
REL_BUCKETS = 32
REL_MAX_DIST = 128
D_FF = 2816
ROPE_BASE = 10000.0
EPS = 1e-6
NEG_INF = -1e30
FORCE = 1e9

_IN_SPLITS = (MLA_Q_RANK, MLA_KV_RANK, MLA_ROPE,
              RET_HEADS * RET_DK, RET_HEADS * RET_DK, RET_HEADS * RET_DV, RET_HEADS * RET_DV,
              NSA_HEADS * NSA_DH) + (NSA_KV_HEADS * NSA_DH,) * 6 + (3 * NSA_HEADS,)
_IN_OFF = [0] + [int(v) for v in np.cumsum(_IN_SPLITS)]
D_IN = _IN_OFF[-1]

LANES = 128
ATT_TILE = 512
CMP_TQ = 512
CMP_CHUNK = 128
WIN_TILE = 512
VMEM_LIMIT = 56 * 1024 * 1024

_MXU_DTYPE = jnp.bfloat16
F32 = jnp.float32
LOG2E = math.log2(math.e)


def _cparams(*sem):
    return pltpu.CompilerParams(dimension_semantics=sem, vmem_limit_bytes=VMEM_LIMIT)


def _dot(a, b):
    return jnp.dot(a, b, preferred_element_type=F32)


def _sigmoid(x):
    return 1.0 / (1.0 + jnp.exp(-x))


def _cast_kernel(x_ref, o_ref):
    o_ref[...] = x_ref[0].astype(o_ref.dtype)


def _to_mxu_dtype(w, layer, rows=256):
    _, r, c = w.shape
    return pl.pallas_call(
        _cast_kernel,
        grid=(r // rows,),
        in_specs=[pl.BlockSpec((1, rows, c), lambda i: (layer, i, 0))],
        out_specs=pl.BlockSpec((rows, c), lambda i: (i, 0)),
        out_shape=jax.ShapeDtypeStruct((r, c), _MXU_DTYPE),
        compiler_params=_cparams("parallel"),
        name="weight_cast",
    )(w)


ONES_ROWS = 16


def _ones_rows(n, dtype):
    row = lax.broadcasted_iota(jnp.int32, (ONES_ROWS, n), 0)
    return jnp.where(row == 0, 1.0, 0.0).astype(dtype)


def _resident(shape):
    nd = len(shape)
    return pl.BlockSpec(shape, lambda *_: (0,) * nd, pipeline_mode=pl.Buffered(1))


def _in_proj_kernel(x_ref, g_ref, w_ref, *o_refs, widths, copy):
    x = x_ref[...]
    y = x * lax.rsqrt(jnp.mean(x * x, axis=-1, keepdims=True) + EPS)
    xn = (y * g_ref[...]).astype(_MXU_DTYPE)
    slab, c0, cw = copy
    off = 0
    for j, (o_ref, wd) in enumerate(zip(o_refs, widths)):
        res = _dot(xn, w_ref[:, off:off + wd])
        o_ref[...] = res
        if j == slab:
            o_refs[-1][...] = res[:, c0:c0 + cw].astype(o_refs[-1].dtype)
        off += wd


def _in_proj(x, g, w, widths, copy, tm=256):
    m, d = x.shape
    n = w.shape[1]
    return pl.pallas_call(
        functools.partial(_in_proj_kernel, widths=widths, copy=copy),
        grid=(m // tm,),
        in_specs=[pl.BlockSpec((tm, d), lambda i: (i, 0)),
                  _resident((1, d)),
                  _resident((d, n))],
        out_specs=[pl.BlockSpec((tm, wd), lambda i: (i, 0)) for wd in widths + (copy[2],)],
        out_shape=[jax.ShapeDtypeStruct((m, wd), F32) for wd in widths]
        + [jax.ShapeDtypeStruct((m, copy[2]), _MXU_DTYPE)],
        compiler_params=_cparams("parallel"),
        name="in_proj",
    )(x, g, w)


def _mla_prep_kernel(pm_ref, cos_ref, sin_ref, gqa_ref, gkva_ref, wq_ref, wkv_ref,
                     gq_ref, gqr_ref, gk_ref, gkr_ref, q_ref, k_ref, v_ref):
    pm = pm_ref[...]
    hs = MLA_HEADS * LANES
    cq = pm[:, 0:256]
    r = lax.rsqrt(jnp.sum(cq * cq, axis=-1, keepdims=True) * (1.0 / MLA_Q_RANK) + EPS)
    qq = _dot((cq * r * gqa_ref[...]).astype(_MXU_DTYPE), wq_ref[...])
    ckv = pm[:, 256:384]
    r = lax.rsqrt(jnp.mean(ckv * ckv, axis=-1, keepdims=True) + EPS)
    kv = _dot((ckv * r * gkva_ref[...]).astype(_MXU_DTYPE), wkv_ref[...])
    kpe = pm[:, 384:512]
    kpe_rot = pm[:, 512:640]
    cos = cos_ref[...]
    sin = sin_ref[...]
    scale = MLA_QK ** -0.5 * LOG2E
    aq = cos * gq_ref[...] * scale
    bq = sin * gqr_ref[...] * scale
    ak = cos * gk_ref[...]
    bk = sin * gkr_ref[...]
    for h in range(MLA_HEADS):
        sl = slice(h * LANES, (h + 1) * LANES)
        sr = slice(hs + h * LANES, hs + (h + 1) * LANES)
        qh = qq[:, sl]
        rq = lax.rsqrt(jnp.sum(qh * qh, axis=-1, keepdims=True) * (1.0 / MLA_QK) + EPS)
        q_ref[0, h] = ((qh * aq + qq[:, sr] * bq) * rq).T.astype(q_ref.dtype)
        kh = kv[:, sl] + kpe
        rk = lax.rsqrt(jnp.sum(kh * kh, axis=-1, keepdims=True) * (1.0 / MLA_QK) + EPS)
        k_ref[0, h] = ((kh * ak + kpe_rot * bk) * rk).astype(k_ref.dtype)
        v_ref[0, h, 0:MLA_V, :] = kv[:, sr].T[0:MLA_V].astype(v_ref.dtype)
        v_ref[0, h, MLA_V:MLA_V + ONES_ROWS, :] = _ones_rows(pm.shape[0], v_ref.dtype)


def _mla_prep(pm, cos, sin, gqa, gkva, wq, wkv, gq, gqr, gk, gkr, b, t, tm=512):
    nt = t // tm
    hs = MLA_HEADS * LANES
    vec = lambda n: _resident((1, n))
    return pl.pallas_call(
        _mla_prep_kernel,
        grid=(b, nt),
        in_specs=[pl.BlockSpec((tm, 640), lambda bi, i: (bi * nt + i, 0)),
                  pl.BlockSpec((tm, LANES), lambda bi, i: (i, 0)),
                  pl.BlockSpec((tm, LANES), lambda bi, i: (i, 0)),
                  vec(256), vec(LANES), _resident((256, 2 * hs)), _resident((LANES, 2 * hs)),
                  vec(LANES), vec(LANES), vec(LANES), vec(LANES)],
        out_specs=[pl.BlockSpec((1, MLA_HEADS, LANES, tm), lambda bi, i: (bi, 0, 0, i)),
                   pl.BlockSpec((1, MLA_HEADS, tm, LANES), lambda bi, i: (bi, 0, i, 0)),
                   pl.BlockSpec((1, MLA_HEADS, MLA_V + ONES_ROWS, tm), lambda bi, i: (bi, 0, 0, i))],
        out_shape=[jax.ShapeDtypeStruct((b, MLA_HEADS, LANES, t), _MXU_DTYPE),
                   jax.ShapeDtypeStruct((b, MLA_HEADS, t, LANES), _MXU_DTYPE),
                   jax.ShapeDtypeStruct((b, MLA_HEADS, MLA_V + ONES_ROWS, t), _MXU_DTYPE)],
        compiler_params=_cparams("parallel", "parallel"),
        name="mla_prep",
    )(pm, cos, sin, gqa, gkva, wq, wkv, gq, gqr, gk, gkr)


def _tile_plan(nq, use_far):
    if use_far and nq % 2 == 0:
        return [lambda j: j, lambda j: nq - 1 - j]
    n = next(c for c in (4, 2, 1) if nq % c == 0) if not use_far else 1
    return [functools.partial(lambda j, s: n * j + s, s=s) for s in range(n)]


def _paired_position(i, nq):
    return jnp.where(i < nq // 2, 2 * i, 2 * (nq - 1 - i) + 1)


def _flash_kernel(*refs, use_far, sub, dv, extra, tiles, tq, near):
    ns = len(tiles)
    q_refs, refs = refs[:ns], refs[ns:]
    if extra:
        qx_refs, refs = refs[:ns], refs[ns:]
        k_ref, kx_ref, v_ref = refs[:3]
        refs = refs[3:]
    else:
        k_ref, v_ref = refs[:2]
        refs = refs[2:]
    tab_refs, o_ref = refs[:ns * near], refs[ns * near]
    step = pl.program_id(2)
    tc = tq

    def block(q, table, start, nkeys, m, acc):
        kb = k_ref[0, 0, pl.ds(start, nkeys), :]
        if extra:
            kb = jnp.concatenate([kb, kx_ref[pl.ds(start, nkeys), :]], axis=1)
        s = _dot(kb, q)
        if table is not None:
            s = s + table
        for j in range(nkeys // sub):
            sj = s[j * sub:(j + 1) * sub]
            vc = v_ref[0, 0, :, pl.ds(pl.multiple_of(start + j * sub, sub), sub)]
            m_new = jnp.maximum(m, jnp.max(sj, axis=0, keepdims=True))
            p = jnp.exp2(sj - m_new).astype(_MXU_DTYPE)
            acc = jnp.exp2(m - m_new) * acc + _dot(vc, p)
            m = m_new
        return m, acc

    state = []
    for slot in range(ns):
        i = tiles[slot](step)
        q = q_refs[slot][0, 0]
        if extra:
            q = jnp.concatenate([q, qx_refs[slot][0, 0]], axis=0)
        m = jnp.full((1, tq), -3e38, F32)
        acc = jnp.zeros((dv + ONES_ROWS, tq), F32)
        n_far = jnp.maximum(i - (near - 1), 0)
        if use_far:
            done = 0
            for width in (8, 4, 2, 1):
                count = (n_far - done) // width

                def body(c, carry, width=width, base=done, q=q):
                    start = pl.multiple_of((base + c * width) * tc, tc)
                    return block(q, None, start, width * tc, *carry)

                m, acc = lax.fori_loop(0, count, body, (m, acc))
                done = done + count * width
        state.append((q, n_far, m, acc))
    for slot, (q, n_far, m, acc) in enumerate(state):
        table = jnp.concatenate([tab_refs[slot * near + p][0, 0] for p in range(near)], axis=0)
        m, acc = block(q, table, pl.multiple_of(n_far * tc, tc), near * tc, m, acc)
        o_ref[0, 0, :, slot * tq:(slot + 1) * tq] = acc[0:dv] / acc[dv:dv + 1]


def _flash(qt, k, va, tab, *, use_far, name, qx=None, kx=None, sub=256):
    b, h, dk, t = qt.shape
    hk, dva = va.shape[1], va.shape[2]
    dv = dva - ONES_ROWS
    rep = h // hk
    ht, near, tq = tab.shape[0], tab.shape[1] - 1, tab.shape[3]
    extra = qx is not None
    tiles = _tile_plan(t // tq, use_far)
    ns = len(tiles)
    q_specs = [pl.BlockSpec((1, 1, dk, tq), lambda bi, hi, j, f=f: (bi, hi, 0, f(j))) for f in tiles]

    def piece(f, p):
        return lambda bi, hi, j: (hi % ht, jnp.minimum(p - jnp.minimum(f(j), near - 1) + near - 1, near),
                                  0, 0)

    tab_specs = [pl.BlockSpec((1, 1, tq, tq), piece(f, p)) for f in tiles for p in range(near)]
    k_spec = pl.BlockSpec((1, 1, t, dk), lambda bi, hi, j: (bi, hi // rep, 0, 0))
    v_spec = pl.BlockSpec((1, 1, dva, t), lambda bi, hi, j: (bi, hi // rep, 0, 0))
    if extra:
        nx = qx.shape[2]
        qx_specs = [pl.BlockSpec((1, 1, nx, tq), lambda bi, hi, j, f=f: (bi, hi // rep, 0, f(j)))
                    for f in tiles]
        in_specs = q_specs + qx_specs + [k_spec, _resident((t, nx)), v_spec] + tab_specs
        args = (qt,) * ns + (qx,) * ns + (k, kx, va) + (tab,) * (ns * near)
    else:
        in_specs = q_specs + [k_spec, v_spec] + tab_specs
        args = (qt,) * ns + (k, va) + (tab,) * (ns * near)
    return pl.pallas_call(
        functools.partial(_flash_kernel, use_far=use_far, sub=min(sub, tq), dv=dv, extra=extra,
                          tiles=tiles, tq=tq, near=near),
        grid=(b, h, t // (tq * ns)),
        in_specs=in_specs,
        out_specs=pl.BlockSpec((1, 1, dv, ns * tq), lambda bi, hi, j: (bi, hi, 0, j)),
        out_shape=jax.ShapeDtypeStruct((b, h, dv, t), F32),
        compiler_params=_cparams("parallel", "parallel", "parallel"),
        name=name,
    )(*args)


def _ret_kernel(q_ref, k_ref, v_ref, g_ref, qr_ref, kr_ref, cos_ref, sin_ref, din_ref, qd_ref,
                kd_ref, cdm_ref, bd_ref, gn_ref, o_ref, state_ref, *, nchunk):
    @pl.when(pl.program_id(1) == 0)
    def _():
        state_ref[...] = jnp.zeros(state_ref.shape, F32)

    w = RET_HEADS * RET_DK
    lane = lax.broadcasted_iota(jnp.int32, (1, w), 1)
    heads = [(lane >= h * RET_DK) & (lane < (h + 1) * RET_DK) for h in range(RET_HEADS)]
    on_diag = bd_ref[...] > 0.5
    c_ = RET_CHUNK
    for c in range(nchunk):
        sl = slice(c * c_, (c + 1) * c_)
        cos = cos_ref[sl, :]
        sin = sin_ref[sl, :]
        qh = q_ref[sl, :] * cos + qr_ref[sl, :] * sin
        kh = (k_ref[sl, :] * cos + kr_ref[sl, :] * sin) * (RET_DK ** -0.5)
        kb = kh.astype(_MXU_DTYPE)
        vb = v_ref[sl, :].astype(_MXU_DTYPE)
        st = state_ref[...]
        out = _dot((qh * qd_ref[...]).astype(_MXU_DTYPE), st.astype(_MXU_DTYPE))
        for h in range(RET_HEADS):
            qm = jnp.where(heads[h], qh, 0.0).astype(_MXU_DTYPE)
            inner = lax.dot_general(qm, kb, (((1,), (1,)), ((), ())),
                                    preferred_element_type=F32) * din_ref[h]
            out = out + jnp.where(heads[h], _dot(inner.astype(_MXU_DTYPE), vb), 0.0)
        kdt = (kh * kd_ref[...]).T
        state_ref[...] = st * cdm_ref[...] + jnp.where(on_diag, _dot(kdt.astype(_MXU_DTYPE), vb), 0.0)
        o2 = out * out
        ms = jnp.zeros_like(out)
        for h in range(RET_HEADS):
            ssum = jnp.sum(jnp.where(heads[h], o2, 0.0), axis=-1, keepdims=True)
            ms = jnp.where(heads[h], ssum * (1.0 / RET_DV), ms)
        y = out * lax.rsqrt(ms + EPS) * gn_ref[...]
        gg = g_ref[sl, :]
        o_ref[sl, :] = (gg * _sigmoid(gg) * y).astype(o_ref.dtype)


def _retention(pr, cos, sin, din, qd, kd, cdm, bd, gn, b, t, tt=256):
    w = RET_HEADS * RET_DK
    nt = t // tt
    col = lambda j: pl.BlockSpec((tt, w), lambda bi, i: (bi * nt + i, j))
    pos = pl.BlockSpec((tt, w), lambda bi, i: (i, 0))
    return pl.pallas_call(
        functools.partial(_ret_kernel, nchunk=tt // RET_CHUNK),
        grid=(b, nt),
        in_specs=[col(0), col(1), col(2), col(3), col(4), col(5), pos, pos,
                  _resident(din.shape), _resident(qd.shape), _resident(kd.shape),
                  _resident(cdm.shape), _resident(bd.shape), _resident((1, w))],
        out_specs=pl.BlockSpec((tt, w), lambda bi, i: (bi * nt + i, 0)),
        out_shape=jax.ShapeDtypeStruct((b * t, w), _MXU_DTYPE),
        scratch_shapes=[pltpu.VMEM((w, w), F32)],
        compiler_params=_cparams("parallel", "arbitrary"),
        name="retention",
    )(pr, pr, pr, pr, pr, pr, cos, sin, din, qd, kd, cdm, bd, gn)


NSA_Q0, NSA_KC0, NSA_VC0, NSA_KS0, NSA_VS0, NSA_KW0, NSA_VW0, NSA_GATE0 = (
    0, 384, 512, 640, 896, 1024, 1280, 1408)
NSA_SLAB = 1536
GATE_STRIDE = 8


def _nsa_prep_kernel(pn_ref, gq_ref, gks_ref, gkw_ref, qt_ref, ks_ref, kw_ref, vs_ref, vw_ref,
                     gt_ref):
    tm = pn_ref.shape[0]
    d = NSA_DH
    dt = qt_ref.dtype
    xq = pn_ref[:, NSA_Q0:NSA_Q0 + NSA_HEADS * d].T
    for h in range(NSA_HEADS):
        blk = xq[h * d:(h + 1) * d]
        r = lax.rsqrt(jnp.mean(blk * blk, axis=0, keepdims=True) + EPS)
        qt_ref[0, h, 0:d, :] = (blk * r * gq_ref[...]).astype(dt)
        qt_ref[0, h, d:2 * d, :] = jnp.zeros((d, tm), dt)
    for g in range(NSA_KV_HEADS):
        for c0, g_ref, o_ref in ((NSA_KS0, gks_ref, ks_ref), (NSA_KW0, gkw_ref, kw_ref)):
            slot = pn_ref[:, c0 + g * LANES:c0 + (g + 1) * LANES]
            r = lax.rsqrt(jnp.sum(slot * slot, axis=-1, keepdims=True) * (1.0 / d) + EPS)
            o_ref[0, g] = (slot * r * g_ref[...]).astype(dt)
    for c0, o_ref in ((NSA_VS0, vs_ref), (NSA_VW0, vw_ref)):
        vt = pn_ref[:, c0:c0 + LANES].T
        for g in range(NSA_KV_HEADS):
            o_ref[0, g, 0:d, :] = vt[g * d:(g + 1) * d].astype(dt)
            o_ref[0, g, d:d + ONES_ROWS, :] = _ones_rows(tm, dt)
    gt = pn_ref[:, NSA_GATE0:NSA_GATE0 + LANES].T
    gt_ref[0] = _sigmoid(gt[0:NSA_HEADS * GATE_STRIDE])


def _nsa_prep(pn, gq, gks, gkw, b, t, tm=512):
    nt = t // tm
    d = NSA_DH
    g = NSA_KV_HEADS
    md = _MXU_DTYPE
    ch_major = lambda n, r: pl.BlockSpec((1, n, r, tm), lambda bi, i: (bi, 0, 0, i))
    natural = pl.BlockSpec((1, g, tm, LANES), lambda bi, i: (bi, 0, i, 0))
    return pl.pallas_call(
        _nsa_prep_kernel,
        grid=(b, nt),
        in_specs=[pl.BlockSpec((tm, NSA_SLAB), lambda bi, i: (bi * nt + i, 0)),
                  _resident((d, tm)), _resident((1, LANES)), _resident((1, LANES))],
        out_specs=[ch_major(NSA_HEADS, 2 * d), natural, natural,
                   ch_major(g, d + ONES_ROWS), ch_major(g, d + ONES_ROWS),
                   pl.BlockSpec((1, NSA_HEADS * GATE_STRIDE, tm), lambda bi, i: (bi, 0, i))],
        out_shape=[jax.ShapeDtypeStruct((b, NSA_HEADS, 2 * d, t), md),
                   jax.ShapeDtypeStruct((b, g, t, LANES), md),
                   jax.ShapeDtypeStruct((b, g, t, LANES), md),
                   jax.ShapeDtypeStruct((b, g, d + ONES_ROWS, t), md),
                   jax.ShapeDtypeStruct((b, g, d + ONES_ROWS, t), md),
                   jax.ShapeDtypeStruct((b, NSA_HEADS * GATE_STRIDE, t), F32)],
        compiler_params=_cparams("parallel", "parallel"),
        name="nsa_prep",
    )(pn, jnp.broadcast_to(gq[:, None], (d, tm)), gks, gkw)


def _gelu_tanh(x):
    return 0.5 * x * (1.0 + jnp.tanh(math.sqrt(2.0 / math.pi) * (x + 0.044715 * (x * x * x))))


def _compress_kernel(ak_ref, av_ref, w1k_ref, w2k_ref, pk_ref, w1v_ref, w2v_ref, pv_ref, gk_ref,
                     kc_ref, vc_ref):
    half = CMP_STRIDE * NSA_DH

    def comp(a_ref, w1_ref, w2_ref, p_ref):
        a = a_ref[0]
        pb = _dot(p_ref[...], w1_ref[...])[0:1]
        second = _dot(a, w1_ref[half:2 * half, :])
        nc = second.shape[0]
        hid = _dot(a, w1_ref[0:half, :]) + pltpu.roll(second, nc - 1, 0) + pb
        return _dot(_gelu_tanh(hid).astype(_MXU_DTYPE), w2_ref[...])

    kc = comp(ak_ref, w1k_ref, w2k_ref, pk_ref)
    y = kc * lax.rsqrt(jnp.mean(kc * kc, axis=-1, keepdims=True) + EPS)
    kc_ref[0] = (y * gk_ref[...]).astype(kc_ref.dtype)
    vc_ref[0] = comp(av_ref, w1v_ref, w2v_ref, pv_ref).astype(vc_ref.dtype)


def _compress(ak, av, w1k, w2k, pk, w1v, w2v, pv, gk):
    n, nc, kk = ak.shape
    blk = pl.BlockSpec((1, nc, kk), lambda i: (i, 0, 0))
    out = pl.BlockSpec((1, nc, NSA_DH), lambda i: (i, 0, 0))
    w1 = _resident((2 * kk, CMP_HIDDEN))
    w2 = _resident((CMP_HIDDEN, NSA_DH))
    pp = _resident((8, 2 * kk))
    return pl.pallas_call(
        _compress_kernel,
        grid=(n,),
        in_specs=[blk, blk, w1, w2, pp, w1, w2, pp, _resident((1, NSA_DH))],
        out_specs=[out, out],
        out_shape=[jax.ShapeDtypeStruct((n, nc, NSA_DH), _MXU_DTYPE)] * 2,
        compiler_params=_cparams("parallel"),
        name="nsa_compress",
    )(ak, av, w1k, w2k, pk, w1v, w2v, pv, gk)


def _cmp_sel_kernel(q_ref, kc_ref, vca_ref, tab_ref, c2s_ref, bk_ref, oc_ref, sel_ref,
                    m_ref, acco_ref, acci_ref, *, tq, ns, d):
    i = pl.program_id(2)
    ch = CMP_CHUNK
    end = (tq // CMP_STRIDE) * (i + 1)
    n_far = (end - 1) // ch
    rowi = lax.broadcasted_iota(jnp.int32, (ch, NSA_GROUP * tq), 0)
    qpos = tq * i + lax.broadcasted_iota(jnp.int32, (1, tq), 1)
    sees_any = jnp.where(qpos >= CMP_BLOCK - 1, 1.0, 0.0)

    q = jnp.concatenate([q_ref[0, r] for r in range(NSA_GROUP)], axis=1)
    m_ref[...] = jnp.full(m_ref.shape, -3e38, F32)
    acco_ref[...] = jnp.zeros(acco_ref.shape, F32)
    acci_ref[...] = jnp.zeros(acci_ref.shape, F32)

    def chunk(g, with_table):
        start = pl.multiple_of(end - ch * g, CMP_STRIDE)
        kc = kc_ref[0, 0, pl.ds(start, ch), :]
        vt = vca_ref[0, 0, pl.ds(start, ch), :].T[0:d + ONES_ROWS].astype(_MXU_DTYPE)
        ct = c2s_ref[pl.ds(start, ch), :].T.astype(_MXU_DTYPE)
        s = _dot(kc, q)
        if with_table:
            s = s + tab_ref[0]
        s = jnp.where(rowi >= ch * (g + 1) - end, s, NEG_INF)
        m_old = m_ref[...]
        m_new = jnp.maximum(m_old, jnp.max(s, axis=0, keepdims=True))
        alpha = jnp.exp2(m_old - m_new)
        e = jnp.exp2(s - m_new).astype(_MXU_DTYPE)
        acco_ref[...] = alpha * acco_ref[...] + _dot(vt, e)
        acci_ref[...] = alpha * acci_ref[...] + _dot(ct, e)
        m_ref[...] = m_new

    chunk(0, True)

    def far(g, carry):
        chunk(g, False)
        return carry

    lax.fori_loop(1, n_far + 1, far, 0)

    imp = jnp.zeros((ns, tq), F32)
    for r in range(NSA_GROUP):
        lanes = slice(r * tq, (r + 1) * tq)
        inv = sees_any / acco_ref[d:d + 1, lanes]
        oc_ref[0, r] = acco_ref[0:d, lanes] * inv
        imp = imp + acci_ref[:, lanes] * inv

    back = bk_ref[...] + i * (tq // SEL_BLOCK)
    jidx = lax.broadcasted_iota(jnp.int32, (ns, tq), 0)
    forced = (jidx == 0) | ((back >= 0) & (back < SEL_LOCAL))
    taken = -3e38

    def pick(imp, rounds, rows):
        imp = imp[0:rows]
        idx = lax.broadcasted_iota(jnp.int32, (rows, tq), 0).astype(F32)
        for _ in range(rounds):
            mx = jnp.max(imp, axis=0, keepdims=True)
            first = jnp.min(jnp.where(imp == mx, idx, 1e9), axis=0, keepdims=True)
            imp = jnp.where(idx == first, taken, imp)
        sel = jnp.where(imp == taken, 0.0, NEG_INF)
        if rows < ns:
            sel = jnp.concatenate([sel, jnp.full((ns - rows, tq), NEG_INF, F32)], axis=0)
        sel_ref[0, 0] = sel.astype(sel_ref.dtype)

    top_k = min(SEL_TOPK, ns)
    n_forced = 1 + SEL_LOCAL
    per_tile = tq // SEL_BLOCK
    half = ns // per_tile // 2

    @pl.when(i == 0)
    def _():
        pick(jnp.where(back >= 0, jnp.where(forced, FORCE, imp), NEG_INF), top_k,
             min(ns, max(16, per_tile)))

    later = jnp.where(forced, taken, jnp.where(back >= 0, imp, NEG_INF))

    @pl.when((i > 0) & (i < half))
    def _():
        pick(later, top_k - n_forced, max(16, ns // 2))

    @pl.when((i > 0) & (i >= half))
    def _():
        pick(later, top_k - n_forced, ns)


def _cmp_sel(qt, kc, vca, tab, c2s, bk, tq):
    b, h, _, t = qt.shape
    g, npad, d = kc.shape[1:]
    ns = c2s.shape[1]
    assert tq >= SEL_LOCAL * SEL_BLOCK and tq // CMP_STRIDE <= CMP_CHUNK
    return pl.pallas_call(
        functools.partial(_cmp_sel_kernel, tq=tq, ns=ns, d=d),
        grid=(b, g, t // tq),
        in_specs=[pl.BlockSpec((1, NSA_GROUP, d, tq), lambda bi, gi, i: (bi, gi, 0, i)),
                  pl.BlockSpec((1, 1, npad, d), lambda bi, gi, i: (bi, gi, 0, 0)),
                  pl.BlockSpec((1, 1, npad, LANES), lambda bi, gi, i: (bi, gi, 0, 0)),
                  pl.BlockSpec((1, CMP_CHUNK, NSA_GROUP * tq), lambda bi, gi, i: (gi, 0, 0)),
                  _resident(c2s.shape), _resident(bk.shape)],
        out_specs=[pl.BlockSpec((1, NSA_GROUP, d, tq), lambda bi, gi, i: (bi, gi, 0, i)),
                   pl.BlockSpec((1, 1, ns, tq), lambda bi, gi, i: (bi, gi, 0, i))],
        out_shape=[jax.ShapeDtypeStruct((b, h, d, t), F32),
                   jax.ShapeDtypeStruct((b, g, ns, t), _MXU_DTYPE)],
        scratch_shapes=[pltpu.VMEM((1, NSA_GROUP * tq), F32),
                        pltpu.VMEM((d + ONES_ROWS, NSA_GROUP * tq), F32),
                        pltpu.VMEM((ns, NSA_GROUP * tq), F32)],
        compiler_params=_cparams("parallel", "parallel", "parallel"),
        name="nsa_cmp_sel",
    )(qt, kc, vca, tab, c2s, bk)


def _out_proj_kernel(mla_ref, ret_ref, oc_ref, os_ref, ow_ref, g_ref, w_ref, r_ref, o_ref):
    d = NSA_DH
    w_mla, w_ret = MLA_HEADS * MLA_V, RET_HEADS * RET_DV
    acc = r_ref[...] + _dot(mla_ref[0].T.astype(_MXU_DTYPE), w_ref[0:w_mla, :])
    acc = acc + _dot(ret_ref[...], w_ref[w_mla:w_mla + w_ret, :])
    g = g_ref[0]
    nsa = []
    for h in range(NSA_HEADS):
        rows = slice(h * d, (h + 1) * d)
        g0 = h * GATE_STRIDE
        nsa.append(g[g0:g0 + 1] * oc_ref[0, rows, :] + g[g0 + 1:g0 + 2] * os_ref[0, rows, :]
                   + g[g0 + 2:g0 + 3] * ow_ref[0, rows, :])
    nsa = jnp.concatenate(nsa, axis=0).T.astype(_MXU_DTYPE)
    o_ref[...] = acc + _dot(nsa, w_ref[w_mla + w_ret:, :])


def _out_proj(o_mla, o_ret, oc, os_, ow, gates, w, res, t, tm=512):
    m, n = res.shape
    assert tm == ATT_TILE
    nt = t // tm
    ch_major = lambda c: pl.BlockSpec((1, c, tm), lambda i: (i // nt, 0, i % nt))
    far_order = len(_tile_plan(nt, True)) == 2
    ch_paired = (lambda c: pl.BlockSpec((1, c, tm), lambda i: (i // nt, 0, _paired_position(i % nt, nt)))
                 ) if far_order else ch_major
    tokens = lambda c: pl.BlockSpec((tm, c), lambda i: (i, 0))
    c_nsa = oc.shape[1]
    return pl.pallas_call(
        _out_proj_kernel,
        grid=(m // tm,),
        in_specs=[ch_paired(o_mla.shape[1]), tokens(o_ret.shape[1]), ch_major(c_nsa), ch_paired(c_nsa),
                  ch_major(c_nsa), ch_major(gates.shape[1]), _resident(w.shape), tokens(n)],
        out_specs=tokens(n),
        out_shape=jax.ShapeDtypeStruct((m, n), F32),
        compiler_params=_cparams("parallel"),
        name="out_proj",
    )(o_mla, o_ret, oc, os_, ow, gates, w, res)


def _ffn_kernel(x_ref, gn_ref, wup_ref, cw_ref, cb_ref, wdn_ref, o_ref, hbuf_ref, acc_ref, *,
                tm, fc):
    @pl.when(pl.program_id(1) == 0)
    def _():
        hbuf_ref[0:8, :] = jnp.zeros((8, hbuf_ref.shape[1]), F32)

    x = x_ref[...]
    y = x * lax.rsqrt(jnp.mean(x * x, axis=-1, keepdims=True) + EPS)
    xn = (y * gn_ref[...]).astype(_MXU_DTYPE)

    def up_proj(col0):
        cols = slice(col0, col0 + fc)
        hbuf_ref[8:tm + 8, cols] = _dot(xn, wup_ref[:, cols])

    def conv(col0):
        cols = slice(col0, col0 + fc)
        w = cw_ref[:, cols]
        h = hbuf_ref[8:tm + 8, cols]
        out = (h * w[2:3] + hbuf_ref[7:tm + 7, cols] * w[1:2] + hbuf_ref[6:tm + 6, cols] * w[0:1]
               + cb_ref[:, cols])
        hbuf_ref[0:8, cols] = hbuf_ref[tm:tm + 8, cols]
        return out

    nf = D_FF // fc
    ahead = 5
    for f in range(min(ahead, nf)):
        up_proj(f * fc)
        up_proj(D_FF + f * fc)
    for f in range(nf):
        if f + ahead < nf:
            up_proj((f + ahead) * fc)
            up_proj(D_FF + (f + ahead) * fc)
        gate = conv(f * fc)
        up = conv(D_FF + f * fc)
        act = (gate * _sigmoid(gate) * up).astype(_MXU_DTYPE)
        contrib = _dot(act, wdn_ref[f * fc:(f + 1) * fc, :])
        if f == 0:
            acc_ref[...] = contrib
        else:
            acc_ref[...] += contrib
    o_ref[...] = x + acc_ref[...]


def _ffn(x, gn, wup, cw, cb, wdn, b, t, tm=512, fc=256):
    d = x.shape[1]
    nt = t // tm
    return pl.pallas_call(
        functools.partial(_ffn_kernel, tm=tm, fc=fc),
        grid=(b, nt),
        in_specs=[pl.BlockSpec((tm, d), lambda bi, i: (bi * nt + i, 0)),
                  _resident((1, d)), _resident(wup.shape), _resident(cw.shape),
                  _resident(cb.shape), _resident(wdn.shape)],
        out_specs=pl.BlockSpec((tm, d), lambda bi, i: (bi * nt + i, 0)),
        out_shape=jax.ShapeDtypeStruct(x.shape, F32),
        scratch_shapes=[pltpu.VMEM((tm + 8, 2 * D_FF), F32), pltpu.VMEM((tm, d), F32)],
        compiler_params=_cparams("parallel", "arbitrary"),
        name="conv_ffn",
    )(x, gn, wup, cw, cb, wdn)


def _rope_tables(t, d):
    inv = ROPE_BASE ** (-np.arange(0, d, 2, dtype=np.float64) / d)
    ang = np.arange(t, dtype=np.float64)[:, None] * inv[None, :]
    return (np.concatenate([np.cos(ang)] * 2, axis=1), np.concatenate([np.sin(ang)] * 2, axis=1))


def _t5_bucket_np(dist):
    max_exact = REL_BUCKETS // 2
    d = np.maximum(dist, 1).astype(np.float64)
    log_b = max_exact + (np.log(d / max_exact) / math.log(REL_MAX_DIST / max_exact)
                         * (REL_BUCKETS - max_exact)).astype(np.int32)
    return np.where(dist < max_exact, dist, np.minimum(log_b, REL_BUCKETS - 1))


@functools.lru_cache(maxsize=None)
def _constants(t):
    c = {}
    cos, sin = _rope_tables(t, MLA_ROPE)
    pad = LANES - MLA_QK
    c["mla_cos"] = np.concatenate([np.ones((t, MLA_NOPE)), cos, np.ones((t, pad))], 1).astype(np.float32)
    c["mla_sin"] = np.concatenate([np.zeros((t, MLA_NOPE)), sin, np.zeros((t, pad))], 1).astype(np.float32)
    cos, sin = _rope_tables(t, RET_DK)
    c["ret_cos"] = np.tile(cos, (1, RET_HEADS)).astype(np.float32)
    c["ret_sin"] = np.tile(sin, (1, RET_HEADS)).astype(np.float32)
    lg = np.log(1.0 - 2.0 ** (-5.0 - np.arange(RET_HEADS, dtype=np.float64)))
    idx = np.arange(RET_CHUNK, dtype=np.float64)
    diff = idx[:, None] - idx[None, :]
    c["ret_din"] = (np.exp(np.maximum(diff, 0.0) * lg[:, None, None]) * (diff >= 0)).astype(np.float32)
    qd = np.exp((idx[:, None] + 1.0) * lg[None, :])
    kd = np.exp((RET_CHUNK - 1.0 - idx[:, None]) * lg[None, :])
    c["ret_qd"] = np.repeat(qd, RET_DK, axis=1).astype(np.float32)
    c["ret_kd"] = np.repeat(kd, RET_DK, axis=1).astype(np.float32)
    head_of = np.arange(RET_HEADS * RET_DK) // RET_DK
    c["ret_cdm"] = np.broadcast_to(np.exp(RET_CHUNK * lg)[head_of][:, None],
                                   (RET_HEADS * RET_DK, RET_HEADS * RET_DV)).astype(np.float32)
    c["ret_bd"] = (head_of[:, None] == head_of[None, :]).astype(np.float32)
    c["bucket"] = _t5_bucket_np(np.arange(LANES)).astype(np.int32)
    kk = np.arange(ATT_TILE)[:, None]
    qq = np.arange(ATT_TILE)[None, :]
    causal = np.where(qq >= kk, 0.0, NEG_INF)
    c["causal_tab"] = np.stack([np.zeros_like(causal), causal,
                                np.full_like(causal, NEG_INF)])[None].astype(np.float32)
    nc, ns = t // CMP_STRIDE, t // SEL_BLOCK
    n_cmp = (t - CMP_BLOCK) // CMP_STRIDE + 1
    c_start = np.arange(nc) * CMP_STRIDE
    s_start = np.arange(ns) * SEL_BLOCK
    overlap = np.clip(np.minimum(c_start[:, None] + CMP_BLOCK, s_start[None, :] + SEL_BLOCK)
                      - np.maximum(c_start[:, None], s_start[None, :]), 0, None).astype(np.float64)
    overlap[n_cmp:] = 0.0
    c["c2s"] = np.concatenate([np.zeros((CMP_CHUNK, ns)), overlap / CMP_BLOCK]).astype(np.float32)
    q = np.arange(CMP_TQ)[None, :]
    c["bk"] = (q // SEL_BLOCK - np.arange(ns)[:, None]).astype(np.int32)
    c["sel_onehot"] = (np.arange(t)[:, None] // SEL_BLOCK == np.arange(ns)[None, :]).astype(np.float32)
    return c


def _cols(w, pieces):
    out = []
    for p in pieces:
        if p[0] is None:
            out.append(jnp.zeros((w.shape[0], p[1]), w.dtype))
        else:
            blk = w[:, p[0]:p[1]]
            out.append(-blk if p[2] < 0 else blk)
    return jnp.concatenate(out, axis=1)


def _rot_pieces(base, d):
    return [(base + d // 2, base + d, -1), (base, base + d // 2, 1)]


def _skew(w, rows, step, col0, cols):
    hh, ll = w.shape
    flat = jnp.tile(w, (1, rows))[:, :rows * (ll - step)]
    return flat.reshape(hh, rows, ll - step)[:, :, col0:col0 + cols]


def _toeplitz(w, n):
    return _skew(w, n, 1, 0, n)


def _pad_to(v, n):
    return jnp.concatenate([v, jnp.zeros((n - v.shape[0],), v.dtype)])


def _layer(xf, b, t, cst, tabs, w_in, w_out, w1_k, w1_v, ffn_w_up, ffn_w_down,
           attn_norm, ffn_norm, mla_q_a_norm, mla_w_uq, mla_kv_a_norm, mla_w_ukv, mla_q_norm,
           mla_k_norm, ret_norm, pos_k, w2_k, pos_v, w2_v, nsa_q_norm, kn_cmp, kn_sel, kn_win,
           ffn_conv_w, ffn_conv_b):
    md = _MXU_DTYPE
    o = _IN_OFF
    pieces = [(o[0], o[1], 1), (None, 64), (o[1], o[2], 1),
              (None, 64), (o[2], o[3], 1), (None, 32),
              (None, 64)] + _rot_pieces(o[2], MLA_ROPE) + [(None, 32)]
    pieces += [(o[3], o[7], 1)]
    for base in (o[3], o[4]):
        for h in range(RET_HEADS):
            pieces += _rot_pieces(base + h * RET_DK, RET_DK)
    d_ = NSA_DH
    pieces += [(o[7], o[10], 1)]
    for base in (o[10], o[12]):
        pieces += [(base, base + d_, 1), (None, LANES - d_), (base + d_, base + 2 * d_, 1),
                   (None, LANES - d_), (base + 2 * d_, base + 4 * d_, 1)]
    for h in range(NSA_HEADS):
        pieces += [(o[14] + 3 * h, o[14] + 3 * h + 3, 1), (None, GATE_STRIDE - 3)]
    pieces += [(None, LANES - NSA_HEADS * GATE_STRIDE)]
    w_in_r = _cols(w_in, pieces)
    pm, pr, pn, pc = _in_proj(xf, attn_norm[None, :], w_in_r, (640, 1536, NSA_SLAB),
                              (2, NSA_KC0, 2 * NSA_KV_HEADS * NSA_DH))

    wq_pieces, wq_rot = [], []
    wkv_k, wkv_v = [], []
    for h in range(MLA_HEADS):
        qb = h * MLA_QK
        wq_pieces += [(qb, qb + MLA_QK, 1), (None, LANES - MLA_QK)]
        wq_rot += [(None, MLA_NOPE)] + _rot_pieces(qb + MLA_NOPE, MLA_ROPE) + [(None, LANES - MLA_QK)]
        kb = h * (MLA_NOPE + MLA_V)
        wkv_k += [(kb, kb + MLA_NOPE, 1), (None, LANES - MLA_NOPE)]
        wkv_v += [(kb + MLA_NOPE, kb + MLA_NOPE + MLA_V, 1), (None, LANES - MLA_V)]
    wq = _cols(mla_w_uq, wq_pieces + wq_rot)
    wq = jnp.concatenate([wq, jnp.zeros((256 - MLA_Q_RANK, wq.shape[1]), wq.dtype)], axis=0).astype(md)
    wkv = _cols(mla_w_ukv, wkv_k + wkv_v).astype(md)
    half = MLA_ROPE // 2

    def rot_gain(gv):
        return jnp.concatenate([jnp.zeros((MLA_NOPE,), gv.dtype), gv[MLA_NOPE + half:],
                                gv[MLA_NOPE:MLA_NOPE + half], jnp.zeros((LANES - MLA_QK,), gv.dtype)])

    q, k, v = _mla_prep(pm, cst["mla_cos"], cst["mla_sin"], _pad_to(mla_q_a_norm, 256)[None, :],
                        mla_kv_a_norm[None, :], wq, wkv,
                        _pad_to(mla_q_norm, LANES)[None, :], rot_gain(mla_q_norm)[None, :],
                        _pad_to(mla_k_norm, LANES)[None, :], rot_gain(mla_k_norm)[None, :], b, t)
    o_mla = _flash(q, k, v, tabs["causal"], use_far=True, name="mla_attention")
    o_mla = o_mla.reshape(b, MLA_HEADS * MLA_V, t)

    o_ret = _retention(pr, cst["ret_cos"], cst["ret_sin"], cst["ret_din"], cst["ret_qd"],
                       cst["ret_kd"], cst["ret_cdm"], cst["ret_bd"],
                       jnp.tile(ret_norm, RET_HEADS)[None, :], b, t)

    g_ = NSA_KV_HEADS
    nc = t // CMP_STRIDE
    qt, ks_n, kw_n, vs_a, vw_a, gates = _nsa_prep(
        pn, nsa_q_norm * (d_ ** -0.5 * LOG2E), _pad_to(kn_sel, LANES)[None, :],
        _pad_to(kn_win, LANES)[None, :], b, t)

    def chunks(c0):
        a = pc[:, c0:c0 + g_ * d_].reshape(b, t, g_, d_).transpose(0, 2, 1, 3)
        return a.reshape(b * g_, nc, CMP_STRIDE * d_)

    pos8 = lambda p: jnp.broadcast_to(p.reshape(1, -1), (8, CMP_BLOCK * d_)).astype(md)
    k_c, v_c = _compress(chunks(0), chunks(g_ * d_), w1_k, w2_k.astype(md), pos8(pos_k),
                         w1_v, w2_v.astype(md), pos8(pos_v), kn_cmp[None, :])
    front = ((0, 0), (0, 0), (CMP_CHUNK, 0), (0, 0))
    k_c = jnp.pad(k_c.reshape(b, g_, nc, d_), front)
    v_c = v_c.reshape(b, g_, nc, d_).astype(F32)
    v_ca = jnp.pad(jnp.concatenate([v_c, jnp.ones_like(v_c[..., :1]),
                                    jnp.zeros_like(v_c[..., :LANES - d_ - 1])], axis=-1), front)
    oc_t, selneg = _cmp_sel(qt, k_c, v_ca, tabs["cmp"], cst["c2s"], cst["bk"], CMP_TQ)
    os_t = _flash(qt, ks_n, vs_a, tabs["sel"], use_far=True, name="nsa_selected",
                  qx=selneg, kx=cst["sel_onehot"].astype(md))
    ow_t = _flash(qt, kw_n, vw_a, tabs["win"], use_far=False, name="nsa_window")

    flat = lambda a: a.reshape(b, NSA_HEADS * d_, t)
    xf = _out_proj(o_mla, o_ret, flat(oc_t), flat(os_t), flat(ow_t), gates, w_out, xf, t)
    return _ffn(xf, ffn_norm[None, :], ffn_w_up, ffn_conv_w, ffn_conv_b[None, :], ffn_w_down, b, t)


def _bias_tables(rel_bias, cst):
    n = ATT_TILE
    lut = rel_bias[cst["bucket"]].T
    delta = (lut - rel_bias[REL_BUCKETS - 1][:, None]) * LOG2E
    hh = delta.shape[0]
    dn = jnp.concatenate([delta, jnp.zeros((hh, n - LANES), F32)], axis=1)
    neg = jnp.full((hh, n), NEG_INF, F32)
    zero = jnp.zeros((hh, n), F32)
    diag = _toeplitz(jnp.concatenate([dn, neg], axis=1), n)
    prev_sel = _toeplitz(jnp.concatenate([zero, dn], axis=1), n)
    off = CMP_STRIDE * CMP_CHUNK - CMP_TQ - (CMP_BLOCK - 1)
    d_max = off + CMP_TQ - 1
    d_min = off - CMP_STRIDE * (CMP_CHUNK - 1)
    by_dist = jnp.concatenate([delta, jnp.zeros((hh, d_max + 1 - LANES), F32),
                               jnp.full((hh, -d_min), NEG_INF, F32)], axis=1)
    cmp_tab = _skew(by_dist, CMP_CHUNK, CMP_STRIDE, off, CMP_TQ)
    masked = jnp.full((hh, n, n), NEG_INF, F32)
    nw = WIN_TILE
    back = WINDOW // nw
    dnw, negw, zerow = dn[:, :nw], neg[:, :nw], zero[:, :nw]
    win = [_toeplitz(jnp.concatenate([negw if r == back else zerow, dnw if r == 1 else zerow], axis=1), nw)
           for r in range(back, 0, -1)]
    win += [_toeplitz(jnp.concatenate([dnw, negw], axis=1), nw), masked[:, :nw, :nw]]
    cmp_tab = cmp_tab.reshape(NSA_KV_HEADS, NSA_GROUP, CMP_CHUNK, CMP_TQ).transpose(0, 2, 1, 3).reshape(
        NSA_KV_HEADS, CMP_CHUNK, NSA_GROUP * CMP_TQ)
    return {"sel": jnp.stack([prev_sel, diag, masked], axis=1), "win": jnp.stack(win, axis=1),
            "cmp": cmp_tab, "causal": jnp.asarray(cst["causal_tab"])}


def kernel(x, w_in, w_out, attn_norm, ffn_norm, mla_q_a_norm, mla_w_uq, mla_kv_a_norm, mla_w_ukv, mla_q_norm, mla_k_norm, ret_norm, nsa_cmp_pos_k, nsa_cmp_w1_k, nsa_cmp_w2_k, nsa_cmp_pos_v, nsa_cmp_w1_v, nsa_cmp_w2_v, nsa_q_norm, nsa_k_norm_cmp, nsa_k_norm_sel, nsa_k_norm_win, rel_bias, ffn_w_up, ffn_conv_w, ffn_conv_b, ffn_w_down):
    b, t, d = x.shape
    assert d == D_MODEL and t % (2 * ATT_TILE) == 0 and WINDOW % WIN_TILE == 0 and WIN_TILE >= LANES
    cst = _constants(t)
    tabs = _bias_tables(rel_bias, cst)
    stacked = (w_in, w_out, nsa_cmp_w1_k, nsa_cmp_w1_v, ffn_w_up, ffn_w_down)
    per_layer = (attn_norm, ffn_norm, mla_q_a_norm, mla_w_uq, mla_kv_a_norm, mla_w_ukv,
                 mla_q_norm, mla_k_norm, ret_norm, nsa_cmp_pos_k, nsa_cmp_w2_k,
                 nsa_cmp_pos_v, nsa_cmp_w2_v, nsa_q_norm, nsa_k_norm_cmp,
                 nsa_k_norm_sel, nsa_k_norm_win, ffn_conv_w, ffn_conv_b)
    xf = x.reshape(b * t, d)
    for l in range(w_in.shape[0]):
        big = [_to_mxu_dtype(w, l) for w in stacked]
        xf = _layer(xf, b, t, cst, tabs, *big, *[p[l] for p in per_layer])
    return xf.reshape(b, t, d)
```

```python
import functools
import math

import numpy as np
import jax
import jax.numpy as jnp
from jax import lax
from jax.experimental import pallas as pl
from jax.experimental.pallas import tpu as pltpu

D_MODEL = 1024
DEPTH = 2
MLA_HEADS = 6
MLA_Q_RANK = 192
MLA_KV_RANK = 128
MLA_NOPE = 64
MLA_ROPE = 32
MLA_V = 64
MLA_QK = MLA_NOPE + MLA_ROPE
RET_HEADS = 4
RET_DK = 64
RET_DV = 64
RET_CHUNK = 128
NSA_HEADS = 6
NSA_KV_HEADS = 2
NSA_GROUP = NSA_HEADS // NSA_KV_HEADS
NSA_DH = 64
CMP_BLOCK = 32
CMP_STRIDE = 16
CMP_HIDDEN = 256
SEL_BLOCK = 64
SEL_TOPK = 16
SEL_LOCAL = 2
WINDOW = 512
REL_BUCKETS = 32
REL_MAX_DIST = 128
D_FF = 2816
ROPE_BASE = 10000.0
EPS = 1e-6
NEG_INF = -1e30
FORCE = 1e9

_IN_SPLITS = (MLA_Q_RANK, MLA_KV_RANK, MLA_ROPE,
              RET_HEADS * RET_DK, RET_HEADS * RET_DK, RET_HEADS * RET_DV, RET_HEADS * RET_DV,
              NSA_HEADS * NSA_DH) + (NSA_KV_HEADS * NSA_DH,) * 6 + (3 * NSA_HEADS,)
_IN_OFF = [0] + [int(v) for v in np.cumsum(_IN_SPLITS)]
D_IN = _IN_OFF[-1]

LANES = 128
ATT_TILE = 512
CMP_TQ = 512
CMP_CHUNK = 128
WIN_TILE = 512
VMEM_LIMIT = 56 * 1024 * 1024

_MXU_DTYPE = jnp.bfloat16
F32 = jnp.float32
LOG2E = math.log2(math.e)


def _cparams(*sem):
    return pltpu.CompilerParams(dimension_semantics=sem, vmem_limit_bytes=VMEM_LIMIT)


def _dot(a, b):
    return jnp.dot(a, b, preferred_element_type=F32)


def _sigmoid(x):
    return 1.0 / (1.0 + jnp.exp(-x))


def _cast_kernel(x_ref, o_ref):
    o_ref[...] = x_ref[0].astype(o_ref.dtype)


def _to_mxu_dtype(w, layer, rows=256):
    _, r, c = w.shape
    return pl.pallas_call(
        _cast_kernel,
        grid=(r // rows,),
        in_specs=[pl.BlockSpec((1, rows, c), lambda i: (layer, i, 0))],
        out_specs=pl.BlockSpec((rows, c), lambda i: (i, 0)),
        out_shape=jax.ShapeDtypeStruct((r, c), _MXU_DTYPE),
        compiler_params=_cparams("parallel"),
        name="weight_cast",
    )(w)


ONES_ROWS = 16


def _ones_rows(n, dtype):
    row = lax.broadcasted_iota(jnp.int32, (ONES_ROWS, n), 0)
    return jnp.where(row == 0, 1.0, 0.0).astype(dtype)


def _resident(shape):
    nd = len(shape)
    return pl.BlockSpec(shape, lambda *_: (0,) * nd, pipeline_mode=pl.Buffered(1))


def _in_proj_kernel(x_ref, g_ref, w_ref, *o_refs, widths, copy):
    x = x_ref[...]
    y = x * lax.rsqrt(jnp.mean(x * x, axis=-1, keepdims=True) + EPS)
    xn = (y * g_ref[...]).astype(_MXU_DTYPE)
    slab, c0, cw = copy
    off = 0
    for j, (o_ref, wd) in enumerate(zip(o_refs, widths)):
        res = _dot(xn, w_ref[:, off:off + wd])
        o_ref[...] = res
        if j == slab:
            o_refs[-1][...] = res[:, c0:c0 + cw].astype(o_refs[-1].dtype)
        off += wd


def _in_proj(x, g, w, widths, copy, tm=256):
    m, d = x.shape
    n = w.shape[1]
    return pl.pallas_call(
        functools.partial(_in_proj_kernel, widths=widths, copy=copy),
        grid=(m // tm,),
        in_specs=[pl.BlockSpec((tm, d), lambda i: (i, 0)),
                  _resident((1, d)),
                  _resident((d, n))],
        out_specs=[pl.BlockSpec((tm, wd), lambda i: (i, 0)) for wd in widths + (copy[2],)],
        out_shape=[jax.ShapeDtypeStruct((m, wd), F32) for wd in widths]
        + [jax.ShapeDtypeStruct((m, copy[2]), _MXU_DTYPE)],
        compiler_params=_cparams("parallel"),
        name="in_proj",
    )(x, g, w)


def _mla_prep_kernel(pm_ref, cos_ref, sin_ref, gqa_ref, gkva_ref, wq_ref, wkv_ref,
                     gq_ref, gqr_ref, gk_ref, gkr_ref, q_ref, k_ref, v_ref):
    pm = pm_ref[...]
    hs = MLA_HEADS * LANES
    cq = pm[:, 0:256]
    r = lax.rsqrt(jnp.sum(cq * cq, axis=-1, keepdims=True) * (1.0 / MLA_Q_RANK) + EPS)
    qq = _dot((cq * r * gqa_ref[...]).astype(_MXU_DTYPE), wq_ref[...])
    ckv = pm[:, 256:384]
    r = lax.rsqrt(jnp.mean(ckv * ckv, axis=-1, keepdims=True) + EPS)
    kv = _dot((ckv * r * gkva_ref[...]).astype(_MXU_DTYPE), wkv_ref[...])
    kpe = pm[:, 384:512]
    kpe_rot = pm[:, 512:640]
    cos = cos_ref[...]
    sin = sin_ref[...]
    scale = MLA_QK ** -0.5 * LOG2E
    aq = cos * gq_ref[...] * scale
    bq = sin * gqr_ref[...] * scale
    ak = cos * gk_ref[...]
    bk = sin * gkr_ref[...]
    for h in range(MLA_HEADS):
        sl = slice(h * LANES, (h + 1) * LANES)
        sr = slice(hs + h * LANES, hs + (h + 1) * LANES)
        qh = qq[:, sl]
        rq = lax.rsqrt(jnp.sum(qh * qh, axis=-1, keepdims=True) * (1.0 / MLA_QK) + EPS)
        q_ref[0, h] = ((qh * aq + qq[:, sr] * bq) * rq).T.astype(q_ref.dtype)
        kh = kv[:, sl] + kpe
        rk = lax.rsqrt(jnp.sum(kh * kh, axis=-1, keepdims=True) * (1.0 / MLA_QK) + EPS)
        k_ref[0, h] = ((kh * ak + kpe_rot * bk) * rk).astype(k_ref.dtype)
        v_ref[0, h, 0:MLA_V, :] = kv[:, sr].T[0:MLA_V].astype(v_ref.dtype)
        v_ref[0, h, MLA_V:MLA_V + ONES_ROWS, :] = _ones_rows(pm.shape[0], v_ref.dtype)


def _mla_prep(pm, cos, sin, gqa, gkva, wq, wkv, gq, gqr, gk, gkr, b, t, tm=512):
    nt = t // tm
    hs = MLA_HEADS * LANES
    vec = lambda n: _resident((1, n))
    return pl.pallas_call(
        _mla_prep_kernel,
        grid=(b, nt),
        in_specs=[pl.BlockSpec((tm, 640), lambda bi, i: (bi * nt + i, 0)),
                  pl.BlockSpec((tm, LANES), lambda bi, i: (i, 0)),
                  pl.BlockSpec((tm, LANES), lambda bi, i: (i, 0)),
                  vec(256), vec(LANES), _resident((256, 2 * hs)), _resident((LANES, 2 * hs)),
                  vec(LANES), vec(LANES), vec(LANES), vec(LANES)],
        out_specs=[pl.BlockSpec((1, MLA_HEADS, LANES, tm), lambda bi, i: (bi, 0, 0, i)),
                   pl.BlockSpec((1, MLA_HEADS, tm, LANES), lambda bi, i: (bi, 0, i, 0)),
                   pl.BlockSpec((1, MLA_HEADS, MLA_V + ONES_ROWS, tm), lambda bi, i: (bi, 0, 0, i))],
        out_shape=[jax.ShapeDtypeStruct((b, MLA_HEADS, LANES, t), _MXU_DTYPE),
                   jax.ShapeDtypeStruct((b, MLA_HEADS, t, LANES), _MXU_DTYPE),
                   jax.ShapeDtypeStruct((b, MLA_HEADS, MLA_V + ONES_ROWS, t), _MXU_DTYPE)],
        compiler_params=_cparams("parallel", "parallel"),
        name="mla_prep",
    )(pm, cos, sin, gqa, gkva, wq, wkv, gq, gqr, gk, gkr)


def _tile_plan(nq, use_far):
    if use_far and nq % 2 == 0:
        return [lambda j: j, lambda j: nq - 1 - j]
    n = next(c for c in (4, 2, 1) if nq % c == 0) if not use_far else 1
    return [functools.partial(lambda j, s: n * j + s, s=s) for s in range(n)]


def _paired_position(i, nq):
    return jnp.where(i < nq // 2, 2 * i, 2 * (nq - 1 - i) + 1)


def _flash_kernel(*refs, use_far, sub, dv, extra, tiles, tq, near):
    ns = len(tiles)
    q_refs, refs = refs[:ns], refs[ns:]
    if extra:
        qx_refs, refs = refs[:ns], refs[ns:]
        k_ref, kx_ref, v_ref = refs[:3]
        refs = refs[3:]
    else:
        k_ref, v_ref = refs[:2]
        refs = refs[2:]
    tab_refs, o_ref = refs[:ns * near], refs[ns * near]
    step = pl.program_id(2)
    tc = tq

    def block(q, table, start, nkeys, m, acc):
        kb = k_ref[0, 0, pl.ds(start, nkeys), :]
        if extra:
            kb = jnp.concatenate([kb, kx_ref[pl.ds(start, nkeys), :]], axis=1)
        s = _dot(kb, q)
        if table is not None:
            s = s + table
        for j in range(nkeys // sub):
            sj = s[j * sub:(j + 1) * sub]
            vc = v_ref[0, 0, :, pl.ds(pl.multiple_of(start + j * sub, sub), sub)]
            m_new = jnp.maximum(m, jnp.max(sj, axis=0, keepdims=True))
            p = jnp.exp2(sj - m_new).astype(_MXU_DTYPE)
            acc = jnp.exp2(m - m_new) * acc + _dot(vc, p)
            m = m_new
        return m, acc

    state = []
    for slot in range(ns):
        i = tiles[slot](step)
        q = q_refs[slot][0, 0]
        if extra:
            q = jnp.concatenate([q, qx_refs[slot][0, 0]], axis=0)
        m = jnp.full((1, tq), -3e38, F32)
        acc = jnp.zeros((dv + ONES_ROWS, tq), F32)
        n_far = jnp.maximum(i - (near - 1), 0)
        if use_far:
            done = 0
            for width in (8, 4, 2, 1):
                count = (n_far - done) // width

                def body(c, carry, width=width, base=done, q=q):
                    start = pl.multiple_of((base + c * width) * tc, tc)
                    return block(q, None, start, width * tc, *carry)

                m, acc = lax.fori_loop(0, count, body, (m, acc))
                done = done + count * width
        state.append((q, n_far, m, acc))
    for slot, (q, n_far, m, acc) in enumerate(state):
        table = jnp.concatenate([tab_refs[slot * near + p][0, 0] for p in range(near)], axis=0)
        m, acc = block(q, table, pl.multiple_of(n_far * tc, tc), near * tc, m, acc)
        o_ref[0, 0, :, slot * tq:(slot + 1) * tq] = acc[0:dv] / acc[dv:dv + 1]


def _flash(qt, k, va, tab, *, use_far, name, qx=None, kx=None, sub=256):
    b, h, dk, t = qt.shape
    hk, dva = va.shape[1], va.shape[2]
    dv = dva - ONES_ROWS
    rep = h // hk
    ht, near, tq = tab.shape[0], tab.shape[1] - 1, tab.shape[3]
    extra = qx is not None
    tiles = _tile_plan(t // tq, use_far)
    ns = len(tiles)
    q_specs = [pl.BlockSpec((1, 1, dk, tq), lambda bi, hi, j, f=f: (bi, hi, 0, f(j))) for f in tiles]

    def piece(f, p):
        return lambda bi, hi, j: (hi % ht, jnp.minimum(p - jnp.minimum(f(j), near - 1) + near - 1, near),
                                  0, 0)

    tab_specs = [pl.BlockSpec((1, 1, tq, tq), piece(f, p)) for f in tiles for p in range(near)]
    k_spec = pl.BlockSpec((1, 1, t, dk), lambda bi, hi, j: (bi, hi // rep, 0, 0))
    v_spec = pl.BlockSpec((1, 1, dva, t), lambda bi, hi, j: (bi, hi // rep, 0, 0))
    if extra:
        nx = qx.shape[2]
        qx_specs = [pl.BlockSpec((1, 1, nx, tq), lambda bi, hi, j, f=f: (bi, hi // rep, 0, f(j)))
                    for f in tiles]
        in_specs = q_specs + qx_specs + [k_spec, _resident((t, nx)), v_spec] + tab_specs
        args = (qt,) * ns + (qx,) * ns + (k, kx, va) + (tab,) * (ns * near)
    else:
        in_specs = q_specs + [k_spec, v_spec] + tab_specs
        args = (qt,) * ns + (k, va) + (tab,) * (ns * near)
    return pl.pallas_call(
        functools.partial(_flash_kernel, use_far=use_far, sub=min(sub, tq), dv=dv, extra=extra,
                          tiles=tiles, tq=tq, near=near),
        grid=(b, h, t // (tq * ns)),
        in_specs=in_specs,
        out_specs=pl.BlockSpec((1, 1, dv, ns * tq), lambda bi, hi, j: (bi, hi, 0, j)),
        out_shape=jax.ShapeDtypeStruct((b, h, dv, t), F32),
        compiler_params=_cparams("parallel", "parallel", "parallel"),
        name=name,
    )(*args)


def _ret_kernel(q_ref, k_ref, v_ref, g_ref, cos_ref, sin_ref, din_ref, qd_ref,
                kd_ref, cdm_ref, bd_ref, gn_ref, o_ref, state_ref, *, nchunk):
    @pl.when(pl.program_id(1) == 0)
    def _():
        state_ref[...] = jnp.zeros(state_ref.shape, F32)

    w = RET_HEADS * RET_DK
    lane = lax.broadcasted_iota(jnp.int32, (1, w), 1)
    heads = [(lane >= h * RET_DK) & (lane < (h + 1) * RET_DK) for h in range(RET_HEADS)]
    on_diag = bd_ref[...] > 0.5
    first_half = (lane & (RET_DK - 1)) < RET_DK // 2

    def rope(x, cos, sin):
        rot = jnp.where(first_half, -pltpu.roll(x, w - RET_DK // 2, 1), pltpu.roll(x, RET_DK // 2, 1))
        return x * cos + rot * sin

    c_ = RET_CHUNK
    for c in range(nchunk):
        sl = slice(c * c_, (c + 1) * c_)
        cos = cos_ref[sl, :]
        sin = sin_ref[sl, :]
        qh = rope(q_ref[sl, :], cos, sin)
        kh = rope(k_ref[sl, :], cos, sin) * (RET_DK ** -0.5)
        kb = kh.astype(_MXU_DTYPE)
        vb = v_ref[sl, :].astype(_MXU_DTYPE)
        st = state_ref[...]
        out = _dot((qh * qd_ref[...]).astype(_MXU_DTYPE), st.astype(_MXU_DTYPE))
        for h in range(RET_HEADS):
            qm = jnp.where(heads[h], qh, 0.0).astype(_MXU_DTYPE)
            inner = lax.dot_general(qm, kb, (((1,), (1,)), ((), ())),
                                    preferred_element_type=F32) * din_ref[h]
            out = out + jnp.where(heads[h], _dot(inner.astype(_MXU_DTYPE), vb), 0.0)
        kdt = (kh * kd_ref[...]).T
        state_ref[...] = st * cdm_ref[...] + jnp.where(on_diag, _dot(kdt.astype(_MXU_DTYPE), vb), 0.0)
        o2 = out * out
        ms = jnp.zeros_like(out)
        for h in range(RET_HEADS):
            ssum = jnp.sum(jnp.where(heads[h], o2, 0.0), axis=-1, keepdims=True)
            ms = jnp.where(heads[h], ssum * (1.0 / RET_DV), ms)
        y = out * lax.rsqrt(ms + EPS) * gn_ref[...]
        gg = g_ref[sl, :]
        o_ref[sl, :] = (gg * _sigmoid(gg) * y).astype(o_ref.dtype)


def _retention(pr, cos, sin, din, qd, kd, cdm, bd, gn, b, t, tt=256):
    w = RET_HEADS * RET_DK
    nt = t // tt
    col = lambda j: pl.BlockSpec((tt, w), lambda bi, i: (bi * nt + i, j))
    pos = pl.BlockSpec((tt, w), lambda bi, i: (i, 0))
    return pl.pallas_call(
        functools.partial(_ret_kernel, nchunk=tt // RET_CHUNK),
        grid=(b, nt),
        in_specs=[col(0), col(1), col(2), col(3), pos, pos,
                  _resident(din.shape), _resident(qd.shape), _resident(kd.shape),
                  _resident(cdm.shape), _resident(bd.shape), _resident((1, w))],
        out_specs=pl.BlockSpec((tt, w), lambda bi, i: (bi * nt + i, 0)),
        out_shape=jax.ShapeDtypeStruct((b * t, w), _MXU_DTYPE),
        scratch_shapes=[pltpu.VMEM((w, w), F32)],
        compiler_params=_cparams("parallel", "arbitrary"),
        name="retention",
    )(pr, pr, pr, pr, cos, sin, din, qd, kd, cdm, bd, gn)


NSA_Q0, NSA_KC0, NSA_VC0, NSA_KS0, NSA_VS0, NSA_KW0, NSA_VW0, NSA_GATE0 = (
    0, 384, 512, 640, 896, 1024, 1280, 1408)
NSA_SLAB = 1536
GATE_STRIDE = 8


def _nsa_prep_kernel(pn_ref, gq_ref, gks_ref, gkw_ref, qt_ref, ks_ref, kw_ref, vs_ref, vw_ref,
                     gt_ref):
    tm = pn_ref.shape[0]
    d = NSA_DH
    dt = qt_ref.dtype
    xq = pn_ref[:, NSA_Q0:NSA_Q0 + NSA_HEADS * d].T
    for h in range(NSA_HEADS):
        blk = xq[h * d:(h + 1) * d]
        r = lax.rsqrt(jnp.mean(blk * blk, axis=0, keepdims=True) + EPS)
        qt_ref[0, h, 0:d, :] = (blk * r * gq_ref[...]).astype(dt)
        qt_ref[0, h, d:2 * d, :] = jnp.zeros((d, tm), dt)
    for g in range(NSA_KV_HEADS):
        for c0, g_ref, o_ref in ((NSA_KS0, gks_ref, ks_ref), (NSA_KW0, gkw_ref, kw_ref)):
            slot = pn_ref[:, c0 + g * LANES:c0 + (g + 1) * LANES]
            r = lax.rsqrt(jnp.sum(slot * slot, axis=-1, keepdims=True) * (1.0 / d) + EPS)
            o_ref[0, g] = (slot * r * g_ref[...]).astype(dt)
    for c0, o_ref in ((NSA_VS0, vs_ref), (NSA_VW0, vw_ref)):
        vt = pn_ref[:, c0:c0 + LANES].T
        for g in range(NSA_KV_HEADS):
            o_ref[0, g, 0:d, :] = vt[g * d:(g + 1) * d].astype(dt)
            o_ref[0, g, d:d + ONES_ROWS, :] = _ones_rows(tm, dt)
    gt = pn_ref[:, NSA_GATE0:NSA_GATE0 + LANES].T
    gt_ref[0] = _sigmoid(gt[0:NSA_HEADS * GATE_STRIDE])


def _nsa_prep(pn, gq, gks, gkw, b, t, tm=512):
    nt = t // tm
    d = NSA_DH
    g = NSA_KV_HEADS
    md = _MXU_DTYPE
    ch_major = lambda n, r: pl.BlockSpec((1, n, r, tm), lambda bi, i: (bi, 0, 0, i))
    natural = pl.BlockSpec((1, g, tm, LANES), lambda bi, i: (bi, 0, i, 0))
    return pl.pallas_call(
        _nsa_prep_kernel,
        grid=(b, nt),
        in_specs=[pl.BlockSpec((tm, NSA_SLAB), lambda bi, i: (bi * nt + i, 0)),
                  _resident((d, tm)), _resident((1, LANES)), _resident((1, LANES))],
        out_specs=[ch_major(NSA_HEADS, 2 * d), natural, natural,
                   ch_major(g, d + ONES_ROWS), ch_major(g, d + ONES_ROWS),
                   pl.BlockSpec((1, NSA_HEADS * GATE_STRIDE, tm), lambda bi, i: (bi, 0, i))],
        out_shape=[jax.ShapeDtypeStruct((b, NSA_HEADS, 2 * d, t), md),
                   jax.ShapeDtypeStruct((b, g, t, LANES), md),
                   jax.ShapeDtypeStruct((b, g, t, LANES), md),
                   jax.ShapeDtypeStruct((b, g, d + ONES_ROWS, t), md),
                   jax.ShapeDtypeStruct((b, g, d + ONES_ROWS, t), md),
                   jax.ShapeDtypeStruct((b, NSA_HEADS * GATE_STRIDE, t), F32)],
        compiler_params=_cparams("parallel", "parallel"),
        name="nsa_prep",
    )(pn, jnp.broadcast_to(gq[:, None], (d, tm)), gks, gkw)


def _gelu_tanh(x):
    return 0.5 * x * (1.0 + jnp.tanh(math.sqrt(2.0 / math.pi) * (x + 0.044715 * (x * x * x))))


def _compress_kernel(ak_ref, av_ref, w1k_ref, w2k_ref, pk_ref, w1v_ref, w2v_ref, pv_ref, gk_ref,
                     kc_ref, vc_ref):
    half = CMP_STRIDE * NSA_DH

    def comp(a_ref, w1_ref, w2_ref, p_ref):
        a = a_ref[0]
        pb = _dot(p_ref[...], w1_ref[...])[0:1]
        second = _dot(a, w1_ref[half:2 * half, :])
        nc = second.shape[0]
        hid = _dot(a, w1_ref[0:half, :]) + pltpu.roll(second, nc - 1, 0) + pb
        return _dot(_gelu_tanh(hid).astype(_MXU_DTYPE), w2_ref[...])

    kc = comp(ak_ref, w1k_ref, w2k_ref, pk_ref)
    y = kc * lax.rsqrt(jnp.mean(kc * kc, axis=-1, keepdims=True) + EPS)
    kc_ref[0] = (y * gk_ref[...]).astype(kc_ref.dtype)
    vc_ref[0] = comp(av_ref, w1v_ref, w2v_ref, pv_ref).astype(vc_ref.dtype)


def _compress(ak, av, w1k, w2k, pk, w1v, w2v, pv, gk):
    n, nc, kk = ak.shape
    blk = pl.BlockSpec((1, nc, kk), lambda i: (i, 0, 0))
    out = pl.BlockSpec((1, nc, NSA_DH), lambda i: (i, 0, 0))
    w1 = _resident((2 * kk, CMP_HIDDEN))
    w2 = _resident((CMP_HIDDEN, NSA_DH))
    pp = _resident((8, 2 * kk))
    return pl.pallas_call(
        _compress_kernel,
        grid=(n,),
        in_specs=[blk, blk, w1, w2, pp, w1, w2, pp, _resident((1, NSA_DH))],
        out_specs=[out, out],
        out_shape=[jax.ShapeDtypeStruct((n, nc, NSA_DH), _MXU_DTYPE)] * 2,
        compiler_params=_cparams("parallel"),
        name="nsa_compress",
    )(ak, av, w1k, w2k, pk, w1v, w2v, pv, gk)


def _cmp_sel_kernel(q_ref, kc_ref, vca_ref, tab_ref, c2s_ref, bk_ref, oc_ref, sel_ref,
                    m_ref, acco_ref, acci_ref, *, tq, ns, d):
    i = pl.program_id(2)
    ch = CMP_CHUNK
    end = (tq // CMP_STRIDE) * (i + 1)
    n_far = (end - 1) // ch
    rowi = lax.broadcasted_iota(jnp.int32, (ch, NSA_GROUP * tq), 0)
    qpos = tq * i + lax.broadcasted_iota(jnp.int32, (1, tq), 1)
    sees_any = jnp.where(qpos >= CMP_BLOCK - 1, 1.0, 0.0)

    q = jnp.concatenate([q_ref[0, r] for r in range(NSA_GROUP)], axis=1)
    m_ref[...] = jnp.full(m_ref.shape, -3e38, F32)
    acco_ref[...] = jnp.zeros(acco_ref.shape, F32)
    acci_ref[...] = jnp.zeros(acci_ref.shape, F32)

    def chunk(g, with_table):
        start = pl.multiple_of(end - ch * g, CMP_STRIDE)
        kc = kc_ref[0, 0, pl.ds(start, ch), :]
        vt = vca_ref[0, 0, pl.ds(start, ch), :].T[0:d + ONES_ROWS].astype(_MXU_DTYPE)
        ct = c2s_ref[pl.ds(start, ch), :].T.astype(_MXU_DTYPE)
        s = _dot(kc, q)
        if with_table:
            s = s + tab_ref[0]
        s = jnp.where(rowi >= ch * (g + 1) - end, s, NEG_INF)
        m_old = m_ref[...]
        m_new = jnp.maximum(m_old, jnp.max(s, axis=0, keepdims=True))
        alpha = jnp.exp2(m_old - m_new)
        e = jnp.exp2(s - m_new).astype(_MXU_DTYPE)
        acco_ref[...] = alpha * acco_ref[...] + _dot(vt, e)
        acci_ref[...] = alpha * acci_ref[...] + _dot(ct, e)
        m_ref[...] = m_new

    chunk(0, True)

    def far(g, carry):
        chunk(g, False)
        return carry

    lax.fori_loop(1, n_far + 1, far, 0)

    imp = jnp.zeros((ns, tq), F32)
    for r in range(NSA_GROUP):
        lanes = slice(r * tq, (r + 1) * tq)
        inv = sees_any / acco_ref[d:d + 1, lanes]
        oc_ref[0, r] = acco_ref[0:d, lanes] * inv
        imp = imp + acci_ref[:, lanes] * inv

    back = bk_ref[...] + i * (tq // SEL_BLOCK)
    jidx = lax.broadcasted_iota(jnp.int32, (ns, tq), 0)
    forced = (jidx == 0) | ((back >= 0) & (back < SEL_LOCAL))
    taken = -3e38

    def pick(imp, rounds, rows):
        imp = imp[0:rows]
        idx = lax.broadcasted_iota(jnp.int32, (rows, tq), 0).astype(F32)
        for _ in range(rounds):
            mx = jnp.max(imp, axis=0, keepdims=True)
            first = jnp.min(jnp.where(imp == mx, idx, 1e9), axis=0, keepdims=True)
            imp = jnp.where(idx == first, taken, imp)
        sel = jnp.where(imp == taken, 0.0, NEG_INF)
        if rows < ns:
            sel = jnp.concatenate([sel, jnp.full((ns - rows, tq), NEG_INF, F32)], axis=0)
        sel_ref[0, 0] = sel.astype(sel_ref.dtype)

    top_k = min(SEL_TOPK, ns)
    n_forced = 1 + SEL_LOCAL
    per_tile = tq // SEL_BLOCK
    half = ns // per_tile // 2

    @pl.when(i == 0)
    def _():
        pick(jnp.where(back >= 0, jnp.where(forced, FORCE, imp), NEG_INF), top_k,
             min(ns, max(16, per_tile)))

    later = jnp.where(forced, taken, jnp.where(back >= 0, imp, NEG_INF))

    @pl.when((i > 0) & (i < half))
    def _():
        pick(later, top_k - n_forced, max(16, ns // 2))

    @pl.when((i > 0) & (i >= half))
    def _():
        pick(later, top_k - n_forced, ns)


def _cmp_sel(qt, kc, vca, tab, c2s, bk, tq):
    b, h, _, t = qt.shape
    g, npad, d = kc.shape[1:]
    ns = c2s.shape[1]
    assert tq >= SEL_LOCAL * SEL_BLOCK and tq // CMP_STRIDE <= CMP_CHUNK
    return pl.pallas_call(
        functools.partial(_cmp_sel_kernel, tq=tq, ns=ns, d=d),
        grid=(b, g, t // tq),
        in_specs=[pl.BlockSpec((1, NSA_GROUP, d, tq), lambda bi, gi, i: (bi, gi, 0, i)),
                  pl.BlockSpec((1, 1, npad, d), lambda bi, gi, i: (bi, gi, 0, 0)),
                  pl.BlockSpec((1, 1, npad, LANES), lambda bi, gi, i: (bi, gi, 0, 0)),
                  pl.BlockSpec((1, CMP_CHUNK, NSA_GROUP * tq), lambda bi, gi, i: (gi, 0, 0)),
                  _resident(c2s.shape), _resident(bk.shape)],
        out_specs=[pl.BlockSpec((1, NSA_GROUP, d, tq), lambda bi, gi, i: (bi, gi, 0, i)),
                   pl.BlockSpec((1, 1, ns, tq), lambda bi, gi, i: (bi, gi, 0, i))],
        out_shape=[jax.ShapeDtypeStruct((b, h, d, t), F32),
                   jax.ShapeDtypeStruct((b, g, ns, t), _MXU_DTYPE)],
        scratch_shapes=[pltpu.VMEM((1, NSA_GROUP * tq), F32),
                        pltpu.VMEM((d + ONES_ROWS, NSA_GROUP * tq), F32),
                        pltpu.VMEM((ns, NSA_GROUP * tq), F32)],
        compiler_params=_cparams("parallel", "parallel", "parallel"),
        name="nsa_cmp_sel",
    )(qt, kc, vca, tab, c2s, bk)


def _out_proj_kernel(mla_ref, ret_ref, oc_ref, os_ref, ow_ref, g_ref, w_ref, r_ref, o_ref):
    d = NSA_DH
    w_mla, w_ret = MLA_HEADS * MLA_V, RET_HEADS * RET_DV
    acc = r_ref[...] + _dot(mla_ref[0].T.astype(_MXU_DTYPE), w_ref[0:w_mla, :])
    acc = acc + _dot(ret_ref[...], w_ref[w_mla:w_mla + w_ret, :])
    g = g_ref[0]
    nsa = []
    for h in range(NSA_HEADS):
        rows = slice(h * d, (h + 1) * d)
        g0 = h * GATE_STRIDE
        nsa.append(g[g0:g0 + 1] * oc_ref[0, rows, :] + g[g0 + 1:g0 + 2] * os_ref[0, rows, :]
                   + g[g0 + 2:g0 + 3] * ow_ref[0, rows, :])
    nsa = jnp.concatenate(nsa, axis=0).T.astype(_MXU_DTYPE)
    o_ref[...] = acc + _dot(nsa, w_ref[w_mla + w_ret:, :])


def _out_proj(o_mla, o_ret, oc, os_, ow, gates, w, res, t, tm=512):
    m, n = res.shape
    assert tm == ATT_TILE
    nt = t // tm
    ch_major = lambda c: pl.BlockSpec((1, c, tm), lambda i: (i // nt, 0, i % nt))
    far_order = len(_tile_plan(nt, True)) == 2
    ch_paired = (lambda c: pl.BlockSpec((1, c, tm), lambda i: (i // nt, 0, _paired_position(i % nt, nt)))
                 ) if far_order else ch_major
    tokens = lambda c: pl.BlockSpec((tm, c), lambda i: (i, 0))
    c_nsa = oc.shape[1]
    return pl.pallas_call(
        _out_proj_kernel,
        grid=(m // tm,),
        in_specs=[ch_paired(o_mla.shape[1]), tokens(o_ret.shape[1]), ch_major(c_nsa), ch_paired(c_nsa),
                  ch_major(c_nsa), ch_major(gates.shape[1]), _resident(w.shape), tokens(n)],
        out_specs=tokens(n),
        out_shape=jax.ShapeDtypeStruct((m, n), F32),
        compiler_params=_cparams("parallel"),
        name="out_proj",
    )(o_mla, o_ret, oc, os_, ow, gates, w, res)


def _ffn_kernel(x_ref, gn_ref, wup_ref, cw_ref, cb_ref, wdn_ref, o_ref, hbuf_ref, acc_ref, *,
                tm, fc):
    @pl.when(pl.program_id(1) == 0)
    def _():
        hbuf_ref[0:8, :] = jnp.zeros((8, hbuf_ref.shape[1]), F32)

    x = x_ref[...]
    y = x * lax.rsqrt(jnp.mean(x * x, axis=-1, keepdims=True) + EPS)
    xn = (y * gn_ref[...]).astype(_MXU_DTYPE)

    def up_proj(col0):
        cols = slice(col0, col0 + fc)
        hbuf_ref[8:tm + 8, cols] = _dot(xn, wup_ref[:, cols])

    def conv(col0):
        cols = slice(col0, col0 + fc)
        w = cw_ref[:, cols]
        h = hbuf_ref[8:tm + 8, cols]
        out = (h * w[2:3] + hbuf_ref[7:tm + 7, cols] * w[1:2] + hbuf_ref[6:tm + 6, cols] * w[0:1]
               + cb_ref[:, cols])
        hbuf_ref[0:8, cols] = hbuf_ref[tm:tm + 8, cols]
        return out

    nf = D_FF // fc
    ahead = 5
    for f in range(min(ahead, nf)):
        up_proj(f * fc)
        up_proj(D_FF + f * fc)
    for f in range(nf):
        if f + ahead < nf:
            up_proj((f + ahead) * fc)
            up_proj(D_FF + (f + ahead) * fc)
        gate = conv(f * fc)
        up = conv(D_FF + f * fc)
        act = (gate * _sigmoid(gate) * up).astype(_MXU_DTYPE)
        contrib = _dot(act, wdn_ref[f * fc:(f + 1) * fc, :])
        if f == 0:
            acc_ref[...] = contrib
        else:
            acc_ref[...] += contrib
    o_ref[...] = x + acc_ref[...]


def _ffn(x, gn, wup, cw, cb, wdn, b, t, tm=512, fc=256):
    d = x.shape[1]
    nt = t // tm
    return pl.pallas_call(
        functools.partial(_ffn_kernel, tm=tm, fc=fc),
        grid=(b, nt),
        in_specs=[pl.BlockSpec((tm, d), lambda bi, i: (bi * nt + i, 0)),
                  _resident((1, d)), _resident(wup.shape), _resident(cw.shape),
                  _resident(cb.shape), _resident(wdn.shape)],
        out_specs=pl.BlockSpec((tm, d), lambda bi, i: (bi * nt + i, 0)),
        out_shape=jax.ShapeDtypeStruct(x.shape, F32),
        scratch_shapes=[pltpu.VMEM((tm + 8, 2 * D_FF), F32), pltpu.VMEM((tm, d), F32)],
        compiler_params=_cparams("parallel", "arbitrary"),
        name="conv_ffn",
    )(x, gn, wup, cw, cb, wdn)


def _rope_tables(t, d):
    inv = ROPE_BASE ** (-np.arange(0, d, 2, dtype=np.float64) / d)
    ang = np.arange(t, dtype=np.float64)[:, None] * inv[None, :]
    return (np.concatenate([np.cos(ang)] * 2, axis=1), np.concatenate([np.sin(ang)] * 2, axis=1))


def _t5_bucket_np(dist):
    max_exact = REL_BUCKETS // 2
    d = np.maximum(dist, 1).astype(np.float64)
    log_b = max_exact + (np.log(d / max_exact) / math.log(REL_MAX_DIST / max_exact)
                         * (REL_BUCKETS - max_exact)).astype(np.int32)
    return np.where(dist < max_exact, dist, np.minimum(log_b, REL_BUCKETS - 1))


@functools.lru_cache(maxsize=None)
def _constants(t):
    c = {}
    cos, sin = _rope_tables(t, MLA_ROPE)
    pad = LANES - MLA_QK
    c["mla_cos"] = np.concatenate([np.ones((t, MLA_NOPE)), cos, np.ones((t, pad))], 1).astype(np.float32)
    c["mla_sin"] = np.concatenate([np.zeros((t, MLA_NOPE)), sin, np.zeros((t, pad))], 1).astype(np.float32)
    cos, sin = _rope_tables(t, RET_DK)
    c["ret_cos"] = np.tile(cos, (1, RET_HEADS)).astype(np.float32)
    c["ret_sin"] = np.tile(sin, (1, RET_HEADS)).astype(np.float32)
    lg = np.log(1.0 - 2.0 ** (-5.0 - np.arange(RET_HEADS, dtype=np.float64)))
    idx = np.arange(RET_CHUNK, dtype=np.float64)
    diff = idx[:, None] - idx[None, :]
    c["ret_din"] = (np.exp(np.maximum(diff, 0.0) * lg[:, None, None]) * (diff >= 0)).astype(np.float32)
    qd = np.exp((idx[:, None] + 1.0) * lg[None, :])
    kd = np.exp((RET_CHUNK - 1.0 - idx[:, None]) * lg[None, :])
    c["ret_qd"] = np.repeat(qd, RET_DK, axis=1).astype(np.float32)
    c["ret_kd"] = np.repeat(kd, RET_DK, axis=1).astype(np.float32)
    head_of = np.arange(RET_HEADS * RET_DK) // RET_DK
    c["ret_cdm"] = np.broadcast_to(np.exp(RET_CHUNK * lg)[head_of][:, None],
                                   (RET_HEADS * RET_DK, RET_HEADS * RET_DV)).astype(np.float32)
    c["ret_bd"] = (head_of[:, None] == head_of[None, :]).astype(np.float32)
    c["bucket"] = _t5_bucket_np(np.arange(LANES)).astype(np.int32)
    kk = np.arange(ATT_TILE)[:, None]
    qq = np.arange(ATT_TILE)[None, :]
    causal = np.where(qq >= kk, 0.0, NEG_INF)
    c["causal_tab"] = np.stack([np.zeros_like(causal), causal,
                                np.full_like(causal, NEG_INF)])[None].astype(np.float32)
    nc, ns = t // CMP_STRIDE, t // SEL_BLOCK
    n_cmp = (t - CMP_BLOCK) // CMP_STRIDE + 1
    c_start = np.arange(nc) * CMP_STRIDE
    s_start = np.arange(ns) * SEL_BLOCK
    overlap = np.clip(np.minimum(c_start[:, None] + CMP_BLOCK, s_start[None, :] + SEL_BLOCK)
                      - np.maximum(c_start[:, None], s_start[None, :]), 0, None).astype(np.float64)
    overlap[n_cmp:] = 0.0
    c["c2s"] = np.concatenate([np.zeros((CMP_CHUNK, ns)), overlap / CMP_BLOCK]).astype(np.float32)
    q = np.arange(CMP_TQ)[None, :]
    c["bk"] = (q // SEL_BLOCK - np.arange(ns)[:, None]).astype(np.int32)
    c["sel_onehot"] = (np.arange(t)[:, None] // SEL_BLOCK == np.arange(ns)[None, :]).astype(np.float32)
    return c


def _cols(w, pieces):
    out = []
    for p in pieces:
        if p[0] is None:
            out.append(jnp.zeros((w.shape[0], p[1]), w.dtype))
        else:
            blk = w[:, p[0]:p[1]]
            out.append(-blk if p[2] < 0 else blk)
    return jnp.concatenate(out, axis=1)


def _rot_pieces(base, d):
    return [(base + d // 2, base + d, -1), (base, base + d // 2, 1)]


def _skew(w, rows, step, col0, cols):
    hh, ll = w.shape
    flat = jnp.tile(w, (1, rows))[:, :rows * (ll - step)]
    return flat.reshape(hh, rows, ll - step)[:, :, col0:col0 + cols]


def _toeplitz(w, n):
    return _skew(w, n, 1, 0, n)


def _pad_to(v, n):
    return jnp.concatenate([v, jnp.zeros((n - v.shape[0],), v.dtype)])


def _layer(xf, b, t, cst, tabs, w_in, w_out, w1_k, w1_v, ffn_w_up, ffn_w_down,
           attn_norm, ffn_norm, mla_q_a_norm, mla_w_uq, mla_kv_a_norm, mla_w_ukv, mla_q_norm,
           mla_k_norm, ret_norm, pos_k, w2_k, pos_v, w2_v, nsa_q_norm, kn_cmp, kn_sel, kn_win,
           ffn_conv_w, ffn_conv_b):
    md = _MXU_DTYPE
    o = _IN_OFF
    pieces = [(o[0], o[1], 1), (None, 64), (o[1], o[2], 1),
              (None, 64), (o[2], o[3], 1), (None, 32),
              (None, 64)] + _rot_pieces(o[2], MLA_ROPE) + [(None, 32)]
    pieces += [(o[3], o[7], 1)]
    d_ = NSA_DH
    pieces += [(o[7], o[10], 1)]
    for base in (o[10], o[12]):
        pieces += [(base, base + d_, 1), (None, LANES - d_), (base + d_, base + 2 * d_, 1),
                   (None, LANES - d_), (base + 2 * d_, base + 4 * d_, 1)]
    for h in range(NSA_HEADS):
        pieces += [(o[14] + 3 * h, o[14] + 3 * h + 3, 1), (None, GATE_STRIDE - 3)]
    pieces += [(None, LANES - NSA_HEADS * GATE_STRIDE)]
    w_in_r = _cols(w_in, pieces)
    pm, pr, pn, pc = _in_proj(xf, attn_norm[None, :], w_in_r, (640, 1024, NSA_SLAB),
                              (2, NSA_KC0, 2 * NSA_KV_HEADS * NSA_DH))

    wq_pieces, wq_rot = [], []
    wkv_k, wkv_v = [], []
    for h in range(MLA_HEADS):
        qb = h * MLA_QK
        wq_pieces += [(qb, qb + MLA_QK, 1), (None, LANES - MLA_QK)]
        wq_rot += [(None, MLA_NOPE)] + _rot_pieces(qb + MLA_NOPE, MLA_ROPE) + [(None, LANES - MLA_QK)]
        kb = h * (MLA_NOPE + MLA_V)
        wkv_k += [(kb, kb + MLA_NOPE, 1), (None, LANES - MLA_NOPE)]
        wkv_v += [(kb + MLA_NOPE, kb + MLA_NOPE + MLA_V, 1), (None, LANES - MLA_V)]
    wq = _cols(mla_w_uq, wq_pieces + wq_rot)
    wq = jnp.concatenate([wq, jnp.zeros((256 - MLA_Q_RANK, wq.shape[1]), wq.dtype)], axis=0).astype(md)
    wkv = _cols(mla_w_ukv, wkv_k + wkv_v).astype(md)
    half = MLA_ROPE // 2

    def rot_gain(gv):
        return jnp.concatenate([jnp.zeros((MLA_NOPE,), gv.dtype), gv[MLA_NOPE + half:],
                                gv[MLA_NOPE:MLA_NOPE + half], jnp.zeros((LANES - MLA_QK,), gv.dtype)])

    q, k, v = _mla_prep(pm, cst["mla_cos"], cst["mla_sin"], _pad_to(mla_q_a_norm, 256)[None, :],
                        mla_kv_a_norm[None, :], wq, wkv,
                        _pad_to(mla_q_norm, LANES)[None, :], rot_gain(mla_q_norm)[None, :],
                        _pad_to(mla_k_norm, LANES)[None, :], rot_gain(mla_k_norm)[None, :], b, t)
    o_mla = _flash(q, k, v, tabs["causal"], use_far=True, name="mla_attention")
    o_mla = o_mla.reshape(b, MLA_HEADS * MLA_V, t)

    o_ret = _retention(pr, cst["ret_cos"], cst["ret_sin"], cst["ret_din"], cst["ret_qd"],
                       cst["ret_kd"], cst["ret_cdm"], cst["ret_bd"],
                       jnp.tile(ret_norm, RET_HEADS)[None, :], b, t)

    g_ = NSA_KV_HEADS
    nc = t // CMP_STRIDE
    qt, ks_n, kw_n, vs_a, vw_a, gates = _nsa_prep(
        pn, nsa_q_norm * (d_ ** -0.5 * LOG2E), _pad_to(kn_sel, LANES)[None, :],
        _pad_to(kn_win, LANES)[None, :], b, t)

    def chunks(c0):
        a = pc[:, c0:c0 + g_ * d_].reshape(b, t, g_, d_).transpose(0, 2, 1, 3)
        return a.reshape(b * g_, nc, CMP_STRIDE * d_)

    pos8 = lambda p: jnp.broadcast_to(p.reshape(1, -1), (8, CMP_BLOCK * d_)).astype(md)
    k_c, v_c = _compress(chunks(0), chunks(g_ * d_), w1_k, w2_k.astype(md), pos8(pos_k),
                         w1_v, w2_v.astype(md), pos8(pos_v), kn_cmp[None, :])
    front = ((0, 0), (0, 0), (CMP_CHUNK, 0), (0, 0))
    k_c = jnp.pad(k_c.reshape(b, g_, nc, d_), front)
    v_c = v_c.reshape(b, g_, nc, d_).astype(F32)
    v_ca = jnp.pad(jnp.concatenate([v_c, jnp.ones_like(v_c[..., :1]),
                                    jnp.zeros_like(v_c[..., :LANES - d_ - 1])], axis=-1), front)
    oc_t, selneg = _cmp_sel(qt, k_c, v_ca, tabs["cmp"], cst["c2s"], cst["bk"], CMP_TQ)
    os_t = _flash(qt, ks_n, vs_a, tabs["sel"], use_far=True, name="nsa_selected",
                  qx=selneg, kx=cst["sel_onehot"].astype(md))
    ow_t = _flash(qt, kw_n, vw_a, tabs["win"], use_far=False, name="nsa_window")

    flat = lambda a: a.reshape(b, NSA_HEADS * d_, t)
    xf = _out_proj(o_mla, o_ret, flat(oc_t), flat(os_t), flat(ow_t), gates, w_out, xf, t)
    return _ffn(xf, ffn_norm[None, :], ffn_w_up, ffn_conv_w, ffn_conv_b[None, :], ffn_w_down, b, t)


def _bias_tables(rel_bias, cst):
    n = ATT_TILE
    lut = rel_bias[cst["bucket"]].T
    delta = (lut - rel_bias[REL_BUCKETS - 1][:, None]) * LOG2E
    hh = delta.shape[0]
    dn = jnp.concatenate([delta, jnp.zeros((hh, n - LANES), F32)], axis=1)
    neg = jnp.full((hh, n), NEG_INF, F32)
    zero = jnp.zeros((hh, n), F32)
    diag = _toeplitz(jnp.concatenate([dn, neg], axis=1), n)
    prev_sel = _toeplitz(jnp.concatenate([zero, dn], axis=1), n)
    off = CMP_STRIDE * CMP_CHUNK - CMP_TQ - (CMP_BLOCK - 1)
    d_max = off + CMP_TQ - 1
    d_min = off - CMP_STRIDE * (CMP_CHUNK - 1)
    by_dist = jnp.concatenate([delta, jnp.zeros((hh, d_max + 1 - LANES), F32),
                               jnp.full((hh, -d_min), NEG_INF, F32)], axis=1)
    cmp_tab = _skew(by_dist, CMP_CHUNK, CMP_STRIDE, off, CMP_TQ)
    masked = jnp.full((hh, n, n), NEG_INF, F32)
    nw = WIN_TILE
    back = WINDOW // nw
    dnw, negw, zerow = dn[:, :nw], neg[:, :nw], zero[:, :nw]
    win = [_toeplitz(jnp.concatenate([negw if r == back else zerow, dnw if r == 1 else zerow], axis=1), nw)
           for r in range(back, 0, -1)]
    win += [_toeplitz(jnp.concatenate([dnw, negw], axis=1), nw), masked[:, :nw, :nw]]
    cmp_tab = cmp_tab.reshape(NSA_KV_HEADS, NSA_GROUP, CMP_CHUNK, CMP_TQ).transpose(0, 2, 1, 3).reshape(
        NSA_KV_HEADS, CMP_CHUNK, NSA_GROUP * CMP_TQ)
    return {"sel": jnp.stack([prev_sel, diag, masked], axis=1), "win": jnp.stack(win, axis=1),
            "cmp": cmp_tab, "causal": jnp.asarray(cst["causal_tab"])}


def kernel(x, w_in, w_out, attn_norm, ffn_norm, mla_q_a_norm, mla_w_uq, mla_kv_a_norm, mla_w_ukv, mla_q_norm, mla_k_norm, ret_norm, nsa_cmp_pos_k, nsa_cmp_w1_k, nsa_cmp_w2_k, nsa_cmp_pos_v, nsa_cmp_w1_v, nsa_cmp_w2_v, nsa_q_norm, nsa_k_norm_cmp, nsa_k_norm_sel, nsa_k_norm_win, rel_bias, ffn_w_up, ffn_conv_w, ffn_conv_b, ffn_w_down):
    b, t, d = x.shape
    assert d == D_MODEL and t % (2 * ATT_TILE) == 0 and WINDOW % WIN_TILE == 0 and WIN_TILE >= LANES
    cst = _constants(t)
    tabs = _bias_tables(rel_bias, cst)
    stacked = (w_in, w_out, nsa_cmp_w1_k, nsa_cmp_w1_v, ffn_w_up, ffn_w_down)
    per_layer = (attn_norm, ffn_norm, mla_q_a_norm, mla_w_uq, mla_kv_a_norm, mla_w_ukv,
                 mla_q_norm, mla_k_norm, ret_norm, nsa_cmp_pos_k, nsa_cmp_w2_k,
                 nsa_cmp_pos_v, nsa_cmp_w2_v, nsa_q_norm, nsa_k_norm_cmp,
                 nsa_k_norm_sel, nsa_k_norm_win, ffn_conv_w, ffn_conv_b)
    xf = x.reshape(b * t, d)
    for l in range(w_in.shape[0]):
        big = [_to_mxu_dtype(w, l) for w in stacked]
        xf = _layer(xf, b, t, cst, tabs, *big, *[p[l] for p in per_layer])
    return xf.reshape(b, t, d)
```
